```python
import math
import jax
import jax.numpy as jnp
from jax import lax
import numpy as np

D_MODEL = 2048
BATCH = 8
SEQ = 2048
DEPTH = 1

D_MIX = D_MODEL
D_SSM = D_MIX // 2
SSM_GROUP = 16
N_SSM_GROUPS = D_SSM // SSM_GROUP
SSM_STATE = 64
D_RWKV = D_MIX - D_SSM
RWKV_HEAD = 64
N_RWKV_HEADS = D_RWKV // RWKV_HEAD
LORA_W = 64
LORA_A = 64
LORA_G = 128
D_RWKV_IN = 3 * D_RWKV + LORA_W + LORA_A + LORA_G
D_IN = D_SSM + D_RWKV_IN
RWKV_SPLITS = (D_RWKV, 2 * D_RWKV, 3 * D_RWKV, 3 * D_RWKV + LORA_W, 3 * D_RWKV + LORA_W + LORA_A)
N_GROUPS = 8
EXPERTS_PER_GROUP = 8
N_EXPERTS = N_GROUPS * EXPERTS_PER_GROUP
TOP_K = 2
D_EXPERT = D_MODEL // 4
EXPERT_BLOCK = 256
RMS_EPS = 1e-6
GN_EPS = 64e-5
DECAY_SCALE = math.exp(-0.5)
DT_MIN = 1e-3
DT_MAX = 1e-1

kernel_name = "hymba_s5_rwkv7_hmoe_block"


def rms_norm(x, g):
    xf = x.astype(jnp.float32)
    y = xf * lax.rsqrt(jnp.mean(xf * xf, axis=-1, keepdims=True) + RMS_EPS)
    return (y * g.astype(jnp.float32)).astype(x.dtype)


def token_shift(z):
    return jnp.pad(z, ((0, 0), (1, 0), (0, 0)))[:, :-1]


def _complex_linear_recurrence(left, right):
    ar1, ai1, br1, bi1 = left
    ar2, ai2, br2, bi2 = right
    return (ar1 * ar2 - ai1 * ai2,
            ar1 * ai2 + ai1 * ar2,
            ar2 * br1 - ai2 * bi1 + br2,
            ar2 * bi1 + ai2 * br1 + bi2)


def s5_mixer(u, lam_re, lam_im, log_dt, b_re, b_im, c_re, c_im, d_skip, glu_w, glu_b):
    f32 = jnp.float32
    bsz, seqlen, _ = u.shape
    uf = u.astype(f32).reshape(bsz, seqlen, N_SSM_GROUPS, SSM_GROUP)
    lr = jnp.minimum(lam_re.astype(f32), -1e-4)
    li = lam_im.astype(f32)
    dt = jnp.exp(log_dt.astype(f32))[:, None]
    mag = jnp.exp(lr * dt)
    ar, ai = mag * jnp.cos(li * dt), mag * jnp.sin(li * dt)
    den = lr * lr + li * li
    qr = ((ar - 1.0) * lr + ai * li) / den
    qi = (ai * lr - (ar - 1.0) * li) / den
    br, bi = b_re.astype(f32), b_im.astype(f32)
    bbr = qr[..., None] * br - qi[..., None] * bi
    bbi = qr[..., None] * bi + qi[..., None] * br
    ur = jnp.einsum("blgc,gpc->lbgp", uf, bbr)
    ui = jnp.einsum("blgc,gpc->lbgp", uf, bbi)
    a_r = jnp.broadcast_to(ar[None, None], (seqlen, 1, N_SSM_GROUPS, SSM_STATE))
    a_i = jnp.broadcast_to(ai[None, None], (seqlen, 1, N_SSM_GROUPS, SSM_STATE))
    _, _, sr, si = lax.associative_scan(_complex_linear_recurrence, (a_r, a_i, ur, ui), axis=0)
    y = (jnp.einsum("lbgp,gcp->blgc", sr, c_re.astype(f32))
         - jnp.einsum("lbgp,gcp->blgc", si, c_im.astype(f32))
         + d_skip.astype(f32).reshape(N_SSM_GROUPS, SSM_GROUP) * uf)
    y = jax.nn.gelu(y.reshape(bsz, seqlen, D_SSM))
    return y * jax.nn.sigmoid(y @ glu_w.astype(f32) + glu_b.astype(f32))


def _rwkv7_step(state, inp):
    r_t, w_t, k_t, v_t, a_t, b_t = inp
    sa = jnp.einsum("bhvk,bhk->bhv", state, a_t)
    state = (state * w_t[:, :, None, :] + sa[..., None] * b_t[:, :, None, :]
             + v_t[..., None] * k_t[:, :, None, :])
    y_t = jnp.einsum("bhvk,bhk->bhv", state, r_t)
    return state, y_t


def rwkv7_mixer(z, mu, w0, w_up, a0, a_up, g_up, k_k, k_a, r_k, ln_w, ln_b):
    f32 = jnp.float32
    bsz, seqlen, _ = z.shape
    z = z.astype(f32)
    z = z + mu.astype(f32) * (token_shift(z) - z)
    r, k, v, xw, xa, xg = jnp.split(z, RWKV_SPLITS, axis=-1)
    decay = jnp.exp(-DECAY_SCALE * jax.nn.sigmoid(w0.astype(f32) + jnp.tanh(xw) @ w_up.astype(f32)))
    a = jax.nn.sigmoid(a0.astype(f32) + xa @ a_up.astype(f32))
    g = jax.nn.sigmoid(xg) @ g_up.astype(f32)

    def heads(t):
        return t.reshape(bsz, seqlen, N_RWKV_HEADS, RWKV_HEAD)

    kk = heads(k * k_k.astype(f32))
    kk = kk / jnp.maximum(jnp.sqrt(jnp.sum(kk * kk, axis=-1, keepdims=True)), 1e-12)
    k = k * (1.0 + (a - 1.0) * k_a.astype(f32))
    r_h, k_h, v_h, w_h, a_h = heads(r), heads(k), heads(v), heads(decay), heads(a)

    def tmaj(t):
        return jnp.moveaxis(t, 1, 0)

    s0 = jnp.zeros((bsz, N_RWKV_HEADS, RWKV_HEAD, RWKV_HEAD), f32)
    _, y = lax.scan(_rwkv7_step, s0,
                    (tmaj(r_h), tmaj(w_h), tmaj(k_h), tmaj(v_h), tmaj(-kk), tmaj(kk * a_h)))
    y = jnp.moveaxis(y, 0, 1)
    mean = jnp.mean(y, axis=-1, keepdims=True)
    var = jnp.mean(jnp.square(y - mean), axis=-1, keepdims=True)
    y = (y - mean) * lax.rsqrt(var + GN_EPS)
    y = y * ln_w.astype(f32).reshape(N_RWKV_HEADS, RWKV_HEAD) + ln_b.astype(f32).reshape(N_RWKV_HEADS, RWKV_HEAD)
    y = y + jnp.sum(r_h * k_h * r_k.astype(f32), axis=-1, keepdims=True) * v_h
    return y.reshape(bsz, seqlen, D_RWKV) * g


def hier_moe(h, w_grp, b_grp, w_exp, b_exp, w1, w3, w2):
    f32 = jnp.float32
    bsz, seqlen, d = h.shape
    n_tok = bsz * seqlen
    xt = h.reshape(n_tok, d)
    grp_prob = jax.nn.softmax((xt @ w_grp + b_grp).astype(f32), axis=-1)
    p_grp, grp_idx = lax.top_k(grp_prob, 1)
    exp_logits = (xt @ w_exp + b_exp).astype(f32).reshape(n_tok, N_GROUPS, EXPERTS_PER_GROUP)
    within = jnp.take_along_axis(exp_logits, grp_idx[:, :, None], axis=1)[:, 0]
    p_exp, exp_local = lax.top_k(jax.nn.softmax(within, axis=-1), TOP_K)
    p_exp = p_exp / jnp.sum(p_exp, axis=-1, keepdims=True)
    gate_w = p_grp * p_exp
    expert_id = grp_idx * EXPERTS_PER_GROUP + exp_local

    n_slots = n_tok * TOP_K
    flat_e = expert_id.reshape(n_slots).astype(jnp.int32)
    flat_tok = jnp.repeat(jnp.arange(n_tok, dtype=jnp.int32), TOP_K)
    order = jnp.argsort(flat_e, stable=True)
    sorted_e = flat_e[order]
    counts = jnp.zeros((N_EXPERTS,), jnp.int32).at[flat_e].add(1)
    padded = (counts + EXPERT_BLOCK - 1) // EXPERT_BLOCK * EXPERT_BLOCK
    start = jnp.cumsum(counts) - counts
    pend = jnp.cumsum(padded)
    pstart = pend - padded
    dest_sorted = pstart[sorted_e] + (jnp.arange(n_slots, dtype=jnp.int32) - start[sorted_e])
    n_blocks = -(-n_slots // EXPERT_BLOCK) + N_EXPERTS
    cap = n_blocks * EXPERT_BLOCK
    slot_tok = jnp.full((cap,), n_tok, jnp.int32).at[dest_sorted].set(flat_tok[order])
    x_pad = jnp.concatenate([xt, jnp.zeros((1, d), xt.dtype)], axis=0)
    x_buf = x_pad[slot_tok].reshape(n_blocks, EXPERT_BLOCK, d)
    block_e = jnp.minimum(
        jnp.searchsorted(pend, jnp.arange(n_blocks, dtype=jnp.int32) * EXPERT_BLOCK, side="right"),
        N_EXPERTS - 1)

    def expert_block(args):
        xb, e = args
        hid = jax.nn.silu(xb @ w1[e]) * (xb @ w3[e])
        return hid @ w2[e]

    y_buf = lax.map(expert_block, (x_buf, block_e)).reshape(cap, d)
    dest_slot = jnp.zeros((n_slots,), jnp.int32).at[order].set(dest_sorted)
    y_slots = y_buf[dest_slot].reshape(n_tok, TOP_K, d)
    y = jnp.einsum("nk,nkd->nd", gate_w.astype(y_slots.dtype), y_slots)
    return y.reshape(bsz, seqlen, d)


def setup_inputs(seed: int = 0) -> dict:
    key = jax.random.key(seed)
    keys = iter(jax.random.split(key, 48))
    f32 = jnp.float32

    def nrm(shape, scale):
        return jax.random.normal(next(keys), shape, f32) * scale

    def unif(shape, lo, hi):
        return jax.random.uniform(next(keys), shape, f32, lo, hi)

    nl = DEPTH
    g, p, cg = N_SSM_GROUPS, SSM_STATE, SSM_GROUP
    n_idx = jnp.arange(p, dtype=f32)
    return {
        "x": nrm((BATCH, SEQ, D_MODEL), 1.0),
        "c": nrm((BATCH, D_MODEL), 1.0),
        "ada_w": nrm((nl, D_MODEL, 6 * D_MODEL), 0.5 * D_MODEL ** -0.5),
        "ada_b": nrm((nl, 6 * D_MODEL), 0.02),
        "norm_mix_pre": 1.0 + nrm((nl, D_MODEL), 0.02),
        "norm_mix_post": 1.0 + nrm((nl, D_MODEL), 0.02),
        "norm_ffn_pre": 1.0 + nrm((nl, D_MODEL), 0.02),
        "norm_ffn_post": 1.0 + nrm((nl, D_MODEL), 0.02),
        "w_in": nrm((nl, D_MODEL, D_IN), D_MODEL ** -0.5),
        "w_out": nrm((nl, D_MIX, D_MODEL), D_MIX ** -0.5),
        "ssm_lam_re": -0.5 + nrm((nl, g, p), 0.01),
        "ssm_lam_im": math.pi * n_idx + nrm((nl, g, p), 0.01),
        "ssm_log_dt": unif((nl, g), math.log(DT_MIN), math.log(DT_MAX)),
        "ssm_b_re": nrm((nl, g, p, cg), (2 * cg) ** -0.5),
        "ssm_b_im": nrm((nl, g, p, cg), (2 * cg) ** -0.5),
        "ssm_c_re": nrm((nl, g, cg, p), (2 * p) ** -0.5),
        "ssm_c_im": nrm((nl, g, cg, p), (2 * p) ** -0.5),
        "ssm_d": nrm((nl, D_SSM), 1.0),
        "glu_w": nrm((nl, D_SSM, D_SSM), D_SSM ** -0.5),
        "glu_b": nrm((nl, D_SSM), 0.02),
        "rwkv_mu": unif((nl, D_RWKV_IN), 0.0, 1.0),
        "rwkv_w0": unif((nl, D_RWKV), -6.0, 1.0),
        "rwkv_w_up": nrm((nl, LORA_W, D_RWKV), 0.5 * LORA_W ** -0.5),
        "rwkv_a0": nrm((nl, D_RWKV), 0.1),
        "rwkv_a_up": nrm((nl, LORA_A, D_RWKV), 0.5 * LORA_A ** -0.5),
        "rwkv_g_up": nrm((nl, LORA_G, D_RWKV), LORA_G ** -0.5),
        "rwkv_k_k": 0.85 + nrm((nl, D_RWKV), 0.05),
        "rwkv_k_a": 1.0 + nrm((nl, D_RWKV), 0.05),
        "rwkv_r_k": nrm((nl, N_RWKV_HEADS, RWKV_HEAD), 0.1),
        "rwkv_ln_w": 1.0 + nrm((nl, D_RWKV), 0.02),
        "rwkv_ln_b": nrm((nl, D_RWKV), 0.02),
        "moe_w_grp": nrm((nl, D_MODEL, N_GROUPS), D_MODEL ** -0.5),
        "moe_b_grp": nrm((nl, N_GROUPS), 0.01),
        "moe_w_exp": nrm((nl, D_MODEL, N_EXPERTS), D_MODEL ** -0.5),
        "moe_b_exp": nrm((nl, N_EXPERTS), 0.01),
        "moe_w1": nrm((nl, N_EXPERTS, D_MODEL, D_EXPERT), D_MODEL ** -0.5),
        "moe_w3": nrm((nl, N_EXPERTS, D_MODEL, D_EXPERT), D_MODEL ** -0.5),
        "moe_w2": nrm((nl, N_EXPERTS, D_EXPERT, D_MODEL), D_EXPERT ** -0.5),
    }


def reference(x, c, ada_w, ada_b, norm_mix_pre, norm_mix_post, norm_ffn_pre, norm_ffn_post,
              w_in, w_out, ssm_lam_re, ssm_lam_im, ssm_log_dt, ssm_b_re, ssm_b_im, ssm_c_re, ssm_c_im,
              ssm_d, glu_w, glu_b, rwkv_mu, rwkv_w0, rwkv_w_up, rwkv_a0, rwkv_a_up, rwkv_g_up,
              rwkv_k_k, rwkv_k_a, rwkv_r_k, rwkv_ln_w, rwkv_ln_b, moe_w_grp, moe_b_grp, moe_w_exp,
              moe_b_exp, moe_w1, moe_w3, moe_w2):
    cond = jax.nn.silu(c)
    for l in range(DEPTH):
        mod = cond @ ada_w[l] + ada_b[l]
        sh1, sc1, gt1, sh2, sc2, gt2 = [m[:, None, :] for m in jnp.split(mod, 6, axis=-1)]

        h = rms_norm(x, norm_mix_pre[l]) * (1.0 + sc1) + sh1
        proj = h @ w_in[l]
        y_ssm = s5_mixer(proj[..., :D_SSM], ssm_lam_re[l], ssm_lam_im[l], ssm_log_dt[l],
                         ssm_b_re[l], ssm_b_im[l], ssm_c_re[l], ssm_c_im[l], ssm_d[l],
                         glu_w[l], glu_b[l])
        y_rwkv = rwkv7_mixer(proj[..., D_SSM:], rwkv_mu[l], rwkv_w0[l], rwkv_w_up[l], rwkv_a0[l],
                             rwkv_a_up[l], rwkv_g_up[l], rwkv_k_k[l], rwkv_k_a[l], rwkv_r_k[l],
                             rwkv_ln_w[l], rwkv_ln_b[l])
        mixed = jnp.concatenate([y_ssm.astype(x.dtype), y_rwkv.astype(x.dtype)], axis=-1) @ w_out[l]
        x = x + gt1 * rms_norm(mixed, norm_mix_post[l])

        h = rms_norm(x, norm_ffn_pre[l]) * (1.0 + sc2) + sh2
        ffn = hier_moe(h, moe_w_grp[l], moe_b_grp[l], moe_w_exp[l], moe_b_exp[l],
                       moe_w1[l], moe_w3[l], moe_w2[l])
        x = x + gt2 * rms_norm(ffn.astype(x.dtype), norm_ffn_post[l])
    return x
```

```python
import functools
import math

import jax
import jax.numpy as jnp
from jax import lax
from jax.experimental import pallas as pl
from jax.experimental.pallas import tpu as pltpu

F32 = jnp.float32
BF16 = jnp.bfloat16

SSM_GROUP = 16
SSM_STATE = 64
SSM_CHUNK = 16
RWKV_HEAD = 64
RWKV_CHUNK = 64
HEADS_PER_TILE = 4
LORA_W = 64
LORA_A = 64
LORA_G = 128
N_GROUPS = 8
EXPERTS_PER_GROUP = 8
N_EXPERTS = N_GROUPS * EXPERTS_PER_GROUP
EXPERT_BLOCK = 256
RMS_EPS = 1e-6
GN_EPS = 64e-5
DECAY_SCALE = math.exp(-0.5)
LANES = 128
ROUTE_LANES = LANES
VMEM_LIMIT = 52 * 1024 * 1024


def _cparams(*sem):
    return pltpu.CompilerParams(dimension_semantics=sem, vmem_limit_bytes=VMEM_LIMIT)


def _sigmoid(x):
    return 1.0 / (1.0 + jnp.exp(-x))


def _dot(a, b):
    return jnp.dot(a, b, preferred_element_type=F32)


def _dot_nt(a, b):
    return lax.dot_general(a, b, (((1,), (1,)), ((), ())), preferred_element_type=F32)


def _dot_tn(a, b):
    return lax.dot_general(a, b, (((0,), (0,)), ((), ())), preferred_element_type=F32)


def _split_bf16(x):
    hi = x.astype(BF16)
    lo = (x - hi.astype(F32)).astype(BF16)
    return hi, lo


def _pack_pair(a, b):
    ua = lax.bitcast_convert_type(a.astype(BF16).astype(F32), jnp.uint32)
    ub = lax.bitcast_convert_type(b.astype(BF16).astype(F32), jnp.uint32)
    return ub | (ua >> 16)


def _unpack_pair(w):
    lo = lax.bitcast_convert_type(w << 16, F32)
    hi = lax.bitcast_convert_type(w & jnp.uint32(0xFFFF0000), F32)
    return lo, hi


def _store_packed(ref, val, n_rows):
    d = val.shape[1]
    slabs = d // (2 * LANES)
    for s in range(slabs):
        a = val[:, s * LANES:(s + 1) * LANES]
        b = val[:, d // 2 + s * LANES:d // 2 + (s + 1) * LANES]
        ref[pl.ds(s, n_rows, stride=slabs), :] = _pack_pair(a, b)


def _load_packed(ref, n_rows, slabs):
    return [_unpack_pair(ref[pl.ds(s, n_rows, stride=slabs), :]) for s in range(slabs)]


def _ada_kernel(c_ref, w_ref, b_ref, o_ref):
    c = c_ref[...]
    cond = c * _sigmoid(c)
    o_ref[...] = jnp.dot(cond, w_ref[...], preferred_element_type=F32,
                         precision=lax.Precision.HIGHEST) + b_ref[...]


def _ada(c, ada_w, ada_b):
    bsz, d = c.shape
    n = ada_w.shape[1]
    tn = 1024
    return pl.pallas_call(
        _ada_kernel,
        grid=(n // tn,),
        in_specs=[pl.BlockSpec((bsz, d), lambda j: (0, 0)),
                  pl.BlockSpec((d, tn), lambda j: (0, j)),
                  pl.BlockSpec((1, tn), lambda j: (0, j))],
        out_specs=pl.BlockSpec((bsz, tn), lambda j: (0, j)),
        out_shape=jax.ShapeDtypeStruct((bsz, n), F32),
        compiler_params=_cparams("arbitrary"),
        name="ada_mod",
    )(c, ada_w, ada_b.reshape(1, n))


def _rms(x, g):
    return x * lax.rsqrt(jnp.mean(x * x, axis=-1, keepdims=True) + RMS_EPS) * g


def _norm_mod_kernel(x_ref, g_ref, sc_ref, sh_ref, o_ref):
    x = x_ref[...]
    h = _rms(x, g_ref[...]) * (1.0 + sc_ref[0]) + sh_ref[0]
    o_ref[...] = h.astype(o_ref.dtype)


def _norm_mod(x2, g, sc, sh, seqlen, tm):
    n, d = x2.shape
    tpb = seqlen // tm
    return pl.pallas_call(
        _norm_mod_kernel,
        grid=(n // tm,),
        in_specs=[pl.BlockSpec((tm, d), lambda i: (i, 0)),
                  pl.BlockSpec((1, d), lambda i: (0, 0)),
                  pl.BlockSpec((1, 1, d), lambda i: (i // tpb, 0, 0)),
                  pl.BlockSpec((1, 1, d), lambda i: (i // tpb, 0, 0))],
        out_specs=pl.BlockSpec((tm, d), lambda i: (i, 0)),
        out_shape=jax.ShapeDtypeStruct((n, d), BF16),
        compiler_params=_cparams("parallel"),
        name="norm_mod",
    )(x2, g.reshape(1, d), sc[:, None, :], sh[:, None, :])


def _mm_kernel(a_ref, w_ref, o_ref):
    o_ref[...] = _dot(a_ref[...], w_ref[...]).astype(o_ref.dtype)


def _matmul(a, w, tm, tn, out_dtype, name):
    m, k = a.shape
    n = w.shape[1]
    return pl.pallas_call(
        _mm_kernel,
        grid=(m // tm, n // tn),
        in_specs=[pl.BlockSpec((tm, k), lambda i, j: (i, 0)),
                  pl.BlockSpec((k, tn), lambda i, j: (0, j))],
        out_specs=pl.BlockSpec((tm, tn), lambda i, j: (i, j)),
        out_shape=jax.ShapeDtypeStruct((m, n), out_dtype),
        compiler_params=_cparams("parallel", "arbitrary"),
        name=name,
    )(a, w)


def _s5_tables(lam_re, lam_im, log_dt, b_re, b_im, c_re, c_im, d_skip, t_chunk):
    hp = lax.Precision.HIGHEST
    g, p, cg = b_re.shape
    lr = jnp.minimum(lam_re.astype(F32), -1e-4)
    li = lam_im.astype(F32)
    dt = jnp.exp(log_dt.astype(F32))[:, None]
    mag = jnp.exp(lr * dt)
    ar, ai = mag * jnp.cos(li * dt), mag * jnp.sin(li * dt)
    den = lr * lr + li * li
    qr = ((ar - 1.0) * lr + ai * li) / den
    qi = (ai * lr - (ar - 1.0) * li) / den
    br, bi = b_re.astype(F32), b_im.astype(F32)
    bbr = qr[..., None] * br - qi[..., None] * bi
    bbi = qr[..., None] * bi + qi[..., None] * br
    tau = jnp.arange(t_chunk + 1, dtype=F32)[:, None, None]
    pmag = jnp.exp(tau * (lr * dt)[None])
    pr = pmag * jnp.cos(tau * (li * dt)[None])
    pi = pmag * jnp.sin(tau * (li * dt)[None])
    abr = pr[..., None] * bbr[None] - pi[..., None] * bbi[None]
    abi = pr[..., None] * bbi[None] + pi[..., None] * bbr[None]
    cr, ci = c_re.astype(F32), c_im.astype(F32)
    kt = (jnp.einsum("gcp,tgpd->tgcd", cr, abr[:t_chunk], precision=hp)
          - jnp.einsum("gcp,tgpd->tgcd", ci, abi[:t_chunk], precision=hp))
    lag = jnp.arange(t_chunk)[None, :] - jnp.arange(t_chunk)[:, None]
    kfull = kt[jnp.clip(lag, 0, t_chunk - 1)]
    kfull = jnp.where((lag >= 0)[:, :, None, None, None], kfull, 0.0)
    eye_c = jnp.eye(cg, dtype=F32)
    dterm = (d_skip.astype(F32).reshape(g, cg)[:, :, None] * eye_c[None])
    kfull = kfull + jnp.where((lag == 0)[:, :, None, None, None], dterm[None, None], 0.0)
    m_tab = kfull.transpose(2, 0, 4, 1, 3).reshape(g, t_chunk * cg, t_chunk * cg)
    rev = t_chunk - 1 - jnp.arange(t_chunk)
    bst_re = abr[rev].transpose(1, 0, 3, 2).reshape(g, t_chunk * cg, p)
    bst_im = abi[rev].transpose(1, 0, 3, 2).reshape(g, t_chunk * cg, p)
    pr1, pi1 = pr[1:], pi[1:]
    cre = (cr[None] * pr1[:, :, None, :] - ci[None] * pi1[:, :, None, :])
    cim = (-cr[None] * pi1[:, :, None, :] - ci[None] * pr1[:, :, None, :])
    cst_re = cre.transpose(1, 3, 0, 2).reshape(g, p, t_chunk * cg)
    cst_im = cim.transpose(1, 3, 0, 2).reshape(g, p, t_chunk * cg)
    a_chunk = jnp.stack([pr[t_chunk], pi[t_chunk]], axis=1)
    return (m_tab.astype(BF16), bst_re.astype(BF16), bst_im.astype(BF16),
            cst_re.astype(BF16), cst_im.astype(BF16), a_chunk)


def _gelu_tanh(x):
    return 0.5 * x * (1.0 + jnp.tanh(math.sqrt(2.0 / math.pi) * (x + 0.044715 * (x * x * x))))


def _s5_kernel(x_ref, m_ref, bre_ref, bim_ref, cre_ref, cim_ref, a_ref, o_ref,
               zre, zim, sre, sim, *, bsz):
    x = x_ref[0]
    rows = x.shape[0]
    zre[...] = _dot(x, bre_ref[0])
    zim[...] = _dot(x, bim_ref[0])
    ar = a_ref[0, 0:1, :]
    ai = a_ref[0, 1:2, :]

    def body(n, carry):
        s_r, s_i = carry
        sl = pl.ds(pl.multiple_of(n * bsz, bsz), bsz)
        sre[sl, :] = s_r
        sim[sl, :] = s_i
        return (ar * s_r - ai * s_i + zre[sl, :], ar * s_i + ai * s_r + zim[sl, :])

    zero = jnp.zeros((bsz, SSM_STATE), F32)
    lax.fori_loop(0, rows // bsz, body, (zero, zero))
    y = (_dot(x, m_ref[0]) + _dot(sre[...].astype(BF16), cre_ref[0])
         + _dot(sim[...].astype(BF16), cim_ref[0]))
    o_ref[0] = _gelu_tanh(y).astype(o_ref.dtype)


def _s5(u, tabs, bsz, seqlen):
    m_tab, bst_re, bst_im, cst_re, cst_im, a_chunk = tabs
    g = m_tab.shape[0]
    t = SSM_CHUNK
    cg = SSM_GROUP
    nch = seqlen // t
    rows = nch * bsz
    w = t * cg
    x = u.reshape(bsz, nch, t, g, cg).transpose(3, 1, 0, 2, 4).reshape(g, rows, w)
    gspec = lambda *s: pl.BlockSpec((1,) + s, lambda i: (i, 0, 0))
    y = pl.pallas_call(
        functools.partial(_s5_kernel, bsz=bsz),
        grid=(g,),
        in_specs=[gspec(rows, w), gspec(w, w), gspec(w, SSM_STATE), gspec(w, SSM_STATE),
                  gspec(SSM_STATE, w), gspec(SSM_STATE, w), gspec(2, SSM_STATE)],
        out_specs=gspec(rows, w),
        out_shape=jax.ShapeDtypeStruct((g, rows, w), BF16),
        scratch_shapes=[pltpu.VMEM((rows, SSM_STATE), F32)] * 4,
        compiler_params=_cparams("parallel"),
        name="s5_mixer",
    )(x, m_tab, bst_re, bst_im, cst_re, cst_im, a_chunk)
    return y.reshape(g, nch, bsz, t, cg).transpose(2, 1, 3, 0, 4).reshape(bsz * seqlen, g * cg)


def _glu_kernel(y_ref, w_ref, b_ref, o_ref):
    y = y_ref[...]
    gate = _sigmoid(_dot(y, w_ref[...]) + b_ref[...])
    o_ref[...] = (y.astype(F32) * gate).astype(o_ref.dtype)


def _glu(y, w, b, tm):
    n, d = y.shape
    return pl.pallas_call(
        _glu_kernel,
        grid=(n // tm,),
        in_specs=[pl.BlockSpec((tm, d), lambda i: (i, 0)),
                  pl.BlockSpec((d, d), lambda i: (0, 0)),
                  pl.BlockSpec((1, d), lambda i: (0, 0))],
        out_specs=pl.BlockSpec((tm, d), lambda i: (i, 0)),
        out_shape=jax.ShapeDtypeStruct((n, d), BF16),
        compiler_params=_cparams("parallel"),
        name="s5_glu",
    )(y, w, b.reshape(1, d))


def _rwkv_prep_kernel(z_ref, halo_ref, mu_ref, w0_ref, a0_ref, kk_ref, ka_ref, wa_ref, gup_ref,
                      r_out, k_out, v_out, kk_out, as_out, lw_out, g_out, *, tiles_per_seq, d_r):
    i = pl.program_id(0)
    tm = z_ref.shape[0]
    not_first = (i % tiles_per_seq != 0).astype(F32)
    row0 = lax.broadcasted_iota(jnp.int32, (tm, 1), 0) == 0

    def lerp(lo, hi):
        zc = z_ref[:, lo:hi].astype(F32)
        prev_row = halo_ref[7:8, lo:hi].astype(F32) * not_first
        shifted = jnp.where(row0, prev_row, pltpu.roll(zc, 1, 0))
        return zc + mu_ref[:, lo:hi] * (shifted - zc)

    r_out[...] = lerp(0, d_r).astype(r_out.dtype)
    v_out[...] = lerp(2 * d_r, 3 * d_r).astype(v_out.dtype)
    xwa = lerp(3 * d_r, 3 * d_r + LORA_W + LORA_A)
    lane = lax.broadcasted_iota(jnp.int32, xwa.shape, 1)
    lhs = jnp.where(lane < LORA_W, jnp.tanh(xwa), xwa).astype(BF16)
    wa = _dot(lhs, wa_ref[...])
    lw_out[...] = -DECAY_SCALE * _sigmoid(w0_ref[...] + wa[:, :d_r])
    asig = _sigmoid(a0_ref[...] + wa[:, d_r:])
    as_out[...] = asig.astype(as_out.dtype)
    k = lerp(d_r, 2 * d_r)
    kk_out[...] = (k * kk_ref[...]).astype(kk_out.dtype)
    k_out[...] = (k * (1.0 + (asig - 1.0) * ka_ref[...])).astype(k_out.dtype)
    xg = lerp(3 * d_r + LORA_W + LORA_A, 3 * d_r + LORA_W + LORA_A + LORA_G)
    g_out[...] = _dot(_sigmoid(xg).astype(BF16), gup_ref[...]).astype(g_out.dtype)


def _rwkv_prep(z, mu, w0, w_up, a0, a_up, g_up, k_k, k_a, seqlen, tm):
    n, dz = z.shape
    d_r = w0.shape[0]
    wa = jnp.zeros((LORA_W + LORA_A, 2 * d_r), F32)
    wa = wa.at[:LORA_W, :d_r].set(w_up.astype(F32)).at[LORA_W:, d_r:].set(a_up.astype(F32)).astype(BF16)
    row = lambda a: a.astype(F32).reshape(1, -1)
    full = lambda a: pl.BlockSpec(a.shape, lambda i: (0, 0))
    hb = tm // 8
    args = (z, z, row(mu), row(w0), row(a0), row(k_k), row(k_a), wa, g_up.astype(BF16))
    in_specs = [pl.BlockSpec((tm, dz), lambda i: (i, 0)),
                pl.BlockSpec((8, dz), lambda i: (jnp.maximum(i * hb - 1, 0), 0))]
    in_specs += [full(a) for a in args[2:]]
    ospec = pl.BlockSpec((tm, d_r), lambda i: (i, 0))
    bf = jax.ShapeDtypeStruct((n, d_r), BF16)
    return pl.pallas_call(
        functools.partial(_rwkv_prep_kernel, tiles_per_seq=seqlen // tm, d_r=d_r),
        grid=(n // tm,),
        in_specs=in_specs,
        out_specs=[ospec] * 7,
        out_shape=[bf, bf, bf, bf, bf, jax.ShapeDtypeStruct((n, d_r), F32), bf],
        compiler_params=_cparams("parallel"),
        name="rwkv_prep",
    )(*args)


def _rwkv_kernel(r_ref, k_ref, v_ref, kk_ref, as_ref, lw_ref, g_ref, rk_ref, lnw_ref, lnb_ref,
                 ones_ref, o_ref, s_ref):
    t = r_ref.shape[0]
    d_r = r_ref.shape[1]
    tile = HEADS_PER_TILE * RWKV_HEAD
    n_tiles = d_r // tile

    @pl.when(pl.program_id(1) == 0)
    def _():
        s_ref[...] = jnp.zeros_like(s_ref)

    ones_bd = ones_ref[...]

    def seg_sum(x):
        hi, lo = _split_bf16(x)
        return _dot(hi, ones_bd) + _dot(lo, ones_bd)

    row = lax.broadcasted_iota(jnp.int32, (t, t), 0)
    col = lax.broadcasted_iota(jnp.int32, (t, t), 1)
    tri = (row >= col).astype(BF16)
    eye = (row == col).astype(F32)
    r2 = lax.broadcasted_iota(jnp.int32, (2 * t, 2 * t), 0)
    c2 = lax.broadcasted_iota(jnp.int32, (2 * t, 2 * t), 1)
    tt = jnp.where(r2 >= t, r2 - t, r2)
    jj = jnp.where(c2 >= t, c2 - t, c2)
    keep = (tt > jj) | ((r2 >= t) & (tt == jj))
    lane = lax.broadcasted_iota(jnp.int32, (1, tile), 1)
    bd_r = lax.broadcasted_iota(jnp.int32, (tile, tile), 0) // RWKV_HEAD
    bd_c = lax.broadcasted_iota(jnp.int32, (tile, tile), 1) // RWKV_HEAD
    bd_mask = bd_r == bd_c
    n_levels = int(math.log2(t))

    for hg in range(n_tiles):
        sl = slice(hg * tile, (hg + 1) * tile)
        lw = lw_ref[:, sl]
        lw_hi, lw_lo = _split_bf16(lw)
        cum = _dot(tri, lw_hi) + _dot(tri, lw_lo)
        mid = cum[t // 2 - 1:t // 2, :]
        tot = cum[t - 1:t, :]
        e1 = jnp.exp(cum - mid)
        e2 = jnp.exp(mid - cum)
        e1p = e1 * jnp.exp(-lw)
        em = jnp.exp(mid)
        etm = jnp.exp(tot - mid)
        wtot = jnp.exp(tot)

        r = r_ref[:, sl].astype(F32)
        kp = k_ref[:, sl].astype(F32)
        v_bf = v_ref[:, sl]
        v = v_bf.astype(F32)
        kk = kk_ref[:, sl].astype(F32)
        asig = as_ref[:, sl].astype(F32)
        kkn = kk / jnp.maximum(jnp.sqrt(seg_sum(kk * kk)), 1e-12)
        qt = r * e1
        kt = kp * e2
        at = -kkn * e1p
        bt = kkn * asig * e2

        s_old = s_ref[hg]
        aq = jnp.concatenate([at, qt], axis=0)
        bk = jnp.concatenate([bt, kt], axis=0).astype(BF16)
        x_state = _dot_nt((aq * em).astype(BF16), s_old.astype(BF16))
        zv = jnp.concatenate([jnp.zeros_like(v_bf), v_bf], axis=0)

        u_all = jnp.zeros((t, tile), F32)
        y_loc = jnp.zeros((t, tile), F32)
        for j in range(HEADS_PER_TILE):
            lm = (lane >= j * RWKV_HEAD) & (lane < (j + 1) * RWKV_HEAD)
            a_mat = _dot_nt(jnp.where(lm, aq, 0.0).astype(BF16), bk)
            a_mat = jnp.where(keep, a_mat, 0.0)
            top = a_mat[:t]
            bot = a_mat[t:]
            n_pow = top[:, :t]
            p_acc = eye + n_pow
            q_pow = _dot(n_pow.astype(BF16), n_pow.astype(BF16))
            for lev in range(1, n_levels):
                if lev < n_levels - 1:
                    both = _dot(jnp.concatenate([p_acc, q_pow], axis=0).astype(BF16), q_pow.astype(BF16))
                    p_acc = p_acc + both[:t]
                    q_pow = both[t:]
                else:
                    p_acc = p_acc + _dot(p_acc.astype(BF16), q_pow.astype(BF16))
            rhs = x_state[:t] + _dot(top.astype(BF16), zv)
            u_j = _dot(p_acc.astype(BF16), rhs.astype(BF16))
            uv = jnp.concatenate([u_j.astype(BF16), v_bf], axis=0)
            y_j = _dot(bot.astype(BF16), uv)
            u_all = jnp.where(lm, u_j, u_all)
            y_loc = jnp.where(lm, y_j, y_loc)

        y = x_state[t:] + y_loc
        uv_all = jnp.concatenate([u_all, v], axis=0).astype(BF16)
        bk_end = (jnp.concatenate([bt, kt], axis=0) * etm).astype(BF16)
        upd = _dot_tn(uv_all, bk_end)
        s_ref[hg] = s_old * wtot + jnp.where(bd_mask, upd, 0.0)

        mean = seg_sum(y) * (1.0 / RWKV_HEAD)
        dlt = y - mean
        var = seg_sum(dlt * dlt) * (1.0 / RWKV_HEAD)
        yn = dlt * lax.rsqrt(var + GN_EPS) * lnw_ref[:, sl] + lnb_ref[:, sl]
        bonus = seg_sum(r * kp * rk_ref[:, sl])
        out = (yn + bonus * v) * g_ref[:, sl].astype(F32)
        o_ref[:, sl] = out.astype(o_ref.dtype)


def _rwkv(r, k, v, kk, asig, lw, g, r_k, ln_w, ln_b, bsz, seqlen):
    n, d_r = r.shape
    t = RWKV_CHUNK
    nch = seqlen // t
    tile = HEADS_PER_TILE * RWKV_HEAD
    hid = jnp.arange(tile) // RWKV_HEAD
    ones_bd = (hid[:, None] == hid[None, :]).astype(BF16)
    row = lambda a: a.astype(F32).reshape(1, d_r)
    tspec = pl.BlockSpec((t, d_r), lambda b, c: (b * nch + c, 0))
    pspec = pl.BlockSpec((1, d_r), lambda b, c: (0, 0))
    return pl.pallas_call(
        _rwkv_kernel,
        grid=(bsz, nch),
        in_specs=[tspec] * 7 + [pspec] * 3 + [pl.BlockSpec((tile, tile), lambda b, c: (0, 0))],
        out_specs=tspec,
        out_shape=jax.ShapeDtypeStruct((n, d_r), BF16),
        scratch_shapes=[pltpu.VMEM((d_r // tile, tile, tile), F32)],
        compiler_params=_cparams("parallel", "arbitrary"),
        name="rwkv7_chunked",
    )(r, k, v, kk, asig, lw, g, row(r_k), row(ln_w), row(ln_b), ones_bd)


def _post_mix_kernel(ys_ref, yr_ref, wo1_ref, wo2_ref, x_ref, g1_ref, gt_ref, g2_ref, sc_ref, sh_ref,
                     wr_hi_ref, wr_lo_ref, br_ref, x1_out, h2_out, lg_out):
    mixed = _dot(ys_ref[...], wo1_ref[...]) + _dot(yr_ref[...], wo2_ref[...])
    x1 = x_ref[...] + gt_ref[0] * _rms(mixed, g1_ref[...])
    x1_out[...] = x1
    h2 = _rms(x1, g2_ref[...]) * (1.0 + sc_ref[0]) + sh_ref[0]
    _store_packed(h2_out, h2, h2.shape[0])
    hi, lo = _split_bf16(h2)
    lg_out[...] = (_dot(hi, wr_hi_ref[...]) + _dot(lo, wr_hi_ref[...]) + _dot(hi, wr_lo_ref[...])
                   + br_ref[...])


def _post_mix(ys, yr, w_out, x2, g1, gt1, g2, sc2, sh2, w_route, b_route, seqlen, tm):
    n, d = x2.shape
    ds = ys.shape[1]
    tpb = seqlen // tm
    slabs = d // (2 * LANES)
    wo = w_out.astype(BF16)
    wr_hi, wr_lo = _split_bf16(w_route)
    rows = lambda w: pl.BlockSpec((tm, w), lambda i: (i, 0))
    full = lambda a: pl.BlockSpec(a.shape, lambda i: (0,) * a.ndim)
    bat = pl.BlockSpec((1, 1, d), lambda i: (i // tpb, 0, 0))
    args = (ys, yr, wo[:ds], wo[ds:], x2, g1.reshape(1, d), gt1[:, None, :], g2.reshape(1, d),
            sc2[:, None, :], sh2[:, None, :], wr_hi, wr_lo, b_route.reshape(1, -1))
    in_specs = [rows(ds), rows(yr.shape[1]), full(args[2]), full(args[3]), rows(d), full(args[5]), bat,
                full(args[7]), bat, bat, full(wr_hi), full(wr_lo), full(args[12])]
    return pl.pallas_call(
        _post_mix_kernel,
        grid=(n // tm,),
        in_specs=in_specs,
        out_specs=[rows(d), pl.BlockSpec((tm * slabs, LANES), lambda i: (i, 0)), rows(ROUTE_LANES)],
        out_shape=[jax.ShapeDtypeStruct((n, d), F32), jax.ShapeDtypeStruct((n * slabs, LANES), jnp.uint32),
                   jax.ShapeDtypeStruct((n, ROUTE_LANES), F32)],
        compiler_params=_cparams("parallel"),
        name="out_proj_post",
    )(*args)


def _route_kernel(lg_ref, info_ref, cnt_ref, carry):
    i = pl.program_id(0)
    tm = lg_ref.shape[0]

    @pl.when(i == 0)
    def _():
        carry[...] = jnp.zeros_like(carry)

    lg = lg_ref[...]
    lane = lax.broadcasted_iota(jnp.int32, lg.shape, 1)
    lane_f = lane.astype(F32)
    neg = jnp.float32(-jnp.inf)
    big = jnp.float32(1e9)
    is_g = (lane >= N_EXPERTS) & (lane < N_EXPERTS + N_GROUPS)
    gl = jnp.where(is_g, lg, neg)
    gmax = jnp.max(gl, axis=-1, keepdims=True)
    gidx = jnp.min(jnp.where(gl == gmax, lane_f - N_EXPERTS, big), axis=-1, keepdims=True)
    p_grp = 1.0 / jnp.sum(jnp.where(is_g, jnp.exp(gl - gmax), 0.0), axis=-1, keepdims=True)
    in_grp = (lane < N_EXPERTS) & ((lane // EXPERTS_PER_GROUP).astype(F32) == gidx)
    el = jnp.where(in_grp, lg, neg)
    m1 = jnp.max(el, axis=-1, keepdims=True)
    i1 = jnp.min(jnp.where(el == m1, lane_f, big), axis=-1, keepdims=True)
    el2 = jnp.where(lane_f == i1, neg, el)
    m2 = jnp.max(el2, axis=-1, keepdims=True)
    i2 = jnp.min(jnp.where(el2 == m2, lane_f, big), axis=-1, keepdims=True)
    ex = jnp.exp(m2 - m1)
    w1 = p_grp / (1.0 + ex)
    w2 = p_grp * ex / (1.0 + ex)

    oh1 = lane_f == i1
    oh2 = lane_f == i2
    onehot = (oh1 | oh2).astype(BF16)
    rr = lax.broadcasted_iota(jnp.int32, (tm, tm), 0)
    cc = lax.broadcasted_iota(jnp.int32, (tm, tm), 1)
    before = _dot((rr > cc).astype(BF16), onehot) + carry[...]
    rank1 = jnp.sum(jnp.where(oh1, before, 0.0), axis=-1, keepdims=True)
    rank2 = jnp.sum(jnp.where(oh2, before, 0.0), axis=-1, keepdims=True)
    carry[...] = carry[...] + jnp.sum(onehot.astype(F32), axis=0, keepdims=True)
    cnt_ref[...] = carry[...]

    info = jnp.where(lane == 0, i1, 0.0)
    info = jnp.where(lane == 1, i2, info)
    info = jnp.where(lane == 2, w1, info)
    info = jnp.where(lane == 3, w2, info)
    info = jnp.where(lane == 4, rank1, info)
    info = jnp.where(lane == 5, rank2, info)
    info_ref[...] = info


def _route(logits, tm):
    n = logits.shape[0]
    return pl.pallas_call(
        _route_kernel,
        grid=(n // tm,),
        in_specs=[pl.BlockSpec((tm, ROUTE_LANES), lambda i: (i, 0))],
        out_specs=[pl.BlockSpec((tm, ROUTE_LANES), lambda i: (i, 0)),
                   pl.BlockSpec((1, ROUTE_LANES), lambda i: (0, 0))],
        out_shape=[jax.ShapeDtypeStruct((n, ROUTE_LANES), F32),
                   jax.ShapeDtypeStruct((1, ROUTE_LANES), F32)],
        scratch_shapes=[pltpu.VMEM((1, ROUTE_LANES), F32)],
        compiler_params=_cparams("arbitrary"),
        name="moe_route",
    )(logits)


def _dispatch_kernel(dest_ref, h_ref, buf_in, buf_out, sem, *, slabs):
    del buf_in
    tm = h_ref.shape[0] // slabs

    def copy(s):
        src = pl.multiple_of((s >> 1) * slabs, slabs)
        dst = pl.multiple_of(dest_ref[s], slabs)
        return pltpu.make_async_copy(h_ref.at[pl.ds(src, slabs)], buf_out.at[pl.ds(dst, slabs)], sem)

    def start(s, c):
        copy(s).start()
        return c

    def wait(s, c):
        copy(s).wait()
        return c

    lax.fori_loop(0, 2 * tm, start, 0)
    lax.fori_loop(0, 2 * tm, wait, 0)


def _dispatch(h2p, dest_rows, cap, slabs, tm):
    n = h2p.shape[0] // slabs
    buf = jnp.zeros((cap * slabs, LANES), h2p.dtype)
    return pl.pallas_call(
        functools.partial(_dispatch_kernel, slabs=slabs),
        grid=(n // tm,),
        in_specs=[pl.BlockSpec((2 * tm,), lambda i: (i,), memory_space=pltpu.SMEM),
                  pl.BlockSpec((tm * slabs, LANES), lambda i: (i, 0)),
                  pl.BlockSpec(memory_space=pl.ANY)],
        out_specs=pl.BlockSpec(memory_space=pl.ANY),
        out_shape=jax.ShapeDtypeStruct(buf.shape, buf.dtype),
        scratch_shapes=[pltpu.SemaphoreType.DMA],
        input_output_aliases={2: 0},
        compiler_params=_cparams("arbitrary"),
        name="moe_dispatch",
    )(dest_rows, h2p, buf)


def _moe_kernel(be_ref, na_ref, x_ref, w1_ref, w3_ref, w2_ref, o_ref, w1b, w3b, w2b):
    j = pl.program_id(0)
    active = j < na_ref[0]
    prev = be_ref[jnp.maximum(j - 1, 0)]
    fresh = (j == 0) | (be_ref[j] != prev)

    slabs = w1b.shape[0]
    half = slabs * LANES

    @pl.when(active & fresh)
    def _():
        for s in range(slabs):
            for dst, src in ((w1b, w1_ref), (w3b, w3_ref)):
                dst[s, :LANES, :] = src[0, s * LANES:(s + 1) * LANES, :].astype(BF16)
                dst[s, LANES:, :] = src[0, half + s * LANES:half + (s + 1) * LANES, :].astype(BF16)
        w2b[...] = w2_ref[0].astype(BF16)

    @pl.when(active)
    def _():
        acc1 = jnp.zeros((EXPERT_BLOCK, w1b.shape[2]), F32)
        acc3 = jnp.zeros((EXPERT_BLOCK, w1b.shape[2]), F32)
        for s, (lo, hi) in enumerate(_load_packed(x_ref, EXPERT_BLOCK, slabs)):
            lhs = jnp.concatenate([lo.astype(BF16), hi.astype(BF16)], axis=1)
            acc1 = acc1 + _dot(lhs, w1b[s])
            acc3 = acc3 + _dot(lhs, w3b[s])
        hid = (acc1 * _sigmoid(acc1)) * acc3
        _store_packed(o_ref, _dot(hid.astype(BF16), w2b[...]), EXPERT_BLOCK)


def _moe(x_buf, block_e, n_active, w1, w3, w2, slabs):
    cap = x_buf.shape[0] // slabs
    d, de = w1.shape[1], w1.shape[2]
    nb = cap // EXPERT_BLOCK

    def xmap(j, be, na):
        return (jnp.minimum(j, na[0] - 1), 0)

    def wmap(j, be, na):
        return (be[jnp.minimum(j, na[0] - 1)], 0, 0)

    xspec = pl.BlockSpec((EXPERT_BLOCK * slabs, LANES), xmap)
    grid_spec = pltpu.PrefetchScalarGridSpec(
        num_scalar_prefetch=2,
        grid=(nb,),
        in_specs=[xspec,
                  pl.BlockSpec((1, d, de), wmap),
                  pl.BlockSpec((1, d, de), wmap),
                  pl.BlockSpec((1, de, d), wmap)],
        out_specs=xspec,
        scratch_shapes=[pltpu.VMEM((slabs, 2 * LANES, de), BF16), pltpu.VMEM((slabs, 2 * LANES, de), BF16),
                        pltpu.VMEM((de, d), BF16)],
    )
    return pl.pallas_call(
        _moe_kernel,
        grid_spec=grid_spec,
        out_shape=jax.ShapeDtypeStruct(x_buf.shape, x_buf.dtype),
        input_output_aliases={2: 0},
        compiler_params=_cparams("arbitrary"),
        name="moe_experts",
    )(block_e, n_active, x_buf, w1, w3, w2)


def _combine_kernel(dest_ref, y_ref, info_ref, x1_ref, g_ref, gt_ref, o_ref, rows0, rows1, sem, *, slabs):
    tm = x1_ref.shape[0]

    def copy(s, buf):
        src = pl.multiple_of(dest_ref[s], slabs)
        dst = pl.multiple_of((s >> 1) * slabs, slabs)
        return pltpu.make_async_copy(y_ref.at[pl.ds(src, slabs)], buf.at[pl.ds(dst, slabs)], sem)

    def start(t, c):
        copy(2 * t, rows0).start()
        copy(2 * t + 1, rows1).start()
        return c

    def wait(t, c):
        copy(2 * t, rows0).wait()
        copy(2 * t + 1, rows1).wait()
        return c

    lax.fori_loop(0, tm, start, 0)
    lax.fori_loop(0, tm, wait, 0)
    info = info_ref[...]
    w1 = info[:, 2:3]
    w2 = info[:, 3:4]
    lo_parts, hi_parts = [], []
    for (lo0, hi0), (lo1, hi1) in zip(_load_packed(rows0, tm, slabs), _load_packed(rows1, tm, slabs)):
        lo_parts.append(w1 * lo0 + w2 * lo1)
        hi_parts.append(w1 * hi0 + w2 * hi1)
    ffn = jnp.concatenate(lo_parts + hi_parts, axis=1)
    o_ref[...] = x1_ref[...] + gt_ref[0] * _rms(ffn, g_ref[...])


def _combine(y_buf, dest_rows, info, x1, g, gt2, seqlen, slabs, tm):
    n, d = x1.shape
    tpb = seqlen // tm
    return pl.pallas_call(
        functools.partial(_combine_kernel, slabs=slabs),
        grid=(n // tm,),
        in_specs=[pl.BlockSpec((2 * tm,), lambda i: (i,), memory_space=pltpu.SMEM),
                  pl.BlockSpec(memory_space=pl.ANY),
                  pl.BlockSpec((tm, ROUTE_LANES), lambda i: (i, 0)),
                  pl.BlockSpec((tm, d), lambda i: (i, 0)),
                  pl.BlockSpec((1, d), lambda i: (0, 0)),
                  pl.BlockSpec((1, 1, d), lambda i: (i // tpb, 0, 0))],
        out_specs=pl.BlockSpec((tm, d), lambda i: (i, 0)),
        out_shape=jax.ShapeDtypeStruct((n, d), F32),
        scratch_shapes=[pltpu.VMEM((tm * slabs, LANES), y_buf.dtype),
                        pltpu.VMEM((tm * slabs, LANES), y_buf.dtype), pltpu.SemaphoreType.DMA],
        compiler_params=_cparams("arbitrary"),
        name="moe_combine",
    )(dest_rows, y_buf, info, x1, g.reshape(1, d), gt2[:, None, :])


def _pick(n, pref):
    while n % pref:
        pref //= 2
    return pref


def _layer(x2, mod, p, bsz, seqlen):
    n, d = x2.shape
    sh1, sc1, gt1, sh2, sc2, gt2 = jnp.split(mod, 6, axis=-1)
    d_ssm = p["ssm_d"].shape[0]
    d_r = p["rwkv_w0"].shape[0]
    tm = _pick(seqlen, 1024)

    h1 = _norm_mod(x2, p["norm_mix_pre"], sc1, sh1, seqlen, _pick(seqlen, 512))
    w_in = p["w_in"].astype(BF16)
    w_z = w_in[:, d_ssm:]
    dz = w_z.shape[1]
    u = _matmul(h1, w_in[:, :d_ssm], tm, d_ssm, BF16, "in_proj_u")
    z = _matmul(h1, w_z, tm, dz // 2 if (dz // 2) % 128 == 0 else dz, BF16, "in_proj_z")

    tabs = _s5_tables(p["ssm_lam_re"], p["ssm_lam_im"], p["ssm_log_dt"], p["ssm_b_re"], p["ssm_b_im"],
                      p["ssm_c_re"], p["ssm_c_im"], p["ssm_d"], SSM_CHUNK)
    y_ssm = _glu(_s5(u, tabs, bsz, seqlen), p["glu_w"].astype(BF16), p["glu_b"].astype(F32),
                 _pick(seqlen, 512))

    r, k, v, kk, asig, lw, g = _rwkv_prep(z, p["rwkv_mu"], p["rwkv_w0"], p["rwkv_w_up"], p["rwkv_a0"],
                                          p["rwkv_a_up"], p["rwkv_g_up"], p["rwkv_k_k"], p["rwkv_k_a"],
                                          seqlen, _pick(seqlen, 256))
    y_rwkv = _rwkv(r, k, v, kk, asig, lw, g, p["rwkv_r_k"], p["rwkv_ln_w"], p["rwkv_ln_b"], bsz, seqlen)

    w_route = jnp.zeros((d, ROUTE_LANES), F32)
    w_route = w_route.at[:, :N_EXPERTS].set(p["moe_w_exp"].astype(F32))
    w_route = w_route.at[:, N_EXPERTS:N_EXPERTS + N_GROUPS].set(p["moe_w_grp"].astype(F32))
    b_route = jnp.zeros((ROUTE_LANES,), F32)
    b_route = b_route.at[:N_EXPERTS].set(p["moe_b_exp"].astype(F32))
    b_route = b_route.at[N_EXPERTS:N_EXPERTS + N_GROUPS].set(p["moe_b_grp"].astype(F32))
    x1, h2p, logits = _post_mix(y_ssm, y_rwkv, p["w_out"], x2, p["norm_mix_post"], gt1, p["norm_ffn_pre"],
                                sc2, sh2, w_route, b_route, seqlen, _pick(seqlen, 256))
    slabs = d // (2 * LANES)

    info, counts = _route(logits, _pick(n, 512))
    cnt = counts[0, :N_EXPERTS].astype(jnp.int32)
    padded = (cnt + EXPERT_BLOCK - 1) // EXPERT_BLOCK * EXPERT_BLOCK
    pend = jnp.cumsum(padded)
    pstart = pend - padded
    n_blocks = -(-(2 * n) // EXPERT_BLOCK) + N_EXPERTS
    cap = n_blocks * EXPERT_BLOCK
    e12 = info[:, 0:2].astype(jnp.int32)
    dest_rows = (pstart[e12] + info[:, 4:6].astype(jnp.int32)).reshape(2 * n) * slabs
    block_e = jnp.minimum(
        jnp.searchsorted(pend, jnp.arange(n_blocks, dtype=jnp.int32) * EXPERT_BLOCK, side="right"),
        N_EXPERTS - 1).astype(jnp.int32)
    n_active = (pend[-1:] // EXPERT_BLOCK).astype(jnp.int32)

    tdm = _pick(n, 512)
    x_buf = _dispatch(h2p, dest_rows, cap, slabs, tdm)
    y_buf = _moe(x_buf, block_e, n_active, p["moe_w1"], p["moe_w3"], p["moe_w2"], slabs)
    return _combine(y_buf, dest_rows, info, x1, p["norm_ffn_post"], gt2, seqlen, slabs, _pick(seqlen, 512))


def kernel(x, c, ada_w, ada_b, norm_mix_pre, norm_mix_post, norm_ffn_pre, norm_ffn_post, w_in, w_out, ssm_lam_re, ssm_lam_im, ssm_log_dt, ssm_b_re, ssm_b_im, ssm_c_re, ssm_c_im, ssm_d, glu_w, glu_b, rwkv_mu, rwkv_w0, rwkv_w_up, rwkv_a0, rwkv_a_up, rwkv_g_up, rwkv_k_k, rwkv_k_a, rwkv_r_k, rwkv_ln_w, rwkv_ln_b, moe_w_grp, moe_b_grp, moe_w_exp, moe_b_exp, moe_w1, moe_w3, moe_w2):
    bsz, seqlen, d = x.shape
    params = dict(norm_mix_pre=norm_mix_pre, norm_mix_post=norm_mix_post, norm_ffn_pre=norm_ffn_pre,
                  norm_ffn_post=norm_ffn_post, w_in=w_in, w_out=w_out, ssm_lam_re=ssm_lam_re,
                  ssm_lam_im=ssm_lam_im, ssm_log_dt=ssm_log_dt, ssm_b_re=ssm_b_re, ssm_b_im=ssm_b_im,
                  ssm_c_re=ssm_c_re, ssm_c_im=ssm_c_im, ssm_d=ssm_d, glu_w=glu_w, glu_b=glu_b,
                  rwkv_mu=rwkv_mu, rwkv_w0=rwkv_w0, rwkv_w_up=rwkv_w_up, rwkv_a0=rwkv_a0,
                  rwkv_a_up=rwkv_a_up, rwkv_g_up=rwkv_g_up, rwkv_k_k=rwkv_k_k, rwkv_k_a=rwkv_k_a,
                  rwkv_r_k=rwkv_r_k, rwkv_ln_w=rwkv_ln_w, rwkv_ln_b=rwkv_ln_b, moe_w_grp=moe_w_grp,
                  moe_b_grp=moe_b_grp, moe_w_exp=moe_w_exp, moe_b_exp=moe_b_exp, moe_w1=moe_w1,
                  moe_w3=moe_w3, moe_w2=moe_w2)
    x2 = x.reshape(bsz * seqlen, d)
    for layer in range(ada_w.shape[0]):
        mod = _ada(c, ada_w[layer], ada_b[layer])
        x2 = _layer(x2, mod, {k: v[layer] for k, v in params.items()}, bsz, seqlen)
    return x2.reshape(bsz, seqlen, d)
```

```python
import functools
import math

import jax
import jax.numpy as jnp
from jax import lax
from jax.experimental import pallas as pl
from jax.experimental.pallas import tpu as pltpu

F32 = jnp.float32
BF16 = jnp.bfloat16

SSM_GROUP = 16
SSM_STATE = 64
SSM_CHUNK = 16
RWKV_HEAD = 64
RWKV_CHUNK = 64
HEADS_PER_TILE = 4
LORA_W = 64
LORA_A = 64
LORA_G = 128
N_GROUPS = 8
EXPERTS_PER_GROUP = 8
N_EXPERTS = N_GROUPS * EXPERTS_PER_GROUP
EXPERT_BLOCK = 256
RMS_EPS = 1e-6
GN_EPS = 64e-5
DECAY_SCALE = math.exp(-0.5)
LANES = 128
ROUTE_LANES = LANES
VMEM_LIMIT = 52 * 1024 * 1024


def _cparams(*sem):
    return pltpu.CompilerParams(dimension_semantics=sem, vmem_limit_bytes=VMEM_LIMIT)


def _sigmoid(x):
    return 1.0 / (1.0 + jnp.exp(-x))


def _dot(a, b):
    return jnp.dot(a, b, preferred_element_type=F32)


def _dot_nt(a, b):
    return lax.dot_general(a, b, (((1,), (1,)), ((), ())), preferred_element_type=F32)


def _dot_tn(a, b):
    return lax.dot_general(a, b, (((0,), (0,)), ((), ())), preferred_element_type=F32)


def _split_bf16(x):
    hi = x.astype(BF16)
    lo = (x - hi.astype(F32)).astype(BF16)
    return hi, lo


def _pack_pair(a, b):
    ua = lax.bitcast_convert_type(a.astype(BF16).astype(F32), jnp.uint32)
    ub = lax.bitcast_convert_type(b.astype(BF16).astype(F32), jnp.uint32)
    return ub | (ua >> 16)


def _unpack_pair(w):
    lo = lax.bitcast_convert_type(w << 16, F32)
    hi = lax.bitcast_convert_type(w & jnp.uint32(0xFFFF0000), F32)
    return lo, hi


def _store_packed(ref, val, n_rows):
    d = val.shape[1]
    slabs = d // (2 * LANES)
    for s in range(slabs):
        a = val[:, s * LANES:(s + 1) * LANES]
        b = val[:, d // 2 + s * LANES:d // 2 + (s + 1) * LANES]
        ref[pl.ds(s, n_rows, stride=slabs), :] = _pack_pair(a, b)


def _load_packed(ref, n_rows, slabs):
    return [_unpack_pair(ref[pl.ds(s, n_rows, stride=slabs), :]) for s in range(slabs)]


def _ada_kernel(c_ref, w_ref, b_ref, o_ref):
    c = c_ref[...]
    cond = c * _sigmoid(c)
    o_ref[...] = jnp.dot(cond, w_ref[...], preferred_element_type=F32,
                         precision=lax.Precision.HIGHEST) + b_ref[...]


def _ada(c, ada_w, ada_b):
    bsz, d = c.shape
    n = ada_w.shape[1]
    tn = 1024
    return pl.pallas_call(
        _ada_kernel,
        grid=(n // tn,),
        in_specs=[pl.BlockSpec((bsz, d), lambda j: (0, 0)),
                  pl.BlockSpec((d, tn), lambda j: (0, j)),
                  pl.BlockSpec((1, tn), lambda j: (0, j))],
        out_specs=pl.BlockSpec((bsz, tn), lambda j: (0, j)),
        out_shape=jax.ShapeDtypeStruct((bsz, n), F32),
        compiler_params=_cparams("arbitrary"),
        name="ada_mod",
    )(c, ada_w, ada_b.reshape(1, n))


def _rms(x, g):
    return x * lax.rsqrt(jnp.mean(x * x, axis=-1, keepdims=True) + RMS_EPS) * g


def _norm_mod_kernel(x_ref, g_ref, sc_ref, sh_ref, o_ref):
    x = x_ref[...]
    h = _rms(x, g_ref[...]) * (1.0 + sc_ref[0]) + sh_ref[0]
    o_ref[...] = h.astype(o_ref.dtype)


def _norm_mod(x2, g, sc, sh, seqlen, tm):
    n, d = x2.shape
    tpb = seqlen // tm
    return pl.pallas_call(
        _norm_mod_kernel,
        grid=(n // tm,),
        in_specs=[pl.BlockSpec((tm, d), lambda i: (i, 0)),
                  pl.BlockSpec((1, d), lambda i: (0, 0)),
                  pl.BlockSpec((1, 1, d), lambda i: (i // tpb, 0, 0)),
                  pl.BlockSpec((1, 1, d), lambda i: (i // tpb, 0, 0))],
        out_specs=pl.BlockSpec((tm, d), lambda i: (i, 0)),
        out_shape=jax.ShapeDtypeStruct((n, d), BF16),
        compiler_params=_cparams("parallel"),
        name="norm_mod",
    )(x2, g.reshape(1, d), sc[:, None, :], sh[:, None, :])


def _mm_kernel(a_ref, w_ref, o_ref):
    o_ref[...] = _dot(a_ref[...], w_ref[...]).astype(o_ref.dtype)


def _matmul(a, w, tm, tn, out_dtype, name):
    m, k = a.shape
    n = w.shape[1]
    return pl.pallas_call(
        _mm_kernel,
        grid=(m // tm, n // tn),
        in_specs=[pl.BlockSpec((tm, k), lambda i, j: (i, 0)),
                  pl.BlockSpec((k, tn), lambda i, j: (0, j))],
        out_specs=pl.BlockSpec((tm, tn), lambda i, j: (i, j)),
        out_shape=jax.ShapeDtypeStruct((m, n), out_dtype),
        compiler_params=_cparams("parallel", "arbitrary"),
        name=name,
    )(a, w)


def _s5_tables(lam_re, lam_im, log_dt, b_re, b_im, c_re, c_im, d_skip, t_chunk):
    hp = lax.Precision.HIGHEST
    g, p, cg = b_re.shape
    lr = jnp.minimum(lam_re.astype(F32), -1e-4)
    li = lam_im.astype(F32)
    dt = jnp.exp(log_dt.astype(F32))[:, None]
    mag = jnp.exp(lr * dt)
    ar, ai = mag * jnp.cos(li * dt), mag * jnp.sin(li * dt)
    den = lr * lr + li * li
    qr = ((ar - 1.0) * lr + ai * li) / den
    qi = (ai * lr - (ar - 1.0) * li) / den
    br, bi = b_re.astype(F32), b_im.astype(F32)
    bbr = qr[..., None] * br - qi[..., None] * bi
    bbi = qr[..., None] * bi + qi[..., None] * br
    tau = jnp.arange(t_chunk + 1, dtype=F32)[:, None, None]
    pmag = jnp.exp(tau * (lr * dt)[None])
    pr = pmag * jnp.cos(tau * (li * dt)[None])
    pi = pmag * jnp.sin(tau * (li * dt)[None])
    abr = pr[..., None] * bbr[None] - pi[..., None] * bbi[None]
    abi = pr[..., None] * bbi[None] + pi[..., None] * bbr[None]
    cr, ci = c_re.astype(F32), c_im.astype(F32)
    kt = (jnp.einsum("gcp,tgpd->tgcd", cr, abr[:t_chunk], precision=hp)
          - jnp.einsum("gcp,tgpd->tgcd", ci, abi[:t_chunk], precision=hp))
    lag = jnp.arange(t_chunk)[None, :] - jnp.arange(t_chunk)[:, None]
    kfull = kt[jnp.clip(lag, 0, t_chunk - 1)]
    kfull = jnp.where((lag >= 0)[:, :, None, None, None], kfull, 0.0)
    eye_c = jnp.eye(cg, dtype=F32)
    dterm = (d_skip.astype(F32).reshape(g, cg)[:, :, None] * eye_c[None])
    kfull = kfull + jnp.where((lag == 0)[:, :, None, None, None], dterm[None, None], 0.0)
    m_tab = kfull.transpose(2, 0, 4, 1, 3).reshape(g, t_chunk * cg, t_chunk * cg)
    rev = t_chunk - 1 - jnp.arange(t_chunk)
    bst_re = abr[rev].transpose(1, 0, 3, 2).reshape(g, t_chunk * cg, p)
    bst_im = abi[rev].transpose(1, 0, 3, 2).reshape(g, t_chunk * cg, p)
    pr1, pi1 = pr[1:], pi[1:]
    cre = (cr[None] * pr1[:, :, None, :] - ci[None] * pi1[:, :, None, :])
    cim = (-cr[None] * pi1[:, :, None, :] - ci[None] * pr1[:, :, None, :])
    cst_re = cre.transpose(1, 3, 0, 2).reshape(g, p, t_chunk * cg)
    cst_im = cim.transpose(1, 3, 0, 2).reshape(g, p, t_chunk * cg)
    a_chunk = jnp.stack([pr[t_chunk], pi[t_chunk]], axis=1)
    return (m_tab.astype(BF16), bst_re.astype(BF16), bst_im.astype(BF16),
            cst_re.astype(BF16), cst_im.astype(BF16), a_chunk)


def _gelu_tanh(x):
    return 0.5 * x * (1.0 + jnp.tanh(math.sqrt(2.0 / math.pi) * (x + 0.044715 * (x * x * x))))


def _s5_kernel(x_ref, m_ref, bre_ref, bim_ref, cre_ref, cim_ref, a_ref, o_ref,
               zre, zim, sre, sim, *, bsz):
    x = x_ref[0]
    rows = x.shape[0]
    zre[...] = _dot(x, bre_ref[0])
    zim[...] = _dot(x, bim_ref[0])
    ar = a_ref[0, 0:1, :]
    ai = a_ref[0, 1:2, :]

    def body(n, carry):
        s_r, s_i = carry
        sl = pl.ds(pl.multiple_of(n * bsz, bsz), bsz)
        sre[sl, :] = s_r
        sim[sl, :] = s_i
        return (ar * s_r - ai * s_i + zre[sl, :], ar * s_i + ai * s_r + zim[sl, :])

    zero = jnp.zeros((bsz, SSM_STATE), F32)
    lax.fori_loop(0, rows // bsz, body, (zero, zero))
    y = (_dot(x, m_ref[0]) + _dot(sre[...].astype(BF16), cre_ref[0])
         + _dot(sim[...].astype(BF16), cim_ref[0]))
    o_ref[0] = _gelu_tanh(y).astype(o_ref.dtype)


def _s5(u, tabs, bsz, seqlen):
    m_tab, bst_re, bst_im, cst_re, cst_im, a_chunk = tabs
    g = m_tab.shape[0]
    t = SSM_CHUNK
    cg = SSM_GROUP
    nch = seqlen // t
    rows = nch * bsz
    w = t * cg
    x = u.reshape(bsz, nch, t, g, cg).transpose(3, 1, 0, 2, 4).reshape(g, rows, w)
    gspec = lambda *s: pl.BlockSpec((1,) + s, lambda i: (i, 0, 0))
    y = pl.pallas_call(
        functools.partial(_s5_kernel, bsz=bsz),
        grid=(g,),
        in_specs=[gspec(rows, w), gspec(w, w), gspec(w, SSM_STATE), gspec(w, SSM_STATE),
                  gspec(SSM_STATE, w), gspec(SSM_STATE, w), gspec(2, SSM_STATE)],
        out_specs=gspec(rows, w),
        out_shape=jax.ShapeDtypeStruct((g, rows, w), BF16),
        scratch_shapes=[pltpu.VMEM((rows, SSM_STATE), F32)] * 4,
        compiler_params=_cparams("parallel"),
        name="s5_mixer",
    )(x, m_tab, bst_re, bst_im, cst_re, cst_im, a_chunk)
    return y.reshape(g, nch, bsz, t, cg).transpose(2, 1, 3, 0, 4).reshape(bsz * seqlen, g * cg)


def _glu_kernel(y_ref, w_ref, b_ref, o_ref):
    y = y_ref[...]
    gate = _sigmoid(_dot(y, w_ref[...]) + b_ref[...])
    o_ref[...] = (y.astype(F32) * gate).astype(o_ref.dtype)


def _glu(y, w, b, tm):
    n, d = y.shape
    return pl.pallas_call(
        _glu_kernel,
        grid=(n // tm,),
        in_specs=[pl.BlockSpec((tm, d), lambda i: (i, 0)),
                  pl.BlockSpec((d, d), lambda i: (0, 0)),
                  pl.BlockSpec((1, d), lambda i: (0, 0))],
        out_specs=pl.BlockSpec((tm, d), lambda i: (i, 0)),
        out_shape=jax.ShapeDtypeStruct((n, d), BF16),
        compiler_params=_cparams("parallel"),
        name="s5_glu",
    )(y, w, b.reshape(1, d))


def _rwkv_prep_kernel(z_ref, halo_ref, mu_ref, w0_ref, a0_ref, kk_ref, ka_ref, wa_ref, gup_ref,
                      r_out, k_out, v_out, kk_out, as_out, lw_out, g_out, *, tiles_per_seq, d_r):
    i = pl.program_id(0)
    tm = z_ref.shape[0]
    not_first = (i % tiles_per_seq != 0).astype(F32)
    row0 = lax.broadcasted_iota(jnp.int32, (tm, 1), 0) == 0

    def lerp(lo, hi):
        zc = z_ref[:, lo:hi].astype(F32)
        prev_row = halo_ref[7:8, lo:hi].astype(F32) * not_first
        shifted = jnp.where(row0, prev_row, pltpu.roll(zc, 1, 0))
        return zc + mu_ref[:, lo:hi] * (shifted - zc)

    r_out[...] = lerp(0, d_r).astype(r_out.dtype)
    v_out[...] = lerp(2 * d_r, 3 * d_r).astype(v_out.dtype)
    xwa = lerp(3 * d_r, 3 * d_r + LORA_W + LORA_A)
    lane = lax.broadcasted_iota(jnp.int32, xwa.shape, 1)
    lhs = jnp.where(lane < LORA_W, jnp.tanh(xwa), xwa).astype(BF16)
    wa = _dot(lhs, wa_ref[...])
    lw_out[...] = -DECAY_SCALE * _sigmoid(w0_ref[...] + wa[:, :d_r])
    asig = _sigmoid(a0_ref[...] + wa[:, d_r:])
    as_out[...] = asig.astype(as_out.dtype)
    k = lerp(d_r, 2 * d_r)
    kk_out[...] = (k * kk_ref[...]).astype(kk_out.dtype)
    k_out[...] = (k * (1.0 + (asig - 1.0) * ka_ref[...])).astype(k_out.dtype)
    xg = lerp(3 * d_r + LORA_W + LORA_A, 3 * d_r + LORA_W + LORA_A + LORA_G)
    g_out[...] = _dot(_sigmoid(xg).astype(BF16), gup_ref[...]).astype(g_out.dtype)


def _rwkv_prep(z, mu, w0, w_up, a0, a_up, g_up, k_k, k_a, seqlen, tm):
    n, dz = z.shape
    d_r = w0.shape[0]
    wa = jnp.zeros((LORA_W + LORA_A, 2 * d_r), F32)
    wa = wa.at[:LORA_W, :d_r].set(w_up.astype(F32)).at[LORA_W:, d_r:].set(a_up.astype(F32)).astype(BF16)
    row = lambda a: a.astype(F32).reshape(1, -1)
    full = lambda a: pl.BlockSpec(a.shape, lambda i: (0, 0))
    hb = tm // 8
    args = (z, z, row(mu), row(w0), row(a0), row(k_k), row(k_a), wa, g_up.astype(BF16))
    in_specs = [pl.BlockSpec((tm, dz), lambda i: (i, 0)),
                pl.BlockSpec((8, dz), lambda i: (jnp.maximum(i * hb - 1, 0), 0))]
    in_specs += [full(a) for a in args[2:]]
    ospec = pl.BlockSpec((tm, d_r), lambda i: (i, 0))
    bf = jax.ShapeDtypeStruct((n, d_r), BF16)
    return pl.pallas_call(
        functools.partial(_rwkv_prep_kernel, tiles_per_seq=seqlen // tm, d_r=d_r),
        grid=(n // tm,),
        in_specs=in_specs,
        out_specs=[ospec] * 7,
        out_shape=[bf, bf, bf, bf, bf, jax.ShapeDtypeStruct((n, d_r), F32), bf],
        compiler_params=_cparams("parallel"),
        name="rwkv_prep",
    )(*args)


def _rwkv_kernel(r_ref, k_ref, v_ref, kk_ref, as_ref, lw_ref, g_ref, rk_ref, lnw_ref, lnb_ref,
                 ones_ref, o_ref, s_ref):
    t = r_ref.shape[0]
    d_r = r_ref.shape[1]
    tile = HEADS_PER_TILE * RWKV_HEAD
    n_tiles = d_r // tile

    @pl.when(pl.program_id(1) == 0)
    def _():
        s_ref[...] = jnp.zeros_like(s_ref)

    ones_bd = ones_ref[...]

    def seg_sum(x):
        hi, lo = _split_bf16(x)
        return _dot(hi, ones_bd) + _dot(lo, ones_bd)

    row = lax.broadcasted_iota(jnp.int32, (t, t), 0)
    col = lax.broadcasted_iota(jnp.int32, (t, t), 1)
    tri = (row >= col).astype(BF16)
    eye = (row == col).astype(F32)
    st = HEADS_PER_TILE * t
    rs = lax.broadcasted_iota(jnp.int32, (2 * st, 2 * st), 0)
    cs = lax.broadcasted_iota(jnp.int32, (2 * st, 2 * st), 1)
    t_r = rs % t
    t_c = cs % t
    keep = (t_r > t_c) | ((rs >= st) & (t_r == t_c))
    eye_st = (lax.broadcasted_iota(jnp.int32, (st, st), 0)
              == lax.broadcasted_iota(jnp.int32, (st, st), 1)).astype(F32)
    lane = lax.broadcasted_iota(jnp.int32, (1, tile), 1)
    head_masks = [(lane >= j * RWKV_HEAD) & (lane < (j + 1) * RWKV_HEAD) for j in range(HEADS_PER_TILE)]
    bd_r = lax.broadcasted_iota(jnp.int32, (tile, tile), 0) // RWKV_HEAD
    bd_c = lax.broadcasted_iota(jnp.int32, (tile, tile), 1) // RWKV_HEAD
    bd_mask = bd_r == bd_c
    n_levels = int(math.log2(t))
    tiles = range(n_tiles)
    slices = [slice(hg * tile, (hg + 1) * tile) for hg in tiles]

    def stack(x):
        return jnp.concatenate([jnp.where(m, x, 0.0) for m in head_masks], axis=0)

    def collapse(x):
        out = x[:t]
        for j in range(1, HEADS_PER_TILE):
            out = out + x[j * t:(j + 1) * t]
        return out

    def bf(x):
        return x.astype(BF16)

    lw = lw_ref[...]
    lw_hi, lw_lo = _split_bf16(lw)
    cum = _dot(tri, lw_hi) + _dot(tri, lw_lo)
    mid = cum[t // 2 - 1:t // 2, :]
    tot = cum[t - 1:t, :]
    e1 = jnp.exp(cum - mid)
    e2 = jnp.exp(mid - cum)
    e1p = e1 * jnp.exp(-lw)
    em = jnp.exp(mid)
    etm = jnp.exp(tot - mid)
    wtot = jnp.exp(tot)
    r = r_ref[...].astype(F32)
    kp = k_ref[...].astype(F32)
    v = v_ref[...].astype(F32)
    kk = kk_ref[...].astype(F32)
    asig = as_ref[...].astype(F32)
    kk2 = kk * kk
    kkn = kk / jnp.maximum(jnp.sqrt(jnp.concatenate([seg_sum(kk2[:, sl]) for sl in slices], axis=1)), 1e-12)
    qt = r * e1
    kt = kp * e2
    at = -kkn * e1p
    bt = kkn * asig * e2
    rkb = r * kp * rk_ref[...]

    s_old = [s_ref[hg] for hg in tiles]
    a_all = []
    for sl in slices:
        lhs = jnp.concatenate([stack(at[:, sl]), stack(qt[:, sl])], axis=0)
        rhs = jnp.concatenate([stack(bt[:, sl]), stack(kt[:, sl])], axis=0)
        a_all.append(jnp.where(keep, _dot_nt(bf(lhs), bf(rhs)), 0.0))
    x_state = []
    for hg, sl in enumerate(slices):
        lhs = jnp.concatenate([stack(at[:, sl] * em[:, sl]), stack(qt[:, sl] * em[:, sl])], axis=0)
        x_state.append(_dot_nt(bf(lhs), bf(s_old[hg])))
    p_acc = [eye_st + a[:st, :st] for a in a_all]
    q_pow = [_dot(bf(a[:st, :st]), bf(a[:st, :st])) for a in a_all]
    av = [_dot(bf(a[:, st:]), bf(stack(v[:, sl]))) for a, sl in zip(a_all, slices)]
    for lev in range(1, n_levels):
        for hg in tiles:
            if lev < n_levels - 1:
                both = _dot(bf(jnp.concatenate([p_acc[hg], q_pow[hg]], axis=0)), bf(q_pow[hg]))
                p_acc[hg] = p_acc[hg] + both[:st]
                q_pow[hg] = both[st:]
            else:
                p_acc[hg] = p_acc[hg] + _dot(bf(p_acc[hg]), bf(q_pow[hg]))
    u_st = [_dot(bf(p_acc[hg]), bf(x_state[hg][:st] + av[hg][:st])) for hg in tiles]
    y_st = [x_state[hg][st:] + av[hg][st:] + _dot(bf(a_all[hg][st:, :st]), bf(u_st[hg])) for hg in tiles]

    for hg, sl in enumerate(slices):
        u = collapse(u_st[hg])
        y = collapse(y_st[hg])
        uv = bf(jnp.concatenate([u, v[:, sl]], axis=0))
        bk_end = bf(jnp.concatenate([bt[:, sl], kt[:, sl]], axis=0) * etm[:, sl])
        s_ref[hg] = s_old[hg] * wtot[:, sl] + jnp.where(bd_mask, _dot_tn(uv, bk_end), 0.0)

        mean = seg_sum(y) * (1.0 / RWKV_HEAD)
        dlt = y - mean
        var = seg_sum(dlt * dlt) * (1.0 / RWKV_HEAD)
        yn = dlt * lax.rsqrt(var + GN_EPS) * lnw_ref[:, sl] + lnb_ref[:, sl]
        out = (yn + seg_sum(rkb[:, sl]) * v[:, sl]) * g_ref[:, sl].astype(F32)
        o_ref[:, sl] = out.astype(o_ref.dtype)


def _rwkv(r, k, v, kk, asig, lw, g, r_k, ln_w, ln_b, bsz, seqlen):
    n, d_r = r.shape
    t = RWKV_CHUNK
    nch = seqlen // t
    tile = HEADS_PER_TILE * RWKV_HEAD
    hid = jnp.arange(tile) // RWKV_HEAD
    ones_bd = (hid[:, None] == hid[None, :]).astype(BF16)
    row = lambda a: a.astype(F32).reshape(1, d_r)
    tspec = pl.BlockSpec((t, d_r), lambda b, c: (b * nch + c, 0))
    pspec = pl.BlockSpec((1, d_r), lambda b, c: (0, 0))
    return pl.pallas_call(
        _rwkv_kernel,
        grid=(bsz, nch),
        in_specs=[tspec] * 7 + [pspec] * 3 + [pl.BlockSpec((tile, tile), lambda b, c: (0, 0))],
        out_specs=tspec,
        out_shape=jax.ShapeDtypeStruct((n, d_r), BF16),
        scratch_shapes=[pltpu.VMEM((d_r // tile, tile, tile), F32)],
        compiler_params=_cparams("parallel", "arbitrary"),
        name="rwkv7_chunked",
    )(r, k, v, kk, asig, lw, g, row(r_k), row(ln_w), row(ln_b), ones_bd)


def _post_mix_kernel(ys_ref, yr_ref, wo1_ref, wo2_ref, x_ref, g1_ref, gt_ref, g2_ref, sc_ref, sh_ref,
                     wr_hi_ref, wr_lo_ref, br_ref, x1_out, h2_out, lg_out):
    mixed = _dot(ys_ref[...], wo1_ref[...]) + _dot(yr_ref[...], wo2_ref[...])
    x1 = x_ref[...] + gt_ref[0] * _rms(mixed, g1_ref[...])
    x1_out[...] = x1
    h2 = _rms(x1, g2_ref[...]) * (1.0 + sc_ref[0]) + sh_ref[0]
    _store_packed(h2_out, h2, h2.shape[0])
    hi, lo = _split_bf16(h2)
    lg_out[...] = (_dot(hi, wr_hi_ref[...]) + _dot(lo, wr_hi_ref[...]) + _dot(hi, wr_lo_ref[...])
                   + br_ref[...])


def _post_mix(ys, yr, w_out, x2, g1, gt1, g2, sc2, sh2, w_route, b_route, seqlen, tm):
    n, d = x2.shape
    ds = ys.shape[1]
    tpb = seqlen // tm
    slabs = d // (2 * LANES)
    wo = w_out.astype(BF16)
    wr_hi, wr_lo = _split_bf16(w_route)
    rows = lambda w: pl.BlockSpec((tm, w), lambda i: (i, 0))
    full = lambda a: pl.BlockSpec(a.shape, lambda i: (0,) * a.ndim)
    bat = pl.BlockSpec((1, 1, d), lambda i: (i // tpb, 0, 0))
    args = (ys, yr, wo[:ds], wo[ds:], x2, g1.reshape(1, d), gt1[:, None, :], g2.reshape(1, d),
            sc2[:, None, :], sh2[:, None, :], wr_hi, wr_lo, b_route.reshape(1, -1))
    in_specs = [rows(ds), rows(yr.shape[1]), full(args[2]), full(args[3]), rows(d), full(args[5]), bat,
                full(args[7]), bat, bat, full(wr_hi), full(wr_lo), full(args[12])]
    return pl.pallas_call(
        _post_mix_kernel,
        grid=(n // tm,),
        in_specs=in_specs,
        out_specs=[rows(d), pl.BlockSpec((tm * slabs, LANES), lambda i: (i, 0)), rows(ROUTE_LANES)],
        out_shape=[jax.ShapeDtypeStruct((n, d), F32), jax.ShapeDtypeStruct((n * slabs, LANES), jnp.uint32),
                   jax.ShapeDtypeStruct((n, ROUTE_LANES), F32)],
        compiler_params=_cparams("parallel"),
        name="out_proj_post",
    )(*args)


def _route_kernel(lg_ref, info_ref, cnt_ref, carry):
    i = pl.program_id(0)
    tm = lg_ref.shape[0]

    @pl.when(i == 0)
    def _():
        carry[...] = jnp.zeros_like(carry)

    lg = lg_ref[...]
    lane = lax.broadcasted_iota(jnp.int32, lg.shape, 1)
    lane_f = lane.astype(F32)
    neg = jnp.float32(-jnp.inf)
    big = jnp.float32(1e9)
    is_g = (lane >= N_EXPERTS) & (lane < N_EXPERTS + N_GROUPS)
    gl = jnp.where(is_g, lg, neg)
    gmax = jnp.max(gl, axis=-1, keepdims=True)
    gidx = jnp.min(jnp.where(gl == gmax, lane_f - N_EXPERTS, big), axis=-1, keepdims=True)
    p_grp = 1.0 / jnp.sum(jnp.where(is_g, jnp.exp(gl - gmax), 0.0), axis=-1, keepdims=True)
    in_grp = (lane < N_EXPERTS) & ((lane // EXPERTS_PER_GROUP).astype(F32) == gidx)
    el = jnp.where(in_grp, lg, neg)
    m1 = jnp.max(el, axis=-1, keepdims=True)
    i1 = jnp.min(jnp.where(el == m1, lane_f, big), axis=-1, keepdims=True)
    el2 = jnp.where(lane_f == i1, neg, el)
    m2 = jnp.max(el2, axis=-1, keepdims=True)
    i2 = jnp.min(jnp.where(el2 == m2, lane_f, big), axis=-1, keepdims=True)
    ex = jnp.exp(m2 - m1)
    w1 = p_grp / (1.0 + ex)
    w2 = p_grp * ex / (1.0 + ex)

    oh1 = lane_f == i1
    oh2 = lane_f == i2
    onehot = (oh1 | oh2).astype(BF16)
    rr = lax.broadcasted_iota(jnp.int32, (tm, tm), 0)
    cc = lax.broadcasted_iota(jnp.int32, (tm, tm), 1)
    before = _dot((rr > cc).astype(BF16), onehot) + carry[...]
    rank1 = jnp.sum(jnp.where(oh1, before, 0.0), axis=-1, keepdims=True)
    rank2 = jnp.sum(jnp.where(oh2, before, 0.0), axis=-1, keepdims=True)
    carry[...] = carry[...] + jnp.sum(onehot.astype(F32), axis=0, keepdims=True)
    cnt_ref[...] = carry[...]

    info = jnp.where(lane == 0, i1, 0.0)
    info = jnp.where(lane == 1, i2, info)
    info = jnp.where(lane == 2, w1, info)
    info = jnp.where(lane == 3, w2, info)
    info = jnp.where(lane == 4, rank1, info)
    info = jnp.where(lane == 5, rank2, info)
    info_ref[...] = info


def _route(logits, tm):
    n = logits.shape[0]
    return pl.pallas_call(
        _route_kernel,
        grid=(n // tm,),
        in_specs=[pl.BlockSpec((tm, ROUTE_LANES), lambda i: (i, 0))],
        out_specs=[pl.BlockSpec((tm, ROUTE_LANES), lambda i: (i, 0)),
                   pl.BlockSpec((1, ROUTE_LANES), lambda i: (0, 0))],
        out_shape=[jax.ShapeDtypeStruct((n, ROUTE_LANES), F32),
                   jax.ShapeDtypeStruct((1, ROUTE_LANES), F32)],
        scratch_shapes=[pltpu.VMEM((1, ROUTE_LANES), F32)],
        compiler_params=_cparams("arbitrary"),
        name="moe_route",
    )(logits)


def _dispatch_kernel(dest_ref, h_ref, buf_in, buf_out, sem, *, slabs):
    del buf_in
    tm = h_ref.shape[0] // slabs

    def copy(s):
        src = pl.multiple_of((s >> 1) * slabs, slabs)
        dst = pl.multiple_of(dest_ref[s], slabs)
        return pltpu.make_async_copy(h_ref.at[pl.ds(src, slabs)], buf_out.at[pl.ds(dst, slabs)], sem)

    def start(s, c):
        copy(s).start()
        return c

    def wait(s, c):
        copy(s).wait()
        return c

    lax.fori_loop(0, 2 * tm, start, 0)
    lax.fori_loop(0, 2 * tm, wait, 0)


def _dispatch(h2p, dest_rows, cap, slabs, tm):
    n = h2p.shape[0] // slabs
    buf = jnp.zeros((cap * slabs, LANES), h2p.dtype)
    return pl.pallas_call(
        functools.partial(_dispatch_kernel, slabs=slabs),
        grid=(n // tm,),
        in_specs=[pl.BlockSpec((2 * tm,), lambda i: (i,), memory_space=pltpu.SMEM),
                  pl.BlockSpec((tm * slabs, LANES), lambda i: (i, 0)),
                  pl.BlockSpec(memory_space=pl.ANY)],
        out_specs=pl.BlockSpec(memory_space=pl.ANY),
        out_shape=jax.ShapeDtypeStruct(buf.shape, buf.dtype),
        scratch_shapes=[pltpu.SemaphoreType.DMA],
        input_output_aliases={2: 0},
        compiler_params=_cparams("arbitrary"),
        name="moe_dispatch",
    )(dest_rows, h2p, buf)


def _moe_kernel(be_ref, na_ref, x_ref, w1_ref, w3_ref, w2_ref, o_ref, w1b, w3b, w2b):
    j = pl.program_id(0)
    active = j < na_ref[0]
    prev = be_ref[jnp.maximum(j - 1, 0)]
    fresh = (j == 0) | (be_ref[j] != prev)

    slabs = w1b.shape[0]
    half = slabs * LANES

    @pl.when(active & fresh)
    def _():
        for s in range(slabs):
            for dst, src in ((w1b, w1_ref), (w3b, w3_ref)):
                dst[s, :LANES, :] = src[0, s * LANES:(s + 1) * LANES, :].astype(BF16)
                dst[s, LANES:, :] = src[0, half + s * LANES:half + (s + 1) * LANES, :].astype(BF16)
        w2b[...] = w2_ref[0].astype(BF16)

    @pl.when(active)
    def _():
        acc1 = jnp.zeros((EXPERT_BLOCK, w1b.shape[2]), F32)
        acc3 = jnp.zeros((EXPERT_BLOCK, w1b.shape[2]), F32)
        for s, (lo, hi) in enumerate(_load_packed(x_ref, EXPERT_BLOCK, slabs)):
            lhs = jnp.concatenate([lo.astype(BF16), hi.astype(BF16)], axis=1)
            acc1 = acc1 + _dot(lhs, w1b[s])
            acc3 = acc3 + _dot(lhs, w3b[s])
        hid = (acc1 * _sigmoid(acc1)) * acc3
        _store_packed(o_ref, _dot(hid.astype(BF16), w2b[...]), EXPERT_BLOCK)


def _moe(x_buf, block_e, n_active, w1, w3, w2, slabs):
    cap = x_buf.shape[0] // slabs
    d, de = w1.shape[1], w1.shape[2]
    nb = cap // EXPERT_BLOCK

    def xmap(j, be, na):
        return (jnp.minimum(j, na[0] - 1), 0)

    def wmap(j, be, na):
        return (be[jnp.minimum(j, na[0] - 1)], 0, 0)

    xspec = pl.BlockSpec((EXPERT_BLOCK * slabs, LANES), xmap)
    grid_spec = pltpu.PrefetchScalarGridSpec(
        num_scalar_prefetch=2,
        grid=(nb,),
        in_specs=[xspec,
                  pl.BlockSpec((1, d, de), wmap),
                  pl.BlockSpec((1, d, de), wmap),
                  pl.BlockSpec((1, de, d), wmap)],
        out_specs=xspec,
        scratch_shapes=[pltpu.VMEM((slabs, 2 * LANES, de), BF16), pltpu.VMEM((slabs, 2 * LANES, de), BF16),
                        pltpu.VMEM((de, d), BF16)],
    )
    return pl.pallas_call(
        _moe_kernel,
        grid_spec=grid_spec,
        out_shape=jax.ShapeDtypeStruct(x_buf.shape, x_buf.dtype),
        input_output_aliases={2: 0},
        compiler_params=_cparams("arbitrary"),
        name="moe_experts",
    )(block_e, n_active, x_buf, w1, w3, w2)


def _combine_kernel(dest_ref, y_ref, info_ref, x1_ref, g_ref, gt_ref, o_ref, rows0, rows1, sem, *, slabs):
    tm = x1_ref.shape[0]

    def copy(s, buf):
        src = pl.multiple_of(dest_ref[s], slabs)
        dst = pl.multiple_of((s >> 1) * slabs, slabs)
        return pltpu.make_async_copy(y_ref.at[pl.ds(src, slabs)], buf.at[pl.ds(dst, slabs)], sem)

    def start(t, c):
        copy(2 * t, rows0).start()
        copy(2 * t + 1, rows1).start()
        return c

    def wait(t, c):
        copy(2 * t, rows0).wait()
        copy(2 * t + 1, rows1).wait()
        return c

    lax.fori_loop(0, tm, start, 0)
    lax.fori_loop(0, tm, wait, 0)
    info = info_ref[...]
    w1 = info[:, 2:3]
    w2 = info[:, 3:4]
    lo_parts, hi_parts = [], []
    for (lo0, hi0), (lo1, hi1) in zip(_load_packed(rows0, tm, slabs), _load_packed(rows1, tm, slabs)):
        lo_parts.append(w1 * lo0 + w2 * lo1)
        hi_parts.append(w1 * hi0 + w2 * hi1)
    ffn = jnp.concatenate(lo_parts + hi_parts, axis=1)
    o_ref[...] = x1_ref[...] + gt_ref[0] * _rms(ffn, g_ref[...])


def _combine(y_buf, dest_rows, info, x1, g, gt2, seqlen, slabs, tm):
    n, d = x1.shape
    tpb = seqlen // tm
    return pl.pallas_call(
        functools.partial(_combine_kernel, slabs=slabs),
        grid=(n // tm,),
        in_specs=[pl.BlockSpec((2 * tm,), lambda i: (i,), memory_space=pltpu.SMEM),
                  pl.BlockSpec(memory_space=pl.ANY),
                  pl.BlockSpec((tm, ROUTE_LANES), lambda i: (i, 0)),
                  pl.BlockSpec((tm, d), lambda i: (i, 0)),
                  pl.BlockSpec((1, d), lambda i: (0, 0)),
                  pl.BlockSpec((1, 1, d), lambda i: (i // tpb, 0, 0))],
        out_specs=pl.BlockSpec((tm, d), lambda i: (i, 0)),
        out_shape=jax.ShapeDtypeStruct((n, d), F32),
        scratch_shapes=[pltpu.VMEM((tm * slabs, LANES), y_buf.dtype),
                        pltpu.VMEM((tm * slabs, LANES), y_buf.dtype), pltpu.SemaphoreType.DMA],
        compiler_params=_cparams("arbitrary"),
        name="moe_combine",
    )(dest_rows, y_buf, info, x1, g.reshape(1, d), gt2[:, None, :])


def _pick(n, pref):
    while n % pref:
        pref //= 2
    return pref


def _layer(x2, mod, p, bsz, seqlen):
    n, d = x2.shape
    sh1, sc1, gt1, sh2, sc2, gt2 = jnp.split(mod, 6, axis=-1)
    d_ssm = p["ssm_d"].shape[0]
    d_r = p["rwkv_w0"].shape[0]
    tm = _pick(seqlen, 1024)

    h1 = _norm_mod(x2, p["norm_mix_pre"], sc1, sh1, seqlen, _pick(seqlen, 512))
    w_in = p["w_in"].astype(BF16)
    w_z = w_in[:, d_ssm:]
    dz = w_z.shape[1]
    u = _matmul(h1, w_in[:, :d_ssm], tm, d_ssm, BF16, "in_proj_u")
    z = _matmul(h1, w_z, tm, dz // 2 if (dz // 2) % 128 == 0 else dz, BF16, "in_proj_z")

    tabs = _s5_tables(p["ssm_lam_re"], p["ssm_lam_im"], p["ssm_log_dt"], p["ssm_b_re"], p["ssm_b_im"],
                      p["ssm_c_re"], p["ssm_c_im"], p["ssm_d"], SSM_CHUNK)
    y_ssm = _glu(_s5(u, tabs, bsz, seqlen), p["glu_w"].astype(BF16), p["glu_b"].astype(F32),
                 _pick(seqlen, 512))

    r, k, v, kk, asig, lw, g = _rwkv_prep(z, p["rwkv_mu"], p["rwkv_w0"], p["rwkv_w_up"], p["rwkv_a0"],
                                          p["rwkv_a_up"], p["rwkv_g_up"], p["rwkv_k_k"], p["rwkv_k_a"],
                                          seqlen, _pick(seqlen, 256))
    y_rwkv = _rwkv(r, k, v, kk, asig, lw, g, p["rwkv_r_k"], p["rwkv_ln_w"], p["rwkv_ln_b"], bsz, seqlen)

    w_route = jnp.zeros((d, ROUTE_LANES), F32)
    w_route = w_route.at[:, :N_EXPERTS].set(p["moe_w_exp"].astype(F32))
    w_route = w_route.at[:, N_EXPERTS:N_EXPERTS + N_GROUPS].set(p["moe_w_grp"].astype(F32))
    b_route = jnp.zeros((ROUTE_LANES,), F32)
    b_route = b_route.at[:N_EXPERTS].set(p["moe_b_exp"].astype(F32))
    b_route = b_route.at[N_EXPERTS:N_EXPERTS + N_GROUPS].set(p["moe_b_grp"].astype(F32))
    x1, h2p, logits = _post_mix(y_ssm, y_rwkv, p["w_out"], x2, p["norm_mix_post"], gt1, p["norm_ffn_pre"],
                                sc2, sh2, w_route, b_route, seqlen, _pick(seqlen, 256))
    slabs = d // (2 * LANES)

    info, counts = _route(logits, _pick(n, 512))
    cnt = counts[0, :N_EXPERTS].astype(jnp.int32)
    padded = (cnt + EXPERT_BLOCK - 1) // EXPERT_BLOCK * EXPERT_BLOCK
    pend = jnp.cumsum(padded)
    pstart = pend - padded
    n_blocks = -(-(2 * n) // EXPERT_BLOCK) + N_EXPERTS
    cap = n_blocks * EXPERT_BLOCK
    e12 = info[:, 0:2].astype(jnp.int32)
    dest_rows = (pstart[e12] + info[:, 4:6].astype(jnp.int32)).reshape(2 * n) * slabs
    block_e = jnp.minimum(
        jnp.searchsorted(pend, jnp.arange(n_blocks, dtype=jnp.int32) * EXPERT_BLOCK, side="right"),
        N_EXPERTS - 1).astype(jnp.int32)
    n_active = (pend[-1:] // EXPERT_BLOCK).astype(jnp.int32)

    tdm = _pick(n, 512)
    x_buf = _dispatch(h2p, dest_rows, cap, slabs, tdm)
    y_buf = _moe(x_buf, block_e, n_active, p["moe_w1"], p["moe_w3"], p["moe_w2"], slabs)
    return _combine(y_buf, dest_rows, info, x1, p["norm_ffn_post"], gt2, seqlen, slabs, _pick(seqlen, 512))


def kernel(x, c, ada_w, ada_b, norm_mix_pre, norm_mix_post, norm_ffn_pre, norm_ffn_post, w_in, w_out, ssm_lam_re, ssm_lam_im, ssm_log_dt, ssm_b_re, ssm_b_im, ssm_c_re, ssm_c_im, ssm_d, glu_w, glu_b, rwkv_mu, rwkv_w0, rwkv_w_up, rwkv_a0, rwkv_a_up, rwkv_g_up, rwkv_k_k, rwkv_k_a, rwkv_r_k, rwkv_ln_w, rwkv_ln_b, moe_w_grp, moe_b_grp, moe_w_exp, moe_b_exp, moe_w1, moe_w3, moe_w2):
    bsz, seqlen, d = x.shape
    params = dict(norm_mix_pre=norm_mix_pre, norm_mix_post=norm_mix_post, norm_ffn_pre=norm_ffn_pre,
                  norm_ffn_post=norm_ffn_post, w_in=w_in, w_out=w_out, ssm_lam_re=ssm_lam_re,
                  ssm_lam_im=ssm_lam_im, ssm_log_dt=ssm_log_dt, ssm_b_re=ssm_b_re, ssm_b_im=ssm_b_im,
                  ssm_c_re=ssm_c_re, ssm_c_im=ssm_c_im, ssm_d=ssm_d, glu_w=glu_w, glu_b=glu_b,
                  rwkv_mu=rwkv_mu, rwkv_w0=rwkv_w0, rwkv_w_up=rwkv_w_up, rwkv_a0=rwkv_a0,
                  rwkv_a_up=rwkv_a_up, rwkv_g_up=rwkv_g_up, rwkv_k_k=rwkv_k_k, rwkv_k_a=rwkv_k_a,
                  rwkv_r_k=rwkv_r_k, rwkv_ln_w=rwkv_ln_w, rwkv_ln_b=rwkv_ln_b, moe_w_grp=moe_w_grp,
                  moe_b_grp=moe_b_grp, moe_w_exp=moe_w_exp, moe_b_exp=moe_b_exp, moe_w1=moe_w1,
                  moe_w3=moe_w3, moe_w2=moe_w2)
    x2 = x.reshape(bsz * seqlen, d)
    for layer in range(ada_w.shape[0]):
        mod = _ada(c, ada_w[layer], ada_b[layer])
        x2 = _layer(x2, mod, {k: v[layer] for k, v in params.items()}, bsz, seqlen)
    return x2.reshape(bsz, seqlen, d)
```

```python
import functools
import math

import jax
import jax.numpy as jnp
from jax import lax
from jax.experimental import pallas as pl
from jax.experimental.pallas import tpu as pltpu

F32 = jnp.float32
BF16 = jnp.bfloat16

SSM_GROUP = 16
SSM_STATE = 64
S5_SLAB = 256
S5_TIME_BLOCK = 64
RWKV_HEAD = 64
RWKV_CHUNK = 64
HEADS_PER_TILE = 4
LORA_W = 64
LORA_A = 64
LORA_G = 128
N_GROUPS = 8
EXPERTS_PER_GROUP = 8
N_EXPERTS = N_GROUPS * EXPERTS_PER_GROUP
EXPERT_BLOCK = 256
RMS_EPS = 1e-6
GN_EPS = 64e-5
DECAY_SCALE = math.exp(-0.5)
LANES = 128
ROUTE_LANES = LANES
VMEM_LIMIT = 52 * 1024 * 1024


def _cparams(*sem):
    return pltpu.CompilerParams(dimension_semantics=sem, vmem_limit_bytes=VMEM_LIMIT)


def _sigmoid(x):
    return 1.0 / (1.0 + jnp.exp(-x))


def _dot(a, b):
    return jnp.dot(a, b, preferred_element_type=F32)


def _dot_nt(a, b):
    return lax.dot_general(a, b, (((1,), (1,)), ((), ())), preferred_element_type=F32)


def _dot_tn(a, b):
    return lax.dot_general(a, b, (((0,), (0,)), ((), ())), preferred_element_type=F32)


def _split_bf16(x):
    hi = x.astype(BF16)
    lo = (x - hi.astype(F32)).astype(BF16)
    return hi, lo


def _pack_pair(a, b):
    ua = lax.bitcast_convert_type(a.astype(BF16).astype(F32), jnp.uint32)
    ub = lax.bitcast_convert_type(b.astype(BF16).astype(F32), jnp.uint32)
    return ub | (ua >> 16)


def _unpack_pair(w):
    lo = lax.bitcast_convert_type(w << 16, F32)
    hi = lax.bitcast_convert_type(w & jnp.uint32(0xFFFF0000), F32)
    return lo, hi


def _store_packed(ref, val, n_rows):
    d = val.shape[1]
    slabs = d // (2 * LANES)
    for s in range(slabs):
        a = val[:, s * LANES:(s + 1) * LANES]
        b = val[:, d // 2 + s * LANES:d // 2 + (s + 1) * LANES]
        ref[pl.ds(s, n_rows, stride=slabs), :] = _pack_pair(a, b)


def _load_packed(ref, n_rows, slabs):
    return [_unpack_pair(ref[pl.ds(s, n_rows, stride=slabs), :]) for s in range(slabs)]


def _ada_kernel(c_ref, w_ref, b_ref, o_ref):
    c = c_ref[...]
    cond = c * _sigmoid(c)
    o_ref[...] = jnp.dot(cond, w_ref[...], preferred_element_type=F32,
                         precision=lax.Precision.HIGHEST) + b_ref[...]


def _ada(c, ada_w, ada_b):
    bsz, d = c.shape
    n = ada_w.shape[1]
    tn = 1024
    return pl.pallas_call(
        _ada_kernel,
        grid=(n // tn,),
        in_specs=[pl.BlockSpec((bsz, d), lambda j: (0, 0)),
                  pl.BlockSpec((d, tn), lambda j: (0, j)),
                  pl.BlockSpec((1, tn), lambda j: (0, j))],
        out_specs=pl.BlockSpec((bsz, tn), lambda j: (0, j)),
        out_shape=jax.ShapeDtypeStruct((bsz, n), F32),
        compiler_params=_cparams("arbitrary"),
        name="ada_mod",
    )(c, ada_w, ada_b.reshape(1, n))


def _rms(x, g):
    return x * lax.rsqrt(jnp.mean(x * x, axis=-1, keepdims=True) + RMS_EPS) * g


def _norm_mod_kernel(x_ref, g_ref, sc_ref, sh_ref, o_ref):
    x = x_ref[...]
    h = _rms(x, g_ref[...]) * (1.0 + sc_ref[0]) + sh_ref[0]
    o_ref[...] = h.astype(o_ref.dtype)


def _norm_mod(x2, g, sc, sh, seqlen, tm):
    n, d = x2.shape
    tpb = seqlen // tm
    return pl.pallas_call(
        _norm_mod_kernel,
        grid=(n // tm,),
        in_specs=[pl.BlockSpec((tm, d), lambda i: (i, 0)),
                  pl.BlockSpec((1, d), lambda i: (0, 0)),
                  pl.BlockSpec((1, 1, d), lambda i: (i // tpb, 0, 0)),
                  pl.BlockSpec((1, 1, d), lambda i: (i // tpb, 0, 0))],
        out_specs=pl.BlockSpec((tm, d), lambda i: (i, 0)),
        out_shape=jax.ShapeDtypeStruct((n, d), BF16),
        compiler_params=_cparams("parallel"),
        name="norm_mod",
    )(x2, g.reshape(1, d), sc[:, None, :], sh[:, None, :])


def _mm_kernel(a_ref, w_ref, o_ref):
    o_ref[...] = _dot(a_ref[...], w_ref[...]).astype(o_ref.dtype)


def _matmul(a, w, tm, tn, out_dtype, name, time_major_batches=None):
    m, k = a.shape
    n = w.shape[1]
    if time_major_batches is None:
        out_shape, out_map = (m, n), (lambda i, j: (i, j))
    else:
        tiles_per_seq = m // time_major_batches // tm
        n_col = n // tn
        out_shape = (m // time_major_batches, time_major_batches * n)
        out_map = lambda i, j: (i % tiles_per_seq, (i // tiles_per_seq) * n_col + j)
    return pl.pallas_call(
        _mm_kernel,
        grid=(m // tm, n // tn),
        in_specs=[pl.BlockSpec((tm, k), lambda i, j: (i, 0)),
                  pl.BlockSpec((k, tn), lambda i, j: (0, j))],
        out_specs=pl.BlockSpec((tm, tn), out_map),
        out_shape=jax.ShapeDtypeStruct(out_shape, out_dtype),
        compiler_params=_cparams("parallel", "arbitrary"),
        name=name,
    )(a, w)


def _s5_tables(lam_re, lam_im, log_dt, b_re, b_im, c_re, c_im):
    g, p, cg = b_re.shape
    gs = S5_SLAB // cg
    ns = g // gs
    lr = jnp.minimum(lam_re.astype(F32), -1e-4)
    li = lam_im.astype(F32)
    dt = jnp.exp(log_dt.astype(F32))[:, None]
    mag = jnp.exp(lr * dt)
    ar, ai = mag * jnp.cos(li * dt), mag * jnp.sin(li * dt)
    den = lr * lr + li * li
    qr = ((ar - 1.0) * lr + ai * li) / den
    qi = (ai * lr - (ar - 1.0) * li) / den
    br, bi = b_re.astype(F32), b_im.astype(F32)
    bbr = qr[..., None] * br - qi[..., None] * bi
    bbi = qr[..., None] * bi + qi[..., None] * br
    eye = jnp.eye(gs, dtype=F32)

    def in_bd(t):
        t = t.reshape(ns, gs, p, cg).transpose(0, 1, 3, 2)
        return (t[:, :, :, None, :] * eye[None, :, None, :, None]).reshape(ns, gs * cg, gs * p)

    def out_bd(t):
        t = t.reshape(ns, gs, cg, p).transpose(0, 1, 3, 2)
        return (t[:, :, :, None, :] * eye[None, :, None, :, None]).reshape(ns, gs * p, gs * cg)

    b_bd = jnp.concatenate([in_bd(bbr), in_bd(bbi)], axis=2).astype(BF16)
    c_bd = jnp.concatenate([out_bd(c_re.astype(F32)), out_bd(-c_im.astype(F32))], axis=1).astype(BF16)
    a_tab = jnp.stack([ar.reshape(ns, gs * p), ai.reshape(ns, gs * p)], axis=1)
    return b_bd, c_bd, a_tab


def _gelu_tanh(x):
    return 0.5 * x * (1.0 + jnp.tanh(math.sqrt(2.0 / math.pi) * (x + 0.044715 * (x * x * x))))


def _s5_kernel(u_ref, b_ref, c_ref, a_ref, d_ref, gw_ref, gb_ref, o_ref, bscr, sscr, yscr, st_ref):
    bsz = st_ref.shape[1]
    rows, dch = u_ref.shape
    half = st_ref.shape[2] // 2

    @pl.when(pl.program_id(0) == 0)
    def _():
        st_ref[...] = jnp.zeros_like(st_ref)

    for s in range(dch // S5_SLAB):
        sl = slice(s * S5_SLAB, (s + 1) * S5_SLAB)
        us = u_ref[:, sl]
        bscr[...] = _dot(us, b_ref[s])
        a_re = a_ref[s, 0:1, :]
        a_im = a_ref[s, 1:2, :]

        def body(l, carry):
            s_r, s_i = carry
            at_l = pl.ds(pl.multiple_of(l * bsz, bsz), bsz)
            n_r = a_re * s_r - a_im * s_i + bscr[at_l, :half]
            n_i = a_re * s_i + a_im * s_r + bscr[at_l, half:]
            sscr[at_l, :half] = n_r
            sscr[at_l, half:] = n_i
            return n_r, n_i

        s_r, s_i = lax.fori_loop(0, rows // bsz, body, (st_ref[s, :, :half], st_ref[s, :, half:]), unroll=2)
        st_ref[s, :, :half] = s_r
        st_ref[s, :, half:] = s_i
        y = _dot(sscr[...].astype(BF16), c_ref[s]) + d_ref[:, sl] * us.astype(F32)
        yscr[:, sl] = _gelu_tanh(y).astype(BF16)
    y = yscr[...]
    gate = _sigmoid(_dot(y, gw_ref[...]) + gb_ref[...])
    o_ref[...] = (y.astype(F32) * gate).astype(o_ref.dtype)


def _s5_glu(u_tm, tabs, d_skip, glu_w, glu_b, bsz, seqlen, lb):
    b_bd, c_bd, a_tab = tabs
    dch = u_tm.shape[1] // bsz
    ns, _, n_state = b_bd.shape
    full = lambda a: pl.BlockSpec(a.shape, lambda i: (0,) * a.ndim)
    blk = pl.BlockSpec((lb * bsz, dch), lambda i: (i, 0))
    args = (u_tm.reshape(seqlen * bsz, dch), b_bd, c_bd, a_tab, d_skip.astype(F32).reshape(1, dch),
            glu_w.astype(BF16), glu_b.astype(F32).reshape(1, dch))
    y = pl.pallas_call(
        _s5_kernel,
        grid=(seqlen // lb,),
        in_specs=[blk] + [full(a) for a in args[1:]],
        out_specs=blk,
        out_shape=jax.ShapeDtypeStruct((seqlen * bsz, dch), BF16),
        scratch_shapes=[pltpu.VMEM((lb * bsz, n_state), F32), pltpu.VMEM((lb * bsz, n_state), F32),
                        pltpu.VMEM((lb * bsz, dch), BF16), pltpu.VMEM((ns, bsz, n_state), F32)],
        compiler_params=_cparams("arbitrary"),
        name="s5_mixer_glu",
    )(*args)
    return y.reshape(seqlen, bsz * dch)


def _rwkv_prep_kernel(z_ref, halo_ref, mu_ref, w0_ref, a0_ref, kk_ref, ka_ref, wa_ref, gup_ref,
                      r_out, k_out, v_out, kk_out, as_out, lw_out, g_out, *, tiles_per_seq, d_r):
    i = pl.program_id(0)
    tm = z_ref.shape[0]
    not_first = (i % tiles_per_seq != 0).astype(F32)
    row0 = lax.broadcasted_iota(jnp.int32, (tm, 1), 0) == 0

    def lerp(lo, hi):
        zc = z_ref[:, lo:hi].astype(F32)
        prev_row = halo_ref[7:8, lo:hi].astype(F32) * not_first
        shifted = jnp.where(row0, prev_row, pltpu.roll(zc, 1, 0))
        return zc + mu_ref[:, lo:hi] * (shifted - zc)

    r_out[...] = lerp(0, d_r).astype(r_out.dtype)
    v_out[...] = lerp(2 * d_r, 3 * d_r).astype(v_out.dtype)
    xwa = lerp(3 * d_r, 3 * d_r + LORA_W + LORA_A)
    lane = lax.broadcasted_iota(jnp.int32, xwa.shape, 1)
    lhs = jnp.where(lane < LORA_W, jnp.tanh(xwa), xwa).astype(BF16)
    wa = _dot(lhs, wa_ref[...])
    lw_out[...] = -DECAY_SCALE * _sigmoid(w0_ref[...] + wa[:, :d_r])
    asig = _sigmoid(a0_ref[...] + wa[:, d_r:])
    as_out[...] = asig.astype(as_out.dtype)
    k = lerp(d_r, 2 * d_r)
    kk_out[...] = (k * kk_ref[...]).astype(kk_out.dtype)
    k_out[...] = (k * (1.0 + (asig - 1.0) * ka_ref[...])).astype(k_out.dtype)
    xg = lerp(3 * d_r + LORA_W + LORA_A, 3 * d_r + LORA_W + LORA_A + LORA_G)
    g_out[...] = _dot(_sigmoid(xg).astype(BF16), gup_ref[...]).astype(g_out.dtype)


def _rwkv_prep(z, mu, w0, w_up, a0, a_up, g_up, k_k, k_a, seqlen, tm):
    n, dz = z.shape
    d_r = w0.shape[0]
    wa = jnp.zeros((LORA_W + LORA_A, 2 * d_r), F32)
    wa = wa.at[:LORA_W, :d_r].set(w_up.astype(F32)).at[LORA_W:, d_r:].set(a_up.astype(F32)).astype(BF16)
    row = lambda a: a.astype(F32).reshape(1, -1)
    full = lambda a: pl.BlockSpec(a.shape, lambda i: (0, 0))
    hb = tm // 8
    args = (z, z, row(mu), row(w0), row(a0), row(k_k), row(k_a), wa, g_up.astype(BF16))
    in_specs = [pl.BlockSpec((tm, dz), lambda i: (i, 0)),
                pl.BlockSpec((8, dz), lambda i: (jnp.maximum(i * hb - 1, 0), 0))]
    in_specs += [full(a) for a in args[2:]]
    ospec = pl.BlockSpec((tm, d_r), lambda i: (i, 0))
    bf = jax.ShapeDtypeStruct((n, d_r), BF16)
    return pl.pallas_call(
        functools.partial(_rwkv_prep_kernel, tiles_per_seq=seqlen // tm, d_r=d_r),
        grid=(n // tm,),
        in_specs=in_specs,
        out_specs=[ospec] * 7,
        out_shape=[bf, bf, bf, bf, bf, jax.ShapeDtypeStruct((n, d_r), F32), bf],
        compiler_params=_cparams("parallel"),
        name="rwkv_prep",
    )(*args)


def _rwkv_kernel(r_ref, k_ref, v_ref, kk_ref, as_ref, lw_ref, g_ref, rk_ref, lnw_ref, lnb_ref,
                 ones_ref, o_ref, s_ref):
    t = r_ref.shape[0]
    d_r = r_ref.shape[1]
    tile = HEADS_PER_TILE * RWKV_HEAD
    n_tiles = d_r // tile

    @pl.when(pl.program_id(1) == 0)
    def _():
        s_ref[...] = jnp.zeros_like(s_ref)

    ones_bd = ones_ref[...]

    def seg_sum(x):
        hi, lo = _split_bf16(x)
        return _dot(hi, ones_bd) + _dot(lo, ones_bd)

    row = lax.broadcasted_iota(jnp.int32, (t, t), 0)
    col = lax.broadcasted_iota(jnp.int32, (t, t), 1)
    tri = (row >= col).astype(BF16)
    eye = (row == col).astype(F32)
    st = HEADS_PER_TILE * t
    rs = lax.broadcasted_iota(jnp.int32, (2 * st, 2 * st), 0)
    cs = lax.broadcasted_iota(jnp.int32, (2 * st, 2 * st), 1)
    t_r = rs % t
    t_c = cs % t
    keep = (t_r > t_c) | ((rs >= st) & (t_r == t_c))
    eye_st = (lax.broadcasted_iota(jnp.int32, (st, st), 0)
              == lax.broadcasted_iota(jnp.int32, (st, st), 1)).astype(F32)
    lane = lax.broadcasted_iota(jnp.int32, (1, tile), 1)
    head_masks = [(lane >= j * RWKV_HEAD) & (lane < (j + 1) * RWKV_HEAD) for j in range(HEADS_PER_TILE)]
    bd_r = lax.broadcasted_iota(jnp.int32, (tile, tile), 0) // RWKV_HEAD
    bd_c = lax.broadcasted_iota(jnp.int32, (tile, tile), 1) // RWKV_HEAD
    bd_mask = bd_r == bd_c
    n_levels = int(math.log2(t))
    tiles = range(n_tiles)
    slices = [slice(hg * tile, (hg + 1) * tile) for hg in tiles]

    def stack(x):
        return jnp.concatenate([jnp.where(m, x, 0.0) for m in head_masks], axis=0)

    def collapse(x):
        out = x[:t]
        for j in range(1, HEADS_PER_TILE):
            out = out + x[j * t:(j + 1) * t]
        return out

    def bf(x):
        return x.astype(BF16)

    lw = lw_ref[...]
    lw_hi, lw_lo = _split_bf16(lw)
    cum = _dot(tri, lw_hi) + _dot(tri, lw_lo)
    mid = cum[t // 2 - 1:t // 2, :]
    tot = cum[t - 1:t, :]
    e1 = jnp.exp(cum - mid)
    e2 = jnp.exp(mid - cum)
    e1p = e1 * jnp.exp(-lw)
    em = jnp.exp(mid)
    etm = jnp.exp(tot - mid)
    wtot = jnp.exp(tot)
    r = r_ref[...].astype(F32)
    kp = k_ref[...].astype(F32)
    v = v_ref[...].astype(F32)
    kk = kk_ref[...].astype(F32)
    asig = as_ref[...].astype(F32)
    kk2 = kk * kk
    kkn = kk / jnp.maximum(jnp.sqrt(jnp.concatenate([seg_sum(kk2[:, sl]) for sl in slices], axis=1)), 1e-12)
    qt = r * e1
    kt = kp * e2
    at = -kkn * e1p
    bt = kkn * asig * e2
    rkb = r * kp * rk_ref[...]

    s_old = [s_ref[hg] for hg in tiles]
    a_all = []
    for sl in slices:
        lhs = jnp.concatenate([stack(at[:, sl]), stack(qt[:, sl])], axis=0)
        rhs = jnp.concatenate([stack(bt[:, sl]), stack(kt[:, sl])], axis=0)
        a_all.append(jnp.where(keep, _dot_nt(bf(lhs), bf(rhs)), 0.0))
    x_state = []
    for hg, sl in enumerate(slices):
        lhs = jnp.concatenate([stack(at[:, sl] * em[:, sl]), stack(qt[:, sl] * em[:, sl])], axis=0)
        x_state.append(_dot_nt(bf(lhs), bf(s_old[hg])))
    p_acc = [eye_st + a[:st, :st] for a in a_all]
    q_pow = [_dot(bf(a[:st, :st]), bf(a[:st, :st])) for a in a_all]
    av = [_dot(bf(a[:, st:]), bf(stack(v[:, sl]))) for a, sl in zip(a_all, slices)]
    for lev in range(1, n_levels):
        for hg in tiles:
            if lev < n_levels - 1:
                both = _dot(bf(jnp.concatenate([p_acc[hg], q_pow[hg]], axis=0)), bf(q_pow[hg]))
                p_acc[hg] = p_acc[hg] + both[:st]
                q_pow[hg] = both[st:]
            else:
                p_acc[hg] = p_acc[hg] + _dot(bf(p_acc[hg]), bf(q_pow[hg]))
    u_st = [_dot(bf(p_acc[hg]), bf(x_state[hg][:st] + av[hg][:st])) for hg in tiles]
    y_st = [x_state[hg][st:] + av[hg][st:] + _dot(bf(a_all[hg][st:, :st]), bf(u_st[hg])) for hg in tiles]

    for hg, sl in enumerate(slices):
        u = collapse(u_st[hg])
        y = collapse(y_st[hg])
        uv = bf(jnp.concatenate([u, v[:, sl]], axis=0))
        bk_end = bf(jnp.concatenate([bt[:, sl], kt[:, sl]], axis=0) * etm[:, sl])
        s_ref[hg] = s_old[hg] * wtot[:, sl] + jnp.where(bd_mask, _dot_tn(uv, bk_end), 0.0)

        mean = seg_sum(y) * (1.0 / RWKV_HEAD)
        dlt = y - mean
        var = seg_sum(dlt * dlt) * (1.0 / RWKV_HEAD)
        yn = dlt * lax.rsqrt(var + GN_EPS) * lnw_ref[:, sl] + lnb_ref[:, sl]
        out = (yn + seg_sum(rkb[:, sl]) * v[:, sl]) * g_ref[:, sl].astype(F32)
        o_ref[:, sl] = out.astype(o_ref.dtype)


def _rwkv(r, k, v, kk, asig, lw, g, r_k, ln_w, ln_b, bsz, seqlen):
    n, d_r = r.shape
    t = RWKV_CHUNK
    nch = seqlen // t
    tile = HEADS_PER_TILE * RWKV_HEAD
    hid = jnp.arange(tile) // RWKV_HEAD
    ones_bd = (hid[:, None] == hid[None, :]).astype(BF16)
    row = lambda a: a.astype(F32).reshape(1, d_r)
    tspec = pl.BlockSpec((t, d_r), lambda b, c: (b * nch + c, 0))
    pspec = pl.BlockSpec((1, d_r), lambda b, c: (0, 0))
    return pl.pallas_call(
        _rwkv_kernel,
        grid=(bsz, nch),
        in_specs=[tspec] * 7 + [pspec] * 3 + [pl.BlockSpec((tile, tile), lambda b, c: (0, 0))],
        out_specs=tspec,
        out_shape=jax.ShapeDtypeStruct((n, d_r), BF16),
        scratch_shapes=[pltpu.VMEM((d_r // tile, tile, tile), F32)],
        compiler_params=_cparams("parallel", "arbitrary"),
        name="rwkv7_chunked",
    )(r, k, v, kk, asig, lw, g, row(r_k), row(ln_w), row(ln_b), ones_bd)


def _post_mix_kernel(ys_ref, yr_ref, wo1_ref, wo2_ref, x_ref, g1_ref, gt_ref, g2_ref, sc_ref, sh_ref,
                     wr_hi_ref, wr_lo_ref, br_ref, x1_out, h2_out, lg_out):
    mixed = _dot(ys_ref[...], wo1_ref[...]) + _dot(yr_ref[...], wo2_ref[...])
    x1 = x_ref[...] + gt_ref[0] * _rms(mixed, g1_ref[...])
    x1_out[...] = x1
    h2 = _rms(x1, g2_ref[...]) * (1.0 + sc_ref[0]) + sh_ref[0]
    _store_packed(h2_out, h2, h2.shape[0])
    hi, lo = _split_bf16(h2)
    lg_out[...] = (_dot(hi, wr_hi_ref[...]) + _dot(lo, wr_hi_ref[...]) + _dot(hi, wr_lo_ref[...])
                   + br_ref[...])


def _post_mix(ys, yr, w_out, x2, g1, gt1, g2, sc2, sh2, w_route, b_route, seqlen, tm):
    n, d = x2.shape
    ds = ys.shape[1] * seqlen // n
    tpb = seqlen // tm
    slabs = d // (2 * LANES)
    wo = w_out.astype(BF16)
    wr_hi, wr_lo = _split_bf16(w_route)
    rows = lambda w: pl.BlockSpec((tm, w), lambda i: (i, 0))
    full = lambda a: pl.BlockSpec(a.shape, lambda i: (0,) * a.ndim)
    bat = pl.BlockSpec((1, 1, d), lambda i: (i // tpb, 0, 0))
    args = (ys, yr, wo[:ds], wo[ds:], x2, g1.reshape(1, d), gt1[:, None, :], g2.reshape(1, d),
            sc2[:, None, :], sh2[:, None, :], wr_hi, wr_lo, b_route.reshape(1, -1))
    ys_spec = pl.BlockSpec((tm, ds), lambda i: (i % tpb, i // tpb))
    in_specs = [ys_spec, rows(yr.shape[1]), full(args[2]), full(args[3]), rows(d), full(args[5]), bat,
                full(args[7]), bat, bat, full(wr_hi), full(wr_lo), full(args[12])]
    return pl.pallas_call(
        _post_mix_kernel,
        grid=(n // tm,),
        in_specs=in_specs,
        out_specs=[rows(d), pl.BlockSpec((tm * slabs, LANES), lambda i: (i, 0)), rows(ROUTE_LANES)],
        out_shape=[jax.ShapeDtypeStruct((n, d), F32), jax.ShapeDtypeStruct((n * slabs, LANES), jnp.uint32),
                   jax.ShapeDtypeStruct((n, ROUTE_LANES), F32)],
        compiler_params=_cparams("parallel"),
        name="out_proj_post",
    )(*args)


def _route_kernel(lg_ref, info_ref, cnt_ref, carry):
    i = pl.program_id(0)
    tm = lg_ref.shape[0]

    @pl.when(i == 0)
    def _():
        carry[...] = jnp.zeros_like(carry)

    lg = lg_ref[...]
    lane = lax.broadcasted_iota(jnp.int32, lg.shape, 1)
    lane_f = lane.astype(F32)
    neg = jnp.float32(-jnp.inf)
    big = jnp.float32(1e9)
    is_g = (lane >= N_EXPERTS) & (lane < N_EXPERTS + N_GROUPS)
    gl = jnp.where(is_g, lg, neg)
    gmax = jnp.max(gl, axis=-1, keepdims=True)
    gidx = jnp.min(jnp.where(gl == gmax, lane_f - N_EXPERTS, big), axis=-1, keepdims=True)
    p_grp = 1.0 / jnp.sum(jnp.where(is_g, jnp.exp(gl - gmax), 0.0), axis=-1, keepdims=True)
    in_grp = (lane < N_EXPERTS) & ((lane // EXPERTS_PER_GROUP).astype(F32) == gidx)
    el = jnp.where(in_grp, lg, neg)
    m1 = jnp.max(el, axis=-1, keepdims=True)
    i1 = jnp.min(jnp.where(el == m1, lane_f, big), axis=-1, keepdims=True)
    el2 = jnp.where(lane_f == i1, neg, el)
    m2 = jnp.max(el2, axis=-1, keepdims=True)
    i2 = jnp.min(jnp.where(el2 == m2, lane_f, big), axis=-1, keepdims=True)
    ex = jnp.exp(m2 - m1)
    w1 = p_grp / (1.0 + ex)
    w2 = p_grp * ex / (1.0 + ex)

    oh1 = lane_f == i1
    oh2 = lane_f == i2
    onehot = (oh1 | oh2).astype(BF16)
    rr = lax.broadcasted_iota(jnp.int32, (tm, tm), 0)
    cc = lax.broadcasted_iota(jnp.int32, (tm, tm), 1)
    before = _dot((rr > cc).astype(BF16), onehot) + carry[...]
    rank1 = jnp.sum(jnp.where(oh1, before, 0.0), axis=-1, keepdims=True)
    rank2 = jnp.sum(jnp.where(oh2, before, 0.0), axis=-1, keepdims=True)
    carry[...] = carry[...] + jnp.sum(onehot.astype(F32), axis=0, keepdims=True)
    cnt_ref[...] = carry[...]

    info = jnp.where(lane == 0, i1, 0.0)
    info = jnp.where(lane == 1, i2, info)
    info = jnp.where(lane == 2, w1, info)
    info = jnp.where(lane == 3, w2, info)
    info = jnp.where(lane == 4, rank1, info)
    info = jnp.where(lane == 5, rank2, info)
    info_ref[...] = info


def _route(logits, tm):
    n = logits.shape[0]
    return pl.pallas_call(
        _route_kernel,
        grid=(n // tm,),
        in_specs=[pl.BlockSpec((tm, ROUTE_LANES), lambda i: (i, 0))],
        out_specs=[pl.BlockSpec((tm, ROUTE_LANES), lambda i: (i, 0)),
                   pl.BlockSpec((1, ROUTE_LANES), lambda i: (0, 0))],
        out_shape=[jax.ShapeDtypeStruct((n, ROUTE_LANES), F32),
                   jax.ShapeDtypeStruct((1, ROUTE_LANES), F32)],
        scratch_shapes=[pltpu.VMEM((1, ROUTE_LANES), F32)],
        compiler_params=_cparams("arbitrary"),
        name="moe_route",
    )(logits)


def _dispatch_kernel(dest_ref, h_ref, buf_in, buf_out, sem, *, slabs):
    del buf_in
    tm = h_ref.shape[0] // slabs

    def copy(s):
        src = pl.multiple_of((s >> 1) * slabs, slabs)
        dst = pl.multiple_of(dest_ref[s], slabs)
        return pltpu.make_async_copy(h_ref.at[pl.ds(src, slabs)], buf_out.at[pl.ds(dst, slabs)], sem)

    def start(s, c):
        copy(s).start()
        return c

    lax.fori_loop(0, 2 * tm, start, 0, unroll=8)
    for _ in range(2):
        pltpu.make_async_copy(h_ref, buf_out.at[pl.ds(0, tm * slabs)], sem).wait()


def _dispatch(h2p, dest_rows, cap, slabs, tm):
    n = h2p.shape[0] // slabs
    buf = jnp.zeros((cap * slabs, LANES), h2p.dtype)
    return pl.pallas_call(
        functools.partial(_dispatch_kernel, slabs=slabs),
        grid=(n // tm,),
        in_specs=[pl.BlockSpec((2 * tm,), lambda i: (i,), memory_space=pltpu.SMEM),
                  pl.BlockSpec((tm * slabs, LANES), lambda i: (i, 0)),
                  pl.BlockSpec(memory_space=pl.ANY)],
        out_specs=pl.BlockSpec(memory_space=pl.ANY),
        out_shape=jax.ShapeDtypeStruct(buf.shape, buf.dtype),
        scratch_shapes=[pltpu.SemaphoreType.DMA],
        input_output_aliases={2: 0},
        compiler_params=_cparams("arbitrary"),
        name="moe_dispatch",
    )(dest_rows, h2p, buf)


def _moe_kernel(be_ref, na_ref, x_ref, w1_ref, w3_ref, w2_ref, o_ref, w1b, w3b, w2b):
    j = pl.program_id(0)
    active = j < na_ref[0]
    prev = be_ref[jnp.maximum(j - 1, 0)]
    fresh = (j == 0) | (be_ref[j] != prev)

    slabs = w1b.shape[0]
    half = slabs * LANES

    @pl.when(active & fresh)
    def _():
        for s in range(slabs):
            for dst, src in ((w1b, w1_ref), (w3b, w3_ref)):
                dst[s, :LANES, :] = src[0, s * LANES:(s + 1) * LANES, :].astype(BF16)
                dst[s, LANES:, :] = src[0, half + s * LANES:half + (s + 1) * LANES, :].astype(BF16)
        w2b[...] = w2_ref[0].astype(BF16)

    @pl.when(active)
    def _():
        acc1 = jnp.zeros((EXPERT_BLOCK, w1b.shape[2]), F32)
        acc3 = jnp.zeros((EXPERT_BLOCK, w1b.shape[2]), F32)
        for s, (lo, hi) in enumerate(_load_packed(x_ref, EXPERT_BLOCK, slabs)):
            lhs = jnp.concatenate([lo.astype(BF16), hi.astype(BF16)], axis=1)
            acc1 = acc1 + _dot(lhs, w1b[s])
            acc3 = acc3 + _dot(lhs, w3b[s])
        hid = (acc1 * _sigmoid(acc1)) * acc3
        _store_packed(o_ref, _dot(hid.astype(BF16), w2b[...]), EXPERT_BLOCK)


def _moe(x_buf, block_e, n_active, w1, w3, w2, slabs):
    cap = x_buf.shape[0] // slabs
    d, de = w1.shape[1], w1.shape[2]
    nb = cap // EXPERT_BLOCK

    def xmap(j, be, na):
        return (jnp.minimum(j, na[0] - 1), 0)

    def wmap(j, be, na):
        return (be[jnp.minimum(j, na[0] - 1)], 0, 0)

    xspec = pl.BlockSpec((EXPERT_BLOCK * slabs, LANES), xmap)
    grid_spec = pltpu.PrefetchScalarGridSpec(
        num_scalar_prefetch=2,
        grid=(nb,),
        in_specs=[xspec,
                  pl.BlockSpec((1, d, de), wmap),
                  pl.BlockSpec((1, d, de), wmap),
                  pl.BlockSpec((1, de, d), wmap)],
        out_specs=xspec,
        scratch_shapes=[pltpu.VMEM((slabs, 2 * LANES, de), BF16), pltpu.VMEM((slabs, 2 * LANES, de), BF16),
                        pltpu.VMEM((de, d), BF16)],
    )
    return pl.pallas_call(
        _moe_kernel,
        grid_spec=grid_spec,
        out_shape=jax.ShapeDtypeStruct(x_buf.shape, x_buf.dtype),
        input_output_aliases={2: 0},
        compiler_params=_cparams("arbitrary"),
        name="moe_experts",
    )(block_e, n_active, x_buf, w1, w3, w2)


def _combine_kernel(dest_ref, y_ref, info_ref, x1_ref, g_ref, gt_ref, o_ref, rows0, rows1, sem, *, slabs):
    tm = x1_ref.shape[0]

    def copy(s, buf):
        src = pl.multiple_of(dest_ref[s], slabs)
        dst = pl.multiple_of((s >> 1) * slabs, slabs)
        return pltpu.make_async_copy(y_ref.at[pl.ds(src, slabs)], buf.at[pl.ds(dst, slabs)], sem)

    def start(t, c):
        copy(2 * t, rows0).start()
        copy(2 * t + 1, rows1).start()
        return c

    lax.fori_loop(0, tm, start, 0, unroll=4)
    for buf in (rows0, rows1):
        pltpu.make_async_copy(y_ref.at[pl.ds(0, tm * slabs)], buf, sem).wait()
    info = info_ref[...]
    w1 = info[:, 2:3]
    w2 = info[:, 3:4]
    lo_parts, hi_parts = [], []
    for (lo0, hi0), (lo1, hi1) in zip(_load_packed(rows0, tm, slabs), _load_packed(rows1, tm, slabs)):
        lo_parts.append(w1 * lo0 + w2 * lo1)
        hi_parts.append(w1 * hi0 + w2 * hi1)
    ffn = jnp.concatenate(lo_parts + hi_parts, axis=1)
    o_ref[...] = x1_ref[...] + gt_ref[0] * _rms(ffn, g_ref[...])


def _combine(y_buf, dest_rows, info, x1, g, gt2, seqlen, slabs, tm):
    n, d = x1.shape
    tpb = seqlen // tm
    return pl.pallas_call(
        functools.partial(_combine_kernel, slabs=slabs),
        grid=(n // tm,),
        in_specs=[pl.BlockSpec((2 * tm,), lambda i: (i,), memory_space=pltpu.SMEM),
                  pl.BlockSpec(memory_space=pl.ANY),
                  pl.BlockSpec((tm, ROUTE_LANES), lambda i: (i, 0)),
                  pl.BlockSpec((tm, d), lambda i: (i, 0)),
                  pl.BlockSpec((1, d), lambda i: (0, 0)),
                  pl.BlockSpec((1, 1, d), lambda i: (i // tpb, 0, 0))],
        out_specs=pl.BlockSpec((tm, d), lambda i: (i, 0)),
        out_shape=jax.ShapeDtypeStruct((n, d), F32),
        scratch_shapes=[pltpu.VMEM((tm * slabs, LANES), y_buf.dtype),
                        pltpu.VMEM((tm * slabs, LANES), y_buf.dtype), pltpu.SemaphoreType.DMA],
        compiler_params=_cparams("arbitrary"),
        name="moe_combine",
    )(dest_rows, y_buf, info, x1, g.reshape(1, d), gt2[:, None, :])


def _pick(n, pref):
    while n % pref:
        pref //= 2
    return pref


def _layer(x2, mod, p, bsz, seqlen):
    n, d = x2.shape
    sh1, sc1, gt1, sh2, sc2, gt2 = jnp.split(mod, 6, axis=-1)
    d_ssm = p["ssm_d"].shape[0]
    d_r = p["rwkv_w0"].shape[0]
    tm = _pick(seqlen, 1024)

    h1 = _norm_mod(x2, p["norm_mix_pre"], sc1, sh1, seqlen, _pick(seqlen, 512))
    w_in = p["w_in"].astype(BF16)
    w_z = w_in[:, d_ssm:]
    dz = w_z.shape[1]
    u = _matmul(h1, w_in[:, :d_ssm], tm, d_ssm, BF16, "in_proj_u", time_major_batches=bsz)
    z = _matmul(h1, w_z, tm, dz // 2 if (dz // 2) % 128 == 0 else dz, BF16, "in_proj_z")

    tabs = _s5_tables(p["ssm_lam_re"], p["ssm_lam_im"], p["ssm_log_dt"], p["ssm_b_re"], p["ssm_b_im"],
                      p["ssm_c_re"], p["ssm_c_im"])
    y_ssm = _s5_glu(u, tabs, p["ssm_d"], p["glu_w"], p["glu_b"], bsz, seqlen, _pick(seqlen, S5_TIME_BLOCK))

    r, k, v, kk, asig, lw, g = _rwkv_prep(z, p["rwkv_mu"], p["rwkv_w0"], p["rwkv_w_up"], p["rwkv_a0"],
                                          p["rwkv_a_up"], p["rwkv_g_up"], p["rwkv_k_k"], p["rwkv_k_a"],
                                          seqlen, _pick(seqlen, 256))
    y_rwkv = _rwkv(r, k, v, kk, asig, lw, g, p["rwkv_r_k"], p["rwkv_ln_w"], p["rwkv_ln_b"], bsz, seqlen)

    w_route = jnp.zeros((d, ROUTE_LANES), F32)
    w_route = w_route.at[:, :N_EXPERTS].set(p["moe_w_exp"].astype(F32))
    w_route = w_route.at[:, N_EXPERTS:N_EXPERTS + N_GROUPS].set(p["moe_w_grp"].astype(F32))
    b_route = jnp.zeros((ROUTE_LANES,), F32)
    b_route = b_route.at[:N_EXPERTS].set(p["moe_b_exp"].astype(F32))
    b_route = b_route.at[N_EXPERTS:N_EXPERTS + N_GROUPS].set(p["moe_b_grp"].astype(F32))
    x1, h2p, logits = _post_mix(y_ssm, y_rwkv, p["w_out"], x2, p["norm_mix_post"], gt1, p["norm_ffn_pre"],
                                sc2, sh2, w_route, b_route, seqlen, _pick(seqlen, 256))
    slabs = d // (2 * LANES)

    info, counts = _route(logits, _pick(n, 512))
    cnt = counts[0, :N_EXPERTS].astype(jnp.int32)
    padded = (cnt + EXPERT_BLOCK - 1) // EXPERT_BLOCK * EXPERT_BLOCK
    pend = jnp.cumsum(padded)
    pstart = pend - padded
    n_blocks = -(-(2 * n) // EXPERT_BLOCK) + N_EXPERTS
    cap = n_blocks * EXPERT_BLOCK
    e12 = info[:, 0:2].astype(jnp.int32)
    dest_rows = (pstart[e12] + info[:, 4:6].astype(jnp.int32)).reshape(2 * n) * slabs
    block_e = jnp.minimum(
        jnp.searchsorted(pend, jnp.arange(n_blocks, dtype=jnp.int32) * EXPERT_BLOCK, side="right"),
        N_EXPERTS - 1).astype(jnp.int32)
    n_active = (pend[-1:] // EXPERT_BLOCK).astype(jnp.int32)

    tdm = _pick(n, 512)
    x_buf = _dispatch(h2p, dest_rows, cap, slabs, tdm)
    y_buf = _moe(x_buf, block_e, n_active, p["moe_w1"], p["moe_w3"], p["moe_w2"], slabs)
    return _combine(y_buf, dest_rows, info, x1, p["norm_ffn_post"], gt2, seqlen, slabs, _pick(seqlen, 512))


def kernel(x, c, ada_w, ada_b, norm_mix_pre, norm_mix_post, norm_ffn_pre, norm_ffn_post, w_in, w_out, ssm_lam_re, ssm_lam_im, ssm_log_dt, ssm_b_re, ssm_b_im, ssm_c_re, ssm_c_im, ssm_d, glu_w, glu_b, rwkv_mu, rwkv_w0, rwkv_w_up, rwkv_a0, rwkv_a_up, rwkv_g_up, rwkv_k_k, rwkv_k_a, rwkv_r_k, rwkv_ln_w, rwkv_ln_b, moe_w_grp, moe_b_grp, moe_w_exp, moe_b_exp, moe_w1, moe_w3, moe_w2):
    bsz, seqlen, d = x.shape
    params = dict(norm_mix_pre=norm_mix_pre, norm_mix_post=norm_mix_post, norm_ffn_pre=norm_ffn_pre,
                  norm_ffn_post=norm_ffn_post, w_in=w_in, w_out=w_out, ssm_lam_re=ssm_lam_re,
                  ssm_lam_im=ssm_lam_im, ssm_log_dt=ssm_log_dt, ssm_b_re=ssm_b_re, ssm_b_im=ssm_b_im,
                  ssm_c_re=ssm_c_re, ssm_c_im=ssm_c_im, ssm_d=ssm_d, glu_w=glu_w, glu_b=glu_b,
                  rwkv_mu=rwkv_mu, rwkv_w0=rwkv_w0, rwkv_w_up=rwkv_w_up, rwkv_a0=rwkv_a0,
                  rwkv_a_up=rwkv_a_up, rwkv_g_up=rwkv_g_up, rwkv_k_k=rwkv_k_k, rwkv_k_a=rwkv_k_a,
                  rwkv_r_k=rwkv_r_k, rwkv_ln_w=rwkv_ln_w, rwkv_ln_b=rwkv_ln_b, moe_w_grp=moe_w_grp,
                  moe_b_grp=moe_b_grp, moe_w_exp=moe_w_exp, moe_b_exp=moe_b_exp, moe_w1=moe_w1,
                  moe_w3=moe_w3, moe_w2=moe_w2)
    x2 = x.reshape(bsz * seqlen, d)
    for layer in range(ada_w.shape[0]):
        mod = _ada(c, ada_w[layer], ada_b[layer])
        x2 = _layer(x2, mod, {k: v[layer] for k, v in params.items()}, bsz, seqlen)
    return x2.reshape(bsz, seqlen, d)
```

```python
import functools
import math

import jax
import jax.numpy as jnp
from jax import lax
from jax.experimental import pallas as pl
from jax.experimental.pallas import tpu as pltpu

F32 = jnp.float32
BF16 = jnp.bfloat16

SSM_GROUP = 16
SSM_STATE = 64
S5_SLAB = 256
S5_TIME_BLOCK = 64
RWKV_HEAD = 64
RWKV_CHUNK = 64
HEADS_PER_TILE = 4
LORA_W = 64
LORA_A = 64
LORA_G = 128
N_GROUPS = 8
EXPERTS_PER_GROUP = 8
N_EXPERTS = N_GROUPS * EXPERTS_PER_GROUP
EXPERT_BLOCK = 256
RMS_EPS = 1e-6
GN_EPS = 64e-5
DECAY_SCALE = math.exp(-0.5)
LANES = 128
ROUTE_LANES = LANES
VMEM_LIMIT = 52 * 1024 * 1024


def _cparams(*sem):
    return pltpu.CompilerParams(dimension_semantics=sem, vmem_limit_bytes=VMEM_LIMIT)


def _sigmoid(x):
    return 1.0 / (1.0 + jnp.exp(-x))


def _dot(a, b):
    return jnp.dot(a, b, preferred_element_type=F32)


def _dot_nt(a, b):
    return lax.dot_general(a, b, (((1,), (1,)), ((), ())), preferred_element_type=F32)


def _dot_tn(a, b):
    return lax.dot_general(a, b, (((0,), (0,)), ((), ())), preferred_element_type=F32)


def _split_bf16(x):
    hi = x.astype(BF16)
    lo = (x - hi.astype(F32)).astype(BF16)
    return hi, lo


def _pack_pair(a, b):
    ua = lax.bitcast_convert_type(a.astype(BF16).astype(F32), jnp.uint32)
    ub = lax.bitcast_convert_type(b.astype(BF16).astype(F32), jnp.uint32)
    return ub | (ua >> 16)


def _unpack_pair(w):
    lo = lax.bitcast_convert_type(w << 16, F32)
    hi = lax.bitcast_convert_type(w & jnp.uint32(0xFFFF0000), F32)
    return lo, hi


def _store_packed(ref, val, n_rows):
    d = val.shape[1]
    slabs = d // (2 * LANES)
    for s in range(slabs):
        a = val[:, s * LANES:(s + 1) * LANES]
        b = val[:, d // 2 + s * LANES:d // 2 + (s + 1) * LANES]
        ref[pl.ds(s, n_rows, stride=slabs), :] = _pack_pair(a, b)


def _load_packed(ref, n_rows, slabs):
    return [_unpack_pair(ref[pl.ds(s, n_rows, stride=slabs), :]) for s in range(slabs)]


def _ada_kernel(c_ref, w_ref, b_ref, o_ref):
    c = c_ref[...]
    cond = c * _sigmoid(c)
    o_ref[...] = jnp.dot(cond, w_ref[...], preferred_element_type=F32,
                         precision=lax.Precision.HIGHEST) + b_ref[...]


def _ada(c, ada_w, ada_b):
    bsz, d = c.shape
    n = ada_w.shape[1]
    tn = 1024
    return pl.pallas_call(
        _ada_kernel,
        grid=(n // tn,),
        in_specs=[pl.BlockSpec((bsz, d), lambda j: (0, 0)),
                  pl.BlockSpec((d, tn), lambda j: (0, j)),
                  pl.BlockSpec((1, tn), lambda j: (0, j))],
        out_specs=pl.BlockSpec((bsz, tn), lambda j: (0, j)),
        out_shape=jax.ShapeDtypeStruct((bsz, n), F32),
        compiler_params=_cparams("arbitrary"),
        name="ada_mod",
    )(c, ada_w, ada_b.reshape(1, n))


def _rms(x, g):
    return x * lax.rsqrt(jnp.mean(x * x, axis=-1, keepdims=True) + RMS_EPS) * g


def _norm_mod_kernel(x_ref, g_ref, sc_ref, sh_ref, o_ref):
    x = x_ref[...]
    h = _rms(x, g_ref[...]) * (1.0 + sc_ref[0]) + sh_ref[0]
    o_ref[...] = h.astype(o_ref.dtype)


def _norm_mod(x2, g, sc, sh, seqlen, tm):
    n, d = x2.shape
    tpb = seqlen // tm
    return pl.pallas_call(
        _norm_mod_kernel,
        grid=(n // tm,),
        in_specs=[pl.BlockSpec((tm, d), lambda i: (i, 0)),
                  pl.BlockSpec((1, d), lambda i: (0, 0)),
                  pl.BlockSpec((1, 1, d), lambda i: (i // tpb, 0, 0)),
                  pl.BlockSpec((1, 1, d), lambda i: (i // tpb, 0, 0))],
        out_specs=pl.BlockSpec((tm, d), lambda i: (i, 0)),
        out_shape=jax.ShapeDtypeStruct((n, d), BF16),
        compiler_params=_cparams("parallel"),
        name="norm_mod",
    )(x2, g.reshape(1, d), sc[:, None, :], sh[:, None, :])


def _mm_kernel(a_ref, w_ref, o_ref):
    o_ref[...] = _dot(a_ref[...], w_ref[...]).astype(o_ref.dtype)


def _matmul(a, w, tm, tn, out_dtype, name):
    m, k = a.shape
    n = w.shape[1]
    return pl.pallas_call(
        _mm_kernel,
        grid=(m // tm, n // tn),
        in_specs=[pl.BlockSpec((tm, k), lambda i, j: (i, 0)),
                  pl.BlockSpec((k, tn), lambda i, j: (0, j))],
        out_specs=pl.BlockSpec((tm, tn), lambda i, j: (i, j)),
        out_shape=jax.ShapeDtypeStruct((m, n), out_dtype),
        compiler_params=_cparams("parallel", "arbitrary"),
        name=name,
    )(a, w)


def _s5_tables(lam_re, lam_im, log_dt, b_re, b_im, c_re, c_im):
    g, p, cg = b_re.shape
    gs = S5_SLAB // cg
    ns = g // gs
    lr = jnp.minimum(lam_re.astype(F32), -1e-4)
    li = lam_im.astype(F32)
    dt = jnp.exp(log_dt.astype(F32))[:, None]
    mag = jnp.exp(lr * dt)
    ar, ai = mag * jnp.cos(li * dt), mag * jnp.sin(li * dt)
    den = lr * lr + li * li
    qr = ((ar - 1.0) * lr + ai * li) / den
    qi = (ai * lr - (ar - 1.0) * li) / den
    br, bi = b_re.astype(F32), b_im.astype(F32)
    bbr = qr[..., None] * br - qi[..., None] * bi
    bbi = qr[..., None] * bi + qi[..., None] * br

    def in_rows(t):
        return t.reshape(ns, gs, p, cg).transpose(0, 1, 3, 2).reshape(ns, gs * cg, p).astype(BF16)

    def out_cols(t):
        return t.reshape(ns, gs, cg, p).transpose(0, 3, 1, 2).reshape(ns, p, gs * cg).astype(BF16)

    a_tab = jnp.stack([ar.reshape(ns, gs * p), ai.reshape(ns, gs * p)], axis=1)
    return (in_rows(bbr), in_rows(bbi), out_cols(c_re.astype(F32)), out_cols(-c_im.astype(F32)), a_tab)


def _gelu_tanh(x):
    return 0.5 * x * (1.0 + jnp.tanh(math.sqrt(2.0 / math.pi) * (x + 0.044715 * (x * x * x))))


def _s5_kernel(h_ref, wu_ref, perm_ref, bre_ref, bim_ref, cre_ref, cim_ref, a_ref, d_ref, gw_ref, gb_ref,
               o_ref, b_ref, c_ref, u_scr, bscr, sscr, yscr, st_ref):
    bsz, lb, d_in = h_ref.shape
    rows = bsz * lb
    dch = wu_ref.shape[1]
    half = st_ref.shape[2] // 2
    n_p = bre_ref.shape[2]

    @pl.when(pl.program_id(0) == 0)
    def _():
        st_ref[...] = jnp.zeros_like(st_ref)
        tile_in = (lax.broadcasted_iota(jnp.int32, (n_p, half), 0)
                   == lax.broadcasted_iota(jnp.int32, (n_p, half), 1) % n_p).astype(BF16)
        tile_out = (lax.broadcasted_iota(jnp.int32, (half, n_p), 0) % n_p
                    == lax.broadcasted_iota(jnp.int32, (half, n_p), 1)).astype(BF16)
        in_mask = (lax.broadcasted_iota(jnp.int32, (S5_SLAB, half), 0) // SSM_GROUP
                   == lax.broadcasted_iota(jnp.int32, (S5_SLAB, half), 1) // n_p)
        out_mask = (lax.broadcasted_iota(jnp.int32, (half, S5_SLAB), 0) // n_p
                    == lax.broadcasted_iota(jnp.int32, (half, S5_SLAB), 1) // SSM_GROUP)
        for s in range(dch // S5_SLAB):
            b_ref[s, :, :half] = jnp.where(in_mask, _dot(bre_ref[s], tile_in), 0.0).astype(BF16)
            b_ref[s, :, half:] = jnp.where(in_mask, _dot(bim_ref[s], tile_in), 0.0).astype(BF16)
            c_ref[s, :half, :] = jnp.where(out_mask, _dot(tile_out, cre_ref[s]), 0.0).astype(BF16)
            c_ref[s, half:, :] = jnp.where(out_mask, _dot(tile_out, cim_ref[s]), 0.0).astype(BF16)

    u_nat = _dot(h_ref[...].reshape(rows, d_in), wu_ref[...]).astype(BF16)
    u_scr[...] = _dot(perm_ref[...], u_nat).astype(BF16)
    for s in range(dch // S5_SLAB):
        sl = slice(s * S5_SLAB, (s + 1) * S5_SLAB)
        us = u_scr[:, sl]
        bscr[...] = _dot(us, b_ref[s])
        a_re = a_ref[s, 0:1, :]
        a_im = a_ref[s, 1:2, :]

        def body(l, carry):
            s_r, s_i = carry
            at_l = pl.ds(pl.multiple_of(l * bsz, bsz), bsz)
            n_r = a_re * s_r - a_im * s_i + bscr[at_l, :half]
            n_i = a_re * s_i + a_im * s_r + bscr[at_l, half:]
            sscr[at_l, :half] = n_r
            sscr[at_l, half:] = n_i
            return n_r, n_i

        s_r, s_i = lax.fori_loop(0, rows // bsz, body, (st_ref[s, :, :half], st_ref[s, :, half:]), unroll=2)
        st_ref[s, :, :half] = s_r
        st_ref[s, :, half:] = s_i
        y = _dot(sscr[...].astype(BF16), c_ref[s]) + d_ref[:, sl] * us.astype(F32)
        yscr[:, sl] = _gelu_tanh(y).astype(BF16)
    y = yscr[...]
    gate = _sigmoid(_dot(y, gw_ref[...]) + gb_ref[...])
    out_tm = (y.astype(F32) * gate).astype(BF16)
    out_nat = _dot_tn(perm_ref[...], out_tm).astype(o_ref.dtype)
    o_ref[...] = out_nat.reshape(bsz, lb, dch)


def _s5_glu(h, w_u, tabs, d_skip, glu_w, glu_b, bsz, seqlen, lb):
    b_re, b_im, c_re, c_im, a_tab = tabs
    d_in, dch = w_u.shape
    ns, _, n_half = a_tab.shape
    n_state = 2 * n_half
    rows = lb * bsz
    r_idx = jnp.arange(rows)
    perm = ((r_idx % bsz) * lb + r_idx // bsz)[:, None] == r_idx[None, :]
    full = lambda a: pl.BlockSpec(a.shape, lambda i: (0,) * a.ndim)
    args = (h.reshape(bsz, seqlen, d_in), w_u, perm.astype(BF16), b_re, b_im, c_re, c_im, a_tab,
            d_skip.astype(F32).reshape(1, dch), glu_w.astype(BF16), glu_b.astype(F32).reshape(1, dch))
    y = pl.pallas_call(
        _s5_kernel,
        grid=(seqlen // lb,),
        in_specs=[pl.BlockSpec((bsz, lb, d_in), lambda i: (0, i, 0))] + [full(a) for a in args[1:]],
        out_specs=pl.BlockSpec((bsz, lb, dch), lambda i: (0, i, 0)),
        out_shape=jax.ShapeDtypeStruct((bsz, seqlen, dch), BF16),
        scratch_shapes=[pltpu.VMEM((ns, S5_SLAB, n_state), BF16), pltpu.VMEM((ns, n_state, S5_SLAB), BF16),
                        pltpu.VMEM((rows, dch), BF16), pltpu.VMEM((rows, n_state), F32),
                        pltpu.VMEM((rows, n_state), F32), pltpu.VMEM((rows, dch), BF16),
                        pltpu.VMEM((ns, bsz, n_state), F32)],
        compiler_params=_cparams("arbitrary"),
        name="s5_mixer_glu",
    )(*args)
    return y.reshape(bsz * seqlen, dch)


def _rwkv_prep_kernel(z_ref, halo_ref, mu_ref, w0_ref, a0_ref, kk_ref, ka_ref, wa_ref, gup_ref,
                      r_out, k_out, v_out, kk_out, as_out, lw_out, g_out, *, tiles_per_seq, d_r):
    i = pl.program_id(0)
    tm = z_ref.shape[0]
    not_first = (i % tiles_per_seq != 0).astype(F32)
    row0 = lax.broadcasted_iota(jnp.int32, (tm, 1), 0) == 0

    def lerp(lo, hi):
        zc = z_ref[:, lo:hi].astype(F32)
        prev_row = halo_ref[7:8, lo:hi].astype(F32) * not_first
        shifted = jnp.where(row0, prev_row, pltpu.roll(zc, 1, 0))
        return zc + mu_ref[:, lo:hi] * (shifted - zc)

    r_out[...] = lerp(0, d_r).astype(r_out.dtype)
    v_out[...] = lerp(2 * d_r, 3 * d_r).astype(v_out.dtype)
    xwa = lerp(3 * d_r, 3 * d_r + LORA_W + LORA_A)
    lane = lax.broadcasted_iota(jnp.int32, xwa.shape, 1)
    lhs = jnp.where(lane < LORA_W, jnp.tanh(xwa), xwa).astype(BF16)
    wa = _dot(lhs, wa_ref[...])
    lw_out[...] = -DECAY_SCALE * _sigmoid(w0_ref[...] + wa[:, :d_r])
    asig = _sigmoid(a0_ref[...] + wa[:, d_r:])
    as_out[...] = asig.astype(as_out.dtype)
    k = lerp(d_r, 2 * d_r)
    kk_out[...] = (k * kk_ref[...]).astype(kk_out.dtype)
    k_out[...] = (k * (1.0 + (asig - 1.0) * ka_ref[...])).astype(k_out.dtype)
    xg = lerp(3 * d_r + LORA_W + LORA_A, 3 * d_r + LORA_W + LORA_A + LORA_G)
    g_out[...] = _dot(_sigmoid(xg).astype(BF16), gup_ref[...]).astype(g_out.dtype)


def _rwkv_prep(z, mu, w0, w_up, a0, a_up, g_up, k_k, k_a, seqlen, tm):
    n, dz = z.shape
    d_r = w0.shape[0]
    wa = jnp.zeros((LORA_W + LORA_A, 2 * d_r), F32)
    wa = wa.at[:LORA_W, :d_r].set(w_up.astype(F32)).at[LORA_W:, d_r:].set(a_up.astype(F32)).astype(BF16)
    row = lambda a: a.astype(F32).reshape(1, -1)
    full = lambda a: pl.BlockSpec(a.shape, lambda i: (0, 0))
    hb = tm // 8
    args = (z, z, row(mu), row(w0), row(a0), row(k_k), row(k_a), wa, g_up.astype(BF16))
    in_specs = [pl.BlockSpec((tm, dz), lambda i: (i, 0)),
                pl.BlockSpec((8, dz), lambda i: (jnp.maximum(i * hb - 1, 0), 0))]
    in_specs += [full(a) for a in args[2:]]
    ospec = pl.BlockSpec((tm, d_r), lambda i: (i, 0))
    bf = jax.ShapeDtypeStruct((n, d_r), BF16)
    return pl.pallas_call(
        functools.partial(_rwkv_prep_kernel, tiles_per_seq=seqlen // tm, d_r=d_r),
        grid=(n // tm,),
        in_specs=in_specs,
        out_specs=[ospec] * 7,
        out_shape=[bf, bf, bf, bf, bf, jax.ShapeDtypeStruct((n, d_r), F32), bf],
        compiler_params=_cparams("parallel"),
        name="rwkv_prep",
    )(*args)


def _rwkv_kernel(r_ref, k_ref, v_ref, kk_ref, as_ref, lw_ref, g_ref, rk_ref, lnw_ref, lnb_ref,
                 ones_ref, o_ref, s_ref):
    t = r_ref.shape[0]
    d_r = r_ref.shape[1]
    tile = HEADS_PER_TILE * RWKV_HEAD
    n_tiles = d_r // tile

    @pl.when(pl.program_id(1) == 0)
    def _():
        s_ref[...] = jnp.zeros_like(s_ref)

    ones_bd = ones_ref[...]

    def seg_sum(x):
        hi, lo = _split_bf16(x)
        return _dot(hi, ones_bd) + _dot(lo, ones_bd)

    row = lax.broadcasted_iota(jnp.int32, (t, t), 0)
    col = lax.broadcasted_iota(jnp.int32, (t, t), 1)
    tri = (row >= col).astype(BF16)
    eye = (row == col).astype(F32)
    st = HEADS_PER_TILE * t
    rs = lax.broadcasted_iota(jnp.int32, (2 * st, 2 * st), 0)
    cs = lax.broadcasted_iota(jnp.int32, (2 * st, 2 * st), 1)
    t_r = rs % t
    t_c = cs % t
    keep = (t_r > t_c) | ((rs >= st) & (t_r == t_c))
    eye_st = (lax.broadcasted_iota(jnp.int32, (st, st), 0)
              == lax.broadcasted_iota(jnp.int32, (st, st), 1)).astype(F32)
    lane = lax.broadcasted_iota(jnp.int32, (1, tile), 1)
    head_masks = [(lane >= j * RWKV_HEAD) & (lane < (j + 1) * RWKV_HEAD) for j in range(HEADS_PER_TILE)]
    bd_r = lax.broadcasted_iota(jnp.int32, (tile, tile), 0) // RWKV_HEAD
    bd_c = lax.broadcasted_iota(jnp.int32, (tile, tile), 1) // RWKV_HEAD
    bd_mask = bd_r == bd_c
    n_levels = int(math.log2(t))
    tiles = range(n_tiles)
    slices = [slice(hg * tile, (hg + 1) * tile) for hg in tiles]

    def stack(x):
        return jnp.concatenate([jnp.where(m, x, 0.0) for m in head_masks], axis=0)

    def collapse(x):
        out = x[:t]
        for j in range(1, HEADS_PER_TILE):
            out = out + x[j * t:(j + 1) * t]
        return out

    def bf(x):
        return x.astype(BF16)

    lw = lw_ref[...]
    lw_hi, lw_lo = _split_bf16(lw)
    cum = _dot(tri, lw_hi) + _dot(tri, lw_lo)
    mid = cum[t // 2 - 1:t // 2, :]
    tot = cum[t - 1:t, :]
    e1 = jnp.exp(cum - mid)
    e2 = jnp.exp(mid - cum)
    e1p = e1 * jnp.exp(-lw)
    em = jnp.exp(mid)
    etm = jnp.exp(tot - mid)
    wtot = jnp.exp(tot)
    r = r_ref[...].astype(F32)
    kp = k_ref[...].astype(F32)
    v = v_ref[...].astype(F32)
    kk = kk_ref[...].astype(F32)
    asig = as_ref[...].astype(F32)
    kk2 = kk * kk
    kkn = kk / jnp.maximum(jnp.sqrt(jnp.concatenate([seg_sum(kk2[:, sl]) for sl in slices], axis=1)), 1e-12)
    qt = r * e1
    kt = kp * e2
    at = -kkn * e1p
    bt = kkn * asig * e2
    rkb = r * kp * rk_ref[...]

    s_old = [s_ref[hg] for hg in tiles]
    a_all = []
    for sl in slices:
        lhs = jnp.concatenate([stack(at[:, sl]), stack(qt[:, sl])], axis=0)
        rhs = jnp.concatenate([stack(bt[:, sl]), stack(kt[:, sl])], axis=0)
        a_all.append(jnp.where(keep, _dot_nt(bf(lhs), bf(rhs)), 0.0))
    x_state = []
    for hg, sl in enumerate(slices):
        lhs = jnp.concatenate([stack(at[:, sl] * em[:, sl]), stack(qt[:, sl] * em[:, sl])], axis=0)
        x_state.append(_dot_nt(bf(lhs), bf(s_old[hg])))
    p_acc = [eye_st + a[:st, :st] for a in a_all]
    q_pow = [_dot(bf(a[:st, :st]), bf(a[:st, :st])) for a in a_all]
    av = [_dot(bf(a[:, st:]), bf(stack(v[:, sl]))) for a, sl in zip(a_all, slices)]
    for lev in range(1, n_levels):
        for hg in tiles:
            if lev < n_levels - 1:
                both = _dot(bf(jnp.concatenate([p_acc[hg], q_pow[hg]], axis=0)), bf(q_pow[hg]))
                p_acc[hg] = p_acc[hg] + both[:st]
                q_pow[hg] = both[st:]
            else:
                p_acc[hg] = p_acc[hg] + _dot(bf(p_acc[hg]), bf(q_pow[hg]))
    u_st = [_dot(bf(p_acc[hg]), bf(x_state[hg][:st] + av[hg][:st])) for hg in tiles]
    y_st = [x_state[hg][st:] + av[hg][st:] + _dot(bf(a_all[hg][st:, :st]), bf(u_st[hg])) for hg in tiles]

    for hg, sl in enumerate(slices):
        u = collapse(u_st[hg])
        y = collapse(y_st[hg])
        uv = bf(jnp.concatenate([u, v[:, sl]], axis=0))
        bk_end = bf(jnp.concatenate([bt[:, sl], kt[:, sl]], axis=0) * etm[:, sl])
        s_ref[hg] = s_old[hg] * wtot[:, sl] + jnp.where(bd_mask, _dot_tn(uv, bk_end), 0.0)

        mean = seg_sum(y) * (1.0 / RWKV_HEAD)
        dlt = y - mean
        var = seg_sum(dlt * dlt) * (1.0 / RWKV_HEAD)
        yn = dlt * lax.rsqrt(var + GN_EPS) * lnw_ref[:, sl] + lnb_ref[:, sl]
        out = (yn + seg_sum(rkb[:, sl]) * v[:, sl]) * g_ref[:, sl].astype(F32)
        o_ref[:, sl] = out.astype(o_ref.dtype)


def _rwkv(r, k, v, kk, asig, lw, g, r_k, ln_w, ln_b, bsz, seqlen):
    n, d_r = r.shape
    t = RWKV_CHUNK
    nch = seqlen // t
    tile = HEADS_PER_TILE * RWKV_HEAD
    hid = jnp.arange(tile) // RWKV_HEAD
    ones_bd = (hid[:, None] == hid[None, :]).astype(BF16)
    row = lambda a: a.astype(F32).reshape(1, d_r)
    tspec = pl.BlockSpec((t, d_r), lambda b, c: (b * nch + c, 0))
    pspec = pl.BlockSpec((1, d_r), lambda b, c: (0, 0))
    return pl.pallas_call(
        _rwkv_kernel,
        grid=(bsz, nch),
        in_specs=[tspec] * 7 + [pspec] * 3 + [pl.BlockSpec((tile, tile), lambda b, c: (0, 0))],
        out_specs=tspec,
        out_shape=jax.ShapeDtypeStruct((n, d_r), BF16),
        scratch_shapes=[pltpu.VMEM((d_r // tile, tile, tile), F32)],
        compiler_params=_cparams("parallel", "arbitrary"),
        name="rwkv7_chunked",
    )(r, k, v, kk, asig, lw, g, row(r_k), row(ln_w), row(ln_b), ones_bd)


def _post_mix_kernel(ys_ref, yr_ref, wo1_ref, wo2_ref, x_ref, g1_ref, gt_ref, g2_ref, sc_ref, sh_ref,
                     wr_hi_ref, wr_lo_ref, br_ref, x1_out, h2_out, lg_out):
    mixed = _dot(ys_ref[...], wo1_ref[...]) + _dot(yr_ref[...], wo2_ref[...])
    x1 = x_ref[...] + gt_ref[0] * _rms(mixed, g1_ref[...])
    x1_out[...] = x1
    h2 = _rms(x1, g2_ref[...]) * (1.0 + sc_ref[0]) + sh_ref[0]
    _store_packed(h2_out, h2, h2.shape[0])
    hi, lo = _split_bf16(h2)
    lg_out[...] = (_dot(hi, wr_hi_ref[...]) + _dot(lo, wr_hi_ref[...]) + _dot(hi, wr_lo_ref[...])
                   + br_ref[...])


def _post_mix(ys, yr, w_out, x2, g1, gt1, g2, sc2, sh2, w_route, b_route, seqlen, tm):
    n, d = x2.shape
    ds = ys.shape[1]
    tpb = seqlen // tm
    slabs = d // (2 * LANES)
    wo = w_out.astype(BF16)
    wr_hi, wr_lo = _split_bf16(w_route)
    rows = lambda w: pl.BlockSpec((tm, w), lambda i: (i, 0))
    full = lambda a: pl.BlockSpec(a.shape, lambda i: (0,) * a.ndim)
    bat = pl.BlockSpec((1, 1, d), lambda i: (i // tpb, 0, 0))
    args = (ys, yr, wo[:ds], wo[ds:], x2, g1.reshape(1, d), gt1[:, None, :], g2.reshape(1, d),
            sc2[:, None, :], sh2[:, None, :], wr_hi, wr_lo, b_route.reshape(1, -1))
    in_specs = [rows(ds), rows(yr.shape[1]), full(args[2]), full(args[3]), rows(d), full(args[5]), bat,
                full(args[7]), bat, bat, full(wr_hi), full(wr_lo), full(args[12])]
    return pl.pallas_call(
        _post_mix_kernel,
        grid=(n // tm,),
        in_specs=in_specs,
        out_specs=[rows(d), pl.BlockSpec((tm * slabs, LANES), lambda i: (i, 0)), rows(ROUTE_LANES)],
        out_shape=[jax.ShapeDtypeStruct((n, d), F32), jax.ShapeDtypeStruct((n * slabs, LANES), jnp.uint32),
                   jax.ShapeDtypeStruct((n, ROUTE_LANES), F32)],
        compiler_params=_cparams("parallel"),
        name="out_proj_post",
    )(*args)


def _route_kernel(lg_ref, info_ref, cnt_ref, carry):
    i = pl.program_id(0)
    tm = lg_ref.shape[0]

    @pl.when(i == 0)
    def _():
        carry[...] = jnp.zeros_like(carry)

    lg = lg_ref[...]
    lane = lax.broadcasted_iota(jnp.int32, lg.shape, 1)
    lane_f = lane.astype(F32)
    neg = jnp.float32(-jnp.inf)
    big = jnp.float32(1e9)
    is_g = (lane >= N_EXPERTS) & (lane < N_EXPERTS + N_GROUPS)
    gl = jnp.where(is_g, lg, neg)
    gmax = jnp.max(gl, axis=-1, keepdims=True)
    gidx = jnp.min(jnp.where(gl == gmax, lane_f - N_EXPERTS, big), axis=-1, keepdims=True)
    p_grp = 1.0 / jnp.sum(jnp.where(is_g, jnp.exp(gl - gmax), 0.0), axis=-1, keepdims=True)
    in_grp = (lane < N_EXPERTS) & ((lane // EXPERTS_PER_GROUP).astype(F32) == gidx)
    el = jnp.where(in_grp, lg, neg)
    m1 = jnp.max(el, axis=-1, keepdims=True)
    i1 = jnp.min(jnp.where(el == m1, lane_f, big), axis=-1, keepdims=True)
    el2 = jnp.where(lane_f == i1, neg, el)
    m2 = jnp.max(el2, axis=-1, keepdims=True)
    i2 = jnp.min(jnp.where(el2 == m2, lane_f, big), axis=-1, keepdims=True)
    ex = jnp.exp(m2 - m1)
    w1 = p_grp / (1.0 + ex)
    w2 = p_grp * ex / (1.0 + ex)

    oh1 = lane_f == i1
    oh2 = lane_f == i2
    onehot = (oh1 | oh2).astype(BF16)
    rr = lax.broadcasted_iota(jnp.int32, (tm, tm), 0)
    cc = lax.broadcasted_iota(jnp.int32, (tm, tm), 1)
    before = _dot((rr > cc).astype(BF16), onehot) + carry[...]
    rank1 = jnp.sum(jnp.where(oh1, before, 0.0), axis=-1, keepdims=True)
    rank2 = jnp.sum(jnp.where(oh2, before, 0.0), axis=-1, keepdims=True)
    carry[...] = carry[...] + jnp.sum(onehot.astype(F32), axis=0, keepdims=True)
    cnt_ref[...] = carry[...]

    info = jnp.where(lane == 0, i1, 0.0)
    info = jnp.where(lane == 1, i2, info)
    info = jnp.where(lane == 2, w1, info)
    info = jnp.where(lane == 3, w2, info)
    info = jnp.where(lane == 4, rank1, info)
    info = jnp.where(lane == 5, rank2, info)
    info_ref[...] = info


def _route(logits, tm):
    n = logits.shape[0]
    return pl.pallas_call(
        _route_kernel,
        grid=(n // tm,),
        in_specs=[pl.BlockSpec((tm, ROUTE_LANES), lambda i: (i, 0))],
        out_specs=[pl.BlockSpec((tm, ROUTE_LANES), lambda i: (i, 0)),
                   pl.BlockSpec((1, ROUTE_LANES), lambda i: (0, 0))],
        out_shape=[jax.ShapeDtypeStruct((n, ROUTE_LANES), F32),
                   jax.ShapeDtypeStruct((1, ROUTE_LANES), F32)],
        scratch_shapes=[pltpu.VMEM((1, ROUTE_LANES), F32)],
        compiler_params=_cparams("arbitrary"),
        name="moe_route",
    )(logits)


def _slot_rows_kernel(info_ref, seg_ref, o_ref, *, slabs):
    info = info_ref[...]
    lane = lax.broadcasted_iota(jnp.int32, info.shape, 1)
    lane_f = lane.astype(F32)
    seg = seg_ref[...]
    d0 = jnp.sum(jnp.where(lane_f == info[:, 0:1], seg, 0.0), axis=-1, keepdims=True) + info[:, 4:5] * slabs
    d1 = jnp.sum(jnp.where(lane_f == info[:, 1:2], seg, 0.0), axis=-1, keepdims=True) + info[:, 5:6] * slabs
    o_ref[...] = jnp.where(lane == 0, d0, jnp.where(lane == 1, d1, 0.0)).astype(jnp.int32)


def _slot_rows(info, seg_row, slabs, tm):
    n = info.shape[0]
    return pl.pallas_call(
        functools.partial(_slot_rows_kernel, slabs=slabs),
        grid=(n // tm,),
        in_specs=[pl.BlockSpec((tm, ROUTE_LANES), lambda i: (i, 0)),
                  pl.BlockSpec((1, ROUTE_LANES), lambda i: (0, 0))],
        out_specs=pl.BlockSpec((tm, ROUTE_LANES), lambda i: (i, 0)),
        out_shape=jax.ShapeDtypeStruct((n, ROUTE_LANES), jnp.int32),
        compiler_params=_cparams("parallel"),
        name="moe_slot_rows",
    )(info, seg_row)


def _dispatch_kernel(d0_ref, d1_ref, h_ref, buf_in, buf_out, sem, *, slabs):
    del buf_in
    tm = h_ref.shape[0] // slabs

    def copy(t, dest_ref):
        src = pl.multiple_of(t * slabs, slabs)
        dst = pl.multiple_of(dest_ref[t], slabs)
        return pltpu.make_async_copy(h_ref.at[pl.ds(src, slabs)], buf_out.at[pl.ds(dst, slabs)], sem)

    def start(t, c):
        copy(t, d0_ref).start(priority=0)
        copy(t, d1_ref).start(priority=1)
        return c

    lax.fori_loop(0, tm, start, 0, unroll=4)
    for _ in range(2):
        pltpu.make_async_copy(h_ref, buf_out.at[pl.ds(0, tm * slabs)], sem).wait()


def _dispatch(h2p, dest_rows, cap, slabs, tm):
    n = h2p.shape[0] // slabs
    buf = jnp.zeros((cap * slabs, LANES), h2p.dtype)
    smem = pl.BlockSpec((tm,), lambda i: (i,), memory_space=pltpu.SMEM)
    return pl.pallas_call(
        functools.partial(_dispatch_kernel, slabs=slabs),
        grid=(n // tm,),
        in_specs=[smem, smem,
                  pl.BlockSpec((tm * slabs, LANES), lambda i: (i, 0)),
                  pl.BlockSpec(memory_space=pl.ANY)],
        out_specs=pl.BlockSpec(memory_space=pl.ANY),
        out_shape=jax.ShapeDtypeStruct(buf.shape, buf.dtype),
        scratch_shapes=[pltpu.SemaphoreType.DMA],
        input_output_aliases={3: 0},
        compiler_params=_cparams("arbitrary"),
        name="moe_dispatch",
    )(dest_rows[0], dest_rows[1], h2p, buf)


def _moe_kernel(na_ref, eseq_ref, epos_ref, nd_ref, x_ref, w1_hbm, w3_hbm, w2_hbm, o_ref,
                w1f, w3f, w2f, w1b, w3b, w2b, sem):
    j = pl.program_id(0)
    active = j < na_ref[0]
    pos = epos_ref[j]
    fresh = (j == 0) | (pos != epos_ref[jnp.maximum(j - 1, 0)])

    slabs = w1b.shape[0]
    half = slabs * LANES

    def weight_copies(p):
        e = eseq_ref[p]
        slot = p % 2
        return [pltpu.make_async_copy(w_hbm.at[e], stage.at[slot], sem.at[slot, i])
                for i, (w_hbm, stage) in enumerate(((w1_hbm, w1f), (w3_hbm, w3f), (w2_hbm, w2f)))]

    @pl.when(j == 0)
    def _():
        for c in weight_copies(0):
            c.start()

    @pl.when(active & fresh)
    def _():
        for c in weight_copies(pos):
            c.wait()

        @pl.when(pos + 1 < nd_ref[0])
        def _():
            for c in weight_copies(pos + 1):
                c.start()

        slot = pos % 2
        for s in range(slabs):
            for dst, src in ((w1b, w1f), (w3b, w3f)):
                dst[s, :LANES, :] = src[slot, s * LANES:(s + 1) * LANES, :].astype(BF16)
                dst[s, LANES:, :] = src[slot, half + s * LANES:half + (s + 1) * LANES, :].astype(BF16)
        w2b[...] = w2f[slot].astype(BF16)

    @pl.when(active)
    def _():
        acc1 = jnp.zeros((EXPERT_BLOCK, w1b.shape[2]), F32)
        acc3 = jnp.zeros((EXPERT_BLOCK, w1b.shape[2]), F32)
        for s, (lo, hi) in enumerate(_load_packed(x_ref, EXPERT_BLOCK, slabs)):
            lhs = jnp.concatenate([lo.astype(BF16), hi.astype(BF16)], axis=1)
            acc1 = acc1 + _dot(lhs, w1b[s])
            acc3 = acc3 + _dot(lhs, w3b[s])
        hid = (acc1 * _sigmoid(acc1)) * acc3
        _store_packed(o_ref, _dot(hid.astype(BF16), w2b[...]), EXPERT_BLOCK)


def _moe(x_buf, n_active, expert_seq, block_pos, n_used, w1, w3, w2, slabs):
    cap = x_buf.shape[0] // slabs
    d, de = w1.shape[1], w1.shape[2]
    nb = cap // EXPERT_BLOCK

    def xmap(j, na, *_):
        return (jnp.minimum(j, na[0] - 1), 0)

    xspec = pl.BlockSpec((EXPERT_BLOCK * slabs, LANES), xmap)
    hbm = pl.BlockSpec(memory_space=pl.ANY)
    grid_spec = pltpu.PrefetchScalarGridSpec(
        num_scalar_prefetch=4,
        grid=(nb,),
        in_specs=[xspec, hbm, hbm, hbm],
        out_specs=xspec,
        scratch_shapes=[pltpu.VMEM((2, d, de), F32), pltpu.VMEM((2, d, de), F32), pltpu.VMEM((2, de, d), F32),
                        pltpu.VMEM((slabs, 2 * LANES, de), BF16), pltpu.VMEM((slabs, 2 * LANES, de), BF16),
                        pltpu.VMEM((de, d), BF16), pltpu.SemaphoreType.DMA((2, 3))],
    )
    return pl.pallas_call(
        _moe_kernel,
        grid_spec=grid_spec,
        out_shape=jax.ShapeDtypeStruct(x_buf.shape, x_buf.dtype),
        input_output_aliases={4: 0},
        compiler_params=_cparams("arbitrary"),
        name="moe_experts",
    )(n_active, expert_seq, block_pos, n_used, x_buf, w1, w3, w2)


def _combine_kernel(d0_ref, d1_ref, y_ref, info_ref, x1_ref, g_ref, gt_ref, o_ref, rows0, rows1, sem, *,
                    slabs):
    tm = x1_ref.shape[0]

    def copy(t, dest_ref, buf):
        src = pl.multiple_of(dest_ref[t], slabs)
        dst = pl.multiple_of(t * slabs, slabs)
        return pltpu.make_async_copy(y_ref.at[pl.ds(src, slabs)], buf.at[pl.ds(dst, slabs)], sem)

    def start(t, c):
        copy(t, d0_ref, rows0).start(priority=0)
        copy(t, d1_ref, rows1).start(priority=1)
        return c

    lax.fori_loop(0, tm, start, 0, unroll=4)
    for buf in (rows0, rows1):
        pltpu.make_async_copy(y_ref.at[pl.ds(0, tm * slabs)], buf, sem).wait()
    info = info_ref[...]
    w1 = info[:, 2:3]
    w2 = info[:, 3:4]
    lo_parts, hi_parts = [], []
    for (lo0, hi0), (lo1, hi1) in zip(_load_packed(rows0, tm, slabs), _load_packed(rows1, tm, slabs)):
        lo_parts.append(w1 * lo0 + w2 * lo1)
        hi_parts.append(w1 * hi0 + w2 * hi1)
    ffn = jnp.concatenate(lo_parts + hi_parts, axis=1)
    o_ref[...] = x1_ref[...] + gt_ref[0] * _rms(ffn, g_ref[...])


def _combine(y_buf, dest_rows, info, x1, g, gt2, seqlen, slabs, tm):
    n, d = x1.shape
    tpb = seqlen // tm
    smem = pl.BlockSpec((tm,), lambda i: (i,), memory_space=pltpu.SMEM)
    return pl.pallas_call(
        functools.partial(_combine_kernel, slabs=slabs),
        grid=(n // tm,),
        in_specs=[smem, smem,
                  pl.BlockSpec(memory_space=pl.ANY),
                  pl.BlockSpec((tm, ROUTE_LANES), lambda i: (i, 0)),
                  pl.BlockSpec((tm, d), lambda i: (i, 0)),
                  pl.BlockSpec((1, d), lambda i: (0, 0)),
                  pl.BlockSpec((1, 1, d), lambda i: (i // tpb, 0, 0))],
        out_specs=pl.BlockSpec((tm, d), lambda i: (i, 0)),
        out_shape=jax.ShapeDtypeStruct((n, d), F32),
        scratch_shapes=[pltpu.VMEM((tm * slabs, LANES), y_buf.dtype),
                        pltpu.VMEM((tm * slabs, LANES), y_buf.dtype), pltpu.SemaphoreType.DMA],
        compiler_params=_cparams("arbitrary"),
        name="moe_combine",
    )(dest_rows[0], dest_rows[1], y_buf, info, x1, g.reshape(1, d), gt2[:, None, :])


def _pick(n, pref):
    while n % pref:
        pref //= 2
    return pref


def _layer(x2, mod, p, bsz, seqlen):
    n, d = x2.shape
    sh1, sc1, gt1, sh2, sc2, gt2 = jnp.split(mod, 6, axis=-1)
    d_ssm = p["ssm_d"].shape[0]
    d_r = p["rwkv_w0"].shape[0]
    tm = _pick(seqlen, 1024)

    h1 = _norm_mod(x2, p["norm_mix_pre"], sc1, sh1, seqlen, _pick(seqlen, 512))
    w_in = p["w_in"].astype(BF16)
    w_z = w_in[:, d_ssm:]
    dz = w_z.shape[1]
    z = _matmul(h1, w_z, tm, dz // 2 if (dz // 2) % 128 == 0 else dz, BF16, "in_proj_z")

    tabs = _s5_tables(p["ssm_lam_re"], p["ssm_lam_im"], p["ssm_log_dt"], p["ssm_b_re"], p["ssm_b_im"],
                      p["ssm_c_re"], p["ssm_c_im"])
    y_ssm = _s5_glu(h1, w_in[:, :d_ssm], tabs, p["ssm_d"], p["glu_w"], p["glu_b"], bsz, seqlen,
                    _pick(seqlen, S5_TIME_BLOCK))

    r, k, v, kk, asig, lw, g = _rwkv_prep(z, p["rwkv_mu"], p["rwkv_w0"], p["rwkv_w_up"], p["rwkv_a0"],
                                          p["rwkv_a_up"], p["rwkv_g_up"], p["rwkv_k_k"], p["rwkv_k_a"],
                                          seqlen, _pick(seqlen, 256))
    y_rwkv = _rwkv(r, k, v, kk, asig, lw, g, p["rwkv_r_k"], p["rwkv_ln_w"], p["rwkv_ln_b"], bsz, seqlen)

    w_route = jnp.zeros((d, ROUTE_LANES), F32)
    w_route = w_route.at[:, :N_EXPERTS].set(p["moe_w_exp"].astype(F32))
    w_route = w_route.at[:, N_EXPERTS:N_EXPERTS + N_GROUPS].set(p["moe_w_grp"].astype(F32))
    b_route = jnp.zeros((ROUTE_LANES,), F32)
    b_route = b_route.at[:N_EXPERTS].set(p["moe_b_exp"].astype(F32))
    b_route = b_route.at[N_EXPERTS:N_EXPERTS + N_GROUPS].set(p["moe_b_grp"].astype(F32))
    x1, h2p, logits = _post_mix(y_ssm, y_rwkv, p["w_out"], x2, p["norm_mix_post"], gt1, p["norm_ffn_pre"],
                                sc2, sh2, w_route, b_route, seqlen, _pick(seqlen, 256))
    slabs = d // (2 * LANES)

    info, counts = _route(logits, _pick(n, 512))
    cnt = counts[0, :N_EXPERTS].astype(jnp.int32)
    padded = (cnt + EXPERT_BLOCK - 1) // EXPERT_BLOCK * EXPERT_BLOCK
    pend = jnp.cumsum(padded)
    pstart = pend - padded
    n_blocks = -(-(2 * n) // EXPERT_BLOCK) + N_EXPERTS
    cap = n_blocks * EXPERT_BLOCK
    seg_row = jnp.zeros((1, ROUTE_LANES), F32).at[0, :N_EXPERTS].set((pstart * slabs).astype(F32))
    dest = _slot_rows(info, seg_row, slabs, _pick(n, 2048))
    dest_rows = (dest[:, 0], dest[:, 1])
    block_e = jnp.minimum(
        jnp.searchsorted(pend, jnp.arange(n_blocks, dtype=jnp.int32) * EXPERT_BLOCK, side="right"),
        N_EXPERTS - 1).astype(jnp.int32)
    n_active = (pend[-1:] // EXPERT_BLOCK).astype(jnp.int32)
    used = cnt > 0
    expert_seq = jnp.nonzero(used, size=N_EXPERTS, fill_value=0)[0].astype(jnp.int32)
    block_pos = (jnp.cumsum(used.astype(jnp.int32)) - 1)[block_e].astype(jnp.int32)
    n_used = jnp.sum(used.astype(jnp.int32)).reshape(1)

    x_buf = _dispatch(h2p, dest_rows, cap, slabs, _pick(n, 1024))
    y_buf = _moe(x_buf, n_active, expert_seq, block_pos, n_used, p["moe_w1"], p["moe_w3"], p["moe_w2"], slabs)
    return _combine(y_buf, dest_rows, info, x1, p["norm_ffn_post"], gt2, seqlen, slabs, _pick(seqlen, 1024))


def kernel(x, c, ada_w, ada_b, norm_mix_pre, norm_mix_post, norm_ffn_pre, norm_ffn_post, w_in, w_out, ssm_lam_re, ssm_lam_im, ssm_log_dt, ssm_b_re, ssm_b_im, ssm_c_re, ssm_c_im, ssm_d, glu_w, glu_b, rwkv_mu, rwkv_w0, rwkv_w_up, rwkv_a0, rwkv_a_up, rwkv_g_up, rwkv_k_k, rwkv_k_a, rwkv_r_k, rwkv_ln_w, rwkv_ln_b, moe_w_grp, moe_b_grp, moe_w_exp, moe_b_exp, moe_w1, moe_w3, moe_w2):
    bsz, seqlen, d = x.shape
    params = dict(norm_mix_pre=norm_mix_pre, norm_mix_post=norm_mix_post, norm_ffn_pre=norm_ffn_pre,
                  norm_ffn_post=norm_ffn_post, w_in=w_in, w_out=w_out, ssm_lam_re=ssm_lam_re,
                  ssm_lam_im=ssm_lam_im, ssm_log_dt=ssm_log_dt, ssm_b_re=ssm_b_re, ssm_b_im=ssm_b_im,
                  ssm_c_re=ssm_c_re, ssm_c_im=ssm_c_im, ssm_d=ssm_d, glu_w=glu_w, glu_b=glu_b,
                  rwkv_mu=rwkv_mu, rwkv_w0=rwkv_w0, rwkv_w_up=rwkv_w_up, rwkv_a0=rwkv_a0,
                  rwkv_a_up=rwkv_a_up, rwkv_g_up=rwkv_g_up, rwkv_k_k=rwkv_k_k, rwkv_k_a=rwkv_k_a,
                  rwkv_r_k=rwkv_r_k, rwkv_ln_w=rwkv_ln_w, rwkv_ln_b=rwkv_ln_b, moe_w_grp=moe_w_grp,
                  moe_b_grp=moe_b_grp, moe_w_exp=moe_w_exp, moe_b_exp=moe_b_exp, moe_w1=moe_w1,
                  moe_w3=moe_w3, moe_w2=moe_w2)
    x2 = x.reshape(bsz * seqlen, d)
    for layer in range(ada_w.shape[0]):
        mod = _ada(c, ada_w[layer], ada_b[layer])
        x2 = _layer(x2, mod, {k: v[layer] for k, v in params.items()}, bsz, seqlen)
    return x2.reshape(bsz, seqlen, d)
```

```python
import functools
import math

import jax
import jax.numpy as jnp
from jax import lax
from jax.experimental import pallas as pl
from jax.experimental.pallas import tpu as pltpu

F32 = jnp.float32
BF16 = jnp.bfloat16

SSM_GROUP = 16
SSM_STATE = 64
S5_SLAB = 256
S5_TIME_BLOCK = 64
RWKV_HEAD = 64
RWKV_CHUNK = 64
RWKV_BATCH_PER_STEP = 2
HEADS_PER_TILE = 4
LORA_W = 64
LORA_A = 64
LORA_G = 128
N_GROUPS = 8
EXPERTS_PER_GROUP = 8
N_EXPERTS = N_GROUPS * EXPERTS_PER_GROUP
EXPERT_BLOCK = 256
RMS_EPS = 1e-6
GN_EPS = 64e-5
DECAY_SCALE = math.exp(-0.5)
LANES = 128
SMEM_INDEX_BLOCK = 1024
ROUTE_LANES = LANES
VMEM_LIMIT = 52 * 1024 * 1024


def _cparams(*sem):
    return pltpu.CompilerParams(dimension_semantics=sem, vmem_limit_bytes=VMEM_LIMIT)


def _sigmoid(x):
    return 1.0 / (1.0 + jnp.exp(-x))


def _dot(a, b):
    return jnp.dot(a, b, preferred_element_type=F32)


def _dot_nt(a, b):
    return lax.dot_general(a, b, (((1,), (1,)), ((), ())), preferred_element_type=F32)


def _dot_tn(a, b):
    return lax.dot_general(a, b, (((0,), (0,)), ((), ())), preferred_element_type=F32)


def _split_bf16(x):
    hi = x.astype(BF16)
    lo = (x - hi.astype(F32)).astype(BF16)
    return hi, lo


def _pack_pair(a, b):
    ua = lax.bitcast_convert_type(a.astype(BF16).astype(F32), jnp.uint32)
    ub = lax.bitcast_convert_type(b.astype(BF16).astype(F32), jnp.uint32)
    return ub | (ua >> 16)


def _unpack_pair(w):
    lo = lax.bitcast_convert_type(w << 16, F32)
    hi = lax.bitcast_convert_type(w & jnp.uint32(0xFFFF0000), F32)
    return lo, hi


def _store_packed(ref, val, n_rows):
    d = val.shape[1]
    slabs = d // (2 * LANES)
    for s in range(slabs):
        a = val[:, s * LANES:(s + 1) * LANES]
        b = val[:, d // 2 + s * LANES:d // 2 + (s + 1) * LANES]
        ref[pl.ds(s, n_rows, stride=slabs), :] = _pack_pair(a, b)


def _load_packed(ref, n_rows, slabs):
    return [_unpack_pair(ref[pl.ds(s, n_rows, stride=slabs), :]) for s in range(slabs)]


def _ada_kernel(c_ref, w_ref, b_ref, o_ref):
    c = c_ref[...]
    cond = c * _sigmoid(c)
    o_ref[...] = jnp.dot(cond, w_ref[...], preferred_element_type=F32,
                         precision=lax.Precision.HIGHEST) + b_ref[...]


def _ada(c, ada_w, ada_b):
    bsz, d = c.shape
    n = ada_w.shape[1]
    tn = 1024
    return pl.pallas_call(
        _ada_kernel,
        grid=(n // tn,),
        in_specs=[pl.BlockSpec((bsz, d), lambda j: (0, 0)),
                  pl.BlockSpec((d, tn), lambda j: (0, j)),
                  pl.BlockSpec((1, tn), lambda j: (0, j))],
        out_specs=pl.BlockSpec((bsz, tn), lambda j: (0, j)),
        out_shape=jax.ShapeDtypeStruct((bsz, n), F32),
        compiler_params=_cparams("arbitrary"),
        name="ada_mod",
    )(c, ada_w, ada_b.reshape(1, n))


def _rms(x, g):
    return x * lax.rsqrt(jnp.mean(x * x, axis=-1, keepdims=True) + RMS_EPS) * g


def _norm_proj_kernel(x_ref, g_ref, sc_ref, sh_ref, w_ref, o_ref, h_scr):
    @pl.when(pl.program_id(1) == 0)
    def _():
        h = _rms(x_ref[...], g_ref[...]) * (1.0 + sc_ref[0]) + sh_ref[0]
        h_scr[...] = h.astype(h_scr.dtype)

    o_ref[...] = _dot(h_scr[...], w_ref[...]).astype(o_ref.dtype)


def _norm_proj(x2, g, sc, sh, w, seqlen, tm, tn, name):
    n, d = x2.shape
    nout = w.shape[1]
    tpb = seqlen // tm
    return pl.pallas_call(
        _norm_proj_kernel,
        grid=(n // tm, nout // tn),
        in_specs=[pl.BlockSpec((tm, d), lambda i, j: (i, 0)),
                  pl.BlockSpec((1, d), lambda i, j: (0, 0)),
                  pl.BlockSpec((1, 1, d), lambda i, j: (i // tpb, 0, 0)),
                  pl.BlockSpec((1, 1, d), lambda i, j: (i // tpb, 0, 0)),
                  pl.BlockSpec((d, tn), lambda i, j: (0, j))],
        out_specs=pl.BlockSpec((tm, tn), lambda i, j: (i, j)),
        out_shape=jax.ShapeDtypeStruct((n, nout), BF16),
        scratch_shapes=[pltpu.VMEM((tm, d), BF16)],
        compiler_params=_cparams("parallel", "arbitrary"),
        name=name,
    )(x2, g.reshape(1, d), sc[:, None, :], sh[:, None, :], w)


def _s5_tables(lam_re, lam_im, log_dt, b_re, b_im, c_re, c_im):
    g, p, cg = b_re.shape
    gs = S5_SLAB // cg
    ns = g // gs
    lr = jnp.minimum(lam_re.astype(F32), -1e-4)
    li = lam_im.astype(F32)
    dt = jnp.exp(log_dt.astype(F32))[:, None]
    mag = jnp.exp(lr * dt)
    ar, ai = mag * jnp.cos(li * dt), mag * jnp.sin(li * dt)
    den = lr * lr + li * li
    qr = ((ar - 1.0) * lr + ai * li) / den
    qi = (ai * lr - (ar - 1.0) * li) / den
    br, bi = b_re.astype(F32), b_im.astype(F32)
    bbr = qr[..., None] * br - qi[..., None] * bi
    bbi = qr[..., None] * bi + qi[..., None] * br

    def in_rows(t):
        return t.reshape(ns, gs, p, cg).transpose(0, 1, 3, 2).reshape(ns, gs * cg, p).astype(BF16)

    def out_cols(t):
        return t.reshape(ns, gs, cg, p).transpose(0, 3, 1, 2).reshape(ns, p, gs * cg).astype(BF16)

    a_tab = jnp.stack([ar.reshape(ns, gs * p), ai.reshape(ns, gs * p)], axis=1)
    return (in_rows(bbr), in_rows(bbi), out_cols(c_re.astype(F32)), out_cols(-c_im.astype(F32)), a_tab)


def _gelu_tanh(x):
    return 0.5 * x * (1.0 + jnp.tanh(math.sqrt(2.0 / math.pi) * (x + 0.044715 * (x * x * x))))


def _s5_kernel(x_ref, g_ref, sc_ref, sh_ref, wu_ref, perm_ref, bre_ref, bim_ref, cre_ref, cim_ref, a_ref,
               d_ref, gw_ref, gb_ref, o_ref, b_ref, c_ref, u_scr, bscr, sscr, yscr, st_ref):
    bsz, lb, d_in = x_ref.shape
    rows = bsz * lb
    dch = wu_ref.shape[1]
    half = st_ref.shape[2] // 2
    n_p = bre_ref.shape[2]

    @pl.when(pl.program_id(0) == 0)
    def _():
        st_ref[...] = jnp.zeros_like(st_ref)
        tile_in = (lax.broadcasted_iota(jnp.int32, (n_p, half), 0)
                   == lax.broadcasted_iota(jnp.int32, (n_p, half), 1) % n_p).astype(BF16)
        tile_out = (lax.broadcasted_iota(jnp.int32, (half, n_p), 0) % n_p
                    == lax.broadcasted_iota(jnp.int32, (half, n_p), 1)).astype(BF16)
        in_mask = (lax.broadcasted_iota(jnp.int32, (S5_SLAB, half), 0) // SSM_GROUP
                   == lax.broadcasted_iota(jnp.int32, (S5_SLAB, half), 1) // n_p)
        out_mask = (lax.broadcasted_iota(jnp.int32, (half, S5_SLAB), 0) // n_p
                    == lax.broadcasted_iota(jnp.int32, (half, S5_SLAB), 1) // SSM_GROUP)
        for s in range(dch // S5_SLAB):
            b_ref[s, :, :half] = jnp.where(in_mask, _dot(bre_ref[s], tile_in), 0.0).astype(BF16)
            b_ref[s, :, half:] = jnp.where(in_mask, _dot(bim_ref[s], tile_in), 0.0).astype(BF16)
            c_ref[s, :half, :] = jnp.where(out_mask, _dot(tile_out, cre_ref[s]), 0.0).astype(BF16)
            c_ref[s, half:, :] = jnp.where(out_mask, _dot(tile_out, cim_ref[s]), 0.0).astype(BF16)

    h = (_rms(x_ref[...], g_ref[...]) * (1.0 + sc_ref[...]) + sh_ref[...]).astype(BF16)
    u_nat = _dot(h.reshape(rows, d_in), wu_ref[...]).astype(BF16)
    u_scr[...] = _dot(perm_ref[...], u_nat).astype(BF16)
    for s in range(dch // S5_SLAB):
        sl = slice(s * S5_SLAB, (s + 1) * S5_SLAB)
        us = u_scr[:, sl]
        bscr[...] = _dot(us, b_ref[s])
        a_re = a_ref[s, 0:1, :]
        a_im = a_ref[s, 1:2, :]

        def body(l, carry):
            s_r, s_i = carry
            at_l = pl.ds(pl.multiple_of(l * bsz, bsz), bsz)
            n_r = a_re * s_r - a_im * s_i + bscr[at_l, :half]
            n_i = a_re * s_i + a_im * s_r + bscr[at_l, half:]
            sscr[at_l, :half] = n_r
            sscr[at_l, half:] = n_i
            return n_r, n_i

        s_r, s_i = lax.fori_loop(0, rows // bsz, body, (st_ref[s, :, :half], st_ref[s, :, half:]), unroll=2)
        st_ref[s, :, :half] = s_r
        st_ref[s, :, half:] = s_i
        y = _dot(sscr[...].astype(BF16), c_ref[s]) + d_ref[:, sl] * us.astype(F32)
        yscr[:, sl] = _gelu_tanh(y).astype(BF16)
    y = yscr[...]
    gate = _sigmoid(_dot(y, gw_ref[...]) + gb_ref[...])
    out_tm = (y.astype(F32) * gate).astype(BF16)
    out_nat = _dot_tn(perm_ref[...], out_tm).astype(o_ref.dtype)
    o_ref[...] = out_nat.reshape(bsz, lb, dch)


def _s5_glu(x2, g, sc, sh, w_u, tabs, d_skip, glu_w, glu_b, bsz, seqlen, lb):
    b_re, b_im, c_re, c_im, a_tab = tabs
    d_in, dch = w_u.shape
    ns, _, n_half = a_tab.shape
    n_state = 2 * n_half
    rows = lb * bsz
    r_idx = jnp.arange(rows)
    perm = ((r_idx % bsz) * lb + r_idx // bsz)[:, None] == r_idx[None, :]
    full = lambda a: pl.BlockSpec(a.shape, lambda i: (0,) * a.ndim)
    args = (x2.reshape(bsz, seqlen, d_in), g.reshape(1, 1, d_in), sc[:, None, :], sh[:, None, :], w_u,
            perm.astype(BF16), b_re, b_im, c_re, c_im, a_tab,
            d_skip.astype(F32).reshape(1, dch), glu_w.astype(BF16), glu_b.astype(F32).reshape(1, dch))
    y = pl.pallas_call(
        _s5_kernel,
        grid=(seqlen // lb,),
        in_specs=[pl.BlockSpec((bsz, lb, d_in), lambda i: (0, i, 0))] + [full(a) for a in args[1:]],
        out_specs=pl.BlockSpec((bsz, lb, dch), lambda i: (0, i, 0)),
        out_shape=jax.ShapeDtypeStruct((bsz, seqlen, dch), BF16),
        scratch_shapes=[pltpu.VMEM((ns, S5_SLAB, n_state), BF16), pltpu.VMEM((ns, n_state, S5_SLAB), BF16),
                        pltpu.VMEM((rows, dch), BF16), pltpu.VMEM((rows, n_state), F32),
                        pltpu.VMEM((rows, n_state), F32), pltpu.VMEM((rows, dch), BF16),
                        pltpu.VMEM((ns, bsz, n_state), F32)],
        compiler_params=_cparams("arbitrary"),
        name="s5_mixer_glu",
    )(*args)
    return y.reshape(bsz * seqlen, dch)


def _rwkv_prep_kernel(z_ref, halo_ref, mu_ref, w0_ref, a0_ref, kk_ref, ka_ref, wa_ref, gup_ref,
                      r_out, k_out, v_out, kk_out, as_out, lw_out, g_out, *, tiles_per_seq, d_r):
    i = pl.program_id(0)
    tm = z_ref.shape[0]
    not_first = (i % tiles_per_seq != 0).astype(F32)
    row0 = lax.broadcasted_iota(jnp.int32, (tm, 1), 0) == 0

    def lerp(lo, hi):
        zc = z_ref[:, lo:hi].astype(F32)
        prev_row = halo_ref[7:8, lo:hi].astype(F32) * not_first
        shifted = jnp.where(row0, prev_row, pltpu.roll(zc, 1, 0))
        return zc + mu_ref[:, lo:hi] * (shifted - zc)

    r_out[...] = lerp(0, d_r).astype(r_out.dtype)
    v_out[...] = lerp(2 * d_r, 3 * d_r).astype(v_out.dtype)
    xwa = lerp(3 * d_r, 3 * d_r + LORA_W + LORA_A)
    lane = lax.broadcasted_iota(jnp.int32, xwa.shape, 1)
    lhs = jnp.where(lane < LORA_W, jnp.tanh(xwa), xwa).astype(BF16)
    wa = _dot(lhs, wa_ref[...])
    lw_out[...] = -DECAY_SCALE * _sigmoid(w0_ref[...] + wa[:, :d_r])
    asig = _sigmoid(a0_ref[...] + wa[:, d_r:])
    as_out[...] = asig.astype(as_out.dtype)
    k = lerp(d_r, 2 * d_r)
    kk_out[...] = (k * kk_ref[...]).astype(kk_out.dtype)
    k_out[...] = (k * (1.0 + (asig - 1.0) * ka_ref[...])).astype(k_out.dtype)
    xg = lerp(3 * d_r + LORA_W + LORA_A, 3 * d_r + LORA_W + LORA_A + LORA_G)
    g_out[...] = _dot(_sigmoid(xg).astype(BF16), gup_ref[...]).astype(g_out.dtype)


def _rwkv_prep(z, mu, w0, w_up, a0, a_up, g_up, k_k, k_a, seqlen, tm):
    n, dz = z.shape
    d_r = w0.shape[0]
    wa = jnp.zeros((LORA_W + LORA_A, 2 * d_r), F32)
    wa = wa.at[:LORA_W, :d_r].set(w_up.astype(F32)).at[LORA_W:, d_r:].set(a_up.astype(F32)).astype(BF16)
    row = lambda a: a.astype(F32).reshape(1, -1)
    full = lambda a: pl.BlockSpec(a.shape, lambda i: (0, 0))
    hb = tm // 8
    args = (z, z, row(mu), row(w0), row(a0), row(k_k), row(k_a), wa, g_up.astype(BF16))
    in_specs = [pl.BlockSpec((tm, dz), lambda i: (i, 0)),
                pl.BlockSpec((8, dz), lambda i: (jnp.maximum(i * hb - 1, 0), 0))]
    in_specs += [full(a) for a in args[2:]]
    ospec = pl.BlockSpec((tm, d_r), lambda i: (i, 0))
    bf = jax.ShapeDtypeStruct((n, d_r), BF16)
    return pl.pallas_call(
        functools.partial(_rwkv_prep_kernel, tiles_per_seq=seqlen // tm, d_r=d_r),
        grid=(n // tm,),
        in_specs=in_specs,
        out_specs=[ospec] * 7,
        out_shape=[bf, bf, bf, bf, bf, jax.ShapeDtypeStruct((n, d_r), F32), bf],
        compiler_params=_cparams("parallel"),
        name="rwkv_prep",
    )(*args)


def _rwkv_kernel(r_ref, k_ref, v_ref, kk_ref, as_ref, lw_ref, g_ref, rk_ref, lnw_ref, lnb_ref,
                 ones_ref, o_ref, s_ref):
    nb, t, d_r = r_ref.shape
    tile = HEADS_PER_TILE * RWKV_HEAD
    n_tiles = d_r // tile

    @pl.when(pl.program_id(1) == 0)
    def _():
        s_ref[...] = jnp.zeros_like(s_ref)

    ones_bd = ones_ref[...]

    def seg_sum(x):
        return _dot(x.astype(BF16), ones_bd)

    row = lax.broadcasted_iota(jnp.int32, (t, t), 0)
    col = lax.broadcasted_iota(jnp.int32, (t, t), 1)
    tri = (row >= col).astype(BF16)
    st = HEADS_PER_TILE * t
    rs = lax.broadcasted_iota(jnp.int32, (2 * st, 2 * st), 0)
    cs = lax.broadcasted_iota(jnp.int32, (2 * st, 2 * st), 1)
    t_r = rs % t
    t_c = cs % t
    keep = (t_r > t_c) | ((rs >= st) & (t_r == t_c))
    eye_w = (lax.broadcasted_iota(jnp.int32, (t, st), 0)
             == lax.broadcasted_iota(jnp.int32, (t, st), 1) % t).astype(F32)
    blk_mask = (lax.broadcasted_iota(jnp.int32, (st, st), 0) // t
                == lax.broadcasted_iota(jnp.int32, (st, st), 1) // t)
    lane = lax.broadcasted_iota(jnp.int32, (1, tile), 1)
    head_masks = [(lane >= j * RWKV_HEAD) & (lane < (j + 1) * RWKV_HEAD) for j in range(HEADS_PER_TILE)]
    bd_r = lax.broadcasted_iota(jnp.int32, (tile, tile), 0) // RWKV_HEAD
    bd_c = lax.broadcasted_iota(jnp.int32, (tile, tile), 1) // RWKV_HEAD
    bd_mask = bd_r == bd_c
    n_levels = int(math.log2(t))
    slices = [slice(hg * tile, (hg + 1) * tile) for hg in range(n_tiles)]
    units = [(bb, hg) for bb in range(nb) for hg in range(n_tiles)]
    n_u = range(len(units))

    def stack(x):
        return jnp.concatenate([jnp.where(m, x, 0.0) for m in head_masks], axis=0)

    def collapse(x):
        out = x[:t]
        for j in range(1, HEADS_PER_TILE):
            out = out + x[j * t:(j + 1) * t]
        return out

    def bf(x):
        return x.astype(BF16)

    at, qt, bt, kt, vv, em, etm, wtot, rkb = ([] for _ in range(9))
    for bb in range(nb):
        lw = lw_ref[bb]
        lw_hi, lw_lo = _split_bf16(lw)
        cum = _dot(tri, lw_hi) + _dot(tri, lw_lo)
        mid = cum[t // 2 - 1:t // 2, :]
        tot = cum[t - 1:t, :]
        e1 = jnp.exp(cum - mid)
        e2 = jnp.exp(mid - cum)
        e1p = e1 * jnp.exp(-lw)
        r = r_ref[bb].astype(F32)
        kp = k_ref[bb].astype(F32)
        kk = kk_ref[bb].astype(F32)
        kk2 = kk * kk
        kkn = kk / jnp.maximum(jnp.sqrt(jnp.concatenate([seg_sum(kk2[:, sl]) for sl in slices], axis=1)), 1e-12)
        full = dict(at=-kkn * e1p, qt=r * e1, bt=kkn * as_ref[bb].astype(F32) * e2, kt=kp * e2,
                    vv=v_ref[bb].astype(F32), em=jnp.exp(mid), etm=jnp.exp(tot - mid), wtot=jnp.exp(tot),
                    rkb=r * kp * rk_ref[...])
        for dst, key in ((at, "at"), (qt, "qt"), (bt, "bt"), (kt, "kt"), (vv, "vv"), (em, "em"),
                         (etm, "etm"), (wtot, "wtot"), (rkb, "rkb")):
            dst.extend(full[key][:, sl] for sl in slices)

    s_old = [s_ref[i] for i in n_u]
    wide = []
    for i in n_u:
        lhs = jnp.concatenate([stack(at[i]), stack(qt[i])], axis=0)
        rhs = jnp.concatenate([stack(bt[i]), stack(kt[i])], axis=0)
        a_mat = jnp.where(keep, _dot_nt(bf(lhs), bf(rhs)), 0.0)
        wide.append([collapse(a_mat[r0:r0 + st, c0:c0 + st]) for r0 in (0, st) for c0 in (0, st)])
    x_state = [_dot_nt(bf(jnp.concatenate([at[i], qt[i]], axis=0) * em[i]), bf(s_old[i]))
               for i in n_u]
    sv = [bf(stack(vv[i])) for i in n_u]
    akv = [_dot(bf(wide[i][1]), sv[i]) for i in n_u]

    def expand(x_w):
        return bf(jnp.where(blk_mask, jnp.concatenate([x_w] * HEADS_PER_TILE, axis=0), 0.0))

    p_acc = [eye_w + w[0] for w in wide]
    q_bd = [expand(w[0]) for w in wide]
    q_pow = [_dot(bf(wide[i][0]), q_bd[i]) for i in n_u]
    for lev in range(1, n_levels):
        for i in n_u:
            q_bd[i] = expand(q_pow[i])
            if lev < n_levels - 1:
                both = _dot(bf(jnp.concatenate([p_acc[i], q_pow[i]], axis=0)), q_bd[i])
                p_acc[i] = p_acc[i] + both[:t]
                q_pow[i] = both[t:]
            else:
                p_acc[i] = p_acc[i] + _dot(bf(p_acc[i]), q_bd[i])
    u_all = [_dot(bf(p_acc[i]), bf(stack(x_state[i][:t] + akv[i]))) for i in n_u]
    y_all = [x_state[i][t:]
             + _dot(bf(jnp.concatenate([wide[i][2], wide[i][3]], axis=1)),
                    jnp.concatenate([bf(stack(u_all[i])), sv[i]], axis=0)) for i in n_u]

    for i, (bb, hg) in enumerate(units):
        sl = slices[hg]
        y = y_all[i]
        uv = bf(jnp.concatenate([u_all[i], vv[i]], axis=0))
        bk_end = bf(jnp.concatenate([bt[i], kt[i]], axis=0) * etm[i])
        s_ref[i] = s_old[i] * wtot[i] + jnp.where(bd_mask, _dot_tn(uv, bk_end), 0.0)

        mean = seg_sum(y) * (1.0 / RWKV_HEAD)
        dlt = y - mean
        var = seg_sum(dlt * dlt) * (1.0 / RWKV_HEAD)
        yn = dlt * lax.rsqrt(var + GN_EPS) * lnw_ref[:, sl] + lnb_ref[:, sl]
        out = (yn + seg_sum(rkb[i]) * vv[i]) * g_ref[bb, :, sl].astype(F32)
        o_ref[bb, :, sl] = out.astype(o_ref.dtype)


def _rwkv(r, k, v, kk, asig, lw, g, r_k, ln_w, ln_b, bsz, seqlen):
    n, d_r = r.shape
    t = RWKV_CHUNK
    nb = RWKV_BATCH_PER_STEP
    nch = seqlen // t
    tile = HEADS_PER_TILE * RWKV_HEAD
    hid = jnp.arange(tile) // RWKV_HEAD
    ones_bd = (hid[:, None] == hid[None, :]).astype(BF16)
    row = lambda a: a.astype(F32).reshape(1, d_r)
    seq = lambda a: a.reshape(bsz, seqlen, d_r)
    tspec = pl.BlockSpec((nb, t, d_r), lambda b, c: (b, c, 0))
    pspec = pl.BlockSpec((1, d_r), lambda b, c: (0, 0))
    y = pl.pallas_call(
        _rwkv_kernel,
        grid=(bsz // nb, nch),
        in_specs=[tspec] * 7 + [pspec] * 3 + [pl.BlockSpec((tile, tile), lambda b, c: (0, 0))],
        out_specs=tspec,
        out_shape=jax.ShapeDtypeStruct((bsz, seqlen, d_r), BF16),
        scratch_shapes=[pltpu.VMEM((nb * (d_r // tile), tile, tile), F32)],
        compiler_params=_cparams("parallel", "arbitrary"),
        name="rwkv7_chunked",
    )(seq(r), seq(k), seq(v), seq(kk), seq(asig), seq(lw), seq(g), row(r_k), row(ln_w), row(ln_b), ones_bd)
    return y.reshape(n, d_r)


def _post_mix_kernel(ys_ref, yr_ref, wo1_ref, wo2_ref, x_ref, g1_ref, gt_ref, g2_ref, sc_ref, sh_ref,
                     wr_hi_ref, wr_lo_ref, br_ref, x1_out, h2_out, lg_out):
    mixed = _dot(ys_ref[...], wo1_ref[...]) + _dot(yr_ref[...], wo2_ref[...])
    x1 = x_ref[...] + gt_ref[0] * _rms(mixed, g1_ref[...])
    x1_out[...] = x1
    h2 = _rms(x1, g2_ref[...]) * (1.0 + sc_ref[0]) + sh_ref[0]
    _store_packed(h2_out, h2, h2.shape[0])
    hi, lo = _split_bf16(h2)
    lg_out[...] = (_dot(hi, wr_hi_ref[...]) + _dot(lo, wr_hi_ref[...]) + _dot(hi, wr_lo_ref[...])
                   + br_ref[...])


def _post_mix(ys, yr, w_out, x2, g1, gt1, g2, sc2, sh2, w_route, b_route, seqlen, tm):
    n, d = x2.shape
    ds = ys.shape[1]
    tpb = seqlen // tm
    slabs = d // (2 * LANES)
    wo = w_out.astype(BF16)
    wr_hi, wr_lo = _split_bf16(w_route)
    rows = lambda w: pl.BlockSpec((tm, w), lambda i: (i, 0))
    full = lambda a: pl.BlockSpec(a.shape, lambda i: (0,) * a.ndim)
    bat = pl.BlockSpec((1, 1, d), lambda i: (i // tpb, 0, 0))
    args = (ys, yr, wo[:ds], wo[ds:], x2, g1.reshape(1, d), gt1[:, None, :], g2.reshape(1, d),
            sc2[:, None, :], sh2[:, None, :], wr_hi, wr_lo, b_route.reshape(1, -1))
    in_specs = [rows(ds), rows(yr.shape[1]), full(args[2]), full(args[3]), rows(d), full(args[5]), bat,
                full(args[7]), bat, bat, full(wr_hi), full(wr_lo), full(args[12])]
    return pl.pallas_call(
        _post_mix_kernel,
        grid=(n // tm,),
        in_specs=in_specs,
        out_specs=[rows(d), pl.BlockSpec((tm * slabs, LANES), lambda i: (i, 0)), rows(ROUTE_LANES)],
        out_shape=[jax.ShapeDtypeStruct((n, d), F32), jax.ShapeDtypeStruct((n * slabs, LANES), jnp.uint32),
                   jax.ShapeDtypeStruct((n, ROUTE_LANES), F32)],
        compiler_params=_cparams("parallel"),
        name="out_proj_post",
    )(*args)


def _route_kernel(lg_ref, info_ref, cnt_ref, carry):
    i = pl.program_id(0)
    tm = lg_ref.shape[0]

    @pl.when(i == 0)
    def _():
        carry[...] = jnp.zeros_like(carry)

    lg = lg_ref[...]
    lane = lax.broadcasted_iota(jnp.int32, lg.shape, 1)
    lane_f = lane.astype(F32)
    neg = jnp.float32(-jnp.inf)
    big = jnp.float32(1e9)
    is_g = (lane >= N_EXPERTS) & (lane < N_EXPERTS + N_GROUPS)
    gl = jnp.where(is_g, lg, neg)
    gmax = jnp.max(gl, axis=-1, keepdims=True)
    gidx = jnp.min(jnp.where(gl == gmax, lane_f - N_EXPERTS, big), axis=-1, keepdims=True)
    p_grp = 1.0 / jnp.sum(jnp.where(is_g, jnp.exp(gl - gmax), 0.0), axis=-1, keepdims=True)
    in_grp = (lane < N_EXPERTS) & ((lane // EXPERTS_PER_GROUP).astype(F32) == gidx)
    el = jnp.where(in_grp, lg, neg)
    m1 = jnp.max(el, axis=-1, keepdims=True)
    i1 = jnp.min(jnp.where(el == m1, lane_f, big), axis=-1, keepdims=True)
    el2 = jnp.where(lane_f == i1, neg, el)
    m2 = jnp.max(el2, axis=-1, keepdims=True)
    i2 = jnp.min(jnp.where(el2 == m2, lane_f, big), axis=-1, keepdims=True)
    ex = jnp.exp(m2 - m1)
    w1 = p_grp / (1.0 + ex)
    w2 = p_grp * ex / (1.0 + ex)

    oh1 = lane_f == i1
    oh2 = lane_f == i2
    onehot = (oh1 | oh2).astype(BF16)
    rr = lax.broadcasted_iota(jnp.int32, (tm, tm), 0)
    cc = lax.broadcasted_iota(jnp.int32, (tm, tm), 1)
    before = _dot((rr > cc).astype(BF16), onehot) + carry[...]
    rank1 = jnp.sum(jnp.where(oh1, before, 0.0), axis=-1, keepdims=True)
    rank2 = jnp.sum(jnp.where(oh2, before, 0.0), axis=-1, keepdims=True)
    carry[...] = carry[...] + jnp.sum(onehot.astype(F32), axis=0, keepdims=True)
    cnt_ref[...] = carry[...]

    info = jnp.where(lane == 0, i1, 0.0)
    info = jnp.where(lane == 1, i2, info)
    info = jnp.where(lane == 2, w1, info)
    info = jnp.where(lane == 3, w2, info)
    info = jnp.where(lane == 4, rank1, info)
    info = jnp.where(lane == 5, rank2, info)
    info_ref[...] = info


def _route(logits, tm):
    n = logits.shape[0]
    return pl.pallas_call(
        _route_kernel,
        grid=(n // tm,),
        in_specs=[pl.BlockSpec((tm, ROUTE_LANES), lambda i: (i, 0))],
        out_specs=[pl.BlockSpec((tm, ROUTE_LANES), lambda i: (i, 0)),
                   pl.BlockSpec((1, ROUTE_LANES), lambda i: (0, 0))],
        out_shape=[jax.ShapeDtypeStruct((n, ROUTE_LANES), F32),
                   jax.ShapeDtypeStruct((1, ROUTE_LANES), F32)],
        scratch_shapes=[pltpu.VMEM((1, ROUTE_LANES), F32)],
        compiler_params=_cparams("arbitrary"),
        name="moe_route",
    )(logits)


def _slot_rows_kernel(info_ref, seg_ref, o_ref, *, slabs):
    info = info_ref[...]
    lane = lax.broadcasted_iota(jnp.int32, info.shape, 1)
    lane_f = lane.astype(F32)
    seg = seg_ref[...]
    d0 = jnp.sum(jnp.where(lane_f == info[:, 0:1], seg, 0.0), axis=-1, keepdims=True) + info[:, 4:5] * slabs
    d1 = jnp.sum(jnp.where(lane_f == info[:, 1:2], seg, 0.0), axis=-1, keepdims=True) + info[:, 5:6] * slabs
    o_ref[...] = jnp.where(lane == 0, d0, jnp.where(lane == 1, d1, 0.0)).astype(jnp.int32)


def _slot_rows(info, seg_row, slabs, tm):
    n = info.shape[0]
    return pl.pallas_call(
        functools.partial(_slot_rows_kernel, slabs=slabs),
        grid=(n // tm,),
        in_specs=[pl.BlockSpec((tm, ROUTE_LANES), lambda i: (i, 0)),
                  pl.BlockSpec((1, ROUTE_LANES), lambda i: (0, 0))],
        out_specs=pl.BlockSpec((tm, ROUTE_LANES), lambda i: (i, 0)),
        out_shape=jax.ShapeDtypeStruct((n, ROUTE_LANES), jnp.int32),
        compiler_params=_cparams("parallel"),
        name="moe_slot_rows",
    )(info, seg_row)


def _dispatch_kernel(d0_ref, d1_ref, h_ref, buf_in, buf_out, sem, *, slabs):
    del buf_in
    tm = h_ref.shape[0] // slabs

    def copy(t, dest_ref):
        src = pl.multiple_of(t * slabs, slabs)
        dst = pl.multiple_of(dest_ref[t], slabs)
        return pltpu.make_async_copy(h_ref.at[pl.ds(src, slabs)], buf_out.at[pl.ds(dst, slabs)], sem)

    def start(t, c):
        copy(t, d0_ref).start(priority=0)
        copy(t, d1_ref).start(priority=1)
        return c

    lax.fori_loop(0, tm, start, 0, unroll=4)
    for _ in range(2):
        pltpu.make_async_copy(h_ref, buf_out.at[pl.ds(0, tm * slabs)], sem).wait()


def _dispatch(h2p, dest_rows, cap, slabs, tm):
    n = h2p.shape[0] // slabs
    buf = jnp.zeros((cap * slabs, LANES), h2p.dtype)
    smem = pl.BlockSpec((tm,), lambda i: (i,), memory_space=pltpu.SMEM)
    return pl.pallas_call(
        functools.partial(_dispatch_kernel, slabs=slabs),
        grid=(n // tm,),
        in_specs=[smem, smem,
                  pl.BlockSpec((tm * slabs, LANES), lambda i: (i, 0)),
                  pl.BlockSpec(memory_space=pl.ANY)],
        out_specs=pl.BlockSpec(memory_space=pl.ANY),
        out_shape=jax.ShapeDtypeStruct(buf.shape, buf.dtype),
        scratch_shapes=[pltpu.SemaphoreType.DMA],
        input_output_aliases={3: 0},
        compiler_params=_cparams("arbitrary"),
        name="moe_dispatch",
    )(dest_rows[0], dest_rows[1], h2p, buf)


def _moe_kernel(na_ref, eseq_ref, epos_ref, nd_ref, x_ref, w1_hbm, w3_hbm, w2_hbm, o_ref,
                w1f, w3f, w2f, w1b, w3b, w2b, sem):
    j = pl.program_id(0)
    active = j < na_ref[0]
    pos = epos_ref[j]
    fresh = (j == 0) | (pos != epos_ref[jnp.maximum(j - 1, 0)])

    slabs = w1b.shape[0]
    half = slabs * LANES

    def weight_copies(p):
        e = eseq_ref[p]
        slot = p % 2
        return [pltpu.make_async_copy(w_hbm.at[e], stage.at[slot], sem.at[slot, i])
                for i, (w_hbm, stage) in enumerate(((w1_hbm, w1f), (w3_hbm, w3f), (w2_hbm, w2f)))]

    @pl.when(j == 0)
    def _():
        for c in weight_copies(0):
            c.start()

    @pl.when(active & fresh)
    def _():
        for c in weight_copies(pos):
            c.wait()

        @pl.when(pos + 1 < nd_ref[0])
        def _():
            for c in weight_copies(pos + 1):
                c.start()

        slot = pos % 2
        for s in range(slabs):
            for dst, src in ((w1b, w1f), (w3b, w3f)):
                dst[s, :LANES, :] = src[slot, s * LANES:(s + 1) * LANES, :].astype(BF16)
                dst[s, LANES:, :] = src[slot, half + s * LANES:half + (s + 1) * LANES, :].astype(BF16)
        w2b[...] = w2f[slot].astype(BF16)

    @pl.when(active)
    def _():
        acc1 = jnp.zeros((EXPERT_BLOCK, w1b.shape[2]), F32)
        acc3 = jnp.zeros((EXPERT_BLOCK, w1b.shape[2]), F32)
        for s, (lo, hi) in enumerate(_load_packed(x_ref, EXPERT_BLOCK, slabs)):
            lhs = jnp.concatenate([lo.astype(BF16), hi.astype(BF16)], axis=1)
            acc1 = acc1 + _dot(lhs, w1b[s])
            acc3 = acc3 + _dot(lhs, w3b[s])
        hid = (acc1 * _sigmoid(acc1)) * acc3
        _store_packed(o_ref, _dot(hid.astype(BF16), w2b[...]), EXPERT_BLOCK)


def _moe(x_buf, n_active, expert_seq, block_pos, n_used, w1, w3, w2, slabs):
    cap = x_buf.shape[0] // slabs
    d, de = w1.shape[1], w1.shape[2]
    nb = cap // EXPERT_BLOCK

    def xmap(j, na, *_):
        return (jnp.minimum(j, na[0] - 1), 0)

    xspec = pl.BlockSpec((EXPERT_BLOCK * slabs, LANES), xmap)
    hbm = pl.BlockSpec(memory_space=pl.ANY)
    grid_spec = pltpu.PrefetchScalarGridSpec(
        num_scalar_prefetch=4,
        grid=(nb,),
        in_specs=[xspec, hbm, hbm, hbm],
        out_specs=xspec,
        scratch_shapes=[pltpu.VMEM((2, d, de), F32), pltpu.VMEM((2, d, de), F32), pltpu.VMEM((2, de, d), F32),
                        pltpu.VMEM((slabs, 2 * LANES, de), BF16), pltpu.VMEM((slabs, 2 * LANES, de), BF16),
                        pltpu.VMEM((de, d), BF16), pltpu.SemaphoreType.DMA((2, 3))],
    )
    return pl.pallas_call(
        _moe_kernel,
        grid_spec=grid_spec,
        out_shape=jax.ShapeDtypeStruct(x_buf.shape, x_buf.dtype),
        input_output_aliases={4: 0},
        compiler_params=_cparams("arbitrary"),
        name="moe_experts",
    )(n_active, expert_seq, block_pos, n_used, x_buf, w1, w3, w2)


def _combine_kernel(d0_ref, d1_ref, d0n_ref, d1n_ref, y_ref, info_ref, x1_ref, g_ref, gt_ref, o_ref,
                    rows, sem, *, slabs, tiles_per_idx):
    i = pl.program_id(0)
    tm = x1_ref.shape[0]

    def fill(tile, d0, d1):
        slot = tile % 2
        off = (tile % tiles_per_idx) * tm

        def copy(t, dest_ref, k):
            src = pl.multiple_of(dest_ref[off + t], slabs)
            dst = pl.multiple_of(t * slabs, slabs)
            return pltpu.make_async_copy(y_ref.at[pl.ds(src, slabs)], rows.at[slot, k, pl.ds(dst, slabs)],
                                         sem.at[slot])

        def start(t, c):
            copy(t, d0, 0).start(priority=0)
            copy(t, d1, 1).start(priority=1)
            return c

        lax.fori_loop(0, tm, start, 0, unroll=4)

    @pl.when(i == 0)
    def _():
        fill(i, d0_ref, d1_ref)

    @pl.when(i + 1 < pl.num_programs(0))
    def _():
        fill(i + 1, d0n_ref, d1n_ref)

    slot = i % 2
    for k in range(2):
        pltpu.make_async_copy(y_ref.at[pl.ds(0, tm * slabs)], rows.at[slot, k], sem.at[slot]).wait()
    rows0 = rows.at[slot, 0]
    rows1 = rows.at[slot, 1]
    info = info_ref[...]
    w1 = info[:, 2:3]
    w2 = info[:, 3:4]
    lo_parts, hi_parts = [], []
    for (lo0, hi0), (lo1, hi1) in zip(_load_packed(rows0, tm, slabs), _load_packed(rows1, tm, slabs)):
        lo_parts.append(w1 * lo0 + w2 * lo1)
        hi_parts.append(w1 * hi0 + w2 * hi1)
    ffn = jnp.concatenate(lo_parts + hi_parts, axis=1)
    o_ref[...] = x1_ref[...] + gt_ref[0] * _rms(ffn, g_ref[...])


def _combine(y_buf, dest_rows, info, x1, g, gt2, seqlen, slabs, tm):
    n, d = x1.shape
    tpb = seqlen // tm
    idx_block = max(tm, SMEM_INDEX_BLOCK)
    per = idx_block // tm
    last = n // tm - 1
    smem = pl.BlockSpec((idx_block,), lambda i: (i // per,), memory_space=pltpu.SMEM)
    smem_next = pl.BlockSpec((idx_block,), lambda i: (jnp.minimum(i + 1, last) // per,), memory_space=pltpu.SMEM)
    return pl.pallas_call(
        functools.partial(_combine_kernel, slabs=slabs, tiles_per_idx=per),
        grid=(n // tm,),
        in_specs=[smem, smem, smem_next, smem_next,
                  pl.BlockSpec(memory_space=pl.ANY),
                  pl.BlockSpec((tm, ROUTE_LANES), lambda i: (i, 0)),
                  pl.BlockSpec((tm, d), lambda i: (i, 0)),
                  pl.BlockSpec((1, d), lambda i: (0, 0)),
                  pl.BlockSpec((1, 1, d), lambda i: (i // tpb, 0, 0))],
        out_specs=pl.BlockSpec((tm, d), lambda i: (i, 0)),
        out_shape=jax.ShapeDtypeStruct((n, d), F32),
        scratch_shapes=[pltpu.VMEM((2, 2, tm * slabs, LANES), y_buf.dtype), pltpu.SemaphoreType.DMA((2,))],
        compiler_params=_cparams("arbitrary"),
        name="moe_combine",
    )(dest_rows[0], dest_rows[1], dest_rows[0], dest_rows[1], y_buf, info, x1, g.reshape(1, d),
      gt2[:, None, :])


def _pick(n, pref):
    while n % pref:
        pref //= 2
    return pref


def _layer(x2, mod, p, bsz, seqlen):
    n, d = x2.shape
    sh1, sc1, gt1, sh2, sc2, gt2 = jnp.split(mod, 6, axis=-1)
    d_ssm = p["ssm_d"].shape[0]
    d_r = p["rwkv_w0"].shape[0]
    tm = _pick(seqlen, 1024)

    w_in = p["w_in"].astype(BF16)
    w_z = w_in[:, d_ssm:]
    dz = w_z.shape[1]
    z = _norm_proj(x2, p["norm_mix_pre"], sc1, sh1, w_z, seqlen, tm,
                   dz // 2 if (dz // 2) % 128 == 0 else dz, "in_proj_z")

    tabs = _s5_tables(p["ssm_lam_re"], p["ssm_lam_im"], p["ssm_log_dt"], p["ssm_b_re"], p["ssm_b_im"],
                      p["ssm_c_re"], p["ssm_c_im"])
    y_ssm = _s5_glu(x2, p["norm_mix_pre"], sc1, sh1, w_in[:, :d_ssm], tabs, p["ssm_d"], p["glu_w"],
                    p["glu_b"], bsz, seqlen, _pick(seqlen, S5_TIME_BLOCK))

    r, k, v, kk, asig, lw, g = _rwkv_prep(z, p["rwkv_mu"], p["rwkv_w0"], p["rwkv_w_up"], p["rwkv_a0"],
                                          p["rwkv_a_up"], p["rwkv_g_up"], p["rwkv_k_k"], p["rwkv_k_a"],
                                          seqlen, _pick(seqlen, 256))
    y_rwkv = _rwkv(r, k, v, kk, asig, lw, g, p["rwkv_r_k"], p["rwkv_ln_w"], p["rwkv_ln_b"], bsz, seqlen)

    w_route = jnp.zeros((d, ROUTE_LANES), F32)
    w_route = w_route.at[:, :N_EXPERTS].set(p["moe_w_exp"].astype(F32))
    w_route = w_route.at[:, N_EXPERTS:N_EXPERTS + N_GROUPS].set(p["moe_w_grp"].astype(F32))
    b_route = jnp.zeros((ROUTE_LANES,), F32)
    b_route = b_route.at[:N_EXPERTS].set(p["moe_b_exp"].astype(F32))
    b_route = b_route.at[N_EXPERTS:N_EXPERTS + N_GROUPS].set(p["moe_b_grp"].astype(F32))
    x1, h2p, logits = _post_mix(y_ssm, y_rwkv, p["w_out"], x2, p["norm_mix_post"], gt1, p["norm_ffn_pre"],
                                sc2, sh2, w_route, b_route, seqlen, _pick(seqlen, 512))
    slabs = d // (2 * LANES)

    info, counts = _route(logits, _pick(n, 512))
    cnt = counts[0, :N_EXPERTS].astype(jnp.int32)
    padded = (cnt + EXPERT_BLOCK - 1) // EXPERT_BLOCK * EXPERT_BLOCK
    pend = jnp.cumsum(padded)
    pstart = pend - padded
    n_blocks = -(-(2 * n) // EXPERT_BLOCK) + N_EXPERTS
    cap = n_blocks * EXPERT_BLOCK
    seg_row = jnp.zeros((1, ROUTE_LANES), F32).at[0, :N_EXPERTS].set((pstart * slabs).astype(F32))
    dest = _slot_rows(info, seg_row, slabs, _pick(n, 2048))
    dest_rows = (dest[:, 0], dest[:, 1])
    block_e = jnp.minimum(
        jnp.searchsorted(pend, jnp.arange(n_blocks, dtype=jnp.int32) * EXPERT_BLOCK, side="right"),
        N_EXPERTS - 1).astype(jnp.int32)
    n_active = (pend[-1:] // EXPERT_BLOCK).astype(jnp.int32)
    used = cnt > 0
    expert_seq = jnp.nonzero(used, size=N_EXPERTS, fill_value=0)[0].astype(jnp.int32)
    block_pos = (jnp.cumsum(used.astype(jnp.int32)) - 1)[block_e].astype(jnp.int32)
    n_used = jnp.sum(used.astype(jnp.int32)).reshape(1)

    x_buf = _dispatch(h2p, dest_rows, cap, slabs, _pick(n, 1024))
    y_buf = _moe(x_buf, n_active, expert_seq, block_pos, n_used, p["moe_w1"], p["moe_w3"], p["moe_w2"], slabs)
    return _combine(y_buf, dest_rows, info, x1, p["norm_ffn_post"], gt2, seqlen, slabs, _pick(seqlen, 512))


def kernel(x, c, ada_w, ada_b, norm_mix_pre, norm_mix_post, norm_ffn_pre, norm_ffn_post, w_in, w_out, ssm_lam_re, ssm_lam_im, ssm_log_dt, ssm_b_re, ssm_b_im, ssm_c_re, ssm_c_im, ssm_d, glu_w, glu_b, rwkv_mu, rwkv_w0, rwkv_w_up, rwkv_a0, rwkv_a_up, rwkv_g_up, rwkv_k_k, rwkv_k_a, rwkv_r_k, rwkv_ln_w, rwkv_ln_b, moe_w_grp, moe_b_grp, moe_w_exp, moe_b_exp, moe_w1, moe_w3, moe_w2):
    bsz, seqlen, d = x.shape
    params = dict(norm_mix_pre=norm_mix_pre, norm_mix_post=norm_mix_post, norm_ffn_pre=norm_ffn_pre,
                  norm_ffn_post=norm_ffn_post, w_in=w_in, w_out=w_out, ssm_lam_re=ssm_lam_re,
                  ssm_lam_im=ssm_lam_im, ssm_log_dt=ssm_log_dt, ssm_b_re=ssm_b_re, ssm_b_im=ssm_b_im,
                  ssm_c_re=ssm_c_re, ssm_c_im=ssm_c_im, ssm_d=ssm_d, glu_w=glu_w, glu_b=glu_b,
                  rwkv_mu=rwkv_mu, rwkv_w0=rwkv_w0, rwkv_w_up=rwkv_w_up, rwkv_a0=rwkv_a0,
                  rwkv_a_up=rwkv_a_up, rwkv_g_up=rwkv_g_up, rwkv_k_k=rwkv_k_k, rwkv_k_a=rwkv_k_a,
                  rwkv_r_k=rwkv_r_k, rwkv_ln_w=rwkv_ln_w, rwkv_ln_b=rwkv_ln_b, moe_w_grp=moe_w_grp,
                  moe_b_grp=moe_b_grp, moe_w_exp=moe_w_exp, moe_b_exp=moe_b_exp, moe_w1=moe_w1,
                  moe_w3=moe_w3, moe_w2=moe_w2)
    x2 = x.reshape(bsz * seqlen, d)
    for layer in range(ada_w.shape[0]):
        mod = _ada(c, ada_w[layer], ada_b[layer])
        x2 = _layer(x2, mod, {k: v[layer] for k, v in params.items()}, bsz, seqlen)
    return x2.reshape(bsz, seqlen, d)
```

```python
import functools
import math

import jax
import jax.numpy as jnp
from jax import lax
from jax.experimental import pallas as pl
from jax.experimental.pallas import tpu as pltpu

F32 = jnp.float32
BF16 = jnp.bfloat16

SSM_GROUP = 16
SSM_STATE = 64
S5_SLAB = 256
S5_TIME_BLOCK = 64
RWKV_HEAD = 64
RWKV_CHUNK = 64
RWKV_BATCH_PER_STEP = 4
HEADS_PER_TILE = 4
LORA_W = 64
LORA_A = 64
LORA_G = 128
N_GROUPS = 8
EXPERTS_PER_GROUP = 8
N_EXPERTS = N_GROUPS * EXPERTS_PER_GROUP
EXPERT_BLOCK = 256
RMS_EPS = 1e-6
GN_EPS = 64e-5
DECAY_SCALE = math.exp(-0.5)
LANES = 128
SMEM_INDEX_BLOCK = 1024
ROUTE_LANES = LANES
VMEM_LIMIT = 52 * 1024 * 1024


def _cparams(*sem):
    return pltpu.CompilerParams(dimension_semantics=sem, vmem_limit_bytes=VMEM_LIMIT)


def _sigmoid(x):
    return 1.0 / (1.0 + jnp.exp(-x))


def _dot(a, b):
    return jnp.dot(a, b, preferred_element_type=F32)


def _dot_nt(a, b):
    return lax.dot_general(a, b, (((1,), (1,)), ((), ())), preferred_element_type=F32)


def _dot_tn(a, b):
    return lax.dot_general(a, b, (((0,), (0,)), ((), ())), preferred_element_type=F32)


def _split_bf16(x):
    hi = x.astype(BF16)
    lo = (x - hi.astype(F32)).astype(BF16)
    return hi, lo


def _pack_pair(a, b):
    ua = lax.bitcast_convert_type(a.astype(BF16).astype(F32), jnp.uint32)
    ub = lax.bitcast_convert_type(b.astype(BF16).astype(F32), jnp.uint32)
    return ub | (ua >> 16)


def _unpack_pair(w):
    lo = lax.bitcast_convert_type(w << 16, F32)
    hi = lax.bitcast_convert_type(w & jnp.uint32(0xFFFF0000), F32)
    return lo, hi


def _store_packed(ref, val, n_rows):
    d = val.shape[1]
    slabs = d // (2 * LANES)
    for s in range(slabs):
        a = val[:, s * LANES:(s + 1) * LANES]
        b = val[:, d // 2 + s * LANES:d // 2 + (s + 1) * LANES]
        ref[pl.ds(s, n_rows, stride=slabs), :] = _pack_pair(a, b)


def _load_packed(ref, n_rows, slabs):
    return [_unpack_pair(ref[pl.ds(s, n_rows, stride=slabs), :]) for s in range(slabs)]


def _ada_kernel(c_ref, w_ref, b_ref, o_ref):
    c = c_ref[...]
    cond = c * _sigmoid(c)
    o_ref[...] = jnp.dot(cond, w_ref[...], preferred_element_type=F32,
                         precision=lax.Precision.HIGHEST) + b_ref[...]


def _ada(c, ada_w, ada_b):
    bsz, d = c.shape
    n = ada_w.shape[1]
    tn = 1024
    return pl.pallas_call(
        _ada_kernel,
        grid=(n // tn,),
        in_specs=[pl.BlockSpec((bsz, d), lambda j: (0, 0)),
                  pl.BlockSpec((d, tn), lambda j: (0, j)),
                  pl.BlockSpec((1, tn), lambda j: (0, j))],
        out_specs=pl.BlockSpec((bsz, tn), lambda j: (0, j)),
        out_shape=jax.ShapeDtypeStruct((bsz, n), F32),
        compiler_params=_cparams("arbitrary"),
        name="ada_mod",
    )(c, ada_w, ada_b.reshape(1, n))


def _rms(x, g):
    return x * lax.rsqrt(jnp.mean(x * x, axis=-1, keepdims=True) + RMS_EPS) * g


def _norm_proj_kernel(x_ref, g_ref, sc_ref, sh_ref, w_ref, o_ref, h_scr):
    @pl.when(pl.program_id(1) == 0)
    def _():
        h = _rms(x_ref[...], g_ref[...]) * (1.0 + sc_ref[0]) + sh_ref[0]
        h_scr[...] = h.astype(h_scr.dtype)

    o_ref[...] = _dot(h_scr[...], w_ref[...]).astype(o_ref.dtype)


def _norm_proj(x2, g, sc, sh, w, seqlen, tm, tn, name):
    n, d = x2.shape
    nout = w.shape[1]
    tpb = seqlen // tm
    return pl.pallas_call(
        _norm_proj_kernel,
        grid=(n // tm, nout // tn),
        in_specs=[pl.BlockSpec((tm, d), lambda i, j: (i, 0)),
                  pl.BlockSpec((1, d), lambda i, j: (0, 0)),
                  pl.BlockSpec((1, 1, d), lambda i, j: (i // tpb, 0, 0)),
                  pl.BlockSpec((1, 1, d), lambda i, j: (i // tpb, 0, 0)),
                  pl.BlockSpec((d, tn), lambda i, j: (0, j))],
        out_specs=pl.BlockSpec((tm, tn), lambda i, j: (i, j)),
        out_shape=jax.ShapeDtypeStruct((n, nout), BF16),
        scratch_shapes=[pltpu.VMEM((tm, d), BF16)],
        compiler_params=_cparams("parallel", "arbitrary"),
        name=name,
    )(x2, g.reshape(1, d), sc[:, None, :], sh[:, None, :], w)


def _s5_tables(lam_re, lam_im, log_dt, b_re, b_im, c_re, c_im):
    g, p, cg = b_re.shape
    gs = S5_SLAB // cg
    ns = g // gs
    lr = jnp.minimum(lam_re.astype(F32), -1e-4)
    li = lam_im.astype(F32)
    dt = jnp.exp(log_dt.astype(F32))[:, None]
    mag = jnp.exp(lr * dt)
    ar, ai = mag * jnp.cos(li * dt), mag * jnp.sin(li * dt)
    den = lr * lr + li * li
    qr = ((ar - 1.0) * lr + ai * li) / den
    qi = (ai * lr - (ar - 1.0) * li) / den
    br, bi = b_re.astype(F32), b_im.astype(F32)
    bbr = qr[..., None] * br - qi[..., None] * bi
    bbi = qr[..., None] * bi + qi[..., None] * br

    def in_rows(t):
        return t.reshape(ns, gs, p, cg).transpose(0, 1, 3, 2).reshape(ns, gs * cg, p).astype(BF16)

    def out_cols(t):
        return t.reshape(ns, gs, cg, p).transpose(0, 3, 1, 2).reshape(ns, p, gs * cg).astype(BF16)

    a_tab = jnp.stack([ar.reshape(ns, gs * p), ai.reshape(ns, gs * p)], axis=1)
    return (in_rows(bbr), in_rows(bbi), out_cols(c_re.astype(F32)), out_cols(-c_im.astype(F32)), a_tab)


def _gelu_tanh(x):
    return 0.5 * x * (1.0 + jnp.tanh(math.sqrt(2.0 / math.pi) * (x + 0.044715 * (x * x * x))))


def _s5_kernel(x_ref, g_ref, sc_ref, sh_ref, wu_ref, perm_ref, bre_ref, bim_ref, cre_ref, cim_ref, a_ref,
               d_ref, gw_ref, gb_ref, o_ref, b_ref, c_ref, u_scr, bscr, sscr, yscr, st_ref):
    bsz, lb, d_in = x_ref.shape
    rows = bsz * lb
    dch = wu_ref.shape[1]
    half = st_ref.shape[2] // 2
    n_p = bre_ref.shape[2]

    @pl.when(pl.program_id(0) == 0)
    def _():
        st_ref[...] = jnp.zeros_like(st_ref)
        tile_in = (lax.broadcasted_iota(jnp.int32, (n_p, half), 0)
                   == lax.broadcasted_iota(jnp.int32, (n_p, half), 1) % n_p).astype(BF16)
        tile_out = (lax.broadcasted_iota(jnp.int32, (half, n_p), 0) % n_p
                    == lax.broadcasted_iota(jnp.int32, (half, n_p), 1)).astype(BF16)
        in_mask = (lax.broadcasted_iota(jnp.int32, (S5_SLAB, half), 0) // SSM_GROUP
                   == lax.broadcasted_iota(jnp.int32, (S5_SLAB, half), 1) // n_p)
        out_mask = (lax.broadcasted_iota(jnp.int32, (half, S5_SLAB), 0) // n_p
                    == lax.broadcasted_iota(jnp.int32, (half, S5_SLAB), 1) // SSM_GROUP)
        for s in range(dch // S5_SLAB):
            b_ref[s, :, :half] = jnp.where(in_mask, _dot(bre_ref[s], tile_in), 0.0).astype(BF16)
            b_ref[s, :, half:] = jnp.where(in_mask, _dot(bim_ref[s], tile_in), 0.0).astype(BF16)
            c_ref[s, :half, :] = jnp.where(out_mask, _dot(tile_out, cre_ref[s]), 0.0).astype(BF16)
            c_ref[s, half:, :] = jnp.where(out_mask, _dot(tile_out, cim_ref[s]), 0.0).astype(BF16)

    h = (_rms(x_ref[...], g_ref[...]) * (1.0 + sc_ref[...]) + sh_ref[...]).astype(BF16)
    u_nat = _dot(h.reshape(rows, d_in), wu_ref[...]).astype(BF16)
    u_scr[...] = _dot(perm_ref[...], u_nat).astype(BF16)
    n_slab = dch // S5_SLAB
    slab = lambda s: slice(s * S5_SLAB, (s + 1) * S5_SLAB)

    def project_in(s):
        bscr[s % 2] = _dot(u_scr[:, slab(s)], b_ref[s])

    project_in(0)
    for s in range(n_slab):
        if s + 1 < n_slab:
            project_in(s + 1)
        buf = s % 2
        a_re = a_ref[s, 0:1, :]
        a_im = a_ref[s, 1:2, :]
        s_r = st_ref[s, :, :half]
        s_i = st_ref[s, :, half:]
        for l in range(rows // bsz):
            at_l = slice(l * bsz, (l + 1) * bsz)
            s_r, s_i = (a_re * s_r - a_im * s_i + bscr[buf, at_l, :half],
                        a_re * s_i + a_im * s_r + bscr[buf, at_l, half:])
            sscr[buf, at_l, :half] = s_r
            sscr[buf, at_l, half:] = s_i
        st_ref[s, :, :half] = s_r
        st_ref[s, :, half:] = s_i
        y = _dot(sscr[buf].astype(BF16), c_ref[s]) + d_ref[:, slab(s)] * u_scr[:, slab(s)].astype(F32)
        yscr[:, slab(s)] = _gelu_tanh(y).astype(BF16)
    y = yscr[...]
    gate = _sigmoid(_dot(y, gw_ref[...]) + gb_ref[...])
    out_tm = (y.astype(F32) * gate).astype(BF16)
    out_nat = _dot_tn(perm_ref[...], out_tm).astype(o_ref.dtype)
    o_ref[...] = out_nat.reshape(bsz, lb, dch)


def _s5_glu(x2, g, sc, sh, w_u, tabs, d_skip, glu_w, glu_b, bsz, seqlen, lb):
    b_re, b_im, c_re, c_im, a_tab = tabs
    d_in, dch = w_u.shape
    ns, _, n_half = a_tab.shape
    n_state = 2 * n_half
    rows = lb * bsz
    r_idx = jnp.arange(rows)
    perm = ((r_idx % bsz) * lb + r_idx // bsz)[:, None] == r_idx[None, :]
    full = lambda a: pl.BlockSpec(a.shape, lambda i: (0,) * a.ndim, pipeline_mode=pl.Buffered(1))
    args = (x2.reshape(bsz, seqlen, d_in), g.reshape(1, 1, d_in), sc[:, None, :], sh[:, None, :], w_u,
            perm.astype(BF16), b_re, b_im, c_re, c_im, a_tab,
            d_skip.astype(F32).reshape(1, dch), glu_w.astype(BF16), glu_b.astype(F32).reshape(1, dch))
    y = pl.pallas_call(
        _s5_kernel,
        grid=(seqlen // lb,),
        in_specs=[pl.BlockSpec((bsz, lb, d_in), lambda i: (0, i, 0))] + [full(a) for a in args[1:]],
        out_specs=pl.BlockSpec((bsz, lb, dch), lambda i: (0, i, 0)),
        out_shape=jax.ShapeDtypeStruct((bsz, seqlen, dch), BF16),
        scratch_shapes=[pltpu.VMEM((ns, S5_SLAB, n_state), BF16), pltpu.VMEM((ns, n_state, S5_SLAB), BF16),
                        pltpu.VMEM((rows, dch), BF16), pltpu.VMEM((2, rows, n_state), F32),
                        pltpu.VMEM((2, rows, n_state), F32), pltpu.VMEM((rows, dch), BF16),
                        pltpu.VMEM((ns, bsz, n_state), F32)],
        compiler_params=_cparams("arbitrary"),
        name="s5_mixer_glu",
    )(*args)
    return y.reshape(bsz * seqlen, dch)


def _rwkv_prep_kernel(z_ref, halo_ref, mu_ref, w0_ref, a0_ref, kk_ref, ka_ref, wa_ref, gup_ref,
                      r_out, k_out, v_out, kk_out, as_out, lw_out, g_out, *, tiles_per_seq, d_r):
    i = pl.program_id(0)
    tm = z_ref.shape[0]
    not_first = (i % tiles_per_seq != 0).astype(F32)
    row0 = lax.broadcasted_iota(jnp.int32, (tm, 1), 0) == 0

    def lerp(lo, hi):
        zc = z_ref[:, lo:hi].astype(F32)
        prev_row = halo_ref[7:8, lo:hi].astype(F32) * not_first
        shifted = jnp.where(row0, prev_row, pltpu.roll(zc, 1, 0))
        return zc + mu_ref[:, lo:hi] * (shifted - zc)

    r_out[...] = lerp(0, d_r).astype(r_out.dtype)
    v_out[...] = lerp(2 * d_r, 3 * d_r).astype(v_out.dtype)
    xwa = lerp(3 * d_r, 3 * d_r + LORA_W + LORA_A)
    lane = lax.broadcasted_iota(jnp.int32, xwa.shape, 1)
    lhs = jnp.where(lane < LORA_W, jnp.tanh(xwa), xwa).astype(BF16)
    wa = _dot(lhs, wa_ref[...])
    lw_out[...] = -DECAY_SCALE * _sigmoid(w0_ref[...] + wa[:, :d_r])
    asig = _sigmoid(a0_ref[...] + wa[:, d_r:])
    as_out[...] = asig.astype(as_out.dtype)
    k = lerp(d_r, 2 * d_r)
    kk_out[...] = (k * kk_ref[...]).astype(kk_out.dtype)
    k_out[...] = (k * (1.0 + (asig - 1.0) * ka_ref[...])).astype(k_out.dtype)
    xg = lerp(3 * d_r + LORA_W + LORA_A, 3 * d_r + LORA_W + LORA_A + LORA_G)
    g_out[...] = _dot(_sigmoid(xg).astype(BF16), gup_ref[...]).astype(g_out.dtype)


def _rwkv_prep(z, mu, w0, w_up, a0, a_up, g_up, k_k, k_a, seqlen, tm):
    n, dz = z.shape
    d_r = w0.shape[0]
    wa = jnp.zeros((LORA_W + LORA_A, 2 * d_r), F32)
    wa = wa.at[:LORA_W, :d_r].set(w_up.astype(F32)).at[LORA_W:, d_r:].set(a_up.astype(F32)).astype(BF16)
    row = lambda a: a.astype(F32).reshape(1, -1)
    full = lambda a: pl.BlockSpec(a.shape, lambda i: (0, 0))
    hb = tm // 8
    args = (z, z, row(mu), row(w0), row(a0), row(k_k), row(k_a), wa, g_up.astype(BF16))
    in_specs = [pl.BlockSpec((tm, dz), lambda i: (i, 0)),
                pl.BlockSpec((8, dz), lambda i: (jnp.maximum(i * hb - 1, 0), 0))]
    in_specs += [full(a) for a in args[2:]]
    ospec = pl.BlockSpec((tm, d_r), lambda i: (i, 0))
    bf = jax.ShapeDtypeStruct((n, d_r), BF16)
    return pl.pallas_call(
        functools.partial(_rwkv_prep_kernel, tiles_per_seq=seqlen // tm, d_r=d_r),
        grid=(n // tm,),
        in_specs=in_specs,
        out_specs=[ospec] * 7,
        out_shape=[bf, bf, bf, bf, bf, jax.ShapeDtypeStruct((n, d_r), F32), bf],
        compiler_params=_cparams("parallel"),
        name="rwkv_prep",
    )(*args)


def _rwkv_kernel(r_ref, k_ref, v_ref, kk_ref, as_ref, lw_ref, g_ref, rk_ref, lnw_ref, lnb_ref,
                 ones_ref, o_ref, s_ref):
    nb, t, d_r = r_ref.shape
    tile = HEADS_PER_TILE * RWKV_HEAD
    n_tiles = d_r // tile

    @pl.when(pl.program_id(1) == 0)
    def _():
        s_ref[...] = jnp.zeros_like(s_ref)

    ones_bd = ones_ref[...]

    def seg_sum(x):
        return _dot(x.astype(BF16), ones_bd)

    row = lax.broadcasted_iota(jnp.int32, (t, t), 0)
    col = lax.broadcasted_iota(jnp.int32, (t, t), 1)
    tri = (row >= col).astype(BF16)
    st = HEADS_PER_TILE * t
    rs = lax.broadcasted_iota(jnp.int32, (2 * st, 2 * st), 0)
    cs = lax.broadcasted_iota(jnp.int32, (2 * st, 2 * st), 1)
    t_r = rs % t
    t_c = cs % t
    keep = (t_r > t_c) | ((rs >= st) & (t_r == t_c))
    eye_w = (lax.broadcasted_iota(jnp.int32, (t, st), 0)
             == lax.broadcasted_iota(jnp.int32, (t, st), 1) % t).astype(F32)
    blk_mask = (lax.broadcasted_iota(jnp.int32, (st, st), 0) // t
                == lax.broadcasted_iota(jnp.int32, (st, st), 1) // t)
    lane = lax.broadcasted_iota(jnp.int32, (1, tile), 1)
    head_masks = [(lane >= j * RWKV_HEAD) & (lane < (j + 1) * RWKV_HEAD) for j in range(HEADS_PER_TILE)]
    bd_r = lax.broadcasted_iota(jnp.int32, (tile, tile), 0) // RWKV_HEAD
    bd_c = lax.broadcasted_iota(jnp.int32, (tile, tile), 1) // RWKV_HEAD
    bd_mask = bd_r == bd_c
    n_levels = int(math.log2(t))
    slices = [slice(hg * tile, (hg + 1) * tile) for hg in range(n_tiles)]
    units = [(bb, hg) for bb in range(nb) for hg in range(n_tiles)]
    n_u = range(len(units))

    def stack(x):
        return jnp.concatenate([jnp.where(m, x, 0.0) for m in head_masks], axis=0)

    def collapse(x):
        out = x[:t]
        for j in range(1, HEADS_PER_TILE):
            out = out + x[j * t:(j + 1) * t]
        return out

    def bf(x):
        return x.astype(BF16)

    at, qt, bt, kt, vv, em, etm, wtot, rkb = ([] for _ in range(9))
    for bb in range(nb):
        lw = lw_ref[bb]
        lw_hi, lw_lo = _split_bf16(lw)
        cum = _dot(tri, lw_hi) + _dot(tri, lw_lo)
        mid = cum[t // 2 - 1:t // 2, :]
        tot = cum[t - 1:t, :]
        e1 = jnp.exp(cum - mid)
        e2 = jnp.exp(mid - cum)
        e1p = e1 * jnp.exp(-lw)
        r = r_ref[bb].astype(F32)
        kp = k_ref[bb].astype(F32)
        kk = kk_ref[bb].astype(F32)
        kk2 = kk * kk
        kkn = kk / jnp.maximum(jnp.sqrt(jnp.concatenate([seg_sum(kk2[:, sl]) for sl in slices], axis=1)), 1e-12)
        full = dict(at=-kkn * e1p, qt=r * e1, bt=kkn * as_ref[bb].astype(F32) * e2, kt=kp * e2,
                    vv=v_ref[bb].astype(F32), em=jnp.exp(mid), etm=jnp.exp(tot - mid), wtot=jnp.exp(tot),
                    rkb=r * kp * rk_ref[...])
        for dst, key in ((at, "at"), (qt, "qt"), (bt, "bt"), (kt, "kt"), (vv, "vv"), (em, "em"),
                         (etm, "etm"), (wtot, "wtot"), (rkb, "rkb")):
            dst.extend(full[key][:, sl] for sl in slices)

    s_old = [s_ref[i] for i in n_u]
    wide = []
    for i in n_u:
        lhs = jnp.concatenate([stack(at[i]), stack(qt[i])], axis=0)
        rhs = jnp.concatenate([stack(bt[i]), stack(kt[i])], axis=0)
        a_mat = jnp.where(keep, _dot_nt(bf(lhs), bf(rhs)), 0.0)
        wide.append([collapse(a_mat[r0:r0 + st, c0:c0 + st]) for r0 in (0, st) for c0 in (0, st)])
    x_state = [_dot_nt(bf(jnp.concatenate([at[i], qt[i]], axis=0) * em[i]), bf(s_old[i]))
               for i in n_u]
    sv = [bf(stack(vv[i])) for i in n_u]
    akv = [_dot(bf(wide[i][1]), sv[i]) for i in n_u]

    def expand(x_w):
        return bf(jnp.where(blk_mask, jnp.concatenate([x_w] * HEADS_PER_TILE, axis=0), 0.0))

    p_acc = [eye_w + w[0] for w in wide]
    q_bd = [expand(w[0]) for w in wide]
    q_pow = [_dot(bf(wide[i][0]), q_bd[i]) for i in n_u]
    for lev in range(1, n_levels):
        for i in n_u:
            q_bd[i] = expand(q_pow[i])
            if lev < n_levels - 1:
                both = _dot(bf(jnp.concatenate([p_acc[i], q_pow[i]], axis=0)), q_bd[i])
                p_acc[i] = p_acc[i] + both[:t]
                q_pow[i] = both[t:]
            else:
                p_acc[i] = p_acc[i] + _dot(bf(p_acc[i]), q_bd[i])
    u_all = [_dot(bf(p_acc[i]), bf(stack(x_state[i][:t] + akv[i]))) for i in n_u]
    y_all = [x_state[i][t:]
             + _dot(bf(jnp.concatenate([wide[i][2], wide[i][3]], axis=1)),
                    jnp.concatenate([bf(stack(u_all[i])), sv[i]], axis=0)) for i in n_u]

    for i, (bb, hg) in enumerate(units):
        sl = slices[hg]
        y = y_all[i]
        uv = bf(jnp.concatenate([u_all[i], vv[i]], axis=0))
        bk_end = bf(jnp.concatenate([bt[i], kt[i]], axis=0) * etm[i])
        s_ref[i] = s_old[i] * wtot[i] + jnp.where(bd_mask, _dot_tn(uv, bk_end), 0.0)

        mean = seg_sum(y) * (1.0 / RWKV_HEAD)
        dlt = y - mean
        var = seg_sum(dlt * dlt) * (1.0 / RWKV_HEAD)
        yn = dlt * lax.rsqrt(var + GN_EPS) * lnw_ref[:, sl] + lnb_ref[:, sl]
        out = (yn + seg_sum(rkb[i]) * vv[i]) * g_ref[bb, :, sl].astype(F32)
        o_ref[bb, :, sl] = out.astype(o_ref.dtype)


def _rwkv(r, k, v, kk, asig, lw, g, r_k, ln_w, ln_b, bsz, seqlen):
    n, d_r = r.shape
    t = RWKV_CHUNK
    nb = RWKV_BATCH_PER_STEP
    nch = seqlen // t
    tile = HEADS_PER_TILE * RWKV_HEAD
    hid = jnp.arange(tile) // RWKV_HEAD
    ones_bd = (hid[:, None] == hid[None, :]).astype(BF16)
    row = lambda a: a.astype(F32).reshape(1, d_r)
    seq = lambda a: a.reshape(bsz, seqlen, d_r)
    tspec = pl.BlockSpec((nb, t, d_r), lambda b, c: (b, c, 0))
    pspec = pl.BlockSpec((1, d_r), lambda b, c: (0, 0))
    y = pl.pallas_call(
        _rwkv_kernel,
        grid=(bsz // nb, nch),
        in_specs=[tspec] * 7 + [pspec] * 3 + [pl.BlockSpec((tile, tile), lambda b, c: (0, 0))],
        out_specs=tspec,
        out_shape=jax.ShapeDtypeStruct((bsz, seqlen, d_r), BF16),
        scratch_shapes=[pltpu.VMEM((nb * (d_r // tile), tile, tile), F32)],
        compiler_params=_cparams("parallel", "arbitrary"),
        name="rwkv7_chunked",
    )(seq(r), seq(k), seq(v), seq(kk), seq(asig), seq(lw), seq(g), row(r_k), row(ln_w), row(ln_b), ones_bd)
    return y.reshape(n, d_r)


def _post_mix_kernel(ys_ref, yr_ref, wo1_ref, wo2_ref, x_ref, g1_ref, gt_ref, g2_ref, sc_ref, sh_ref,
                     wr_hi_ref, wr_lo_ref, br_ref, x1_out, h2_out, lg_out):
    mixed = _dot(ys_ref[...], wo1_ref[...]) + _dot(yr_ref[...], wo2_ref[...])
    x1 = x_ref[...] + gt_ref[0] * _rms(mixed, g1_ref[...])
    x1_out[...] = x1
    h2 = _rms(x1, g2_ref[...]) * (1.0 + sc_ref[0]) + sh_ref[0]
    _store_packed(h2_out, h2, h2.shape[0])
    hi, lo = _split_bf16(h2)
    lg_out[...] = (_dot(hi, wr_hi_ref[...]) + _dot(lo, wr_hi_ref[...]) + _dot(hi, wr_lo_ref[...])
                   + br_ref[...])


def _post_mix(ys, yr, w_out, x2, g1, gt1, g2, sc2, sh2, w_route, b_route, seqlen, tm):
    n, d = x2.shape
    ds = ys.shape[1]
    tpb = seqlen // tm
    slabs = d // (2 * LANES)
    wo = w_out.astype(BF16)
    wr_hi, wr_lo = _split_bf16(w_route)
    rows = lambda w: pl.BlockSpec((tm, w), lambda i: (i, 0))
    full = lambda a: pl.BlockSpec(a.shape, lambda i: (0,) * a.ndim)
    bat = pl.BlockSpec((1, 1, d), lambda i: (i // tpb, 0, 0))
    args = (ys, yr, wo[:ds], wo[ds:], x2, g1.reshape(1, d), gt1[:, None, :], g2.reshape(1, d),
            sc2[:, None, :], sh2[:, None, :], wr_hi, wr_lo, b_route.reshape(1, -1))
    in_specs = [rows(ds), rows(yr.shape[1]), full(args[2]), full(args[3]), rows(d), full(args[5]), bat,
                full(args[7]), bat, bat, full(wr_hi), full(wr_lo), full(args[12])]
    return pl.pallas_call(
        _post_mix_kernel,
        grid=(n // tm,),
        in_specs=in_specs,
        out_specs=[rows(d), pl.BlockSpec((tm * slabs, LANES), lambda i: (i, 0)), rows(ROUTE_LANES)],
        out_shape=[jax.ShapeDtypeStruct((n, d), F32), jax.ShapeDtypeStruct((n * slabs, LANES), jnp.uint32),
                   jax.ShapeDtypeStruct((n, ROUTE_LANES), F32)],
        compiler_params=_cparams("parallel"),
        name="out_proj_post",
    )(*args)


def _route_kernel(lg_ref, info_ref, cnt_ref, carry):
    i = pl.program_id(0)
    tm = lg_ref.shape[0]

    @pl.when(i == 0)
    def _():
        carry[...] = jnp.zeros_like(carry)

    lg = lg_ref[...]
    lane = lax.broadcasted_iota(jnp.int32, lg.shape, 1)
    lane_f = lane.astype(F32)
    neg = jnp.float32(-jnp.inf)
    big = jnp.float32(1e9)
    is_g = (lane >= N_EXPERTS) & (lane < N_EXPERTS + N_GROUPS)
    gl = jnp.where(is_g, lg, neg)
    gmax = jnp.max(gl, axis=-1, keepdims=True)
    gidx = jnp.min(jnp.where(gl == gmax, lane_f - N_EXPERTS, big), axis=-1, keepdims=True)
    p_grp = 1.0 / jnp.sum(jnp.where(is_g, jnp.exp(gl - gmax), 0.0), axis=-1, keepdims=True)
    in_grp = (lane < N_EXPERTS) & ((lane // EXPERTS_PER_GROUP).astype(F32) == gidx)
    el = jnp.where(in_grp, lg, neg)
    m1 = jnp.max(el, axis=-1, keepdims=True)
    i1 = jnp.min(jnp.where(el == m1, lane_f, big), axis=-1, keepdims=True)
    el2 = jnp.where(lane_f == i1, neg, el)
    m2 = jnp.max(el2, axis=-1, keepdims=True)
    i2 = jnp.min(jnp.where(el2 == m2, lane_f, big), axis=-1, keepdims=True)
    ex = jnp.exp(m2 - m1)
    w1 = p_grp / (1.0 + ex)
    w2 = p_grp * ex / (1.0 + ex)

    oh1 = lane_f == i1
    oh2 = lane_f == i2
    onehot = (oh1 | oh2).astype(BF16)
    rr = lax.broadcasted_iota(jnp.int32, (tm, tm), 0)
    cc = lax.broadcasted_iota(jnp.int32, (tm, tm), 1)
    before = _dot((rr > cc).astype(BF16), onehot) + carry[...]
    rank1 = jnp.sum(jnp.where(oh1, before, 0.0), axis=-1, keepdims=True)
    rank2 = jnp.sum(jnp.where(oh2, before, 0.0), axis=-1, keepdims=True)
    carry[...] = carry[...] + jnp.sum(onehot.astype(F32), axis=0, keepdims=True)
    cnt_ref[...] = carry[...]

    info = jnp.where(lane == 0, i1, 0.0)
    info = jnp.where(lane == 1, i2, info)
    info = jnp.where(lane == 2, w1, info)
    info = jnp.where(lane == 3, w2, info)
    info = jnp.where(lane == 4, rank1, info)
    info = jnp.where(lane == 5, rank2, info)
    info_ref[...] = info


def _route(logits, tm):
    n = logits.shape[0]
    return pl.pallas_call(
        _route_kernel,
        grid=(n // tm,),
        in_specs=[pl.BlockSpec((tm, ROUTE_LANES), lambda i: (i, 0))],
        out_specs=[pl.BlockSpec((tm, ROUTE_LANES), lambda i: (i, 0)),
                   pl.BlockSpec((1, ROUTE_LANES), lambda i: (0, 0))],
        out_shape=[jax.ShapeDtypeStruct((n, ROUTE_LANES), F32),
                   jax.ShapeDtypeStruct((1, ROUTE_LANES), F32)],
        scratch_shapes=[pltpu.VMEM((1, ROUTE_LANES), F32)],
        compiler_params=_cparams("arbitrary"),
        name="moe_route",
    )(logits)


def _slot_rows_kernel(info_ref, seg_ref, o_ref, *, slabs):
    info = info_ref[...]
    lane = lax.broadcasted_iota(jnp.int32, info.shape, 1)
    lane_f = lane.astype(F32)
    seg = seg_ref[...]
    d0 = jnp.sum(jnp.where(lane_f == info[:, 0:1], seg, 0.0), axis=-1, keepdims=True) + info[:, 4:5] * slabs
    d1 = jnp.sum(jnp.where(lane_f == info[:, 1:2], seg, 0.0), axis=-1, keepdims=True) + info[:, 5:6] * slabs
    o_ref[...] = jnp.where(lane == 0, d0, jnp.where(lane == 1, d1, 0.0)).astype(jnp.int32)


def _slot_rows(info, seg_row, slabs, tm):
    n = info.shape[0]
    return pl.pallas_call(
        functools.partial(_slot_rows_kernel, slabs=slabs),
        grid=(n // tm,),
        in_specs=[pl.BlockSpec((tm, ROUTE_LANES), lambda i: (i, 0)),
                  pl.BlockSpec((1, ROUTE_LANES), lambda i: (0, 0))],
        out_specs=pl.BlockSpec((tm, ROUTE_LANES), lambda i: (i, 0)),
        out_shape=jax.ShapeDtypeStruct((n, ROUTE_LANES), jnp.int32),
        compiler_params=_cparams("parallel"),
        name="moe_slot_rows",
    )(info, seg_row)


def _dispatch_kernel(d0_ref, d1_ref, h_ref, buf_in, buf_out, sem, *, slabs):
    del buf_in
    tm = h_ref.shape[0] // slabs

    def copy(t, dest_ref):
        src = pl.multiple_of(t * slabs, slabs)
        dst = pl.multiple_of(dest_ref[t], slabs)
        return pltpu.make_async_copy(h_ref.at[pl.ds(src, slabs)], buf_out.at[pl.ds(dst, slabs)], sem)

    def start(t, c):
        copy(t, d0_ref).start(priority=0)
        copy(t, d1_ref).start(priority=1)
        return c

    lax.fori_loop(0, tm, start, 0, unroll=4)
    for _ in range(2):
        pltpu.make_async_copy(h_ref, buf_out.at[pl.ds(0, tm * slabs)], sem).wait()


def _dispatch(h2p, dest_rows, cap, slabs, tm):
    n = h2p.shape[0] // slabs
    buf = jnp.zeros((cap * slabs, LANES), h2p.dtype)
    smem = pl.BlockSpec((tm,), lambda i: (i,), memory_space=pltpu.SMEM)
    return pl.pallas_call(
        functools.partial(_dispatch_kernel, slabs=slabs),
        grid=(n // tm,),
        in_specs=[smem, smem,
                  pl.BlockSpec((tm * slabs, LANES), lambda i: (i, 0)),
                  pl.BlockSpec(memory_space=pl.ANY)],
        out_specs=pl.BlockSpec(memory_space=pl.ANY),
        out_shape=jax.ShapeDtypeStruct(buf.shape, buf.dtype),
        scratch_shapes=[pltpu.SemaphoreType.DMA],
        input_output_aliases={3: 0},
        compiler_params=_cparams("arbitrary"),
        name="moe_dispatch",
    )(dest_rows[0], dest_rows[1], h2p, buf)


def _moe_kernel(na_ref, eseq_ref, epos_ref, nd_ref, x_ref, w1_hbm, w3_hbm, w2_hbm, o_ref,
                w1f, w3f, w2f, w1b, w3b, w2b, sem):
    j = pl.program_id(0)
    active = j < na_ref[0]
    pos = epos_ref[j]
    fresh = (j == 0) | (pos != epos_ref[jnp.maximum(j - 1, 0)])

    slabs = w1b.shape[0]
    half = slabs * LANES

    def weight_copies(p):
        e = eseq_ref[p]
        slot = p % 2
        return [pltpu.make_async_copy(w_hbm.at[e], stage.at[slot], sem.at[slot, i])
                for i, (w_hbm, stage) in enumerate(((w1_hbm, w1f), (w3_hbm, w3f), (w2_hbm, w2f)))]

    @pl.when(j == 0)
    def _():
        for c in weight_copies(0):
            c.start()

    @pl.when(active & fresh)
    def _():
        for c in weight_copies(pos):
            c.wait()

        @pl.when(pos + 1 < nd_ref[0])
        def _():
            for c in weight_copies(pos + 1):
                c.start()

        slot = pos % 2
        for s in range(slabs):
            for dst, src in ((w1b, w1f), (w3b, w3f)):
                dst[s, :LANES, :] = src[slot, s * LANES:(s + 1) * LANES, :].astype(BF16)
                dst[s, LANES:, :] = src[slot, half + s * LANES:half + (s + 1) * LANES, :].astype(BF16)
        w2b[...] = w2f[slot].astype(BF16)

    @pl.when(active)
    def _():
        acc1 = jnp.zeros((EXPERT_BLOCK, w1b.shape[2]), F32)
        acc3 = jnp.zeros((EXPERT_BLOCK, w1b.shape[2]), F32)
        for s, (lo, hi) in enumerate(_load_packed(x_ref, EXPERT_BLOCK, slabs)):
            lhs = jnp.concatenate([lo.astype(BF16), hi.astype(BF16)], axis=1)
            acc1 = acc1 + _dot(lhs, w1b[s])
            acc3 = acc3 + _dot(lhs, w3b[s])
        hid = (acc1 * _sigmoid(acc1)) * acc3
        _store_packed(o_ref, _dot(hid.astype(BF16), w2b[...]), EXPERT_BLOCK)


def _moe(x_buf, n_active, expert_seq, block_pos, n_used, w1, w3, w2, slabs):
    cap = x_buf.shape[0] // slabs
    d, de = w1.shape[1], w1.shape[2]
    nb = cap // EXPERT_BLOCK

    def xmap(j, na, *_):
        return (jnp.minimum(j, na[0] - 1), 0)

    xspec = pl.BlockSpec((EXPERT_BLOCK * slabs, LANES), xmap)
    hbm = pl.BlockSpec(memory_space=pl.ANY)
    grid_spec = pltpu.PrefetchScalarGridSpec(
        num_scalar_prefetch=4,
        grid=(nb,),
        in_specs=[xspec, hbm, hbm, hbm],
        out_specs=xspec,
        scratch_shapes=[pltpu.VMEM((2, d, de), F32), pltpu.VMEM((2, d, de), F32), pltpu.VMEM((2, de, d), F32),
                        pltpu.VMEM((slabs, 2 * LANES, de), BF16), pltpu.VMEM((slabs, 2 * LANES, de), BF16),
                        pltpu.VMEM((de, d), BF16), pltpu.SemaphoreType.DMA((2, 3))],
    )
    return pl.pallas_call(
        _moe_kernel,
        grid_spec=grid_spec,
        out_shape=jax.ShapeDtypeStruct(x_buf.shape, x_buf.dtype),
        input_output_aliases={4: 0},
        compiler_params=_cparams("arbitrary"),
        name="moe_experts",
    )(n_active, expert_seq, block_pos, n_used, x_buf, w1, w3, w2)


def _combine_kernel(d0_ref, d1_ref, d0n_ref, d1n_ref, y_ref, info_ref, x1_ref, g_ref, gt_ref, o_ref,
                    rows, sem, *, slabs, tiles_per_idx):
    i = pl.program_id(0)
    tm = x1_ref.shape[0]

    def fill(tile, d0, d1):
        slot = tile % 2
        off = (tile % tiles_per_idx) * tm

        def copy(t, dest_ref, k):
            src = pl.multiple_of(dest_ref[off + t], slabs)
            dst = pl.multiple_of(t * slabs, slabs)
            return pltpu.make_async_copy(y_ref.at[pl.ds(src, slabs)], rows.at[slot, k, pl.ds(dst, slabs)],
                                         sem.at[slot])

        def start(t, c):
            copy(t, d0, 0).start(priority=0)
            copy(t, d1, 1).start(priority=1)
            return c

        lax.fori_loop(0, tm, start, 0, unroll=4)

    @pl.when(i == 0)
    def _():
        fill(i, d0_ref, d1_ref)

    @pl.when(i + 1 < pl.num_programs(0))
    def _():
        fill(i + 1, d0n_ref, d1n_ref)

    slot = i % 2
    for k in range(2):
        pltpu.make_async_copy(y_ref.at[pl.ds(0, tm * slabs)], rows.at[slot, k], sem.at[slot]).wait()
    rows0 = rows.at[slot, 0]
    rows1 = rows.at[slot, 1]
    info = info_ref[...]
    w1 = info[:, 2:3]
    w2 = info[:, 3:4]
    lo_parts, hi_parts = [], []
    for (lo0, hi0), (lo1, hi1) in zip(_load_packed(rows0, tm, slabs), _load_packed(rows1, tm, slabs)):
        lo_parts.append(w1 * lo0 + w2 * lo1)
        hi_parts.append(w1 * hi0 + w2 * hi1)
    ffn = jnp.concatenate(lo_parts + hi_parts, axis=1)
    o_ref[...] = x1_ref[...] + gt_ref[0] * _rms(ffn, g_ref[...])


def _combine(y_buf, dest_rows, info, x1, g, gt2, seqlen, slabs, tm):
    n, d = x1.shape
    tpb = seqlen // tm
    idx_block = max(tm, SMEM_INDEX_BLOCK)
    per = idx_block // tm
    last = n // tm - 1
    smem = pl.BlockSpec((idx_block,), lambda i: (i // per,), memory_space=pltpu.SMEM)
    smem_next = pl.BlockSpec((idx_block,), lambda i: (jnp.minimum(i + 1, last) // per,), memory_space=pltpu.SMEM)
    return pl.pallas_call(
        functools.partial(_combine_kernel, slabs=slabs, tiles_per_idx=per),
        grid=(n // tm,),
        in_specs=[smem, smem, smem_next, smem_next,
                  pl.BlockSpec(memory_space=pl.ANY),
                  pl.BlockSpec((tm, ROUTE_LANES), lambda i: (i, 0)),
                  pl.BlockSpec((tm, d), lambda i: (i, 0)),
                  pl.BlockSpec((1, d), lambda i: (0, 0)),
                  pl.BlockSpec((1, 1, d), lambda i: (i // tpb, 0, 0))],
        out_specs=pl.BlockSpec((tm, d), lambda i: (i, 0)),
        out_shape=jax.ShapeDtypeStruct((n, d), F32),
        scratch_shapes=[pltpu.VMEM((2, 2, tm * slabs, LANES), y_buf.dtype), pltpu.SemaphoreType.DMA((2,))],
        compiler_params=_cparams("arbitrary"),
        name="moe_combine",
    )(dest_rows[0], dest_rows[1], dest_rows[0], dest_rows[1], y_buf, info, x1, g.reshape(1, d),
      gt2[:, None, :])


def _pick(n, pref):
    while n % pref:
        pref //= 2
    return pref


def _layer(x2, mod, p, bsz, seqlen):
    n, d = x2.shape
    sh1, sc1, gt1, sh2, sc2, gt2 = jnp.split(mod, 6, axis=-1)
    d_ssm = p["ssm_d"].shape[0]
    d_r = p["rwkv_w0"].shape[0]
    tm = _pick(seqlen, 1024)

    w_in = p["w_in"].astype(BF16)
    w_z = w_in[:, d_ssm:]
    dz = w_z.shape[1]
    z = _norm_proj(x2, p["norm_mix_pre"], sc1, sh1, w_z, seqlen, tm,
                   dz // 2 if (dz // 2) % 128 == 0 else dz, "in_proj_z")

    tabs = _s5_tables(p["ssm_lam_re"], p["ssm_lam_im"], p["ssm_log_dt"], p["ssm_b_re"], p["ssm_b_im"],
                      p["ssm_c_re"], p["ssm_c_im"])
    y_ssm = _s5_glu(x2, p["norm_mix_pre"], sc1, sh1, w_in[:, :d_ssm], tabs, p["ssm_d"], p["glu_w"],
                    p["glu_b"], bsz, seqlen, _pick(seqlen, S5_TIME_BLOCK))

    r, k, v, kk, asig, lw, g = _rwkv_prep(z, p["rwkv_mu"], p["rwkv_w0"], p["rwkv_w_up"], p["rwkv_a0"],
                                          p["rwkv_a_up"], p["rwkv_g_up"], p["rwkv_k_k"], p["rwkv_k_a"],
                                          seqlen, _pick(seqlen, 256))
    y_rwkv = _rwkv(r, k, v, kk, asig, lw, g, p["rwkv_r_k"], p["rwkv_ln_w"], p["rwkv_ln_b"], bsz, seqlen)

    w_route = jnp.zeros((d, ROUTE_LANES), F32)
    w_route = w_route.at[:, :N_EXPERTS].set(p["moe_w_exp"].astype(F32))
    w_route = w_route.at[:, N_EXPERTS:N_EXPERTS + N_GROUPS].set(p["moe_w_grp"].astype(F32))
    b_route = jnp.zeros((ROUTE_LANES,), F32)
    b_route = b_route.at[:N_EXPERTS].set(p["moe_b_exp"].astype(F32))
    b_route = b_route.at[N_EXPERTS:N_EXPERTS + N_GROUPS].set(p["moe_b_grp"].astype(F32))
    x1, h2p, logits = _post_mix(y_ssm, y_rwkv, p["w_out"], x2, p["norm_mix_post"], gt1, p["norm_ffn_pre"],
                                sc2, sh2, w_route, b_route, seqlen, _pick(seqlen, 512))
    slabs = d // (2 * LANES)

    info, counts = _route(logits, _pick(n, 512))
    cnt = counts[0, :N_EXPERTS].astype(jnp.int32)
    padded = (cnt + EXPERT_BLOCK - 1) // EXPERT_BLOCK * EXPERT_BLOCK
    pend = jnp.cumsum(padded)
    pstart = pend - padded
    n_blocks = -(-(2 * n) // EXPERT_BLOCK) + N_EXPERTS
    cap = n_blocks * EXPERT_BLOCK
    seg_row = jnp.zeros((1, ROUTE_LANES), F32).at[0, :N_EXPERTS].set((pstart * slabs).astype(F32))
    dest = _slot_rows(info, seg_row, slabs, _pick(n, 2048))
    dest_rows = (dest[:, 0], dest[:, 1])
    block_e = jnp.minimum(
        jnp.searchsorted(pend, jnp.arange(n_blocks, dtype=jnp.int32) * EXPERT_BLOCK, side="right"),
        N_EXPERTS - 1).astype(jnp.int32)
    n_active = (pend[-1:] // EXPERT_BLOCK).astype(jnp.int32)
    used = cnt > 0
    expert_seq = jnp.nonzero(used, size=N_EXPERTS, fill_value=0)[0].astype(jnp.int32)
    block_pos = (jnp.cumsum(used.astype(jnp.int32)) - 1)[block_e].astype(jnp.int32)
    n_used = jnp.sum(used.astype(jnp.int32)).reshape(1)

    x_buf = _dispatch(h2p, dest_rows, cap, slabs, _pick(n, 1024))
    y_buf = _moe(x_buf, n_active, expert_seq, block_pos, n_used, p["moe_w1"], p["moe_w3"], p["moe_w2"], slabs)
    return _combine(y_buf, dest_rows, info, x1, p["norm_ffn_post"], gt2, seqlen, slabs, _pick(seqlen, 512))


def kernel(x, c, ada_w, ada_b, norm_mix_pre, norm_mix_post, norm_ffn_pre, norm_ffn_post, w_in, w_out, ssm_lam_re, ssm_lam_im, ssm_log_dt, ssm_b_re, ssm_b_im, ssm_c_re, ssm_c_im, ssm_d, glu_w, glu_b, rwkv_mu, rwkv_w0, rwkv_w_up, rwkv_a0, rwkv_a_up, rwkv_g_up, rwkv_k_k, rwkv_k_a, rwkv_r_k, rwkv_ln_w, rwkv_ln_b, moe_w_grp, moe_b_grp, moe_w_exp, moe_b_exp, moe_w1, moe_w3, moe_w2):
    bsz, seqlen, d = x.shape
    params = dict(norm_mix_pre=norm_mix_pre, norm_mix_post=norm_mix_post, norm_ffn_pre=norm_ffn_pre,
                  norm_ffn_post=norm_ffn_post, w_in=w_in, w_out=w_out, ssm_lam_re=ssm_lam_re,
                  ssm_lam_im=ssm_lam_im, ssm_log_dt=ssm_log_dt, ssm_b_re=ssm_b_re, ssm_b_im=ssm_b_im,
                  ssm_c_re=ssm_c_re, ssm_c_im=ssm_c_im, ssm_d=ssm_d, glu_w=glu_w, glu_b=glu_b,
                  rwkv_mu=rwkv_mu, rwkv_w0=rwkv_w0, rwkv_w_up=rwkv_w_up, rwkv_a0=rwkv_a0,
                  rwkv_a_up=rwkv_a_up, rwkv_g_up=rwkv_g_up, rwkv_k_k=rwkv_k_k, rwkv_k_a=rwkv_k_a,
                  rwkv_r_k=rwkv_r_k, rwkv_ln_w=rwkv_ln_w, rwkv_ln_b=rwkv_ln_b, moe_w_grp=moe_w_grp,
                  moe_b_grp=moe_b_grp, moe_w_exp=moe_w_exp, moe_b_exp=moe_b_exp, moe_w1=moe_w1,
                  moe_w3=moe_w3, moe_w2=moe_w2)
    x2 = x.reshape(bsz * seqlen, d)
    for layer in range(ada_w.shape[0]):
        mod = _ada(c, ada_w[layer], ada_b[layer])
        x2 = _layer(x2, mod, {k: v[layer] for k, v in params.items()}, bsz, seqlen)
    return x2.reshape(bsz, seqlen, d)
```

```python
import functools
import math

import jax
import jax.numpy as jnp
from jax import lax
from jax.experimental import pallas as pl
from jax.experimental.pallas import tpu as pltpu

F32 = jnp.float32
BF16 = jnp.bfloat16

SSM_GROUP = 16
SSM_STATE = 64
S5_SLAB = 256
S5_TIME_BLOCK = 64
RWKV_HEAD = 64
RWKV_CHUNK = 64
RWKV_BATCH_PER_STEP = 4
HEADS_PER_TILE = 4
LORA_W = 64
LORA_A = 64
LORA_G = 128
N_GROUPS = 8
EXPERTS_PER_GROUP = 8
N_EXPERTS = N_GROUPS * EXPERTS_PER_GROUP
EXPERT_BLOCK = 256
RMS_EPS = 1e-6
GN_EPS = 64e-5
DECAY_SCALE = math.exp(-0.5)
LANES = 128
COMBINE_CHUNK = 64
SMEM_INDEX_BLOCK = 1024
ROUTE_LANES = LANES
VMEM_LIMIT = 52 * 1024 * 1024


def _cparams(*sem):
    return pltpu.CompilerParams(dimension_semantics=sem, vmem_limit_bytes=VMEM_LIMIT)


def _sigmoid(x):
    return 1.0 / (1.0 + jnp.exp(-x))


def _dot(a, b):
    return jnp.dot(a, b, preferred_element_type=F32)


def _dot_nt(a, b):
    return lax.dot_general(a, b, (((1,), (1,)), ((), ())), preferred_element_type=F32)


def _dot_tn(a, b):
    return lax.dot_general(a, b, (((0,), (0,)), ((), ())), preferred_element_type=F32)


def _split_bf16(x):
    hi = x.astype(BF16)
    lo = (x - hi.astype(F32)).astype(BF16)
    return hi, lo


def _pack_pair(a, b):
    ua = lax.bitcast_convert_type(a.astype(BF16).astype(F32), jnp.uint32)
    ub = lax.bitcast_convert_type(b.astype(BF16).astype(F32), jnp.uint32)
    return ub | (ua >> 16)


def _unpack_pair(w):
    lo = lax.bitcast_convert_type(w << 16, F32)
    hi = lax.bitcast_convert_type(w & jnp.uint32(0xFFFF0000), F32)
    return lo, hi


def _store_packed(ref, val, n_rows):
    d = val.shape[1]
    slabs = d // (2 * LANES)
    for s in range(slabs):
        a = val[:, s * LANES:(s + 1) * LANES]
        b = val[:, d // 2 + s * LANES:d // 2 + (s + 1) * LANES]
        ref[pl.ds(s, n_rows, stride=slabs), :] = _pack_pair(a, b)


def _load_packed(ref, n_rows, slabs):
    return [_unpack_pair(ref[pl.ds(s, n_rows, stride=slabs), :]) for s in range(slabs)]


def _ada_kernel(c_ref, w_ref, b_ref, o_ref):
    c = c_ref[...]
    cond = c * _sigmoid(c)
    o_ref[...] = jnp.dot(cond, w_ref[...], preferred_element_type=F32,
                         precision=lax.Precision.HIGHEST) + b_ref[...]


def _ada(c, ada_w, ada_b):
    bsz, d = c.shape
    n = ada_w.shape[1]
    tn = 1024
    return pl.pallas_call(
        _ada_kernel,
        grid=(n // tn,),
        in_specs=[pl.BlockSpec((bsz, d), lambda j: (0, 0)),
                  pl.BlockSpec((d, tn), lambda j: (0, j)),
                  pl.BlockSpec((1, tn), lambda j: (0, j))],
        out_specs=pl.BlockSpec((bsz, tn), lambda j: (0, j)),
        out_shape=jax.ShapeDtypeStruct((bsz, n), F32),
        compiler_params=_cparams("arbitrary"),
        name="ada_mod",
    )(c, ada_w, ada_b.reshape(1, n))


def _rms(x, g):
    return x * lax.rsqrt(jnp.mean(x * x, axis=-1, keepdims=True) + RMS_EPS) * g


def _norm_proj_kernel(x_ref, g_ref, sc_ref, sh_ref, w_ref, o_ref, h_scr):
    @pl.when(pl.program_id(1) == 0)
    def _():
        h = _rms(x_ref[...], g_ref[...]) * (1.0 + sc_ref[0]) + sh_ref[0]
        h_scr[...] = h.astype(h_scr.dtype)

    o_ref[...] = _dot(h_scr[...], w_ref[...]).astype(o_ref.dtype)


def _norm_proj(x2, g, sc, sh, w, seqlen, tm, tn, name):
    n, d = x2.shape
    nout = w.shape[1]
    tpb = seqlen // tm
    return pl.pallas_call(
        _norm_proj_kernel,
        grid=(n // tm, nout // tn),
        in_specs=[pl.BlockSpec((tm, d), lambda i, j: (i, 0)),
                  pl.BlockSpec((1, d), lambda i, j: (0, 0)),
                  pl.BlockSpec((1, 1, d), lambda i, j: (i // tpb, 0, 0)),
                  pl.BlockSpec((1, 1, d), lambda i, j: (i // tpb, 0, 0)),
                  pl.BlockSpec((d, tn), lambda i, j: (0, j))],
        out_specs=pl.BlockSpec((tm, tn), lambda i, j: (i, j)),
        out_shape=jax.ShapeDtypeStruct((n, nout), BF16),
        scratch_shapes=[pltpu.VMEM((tm, d), BF16)],
        compiler_params=_cparams("parallel", "arbitrary"),
        name=name,
    )(x2, g.reshape(1, d), sc[:, None, :], sh[:, None, :], w)


def _s5_tables(lam_re, lam_im, log_dt, b_re, b_im, c_re, c_im):
    g, p, cg = b_re.shape
    gs = S5_SLAB // cg
    ns = g // gs
    lr = jnp.minimum(lam_re.astype(F32), -1e-4)
    li = lam_im.astype(F32)
    dt = jnp.exp(log_dt.astype(F32))[:, None]
    mag = jnp.exp(lr * dt)
    ar, ai = mag * jnp.cos(li * dt), mag * jnp.sin(li * dt)
    den = lr * lr + li * li
    qr = ((ar - 1.0) * lr + ai * li) / den
    qi = (ai * lr - (ar - 1.0) * li) / den
    br, bi = b_re.astype(F32), b_im.astype(F32)
    bbr = qr[..., None] * br - qi[..., None] * bi
    bbi = qr[..., None] * bi + qi[..., None] * br

    def in_rows(t):
        return t.reshape(ns, gs, p, cg).transpose(0, 1, 3, 2).reshape(ns, gs * cg, p).astype(BF16)

    def out_cols(t):
        return t.reshape(ns, gs, cg, p).transpose(0, 3, 1, 2).reshape(ns, p, gs * cg).astype(BF16)

    a_tab = jnp.stack([ar.reshape(ns, gs * p), ai.reshape(ns, gs * p)], axis=1)
    return (in_rows(bbr), in_rows(bbi), out_cols(c_re.astype(F32)), out_cols(-c_im.astype(F32)), a_tab)


def _gelu_tanh(x):
    return 0.5 * x * (1.0 + jnp.tanh(math.sqrt(2.0 / math.pi) * (x + 0.044715 * (x * x * x))))


def _s5_kernel(x_ref, g_ref, sc_ref, sh_ref, wu_ref, perm_ref, bre_ref, bim_ref, cre_ref, cim_ref, a_ref,
               d_ref, gw_ref, gb_ref, o_ref, b_ref, c_ref, u_scr, bscr, sscr, yscr, st_ref):
    bsz, lb, d_in = x_ref.shape
    rows = bsz * lb
    dch = wu_ref.shape[1]
    half = st_ref.shape[2] // 2
    n_p = bre_ref.shape[2]

    @pl.when(pl.program_id(0) == 0)
    def _():
        st_ref[...] = jnp.zeros_like(st_ref)
        tile_in = (lax.broadcasted_iota(jnp.int32, (n_p, half), 0)
                   == lax.broadcasted_iota(jnp.int32, (n_p, half), 1) % n_p).astype(BF16)
        tile_out = (lax.broadcasted_iota(jnp.int32, (half, n_p), 0) % n_p
                    == lax.broadcasted_iota(jnp.int32, (half, n_p), 1)).astype(BF16)
        in_mask = (lax.broadcasted_iota(jnp.int32, (S5_SLAB, half), 0) // SSM_GROUP
                   == lax.broadcasted_iota(jnp.int32, (S5_SLAB, half), 1) // n_p)
        out_mask = (lax.broadcasted_iota(jnp.int32, (half, S5_SLAB), 0) // n_p
                    == lax.broadcasted_iota(jnp.int32, (half, S5_SLAB), 1) // SSM_GROUP)
        for s in range(dch // S5_SLAB):
            b_ref[s, :, :half] = jnp.where(in_mask, _dot(bre_ref[s], tile_in), 0.0).astype(BF16)
            b_ref[s, :, half:] = jnp.where(in_mask, _dot(bim_ref[s], tile_in), 0.0).astype(BF16)
            c_ref[s, :half, :] = jnp.where(out_mask, _dot(tile_out, cre_ref[s]), 0.0).astype(BF16)
            c_ref[s, half:, :] = jnp.where(out_mask, _dot(tile_out, cim_ref[s]), 0.0).astype(BF16)

    h = (_rms(x_ref[...], g_ref[...]) * (1.0 + sc_ref[...]) + sh_ref[...]).astype(BF16)
    u_nat = _dot(h.reshape(rows, d_in), wu_ref[...]).astype(BF16)
    u_scr[...] = _dot(perm_ref[...], u_nat).astype(BF16)
    n_slab = dch // S5_SLAB
    slab = lambda s: slice(s * S5_SLAB, (s + 1) * S5_SLAB)

    def project_in(s):
        bscr[s % 2] = _dot(u_scr[:, slab(s)], b_ref[s])

    project_in(0)
    for s in range(n_slab):
        if s + 1 < n_slab:
            project_in(s + 1)
        buf = s % 2
        a_re = a_ref[s, 0:1, :]
        a_im = a_ref[s, 1:2, :]
        s_r = st_ref[s, :, :half]
        s_i = st_ref[s, :, half:]
        for l in range(rows // bsz):
            at_l = slice(l * bsz, (l + 1) * bsz)
            s_r, s_i = (a_re * s_r - a_im * s_i + bscr[buf, at_l, :half],
                        a_re * s_i + a_im * s_r + bscr[buf, at_l, half:])
            sscr[buf, at_l, :half] = s_r
            sscr[buf, at_l, half:] = s_i
        st_ref[s, :, :half] = s_r
        st_ref[s, :, half:] = s_i
        y = _dot(sscr[buf].astype(BF16), c_ref[s]) + d_ref[:, slab(s)] * u_scr[:, slab(s)].astype(F32)
        yscr[:, slab(s)] = _gelu_tanh(y).astype(BF16)
    y = yscr[...]
    gate = _sigmoid(_dot(y, gw_ref[...]) + gb_ref[...])
    out_tm = (y.astype(F32) * gate).astype(BF16)
    out_nat = _dot_tn(perm_ref[...], out_tm).astype(o_ref.dtype)
    o_ref[...] = out_nat.reshape(bsz, lb, dch)


def _s5_glu(x2, g, sc, sh, w_u, tabs, d_skip, glu_w, glu_b, bsz, seqlen, lb):
    b_re, b_im, c_re, c_im, a_tab = tabs
    d_in, dch = w_u.shape
    ns, _, n_half = a_tab.shape
    n_state = 2 * n_half
    rows = lb * bsz
    r_idx = jnp.arange(rows)
    perm = ((r_idx % bsz) * lb + r_idx // bsz)[:, None] == r_idx[None, :]
    full = lambda a: pl.BlockSpec(a.shape, lambda i: (0,) * a.ndim, pipeline_mode=pl.Buffered(1))
    args = (x2.reshape(bsz, seqlen, d_in), g.reshape(1, 1, d_in), sc[:, None, :], sh[:, None, :], w_u,
            perm.astype(BF16), b_re, b_im, c_re, c_im, a_tab,
            d_skip.astype(F32).reshape(1, dch), glu_w.astype(BF16), glu_b.astype(F32).reshape(1, dch))
    y = pl.pallas_call(
        _s5_kernel,
        grid=(seqlen // lb,),
        in_specs=[pl.BlockSpec((bsz, lb, d_in), lambda i: (0, i, 0))] + [full(a) for a in args[1:]],
        out_specs=pl.BlockSpec((bsz, lb, dch), lambda i: (0, i, 0)),
        out_shape=jax.ShapeDtypeStruct((bsz, seqlen, dch), BF16),
        scratch_shapes=[pltpu.VMEM((ns, S5_SLAB, n_state), BF16), pltpu.VMEM((ns, n_state, S5_SLAB), BF16),
                        pltpu.VMEM((rows, dch), BF16), pltpu.VMEM((2, rows, n_state), F32),
                        pltpu.VMEM((2, rows, n_state), F32), pltpu.VMEM((rows, dch), BF16),
                        pltpu.VMEM((ns, bsz, n_state), F32)],
        compiler_params=_cparams("arbitrary"),
        name="s5_mixer_glu",
    )(*args)
    return y.reshape(bsz * seqlen, dch)


def _rwkv_kernel(z_ref, mu_ref, w0_ref, a0_ref, kkw_ref, ka_ref, wa_ref, gup_ref, rk_ref, lnw_ref, lnb_ref,
                 ones_ref, o_ref, s_ref, zlast_ref):
    nb, t, _ = z_ref.shape
    d_r = o_ref.shape[2]
    tile = HEADS_PER_TILE * RWKV_HEAD
    n_tiles = d_r // tile
    first_chunk = pl.program_id(1) == 0

    @pl.when(first_chunk)
    def _():
        s_ref[...] = jnp.zeros_like(s_ref)

    ones_bd = ones_ref[...]

    def seg_sum(x):
        return _dot(x.astype(BF16), ones_bd)

    row = lax.broadcasted_iota(jnp.int32, (t, t), 0)
    col = lax.broadcasted_iota(jnp.int32, (t, t), 1)
    tri = (row >= col).astype(BF16)
    st = HEADS_PER_TILE * t
    rs = lax.broadcasted_iota(jnp.int32, (2 * st, 2 * st), 0)
    cs = lax.broadcasted_iota(jnp.int32, (2 * st, 2 * st), 1)
    t_r = rs % t
    t_c = cs % t
    keep = (t_r > t_c) | ((rs >= st) & (t_r == t_c))
    eye_w = (lax.broadcasted_iota(jnp.int32, (t, st), 0)
             == lax.broadcasted_iota(jnp.int32, (t, st), 1) % t).astype(F32)
    blk_mask = (lax.broadcasted_iota(jnp.int32, (st, st), 0) // t
                == lax.broadcasted_iota(jnp.int32, (st, st), 1) // t)
    lane = lax.broadcasted_iota(jnp.int32, (1, tile), 1)
    head_masks = [(lane >= j * RWKV_HEAD) & (lane < (j + 1) * RWKV_HEAD) for j in range(HEADS_PER_TILE)]
    bd_r = lax.broadcasted_iota(jnp.int32, (tile, tile), 0) // RWKV_HEAD
    bd_c = lax.broadcasted_iota(jnp.int32, (tile, tile), 1) // RWKV_HEAD
    bd_mask = bd_r == bd_c
    n_levels = int(math.log2(t))
    slices = [slice(hg * tile, (hg + 1) * tile) for hg in range(n_tiles)]
    units = [(bb, hg) for bb in range(nb) for hg in range(n_tiles)]
    n_u = range(len(units))

    def stack(x):
        return jnp.concatenate([jnp.where(m, x, 0.0) for m in head_masks], axis=0)

    def collapse(x):
        out = x[:t]
        for j in range(1, HEADS_PER_TILE):
            out = out + x[j * t:(j + 1) * t]
        return out

    def bf(x):
        return x.astype(BF16)

    at, qt, bt, kt, vv, em, etm, wtot, rkb, gate = ([] for _ in range(10))
    row0 = lax.broadcasted_iota(jnp.int32, (t, 1), 0) == 0
    lora_lane = lax.broadcasted_iota(jnp.int32, (t, LORA_W + LORA_A), 1)
    for bb in range(nb):
        z = z_ref[bb].astype(F32)
        prev_row = jnp.where(first_chunk, 0.0, zlast_ref[bb, 0:1, :])
        zl = z + mu_ref[...] * (jnp.where(row0, prev_row, pltpu.roll(z, 1, 0)) - z)
        zlast_ref[bb, 0:1, :] = z[t - 1:t, :]
        xwa = zl[:, 3 * d_r:3 * d_r + LORA_W + LORA_A]
        lhs = jnp.where(lora_lane < LORA_W, jnp.tanh(xwa), xwa).astype(BF16)
        wa = _dot(lhs, wa_ref[...])
        lw = -DECAY_SCALE * _sigmoid(w0_ref[...] + wa[:, :d_r])
        asig = _sigmoid(a0_ref[...] + wa[:, d_r:])
        r = zl[:, :d_r]
        k_raw = zl[:, d_r:2 * d_r]
        kk = k_raw * kkw_ref[...]
        kp = k_raw * (1.0 + (asig - 1.0) * ka_ref[...])
        xg = zl[:, 3 * d_r + LORA_W + LORA_A:3 * d_r + LORA_W + LORA_A + LORA_G]
        g_full = _dot(_sigmoid(xg).astype(BF16), gup_ref[...])
        lw_hi, lw_lo = _split_bf16(lw)
        cum = _dot(tri, lw_hi) + _dot(tri, lw_lo)
        mid = cum[t // 2 - 1:t // 2, :]
        tot = cum[t - 1:t, :]
        e1 = jnp.exp(cum - mid)
        e2 = jnp.exp(mid - cum)
        e1p = e1 * jnp.exp(-lw)
        kk2 = kk * kk
        kkn = kk / jnp.maximum(jnp.sqrt(jnp.concatenate([seg_sum(kk2[:, sl]) for sl in slices], axis=1)), 1e-12)
        full = dict(at=-kkn * e1p, qt=r * e1, bt=kkn * asig * e2, kt=kp * e2,
                    vv=zl[:, 2 * d_r:3 * d_r], em=jnp.exp(mid), etm=jnp.exp(tot - mid), wtot=jnp.exp(tot),
                    rkb=r * kp * rk_ref[...], gate=g_full)
        for dst, key in ((at, "at"), (qt, "qt"), (bt, "bt"), (kt, "kt"), (vv, "vv"), (em, "em"),
                         (etm, "etm"), (wtot, "wtot"), (rkb, "rkb"), (gate, "gate")):
            dst.extend(full[key][:, sl] for sl in slices)

    s_old = [s_ref[i] for i in n_u]
    wide = []
    for i in n_u:
        lhs = jnp.concatenate([stack(at[i]), stack(qt[i])], axis=0)
        rhs = jnp.concatenate([stack(bt[i]), stack(kt[i])], axis=0)
        a_mat = jnp.where(keep, _dot_nt(bf(lhs), bf(rhs)), 0.0)
        wide.append([collapse(a_mat[r0:r0 + st, c0:c0 + st]) for r0 in (0, st) for c0 in (0, st)])
    x_state = [_dot_nt(bf(jnp.concatenate([at[i], qt[i]], axis=0) * em[i]), bf(s_old[i]))
               for i in n_u]
    sv = [bf(stack(vv[i])) for i in n_u]
    akv = [_dot(bf(wide[i][1]), sv[i]) for i in n_u]

    def expand(x_w):
        return bf(jnp.where(blk_mask, jnp.concatenate([x_w] * HEADS_PER_TILE, axis=0), 0.0))

    p_acc = [eye_w + w[0] for w in wide]
    q_bd = [expand(w[0]) for w in wide]
    q_pow = [_dot(bf(wide[i][0]), q_bd[i]) for i in n_u]
    for lev in range(1, n_levels):
        for i in n_u:
            q_bd[i] = expand(q_pow[i])
            if lev < n_levels - 1:
                both = _dot(bf(jnp.concatenate([p_acc[i], q_pow[i]], axis=0)), q_bd[i])
                p_acc[i] = p_acc[i] + both[:t]
                q_pow[i] = both[t:]
            else:
                p_acc[i] = p_acc[i] + _dot(bf(p_acc[i]), q_bd[i])
    u_all = [_dot(bf(p_acc[i]), bf(stack(x_state[i][:t] + akv[i]))) for i in n_u]
    y_all = [x_state[i][t:]
             + _dot(bf(jnp.concatenate([wide[i][2], wide[i][3]], axis=1)),
                    jnp.concatenate([bf(stack(u_all[i])), sv[i]], axis=0)) for i in n_u]

    for i, (bb, hg) in enumerate(units):
        sl = slices[hg]
        y = y_all[i]
        uv = bf(jnp.concatenate([u_all[i], vv[i]], axis=0))
        bk_end = bf(jnp.concatenate([bt[i], kt[i]], axis=0) * etm[i])
        s_ref[i] = s_old[i] * wtot[i] + jnp.where(bd_mask, _dot_tn(uv, bk_end), 0.0)

        mean = seg_sum(y) * (1.0 / RWKV_HEAD)
        dlt = y - mean
        var = seg_sum(dlt * dlt) * (1.0 / RWKV_HEAD)
        yn = dlt * lax.rsqrt(var + GN_EPS) * lnw_ref[:, sl] + lnb_ref[:, sl]
        out = (yn + seg_sum(rkb[i]) * vv[i]) * gate[i]
        o_ref[bb, :, sl] = out.astype(o_ref.dtype)


def _rwkv(z, mu, w0, w_up, a0, a_up, g_up, k_k, k_a, r_k, ln_w, ln_b, bsz, seqlen):
    n, dz = z.shape
    d_r = w0.shape[0]
    t = RWKV_CHUNK
    nb = RWKV_BATCH_PER_STEP
    nch = seqlen // t
    tile = HEADS_PER_TILE * RWKV_HEAD
    hid = jnp.arange(tile) // RWKV_HEAD
    ones_bd = (hid[:, None] == hid[None, :]).astype(BF16)
    wa = jnp.zeros((LORA_W + LORA_A, 2 * d_r), F32)
    wa = wa.at[:LORA_W, :d_r].set(w_up.astype(F32)).at[LORA_W:, d_r:].set(a_up.astype(F32)).astype(BF16)
    row = lambda a: a.astype(F32).reshape(1, -1)
    full = lambda a: pl.BlockSpec(a.shape, lambda b, c: (0, 0))
    params = (row(mu), row(w0), row(a0), row(k_k), row(k_a), wa, g_up.astype(BF16), row(r_k), row(ln_w),
              row(ln_b), ones_bd)
    y = pl.pallas_call(
        _rwkv_kernel,
        grid=(bsz // nb, nch),
        in_specs=[pl.BlockSpec((nb, t, dz), lambda b, c: (b, c, 0))] + [full(a) for a in params],
        out_specs=pl.BlockSpec((nb, t, d_r), lambda b, c: (b, c, 0)),
        out_shape=jax.ShapeDtypeStruct((bsz, seqlen, d_r), BF16),
        scratch_shapes=[pltpu.VMEM((nb * (d_r // tile), tile, tile), F32), pltpu.VMEM((nb, 8, dz), F32)],
        compiler_params=_cparams("parallel", "arbitrary"),
        name="rwkv7_chunked",
    )(z.reshape(bsz, seqlen, dz), *params)
    return y.reshape(n, d_r)


def _post_mix_kernel(ys_ref, yr_ref, wo1_ref, wo2_ref, x_ref, g1_ref, gt_ref, g2_ref, sc_ref, sh_ref,
                     wr_both_ref, br_ref, x1_out, h2_out, lg_out):
    mixed = _dot(ys_ref[...], wo1_ref[...]) + _dot(yr_ref[...], wo2_ref[...])
    x1 = x_ref[...] + gt_ref[0] * _rms(mixed, g1_ref[...])
    x1_out[...] = x1
    h2 = _rms(x1, g2_ref[...]) * (1.0 + sc_ref[0]) + sh_ref[0]
    _store_packed(h2_out, h2, h2.shape[0])
    hi, lo = _split_bf16(h2)
    both = _dot(hi, wr_both_ref[...])
    nl = lg_out.shape[1]
    lg_out[...] = both[:, :nl] + both[:, nl:] + _dot(lo, wr_both_ref[:, :nl]) + br_ref[...]


def _post_mix(ys, yr, w_out, x2, g1, gt1, g2, sc2, sh2, w_route, b_route, seqlen, tm):
    n, d = x2.shape
    ds = ys.shape[1]
    tpb = seqlen // tm
    slabs = d // (2 * LANES)
    wo = w_out.astype(BF16)
    wr_both = jnp.concatenate(_split_bf16(w_route), axis=1)
    rows = lambda w: pl.BlockSpec((tm, w), lambda i: (i, 0))
    full = lambda a: pl.BlockSpec(a.shape, lambda i: (0,) * a.ndim)
    bat = pl.BlockSpec((1, 1, d), lambda i: (i // tpb, 0, 0))
    args = (ys, yr, wo[:ds], wo[ds:], x2, g1.reshape(1, d), gt1[:, None, :], g2.reshape(1, d),
            sc2[:, None, :], sh2[:, None, :], wr_both, b_route.reshape(1, -1))
    in_specs = [rows(ds), rows(yr.shape[1]), full(args[2]), full(args[3]), rows(d), full(args[5]), bat,
                full(args[7]), bat, bat, full(wr_both), full(args[11])]
    return pl.pallas_call(
        _post_mix_kernel,
        grid=(n // tm,),
        in_specs=in_specs,
        out_specs=[rows(d), pl.BlockSpec((tm * slabs, LANES), lambda i: (i, 0)), rows(ROUTE_LANES)],
        out_shape=[jax.ShapeDtypeStruct((n, d), F32), jax.ShapeDtypeStruct((n * slabs, LANES), jnp.uint32),
                   jax.ShapeDtypeStruct((n, ROUTE_LANES), F32)],
        compiler_params=_cparams("parallel"),
        name="out_proj_post",
    )(*args)


def _route_kernel(lg_ref, info_ref, cnt_ref, carry):
    i = pl.program_id(0)
    tm = lg_ref.shape[0]

    @pl.when(i == 0)
    def _():
        carry[...] = jnp.zeros_like(carry)

    lg = lg_ref[...]
    lane = lax.broadcasted_iota(jnp.int32, lg.shape, 1)
    lane_f = lane.astype(F32)
    neg = jnp.float32(-jnp.inf)
    big = jnp.float32(1e9)
    is_g = (lane >= N_EXPERTS) & (lane < N_EXPERTS + N_GROUPS)
    gl = jnp.where(is_g, lg, neg)
    gmax = jnp.max(gl, axis=-1, keepdims=True)
    gidx = jnp.min(jnp.where(gl == gmax, lane_f - N_EXPERTS, big), axis=-1, keepdims=True)
    p_grp = 1.0 / jnp.sum(jnp.where(is_g, jnp.exp(gl - gmax), 0.0), axis=-1, keepdims=True)
    in_grp = (lane < N_EXPERTS) & ((lane // EXPERTS_PER_GROUP).astype(F32) == gidx)
    el = jnp.where(in_grp, lg, neg)
    m1 = jnp.max(el, axis=-1, keepdims=True)
    i1 = jnp.min(jnp.where(el == m1, lane_f, big), axis=-1, keepdims=True)
    el2 = jnp.where(lane_f == i1, neg, el)
    m2 = jnp.max(el2, axis=-1, keepdims=True)
    i2 = jnp.min(jnp.where(el2 == m2, lane_f, big), axis=-1, keepdims=True)
    ex = jnp.exp(m2 - m1)
    w1 = p_grp / (1.0 + ex)
    w2 = p_grp * ex / (1.0 + ex)

    oh1 = lane_f == i1
    oh2 = lane_f == i2
    onehot = (oh1 | oh2).astype(BF16)
    rr = lax.broadcasted_iota(jnp.int32, (tm, tm), 0)
    cc = lax.broadcasted_iota(jnp.int32, (tm, tm), 1)
    before = _dot((rr > cc).astype(BF16), onehot) + carry[...]
    rank1 = jnp.sum(jnp.where(oh1, before, 0.0), axis=-1, keepdims=True)
    rank2 = jnp.sum(jnp.where(oh2, before, 0.0), axis=-1, keepdims=True)
    carry[...] = carry[...] + jnp.sum(onehot.astype(F32), axis=0, keepdims=True)
    cnt_ref[...] = carry[...]

    info = jnp.where(lane == 0, i1, 0.0)
    info = jnp.where(lane == 1, i2, info)
    info = jnp.where(lane == 2, w1, info)
    info = jnp.where(lane == 3, w2, info)
    info = jnp.where(lane == 4, rank1, info)
    info = jnp.where(lane == 5, rank2, info)
    info_ref[...] = info


def _route(logits, tm):
    n = logits.shape[0]
    return pl.pallas_call(
        _route_kernel,
        grid=(n // tm,),
        in_specs=[pl.BlockSpec((tm, ROUTE_LANES), lambda i: (i, 0))],
        out_specs=[pl.BlockSpec((tm, ROUTE_LANES), lambda i: (i, 0)),
                   pl.BlockSpec((1, ROUTE_LANES), lambda i: (0, 0))],
        out_shape=[jax.ShapeDtypeStruct((n, ROUTE_LANES), F32),
                   jax.ShapeDtypeStruct((1, ROUTE_LANES), F32)],
        scratch_shapes=[pltpu.VMEM((1, ROUTE_LANES), F32)],
        compiler_params=_cparams("arbitrary"),
        name="moe_route",
    )(logits)


def _slot_rows_kernel(info_ref, seg_ref, o_ref, *, slabs):
    info = info_ref[...]
    lane = lax.broadcasted_iota(jnp.int32, info.shape, 1)
    lane_f = lane.astype(F32)
    seg = seg_ref[...]
    d0 = jnp.sum(jnp.where(lane_f == info[:, 0:1], seg, 0.0), axis=-1, keepdims=True) + info[:, 4:5] * slabs
    d1 = jnp.sum(jnp.where(lane_f == info[:, 1:2], seg, 0.0), axis=-1, keepdims=True) + info[:, 5:6] * slabs
    o_ref[...] = jnp.where(lane == 0, d0, jnp.where(lane == 1, d1, 0.0)).astype(jnp.int32)


def _slot_rows(info, seg_row, slabs, tm):
    n = info.shape[0]
    return pl.pallas_call(
        functools.partial(_slot_rows_kernel, slabs=slabs),
        grid=(n // tm,),
        in_specs=[pl.BlockSpec((tm, ROUTE_LANES), lambda i: (i, 0)),
                  pl.BlockSpec((1, ROUTE_LANES), lambda i: (0, 0))],
        out_specs=pl.BlockSpec((tm, ROUTE_LANES), lambda i: (i, 0)),
        out_shape=jax.ShapeDtypeStruct((n, ROUTE_LANES), jnp.int32),
        compiler_params=_cparams("parallel"),
        name="moe_slot_rows",
    )(info, seg_row)


def _dispatch_kernel(d0_ref, d1_ref, h_ref, buf_in, buf_out, sem, *, slabs):
    del buf_in
    tm = h_ref.shape[0] // slabs

    def copy(t, dest_ref):
        src = pl.multiple_of(t * slabs, slabs)
        dst = pl.multiple_of(dest_ref[t], slabs)
        return pltpu.make_async_copy(h_ref.at[pl.ds(src, slabs)], buf_out.at[pl.ds(dst, slabs)], sem)

    def start(t, c):
        copy(t, d0_ref).start(priority=0)
        copy(t, d1_ref).start(priority=1)
        return c

    lax.fori_loop(0, tm, start, 0, unroll=4)
    for _ in range(2):
        pltpu.make_async_copy(h_ref, buf_out.at[pl.ds(0, tm * slabs)], sem).wait()


def _dispatch(h2p, dest_rows, cap, slabs, tm):
    n = h2p.shape[0] // slabs
    buf = jnp.zeros((cap * slabs, LANES), h2p.dtype)
    smem = pl.BlockSpec((tm,), lambda i: (i,), memory_space=pltpu.SMEM)
    return pl.pallas_call(
        functools.partial(_dispatch_kernel, slabs=slabs),
        grid=(n // tm,),
        in_specs=[smem, smem,
                  pl.BlockSpec((tm * slabs, LANES), lambda i: (i, 0)),
                  pl.BlockSpec(memory_space=pl.ANY)],
        out_specs=pl.BlockSpec(memory_space=pl.ANY),
        out_shape=jax.ShapeDtypeStruct(buf.shape, buf.dtype),
        scratch_shapes=[pltpu.SemaphoreType.DMA],
        input_output_aliases={3: 0},
        compiler_params=_cparams("arbitrary"),
        name="moe_dispatch",
    )(dest_rows[0], dest_rows[1], h2p, buf)


def _moe_kernel(na_ref, eseq_ref, epos_ref, nd_ref, x_ref, w1_hbm, w3_hbm, w2_hbm, o_ref,
                w1f, w3f, w2f, w1b, w3b, w2b, sem):
    j = pl.program_id(0)
    active = j < na_ref[0]
    pos = epos_ref[j]
    fresh = (j == 0) | (pos != epos_ref[jnp.maximum(j - 1, 0)])

    slabs = w1b.shape[0]
    half = slabs * LANES

    def weight_copies(p):
        e = eseq_ref[p]
        slot = p % 2
        return [pltpu.make_async_copy(w_hbm.at[e], stage.at[slot], sem.at[slot, i])
                for i, (w_hbm, stage) in enumerate(((w1_hbm, w1f), (w3_hbm, w3f), (w2_hbm, w2f)))]

    @pl.when(j == 0)
    def _():
        for c in weight_copies(0):
            c.start()

    @pl.when(active & fresh)
    def _():
        for c in weight_copies(pos):
            c.wait()

        @pl.when(pos + 1 < nd_ref[0])
        def _():
            for c in weight_copies(pos + 1):
                c.start()

        slot = pos % 2
        for s in range(slabs):
            for dst, src in ((w1b, w1f), (w3b, w3f)):
                dst[s, :LANES, :] = src[slot, s * LANES:(s + 1) * LANES, :].astype(BF16)
                dst[s, LANES:, :] = src[slot, half + s * LANES:half + (s + 1) * LANES, :].astype(BF16)
        w2b[...] = w2f[slot].astype(BF16)

    @pl.when(active)
    def _():
        acc1 = jnp.zeros((EXPERT_BLOCK, w1b.shape[2]), F32)
        acc3 = jnp.zeros((EXPERT_BLOCK, w1b.shape[2]), F32)
        for s, (lo, hi) in enumerate(_load_packed(x_ref, EXPERT_BLOCK, slabs)):
            lhs = jnp.concatenate([lo.astype(BF16), hi.astype(BF16)], axis=1)
            acc1 = acc1 + _dot(lhs, w1b[s])
            acc3 = acc3 + _dot(lhs, w3b[s])
        hid = (acc1 * _sigmoid(acc1)) * acc3
        _store_packed(o_ref, _dot(hid.astype(BF16), w2b[...]), EXPERT_BLOCK)


def _moe(x_buf, n_active, expert_seq, block_pos, n_used, w1, w3, w2, slabs):
    cap = x_buf.shape[0] // slabs
    d, de = w1.shape[1], w1.shape[2]
    nb = cap // EXPERT_BLOCK

    def xmap(j, na, *_):
        return (jnp.minimum(j, na[0] - 1), 0)

    xspec = pl.BlockSpec((EXPERT_BLOCK * slabs, LANES), xmap)
    hbm = pl.BlockSpec(memory_space=pl.ANY)
    grid_spec = pltpu.PrefetchScalarGridSpec(
        num_scalar_prefetch=4,
        grid=(nb,),
        in_specs=[xspec, hbm, hbm, hbm],
        out_specs=xspec,
        scratch_shapes=[pltpu.VMEM((2, d, de), F32), pltpu.VMEM((2, d, de), F32), pltpu.VMEM((2, de, d), F32),
                        pltpu.VMEM((slabs, 2 * LANES, de), BF16), pltpu.VMEM((slabs, 2 * LANES, de), BF16),
                        pltpu.VMEM((de, d), BF16), pltpu.SemaphoreType.DMA((2, 3))],
    )
    return pl.pallas_call(
        _moe_kernel,
        grid_spec=grid_spec,
        out_shape=jax.ShapeDtypeStruct(x_buf.shape, x_buf.dtype),
        input_output_aliases={4: 0},
        compiler_params=_cparams("arbitrary"),
        name="moe_experts",
    )(n_active, expert_seq, block_pos, n_used, x_buf, w1, w3, w2)


def _combine_kernel(d0_ref, d1_ref, d0n_ref, d1n_ref, y_ref, info_ref, x1_ref, g_ref, gt_ref, o_ref,
                    rows, sem, *, slabs, tiles_per_idx, chunk):
    i = pl.program_id(0)
    n_tiles = pl.num_programs(0)
    tm = x1_ref.shape[0]
    slot = i % 2
    nxt_slot = 1 - slot
    nxt = jnp.minimum(i + 1, n_tiles - 1)

    def copy(tile, to_slot, t, dest_ref, k):
        off = (tile % tiles_per_idx) * tm
        src = pl.multiple_of(dest_ref[off + t], slabs)
        dst = pl.multiple_of(t * slabs, slabs)
        return pltpu.make_async_copy(y_ref.at[pl.ds(src, slabs)], rows.at[to_slot, k, pl.ds(dst, slabs)],
                                     sem.at[to_slot])

    def drain(which):
        for k in range(2):
            pltpu.make_async_copy(y_ref.at[pl.ds(0, tm * slabs)], rows.at[which, k], sem.at[which]).wait()

    @pl.when(i == 0)
    def _():
        def start(t, c):
            copy(i, slot, t, d0_ref, 0).start(priority=0)
            copy(i, slot, t, d1_ref, 1).start(priority=1)
            return c

        lax.fori_loop(0, tm, start, 0, unroll=4)

    drain(slot)
    g_row = g_ref[...]
    gt_row = gt_ref[0]
    for c in range(tm // chunk):
        tok = slice(c * chunk, (c + 1) * chunk)
        info = info_ref[tok, :]
        w1 = info[:, 2:3]
        w2 = info[:, 3:4]
        lo_parts, hi_parts = [], []
        for s in range(slabs):
            at_s = pl.ds(c * chunk * slabs + s, chunk, stride=slabs)
            lo0, hi0 = _unpack_pair(rows[slot, 0, at_s, :])
            lo1, hi1 = _unpack_pair(rows[slot, 1, at_s, :])
            lo_parts.append(w1 * lo0 + w2 * lo1)
            hi_parts.append(w1 * hi0 + w2 * hi1)
        ffn = jnp.concatenate(lo_parts + hi_parts, axis=1)
        o_ref[tok, :] = x1_ref[tok, :] + gt_row * _rms(ffn, g_row)
        for t in range(c * chunk, (c + 1) * chunk):
            copy(nxt, nxt_slot, t, d0n_ref, 0).start(priority=0)
            copy(nxt, nxt_slot, t, d1n_ref, 1).start(priority=1)

    @pl.when(i == n_tiles - 1)
    def _():
        drain(nxt_slot)


def _combine(y_buf, dest_rows, info, x1, g, gt2, seqlen, slabs, tm):
    n, d = x1.shape
    tpb = seqlen // tm
    idx_block = max(tm, SMEM_INDEX_BLOCK)
    per = idx_block // tm
    last = n // tm - 1
    smem = pl.BlockSpec((idx_block,), lambda i: (i // per,), memory_space=pltpu.SMEM)
    smem_next = pl.BlockSpec((idx_block,), lambda i: (jnp.minimum(i + 1, last) // per,), memory_space=pltpu.SMEM)
    return pl.pallas_call(
        functools.partial(_combine_kernel, slabs=slabs, tiles_per_idx=per, chunk=min(tm, COMBINE_CHUNK)),
        grid=(n // tm,),
        in_specs=[smem, smem, smem_next, smem_next,
                  pl.BlockSpec(memory_space=pl.ANY),
                  pl.BlockSpec((tm, ROUTE_LANES), lambda i: (i, 0)),
                  pl.BlockSpec((tm, d), lambda i: (i, 0)),
                  pl.BlockSpec((1, d), lambda i: (0, 0)),
                  pl.BlockSpec((1, 1, d), lambda i: (i // tpb, 0, 0))],
        out_specs=pl.BlockSpec((tm, d), lambda i: (i, 0)),
        out_shape=jax.ShapeDtypeStruct((n, d), F32),
        scratch_shapes=[pltpu.VMEM((2, 2, tm * slabs, LANES), y_buf.dtype), pltpu.SemaphoreType.DMA((2,))],
        compiler_params=_cparams("arbitrary"),
        name="moe_combine",
    )(dest_rows[0], dest_rows[1], dest_rows[0], dest_rows[1], y_buf, info, x1, g.reshape(1, d),
      gt2[:, None, :])


def _pick(n, pref):
    while n % pref:
        pref //= 2
    return pref


def _layer(x2, mod, p, bsz, seqlen):
    n, d = x2.shape
    sh1, sc1, gt1, sh2, sc2, gt2 = jnp.split(mod, 6, axis=-1)
    d_ssm = p["ssm_d"].shape[0]
    d_r = p["rwkv_w0"].shape[0]
    tm = _pick(seqlen, 1024)

    w_in = p["w_in"].astype(BF16)
    w_z = w_in[:, d_ssm:]
    dz = w_z.shape[1]
    z = _norm_proj(x2, p["norm_mix_pre"], sc1, sh1, w_z, seqlen, tm,
                   dz // 2 if (dz // 2) % 128 == 0 else dz, "in_proj_z")

    tabs = _s5_tables(p["ssm_lam_re"], p["ssm_lam_im"], p["ssm_log_dt"], p["ssm_b_re"], p["ssm_b_im"],
                      p["ssm_c_re"], p["ssm_c_im"])
    y_ssm = _s5_glu(x2, p["norm_mix_pre"], sc1, sh1, w_in[:, :d_ssm], tabs, p["ssm_d"], p["glu_w"],
                    p["glu_b"], bsz, seqlen, _pick(seqlen, S5_TIME_BLOCK))

    y_rwkv = _rwkv(z, p["rwkv_mu"], p["rwkv_w0"], p["rwkv_w_up"], p["rwkv_a0"], p["rwkv_a_up"],
                   p["rwkv_g_up"], p["rwkv_k_k"], p["rwkv_k_a"], p["rwkv_r_k"], p["rwkv_ln_w"],
                   p["rwkv_ln_b"], bsz, seqlen)

    w_route = jnp.zeros((d, ROUTE_LANES), F32)
    w_route = w_route.at[:, :N_EXPERTS].set(p["moe_w_exp"].astype(F32))
    w_route = w_route.at[:, N_EXPERTS:N_EXPERTS + N_GROUPS].set(p["moe_w_grp"].astype(F32))
    b_route = jnp.zeros((ROUTE_LANES,), F32)
    b_route = b_route.at[:N_EXPERTS].set(p["moe_b_exp"].astype(F32))
    b_route = b_route.at[N_EXPERTS:N_EXPERTS + N_GROUPS].set(p["moe_b_grp"].astype(F32))
    x1, h2p, logits = _post_mix(y_ssm, y_rwkv, p["w_out"], x2, p["norm_mix_post"], gt1, p["norm_ffn_pre"],
                                sc2, sh2, w_route, b_route, seqlen, _pick(seqlen, 512))
    slabs = d // (2 * LANES)

    info, counts = _route(logits, _pick(n, 512))
    cnt = counts[0, :N_EXPERTS].astype(jnp.int32)
    padded = (cnt + EXPERT_BLOCK - 1) // EXPERT_BLOCK * EXPERT_BLOCK
    pend = jnp.cumsum(padded)
    pstart = pend - padded
    n_blocks = -(-(2 * n) // EXPERT_BLOCK) + N_EXPERTS
    cap = n_blocks * EXPERT_BLOCK
    seg_row = jnp.zeros((1, ROUTE_LANES), F32).at[0, :N_EXPERTS].set((pstart * slabs).astype(F32))
    dest = _slot_rows(info, seg_row, slabs, _pick(n, 2048))
    dest_rows = (dest[:, 0], dest[:, 1])
    block_e = jnp.minimum(
        jnp.searchsorted(pend, jnp.arange(n_blocks, dtype=jnp.int32) * EXPERT_BLOCK, side="right"),
        N_EXPERTS - 1).astype(jnp.int32)
    n_active = (pend[-1:] // EXPERT_BLOCK).astype(jnp.int32)
    used = cnt > 0
    expert_seq = jnp.nonzero(used, size=N_EXPERTS, fill_value=0)[0].astype(jnp.int32)
    block_pos = (jnp.cumsum(used.astype(jnp.int32)) - 1)[block_e].astype(jnp.int32)
    n_used = jnp.sum(used.astype(jnp.int32)).reshape(1)

    x_buf = _dispatch(h2p, dest_rows, cap, slabs, _pick(n, 1024))
    y_buf = _moe(x_buf, n_active, expert_seq, block_pos, n_used, p["moe_w1"], p["moe_w3"], p["moe_w2"], slabs)
    return _combine(y_buf, dest_rows, info, x1, p["norm_ffn_post"], gt2, seqlen, slabs, _pick(seqlen, 512))


def kernel(x, c, ada_w, ada_b, norm_mix_pre, norm_mix_post, norm_ffn_pre, norm_ffn_post, w_in, w_out, ssm_lam_re, ssm_lam_im, ssm_log_dt, ssm_b_re, ssm_b_im, ssm_c_re, ssm_c_im, ssm_d, glu_w, glu_b, rwkv_mu, rwkv_w0, rwkv_w_up, rwkv_a0, rwkv_a_up, rwkv_g_up, rwkv_k_k, rwkv_k_a, rwkv_r_k, rwkv_ln_w, rwkv_ln_b, moe_w_grp, moe_b_grp, moe_w_exp, moe_b_exp, moe_w1, moe_w3, moe_w2):
    bsz, seqlen, d = x.shape
    params = dict(norm_mix_pre=norm_mix_pre, norm_mix_post=norm_mix_post, norm_ffn_pre=norm_ffn_pre,
                  norm_ffn_post=norm_ffn_post, w_in=w_in, w_out=w_out, ssm_lam_re=ssm_lam_re,
                  ssm_lam_im=ssm_lam_im, ssm_log_dt=ssm_log_dt, ssm_b_re=ssm_b_re, ssm_b_im=ssm_b_im,
                  ssm_c_re=ssm_c_re, ssm_c_im=ssm_c_im, ssm_d=ssm_d, glu_w=glu_w, glu_b=glu_b,
                  rwkv_mu=rwkv_mu, rwkv_w0=rwkv_w0, rwkv_w_up=rwkv_w_up, rwkv_a0=rwkv_a0,
                  rwkv_a_up=rwkv_a_up, rwkv_g_up=rwkv_g_up, rwkv_k_k=rwkv_k_k, rwkv_k_a=rwkv_k_a,
                  rwkv_r_k=rwkv_r_k, rwkv_ln_w=rwkv_ln_w, rwkv_ln_b=rwkv_ln_b, moe_w_grp=moe_w_grp,
                  moe_b_grp=moe_b_grp, moe_w_exp=moe_w_exp, moe_b_exp=moe_b_exp, moe_w1=moe_w1,
                  moe_w3=moe_w3, moe_w2=moe_w2)
    x2 = x.reshape(bsz * seqlen, d)
    for layer in range(ada_w.shape[0]):
        mod = _ada(c, ada_w[layer], ada_b[layer])
        x2 = _layer(x2, mod, {k: v[layer] for k, v in params.items()}, bsz, seqlen)
    return x2.reshape(bsz, seqlen, d)
```

```python
import functools
import math

import jax
import jax.numpy as jnp
from jax import lax
from jax.experimental import pallas as pl
from jax.experimental.pallas import tpu as pltpu

F32 = jnp.float32
BF16 = jnp.bfloat16

SSM_GROUP = 16
SSM_STATE = 64
S5_SLAB = 256
S5_TIME_BLOCK = 64
RWKV_HEAD = 64
RWKV_CHUNK = 64
RWKV_BATCH_PER_STEP = 4
HEADS_PER_TILE = 4
LORA_W = 64
LORA_A = 64
LORA_G = 128
N_GROUPS = 8
EXPERTS_PER_GROUP = 8
N_EXPERTS = N_GROUPS * EXPERTS_PER_GROUP
EXPERT_BLOCK = 256
RMS_EPS = 1e-6
GN_EPS = 64e-5
DECAY_SCALE = math.exp(-0.5)
LANES = 128
COMBINE_CHUNK = 64
SMEM_INDEX_BLOCK = 1024
ROUTE_LANES = LANES
VMEM_LIMIT = 52 * 1024 * 1024


def _cparams(*sem):
    return pltpu.CompilerParams(dimension_semantics=sem, vmem_limit_bytes=VMEM_LIMIT)


def _sigmoid(x):
    return 1.0 / (1.0 + jnp.exp(-x))


def _dot(a, b):
    return jnp.dot(a, b, preferred_element_type=F32)


def _dot_nt(a, b):
    return lax.dot_general(a, b, (((1,), (1,)), ((), ())), preferred_element_type=F32)


def _dot_tn(a, b):
    return lax.dot_general(a, b, (((0,), (0,)), ((), ())), preferred_element_type=F32)


def _split_bf16(x):
    hi = x.astype(BF16)
    lo = (x - hi.astype(F32)).astype(BF16)
    return hi, lo


def _pack_pair(a, b):
    ua = lax.bitcast_convert_type(a.astype(BF16).astype(F32), jnp.uint32)
    ub = lax.bitcast_convert_type(b.astype(BF16).astype(F32), jnp.uint32)
    return ub | (ua >> 16)


def _unpack_pair(w):
    lo = lax.bitcast_convert_type(w << 16, F32)
    hi = lax.bitcast_convert_type(w & jnp.uint32(0xFFFF0000), F32)
    return lo, hi


def _store_packed(ref, val, n_rows):
    d = val.shape[1]
    slabs = d // (2 * LANES)
    for s in range(slabs):
        a = val[:, s * LANES:(s + 1) * LANES]
        b = val[:, d // 2 + s * LANES:d // 2 + (s + 1) * LANES]
        ref[pl.ds(s, n_rows, stride=slabs), :] = _pack_pair(a, b)


def _load_packed(ref, n_rows, slabs):
    return [_unpack_pair(ref[pl.ds(s, n_rows, stride=slabs), :]) for s in range(slabs)]


def _ada_kernel(c_ref, w_ref, b_ref, o_ref):
    c = c_ref[...]
    cond = c * _sigmoid(c)
    o_ref[...] = jnp.dot(cond, w_ref[...], preferred_element_type=F32,
                         precision=lax.Precision.HIGHEST) + b_ref[...]


def _ada(c, ada_w, ada_b):
    bsz, d = c.shape
    n = ada_w.shape[1]
    tn = 1024
    return pl.pallas_call(
        _ada_kernel,
        grid=(n // tn,),
        in_specs=[pl.BlockSpec((bsz, d), lambda j: (0, 0)),
                  pl.BlockSpec((d, tn), lambda j: (0, j)),
                  pl.BlockSpec((1, tn), lambda j: (0, j))],
        out_specs=pl.BlockSpec((bsz, tn), lambda j: (0, j)),
        out_shape=jax.ShapeDtypeStruct((bsz, n), F32),
        compiler_params=_cparams("arbitrary"),
        name="ada_mod",
    )(c, ada_w, ada_b.reshape(1, n))


def _rms(x, g):
    return x * lax.rsqrt(jnp.mean(x * x, axis=-1, keepdims=True) + RMS_EPS) * g


def _norm_proj_kernel(x_ref, g_ref, sc_ref, sh_ref, w_ref, o_ref):
    h = _rms(x_ref[...], g_ref[...]) * (1.0 + sc_ref[0]) + sh_ref[0]
    o_ref[...] = _dot(h.astype(BF16), w_ref[...]).astype(o_ref.dtype)


def _norm_proj(x2, g, sc, sh, w, seqlen, tm, name):
    n, d = x2.shape
    nout = w.shape[1]
    tpb = seqlen // tm
    return pl.pallas_call(
        _norm_proj_kernel,
        grid=(n // tm,),
        in_specs=[pl.BlockSpec((tm, d), lambda i: (i, 0)),
                  pl.BlockSpec((1, d), lambda i: (0, 0)),
                  pl.BlockSpec((1, 1, d), lambda i: (i // tpb, 0, 0)),
                  pl.BlockSpec((1, 1, d), lambda i: (i // tpb, 0, 0)),
                  pl.BlockSpec((d, nout), lambda i: (0, 0), pipeline_mode=pl.Buffered(1))],
        out_specs=pl.BlockSpec((tm, nout), lambda i: (i, 0)),
        out_shape=jax.ShapeDtypeStruct((n, nout), BF16),
        compiler_params=_cparams("parallel"),
        name=name,
    )(x2, g.reshape(1, d), sc[:, None, :], sh[:, None, :], w)


def _s5_tables(lam_re, lam_im, log_dt, b_re, b_im, c_re, c_im):
    g, p, cg = b_re.shape
    gs = S5_SLAB // cg
    ns = g // gs
    lr = jnp.minimum(lam_re.astype(F32), -1e-4)
    li = lam_im.astype(F32)
    dt = jnp.exp(log_dt.astype(F32))[:, None]
    mag = jnp.exp(lr * dt)
    ar, ai = mag * jnp.cos(li * dt), mag * jnp.sin(li * dt)
    den = lr * lr + li * li
    qr = ((ar - 1.0) * lr + ai * li) / den
    qi = (ai * lr - (ar - 1.0) * li) / den
    br, bi = b_re.astype(F32), b_im.astype(F32)
    bbr = qr[..., None] * br - qi[..., None] * bi
    bbi = qr[..., None] * bi + qi[..., None] * br

    def in_rows(t):
        return t.reshape(ns, gs, p, cg).transpose(0, 1, 3, 2).reshape(ns, gs * cg, p).astype(BF16)

    def out_cols(t):
        return t.reshape(ns, gs, cg, p).transpose(0, 3, 1, 2).reshape(ns, p, gs * cg).astype(BF16)

    a_tab = jnp.stack([ar.reshape(ns, gs * p), ai.reshape(ns, gs * p)], axis=1)
    return (in_rows(bbr), in_rows(bbi), out_cols(c_re.astype(F32)), out_cols(-c_im.astype(F32)), a_tab)


def _gelu_tanh(x):
    return 0.5 * x * (1.0 + jnp.tanh(math.sqrt(2.0 / math.pi) * (x + 0.044715 * (x * x * x))))


def _s5_kernel(x_ref, g_ref, sc_ref, sh_ref, wu_ref, perm_ref, bre_ref, bim_ref, cre_ref, cim_ref, a_ref,
               d_ref, gw_ref, gb_ref, o_ref, b_ref, c_ref, u_scr, bscr, sscr, yscr, st_ref):
    bsz, lb, d_in = x_ref.shape
    rows = bsz * lb
    dch = wu_ref.shape[1]
    half = st_ref.shape[2] // 2
    n_p = bre_ref.shape[2]

    @pl.when(pl.program_id(0) == 0)
    def _():
        st_ref[...] = jnp.zeros_like(st_ref)
        tile_in = (lax.broadcasted_iota(jnp.int32, (n_p, half), 0)
                   == lax.broadcasted_iota(jnp.int32, (n_p, half), 1) % n_p).astype(BF16)
        tile_out = (lax.broadcasted_iota(jnp.int32, (half, n_p), 0) % n_p
                    == lax.broadcasted_iota(jnp.int32, (half, n_p), 1)).astype(BF16)
        in_mask = (lax.broadcasted_iota(jnp.int32, (S5_SLAB, half), 0) // SSM_GROUP
                   == lax.broadcasted_iota(jnp.int32, (S5_SLAB, half), 1) // n_p)
        out_mask = (lax.broadcasted_iota(jnp.int32, (half, S5_SLAB), 0) // n_p
                    == lax.broadcasted_iota(jnp.int32, (half, S5_SLAB), 1) // SSM_GROUP)
        for s in range(dch // S5_SLAB):
            b_ref[s, :, :half] = jnp.where(in_mask, _dot(bre_ref[s], tile_in), 0.0).astype(BF16)
            b_ref[s, :, half:] = jnp.where(in_mask, _dot(bim_ref[s], tile_in), 0.0).astype(BF16)
            c_ref[s, :half, :] = jnp.where(out_mask, _dot(tile_out, cre_ref[s]), 0.0).astype(BF16)
            c_ref[s, half:, :] = jnp.where(out_mask, _dot(tile_out, cim_ref[s]), 0.0).astype(BF16)

    h = (_rms(x_ref[...], g_ref[...]) * (1.0 + sc_ref[...]) + sh_ref[...]).astype(BF16)
    u_nat = _dot(h.reshape(rows, d_in), wu_ref[...]).astype(BF16)
    u_scr[...] = _dot(perm_ref[...], u_nat).astype(BF16)
    n_slab = dch // S5_SLAB
    slab = lambda s: slice(s * S5_SLAB, (s + 1) * S5_SLAB)

    def project_in(s):
        bscr[s % 2] = _dot(u_scr[:, slab(s)], b_ref[s])

    project_in(0)
    for s in range(n_slab):
        if s + 1 < n_slab:
            project_in(s + 1)
        buf = s % 2
        a_re = a_ref[s, 0:1, :]
        a_im = a_ref[s, 1:2, :]
        s_r = st_ref[s, :, :half]
        s_i = st_ref[s, :, half:]
        for l in range(rows // bsz):
            at_l = slice(l * bsz, (l + 1) * bsz)
            s_r, s_i = (a_re * s_r - a_im * s_i + bscr[buf, at_l, :half],
                        a_re * s_i + a_im * s_r + bscr[buf, at_l, half:])
            sscr[buf, at_l, :half] = s_r
            sscr[buf, at_l, half:] = s_i
        st_ref[s, :, :half] = s_r
        st_ref[s, :, half:] = s_i
        y = _dot(sscr[buf].astype(BF16), c_ref[s]) + d_ref[:, slab(s)] * u_scr[:, slab(s)].astype(F32)
        yscr[:, slab(s)] = _gelu_tanh(y).astype(BF16)
    y = yscr[...]
    gate = _sigmoid(_dot(y, gw_ref[...]) + gb_ref[...])
    out_tm = (y.astype(F32) * gate).astype(BF16)
    out_nat = _dot_tn(perm_ref[...], out_tm).astype(o_ref.dtype)
    o_ref[...] = out_nat.reshape(bsz, lb, dch)


def _s5_glu(x2, g, sc, sh, w_u, tabs, d_skip, glu_w, glu_b, bsz, seqlen, lb):
    b_re, b_im, c_re, c_im, a_tab = tabs
    d_in, dch = w_u.shape
    ns, _, n_half = a_tab.shape
    n_state = 2 * n_half
    rows = lb * bsz
    r_idx = jnp.arange(rows)
    perm = ((r_idx % bsz) * lb + r_idx // bsz)[:, None] == r_idx[None, :]
    full = lambda a: pl.BlockSpec(a.shape, lambda i: (0,) * a.ndim, pipeline_mode=pl.Buffered(1))
    args = (x2.reshape(bsz, seqlen, d_in), g.reshape(1, 1, d_in), sc[:, None, :], sh[:, None, :], w_u,
            perm.astype(BF16), b_re, b_im, c_re, c_im, a_tab,
            d_skip.astype(F32).reshape(1, dch), glu_w.astype(BF16), glu_b.astype(F32).reshape(1, dch))
    y = pl.pallas_call(
        _s5_kernel,
        grid=(seqlen // lb,),
        in_specs=[pl.BlockSpec((bsz, lb, d_in), lambda i: (0, i, 0))] + [full(a) for a in args[1:]],
        out_specs=pl.BlockSpec((bsz, lb, dch), lambda i: (0, i, 0)),
        out_shape=jax.ShapeDtypeStruct((bsz, seqlen, dch), BF16),
        scratch_shapes=[pltpu.VMEM((ns, S5_SLAB, n_state), BF16), pltpu.VMEM((ns, n_state, S5_SLAB), BF16),
                        pltpu.VMEM((rows, dch), BF16), pltpu.VMEM((2, rows, n_state), F32),
                        pltpu.VMEM((2, rows, n_state), F32), pltpu.VMEM((rows, dch), BF16),
                        pltpu.VMEM((ns, bsz, n_state), F32)],
        compiler_params=_cparams("arbitrary"),
        name="s5_mixer_glu",
    )(*args)
    return y.reshape(bsz * seqlen, dch)


def _rwkv_kernel(z_ref, mu_ref, w0_ref, a0_ref, kkw_ref, ka_ref, wa_ref, gup_ref, rk_ref, lnw_ref, lnb_ref,
                 ones_ref, o_ref, s_ref, zlast_ref):
    nb, t, _ = z_ref.shape
    d_r = o_ref.shape[2]
    tile = HEADS_PER_TILE * RWKV_HEAD
    n_tiles = d_r // tile
    first_chunk = pl.program_id(1) == 0

    @pl.when(first_chunk)
    def _():
        s_ref[...] = jnp.zeros_like(s_ref)

    ones_bd = ones_ref[...]

    def seg_sum(x):
        return _dot(x.astype(BF16), ones_bd)

    row = lax.broadcasted_iota(jnp.int32, (t, t), 0)
    col = lax.broadcasted_iota(jnp.int32, (t, t), 1)
    tri = (row >= col).astype(BF16)
    st = HEADS_PER_TILE * t
    rs = lax.broadcasted_iota(jnp.int32, (2 * t, 2 * st), 0)
    t_r = rs % t
    t_c = lax.broadcasted_iota(jnp.int32, (2 * t, 2 * st), 1) % t
    keep = (t_r > t_c) | ((rs >= t) & (t_r == t_c))
    eye_w = (lax.broadcasted_iota(jnp.int32, (t, st), 0)
             == lax.broadcasted_iota(jnp.int32, (t, st), 1) % t).astype(F32)
    blk_mask = (lax.broadcasted_iota(jnp.int32, (st, st), 0) // t
                == lax.broadcasted_iota(jnp.int32, (st, st), 1) // t)
    lane = lax.broadcasted_iota(jnp.int32, (1, tile), 1)
    head_masks = [(lane >= j * RWKV_HEAD) & (lane < (j + 1) * RWKV_HEAD) for j in range(HEADS_PER_TILE)]
    bd_r = lax.broadcasted_iota(jnp.int32, (tile, tile), 0) // RWKV_HEAD
    bd_c = lax.broadcasted_iota(jnp.int32, (tile, tile), 1) // RWKV_HEAD
    bd_mask = bd_r == bd_c
    n_levels = int(math.log2(t))
    slices = [slice(hg * tile, (hg + 1) * tile) for hg in range(n_tiles)]
    units = [(bb, hg) for bb in range(nb) for hg in range(n_tiles)]
    n_u = range(len(units))

    def stack(x):
        zero = jnp.zeros_like(x)
        return jnp.concatenate([jnp.where(m, x, zero) for m in head_masks], axis=0)

    def bf(x):
        return x.astype(BF16)

    at, qt, bt, kt, vv, em, etm, wtot, rkb, gate = ([] for _ in range(10))
    row0 = lax.broadcasted_iota(jnp.int32, (t, 1), 0) == 0
    lora_lane = lax.broadcasted_iota(jnp.int32, (t, LORA_W + LORA_A), 1)
    for bb in range(nb):
        z = z_ref[bb].astype(F32)
        prev_row = jnp.where(first_chunk, 0.0, zlast_ref[bb, 0:1, :])
        zl = z + mu_ref[...] * (jnp.where(row0, prev_row, pltpu.roll(z, 1, 0)) - z)
        zlast_ref[bb, 0:1, :] = z[t - 1:t, :]
        xwa = zl[:, 3 * d_r:3 * d_r + LORA_W + LORA_A]
        lhs = jnp.where(lora_lane < LORA_W, jnp.tanh(xwa), xwa).astype(BF16)
        wa = _dot(lhs, wa_ref[...])
        lw = -DECAY_SCALE * _sigmoid(w0_ref[...] + wa[:, :d_r])
        asig = _sigmoid(a0_ref[...] + wa[:, d_r:])
        r = zl[:, :d_r]
        k_raw = zl[:, d_r:2 * d_r]
        kk = k_raw * kkw_ref[...]
        kp = k_raw * (1.0 + (asig - 1.0) * ka_ref[...])
        xg = zl[:, 3 * d_r + LORA_W + LORA_A:3 * d_r + LORA_W + LORA_A + LORA_G]
        g_full = _dot(_sigmoid(xg).astype(BF16), gup_ref[...])
        lw_hi, lw_lo = _split_bf16(lw)
        cum = _dot(tri, lw_hi) + _dot(tri, lw_lo)
        mid = cum[t // 2 - 1:t // 2, :]
        tot = cum[t - 1:t, :]
        e1 = jnp.exp(cum - mid)
        e2 = jnp.exp(mid - cum)
        e1p = e1 * jnp.exp(-lw)
        kk2 = kk * kk
        kkn = kk / jnp.maximum(jnp.sqrt(jnp.concatenate([seg_sum(kk2[:, sl]) for sl in slices], axis=1)), 1e-12)
        full = dict(at=-kkn * e1p, qt=r * e1, bt=kkn * asig * e2, kt=kp * e2,
                    vv=zl[:, 2 * d_r:3 * d_r], em=jnp.exp(mid), etm=jnp.exp(tot - mid), wtot=jnp.exp(tot),
                    rkb=r * kp * rk_ref[...], gate=g_full)
        for dst, key in ((at, "at"), (qt, "qt"), (bt, "bt"), (kt, "kt"), (vv, "vv"), (em, "em"),
                         (etm, "etm"), (wtot, "wtot"), (rkb, "rkb"), (gate, "gate")):
            dst.extend(full[key][:, sl] for sl in slices)

    s_old = [s_ref[i] for i in n_u]
    wide = []
    for i in n_u:
        lhs = bf(jnp.concatenate([at[i], qt[i]], axis=0))
        rhs = jnp.concatenate([stack(bf(bt[i])), stack(bf(kt[i]))], axis=0)
        a_w = jnp.where(keep, _dot_nt(lhs, rhs), 0.0)
        wide.append([a_w[:t, :st], a_w[:t, st:], a_w[t:, :st], a_w[t:, st:]])
    x_state = [_dot_nt(bf(jnp.concatenate([at[i], qt[i]], axis=0) * em[i]), bf(s_old[i]))
               for i in n_u]
    sv = [stack(bf(vv[i])) for i in n_u]
    akv = [_dot(bf(wide[i][1]), sv[i]) for i in n_u]

    def expand(x_w):
        x_b = bf(x_w)
        return jnp.where(blk_mask, jnp.concatenate([x_b] * HEADS_PER_TILE, axis=0), jnp.zeros((st, st), BF16))

    p_acc = [eye_w + w[0] for w in wide]
    q_bd = [expand(w[0]) for w in wide]
    q_pow = [_dot(bf(wide[i][0]), q_bd[i]) for i in n_u]
    for lev in range(1, n_levels):
        for i in n_u:
            q_bd[i] = expand(q_pow[i])
            if lev < n_levels - 1:
                both = _dot(bf(jnp.concatenate([p_acc[i], q_pow[i]], axis=0)), q_bd[i])
                p_acc[i] = p_acc[i] + both[:t]
                q_pow[i] = both[t:]
            else:
                p_acc[i] = p_acc[i] + _dot(bf(p_acc[i]), q_bd[i])
    u_all = [_dot(bf(p_acc[i]), stack(bf(x_state[i][:t] + akv[i]))) for i in n_u]
    y_all = [x_state[i][t:]
             + _dot(bf(jnp.concatenate([wide[i][2], wide[i][3]], axis=1)),
                    jnp.concatenate([stack(bf(u_all[i])), sv[i]], axis=0)) for i in n_u]

    for i, (bb, hg) in enumerate(units):
        sl = slices[hg]
        y = y_all[i]
        uv = bf(jnp.concatenate([u_all[i], vv[i]], axis=0))
        bk_end = bf(jnp.concatenate([bt[i], kt[i]], axis=0) * etm[i])
        s_ref[i] = s_old[i] * wtot[i] + jnp.where(bd_mask, _dot_tn(uv, bk_end), 0.0)

        mean = seg_sum(y) * (1.0 / RWKV_HEAD)
        dlt = y - mean
        var = seg_sum(dlt * dlt) * (1.0 / RWKV_HEAD)
        yn = dlt * lax.rsqrt(var + GN_EPS) * lnw_ref[:, sl] + lnb_ref[:, sl]
        out = (yn + seg_sum(rkb[i]) * vv[i]) * gate[i]
        o_ref[bb, :, sl] = out.astype(o_ref.dtype)


def _rwkv(z, mu, w0, w_up, a0, a_up, g_up, k_k, k_a, r_k, ln_w, ln_b, bsz, seqlen):
    n, dz = z.shape
    d_r = w0.shape[0]
    t = RWKV_CHUNK
    nb = RWKV_BATCH_PER_STEP
    nch = seqlen // t
    tile = HEADS_PER_TILE * RWKV_HEAD
    hid = jnp.arange(tile) // RWKV_HEAD
    ones_bd = (hid[:, None] == hid[None, :]).astype(BF16)
    wa = jnp.zeros((LORA_W + LORA_A, 2 * d_r), F32)
    wa = wa.at[:LORA_W, :d_r].set(w_up.astype(F32)).at[LORA_W:, d_r:].set(a_up.astype(F32)).astype(BF16)
    row = lambda a: a.astype(F32).reshape(1, -1)
    full = lambda a: pl.BlockSpec(a.shape, lambda b, c: (0, 0))
    params = (row(mu), row(w0), row(a0), row(k_k), row(k_a), wa, g_up.astype(BF16), row(r_k), row(ln_w),
              row(ln_b), ones_bd)
    y = pl.pallas_call(
        _rwkv_kernel,
        grid=(bsz // nb, nch),
        in_specs=[pl.BlockSpec((nb, t, dz), lambda b, c: (b, c, 0))] + [full(a) for a in params],
        out_specs=pl.BlockSpec((nb, t, d_r), lambda b, c: (b, c, 0)),
        out_shape=jax.ShapeDtypeStruct((bsz, seqlen, d_r), BF16),
        scratch_shapes=[pltpu.VMEM((nb * (d_r // tile), tile, tile), F32), pltpu.VMEM((nb, 8, dz), F32)],
        compiler_params=_cparams("parallel", "arbitrary"),
        name="rwkv7_chunked",
    )(z.reshape(bsz, seqlen, dz), *params)
    return y.reshape(n, d_r)


def _post_mix_kernel(ys_ref, yr_ref, wo1_ref, wo2_ref, x_ref, g1_ref, gt_ref, g2_ref, sc_ref, sh_ref,
                     wr_both_ref, br_ref, x1_out, h2_out, lg_out):
    mixed = _dot(ys_ref[...], wo1_ref[...]) + _dot(yr_ref[...], wo2_ref[...])
    x1 = x_ref[...] + gt_ref[0] * _rms(mixed, g1_ref[...])
    x1_out[...] = x1
    h2 = _rms(x1, g2_ref[...]) * (1.0 + sc_ref[0]) + sh_ref[0]
    _store_packed(h2_out, h2, h2.shape[0])
    hi, lo = _split_bf16(h2)
    both = _dot(hi, wr_both_ref[...])
    nl = lg_out.shape[1]
    lg_out[...] = both[:, :nl] + both[:, nl:] + _dot(lo, wr_both_ref[:, :nl]) + br_ref[...]


def _post_mix(ys, yr, w_out, x2, g1, gt1, g2, sc2, sh2, w_route, b_route, seqlen, tm):
    n, d = x2.shape
    ds = ys.shape[1]
    tpb = seqlen // tm
    slabs = d // (2 * LANES)
    wo = w_out.astype(BF16)
    wr_both = jnp.concatenate(_split_bf16(w_route), axis=1)
    rows = lambda w: pl.BlockSpec((tm, w), lambda i: (i, 0))
    full = lambda a: pl.BlockSpec(a.shape, lambda i: (0,) * a.ndim)
    bat = pl.BlockSpec((1, 1, d), lambda i: (i // tpb, 0, 0))
    args = (ys, yr, wo[:ds], wo[ds:], x2, g1.reshape(1, d), gt1[:, None, :], g2.reshape(1, d),
            sc2[:, None, :], sh2[:, None, :], wr_both, b_route.reshape(1, -1))
    in_specs = [rows(ds), rows(yr.shape[1]), full(args[2]), full(args[3]), rows(d), full(args[5]), bat,
                full(args[7]), bat, bat, full(wr_both), full(args[11])]
    return pl.pallas_call(
        _post_mix_kernel,
        grid=(n // tm,),
        in_specs=in_specs,
        out_specs=[rows(d), pl.BlockSpec((tm * slabs, LANES), lambda i: (i, 0)), rows(ROUTE_LANES)],
        out_shape=[jax.ShapeDtypeStruct((n, d), F32), jax.ShapeDtypeStruct((n * slabs, LANES), jnp.uint32),
                   jax.ShapeDtypeStruct((n, ROUTE_LANES), F32)],
        compiler_params=_cparams("parallel"),
        name="out_proj_post",
    )(*args)


def _route_kernel(lg_ref, info_ref, cnt_ref, carry):
    i = pl.program_id(0)
    tm = lg_ref.shape[0]

    @pl.when(i == 0)
    def _():
        carry[...] = jnp.zeros_like(carry)

    lg = lg_ref[...]
    lane = lax.broadcasted_iota(jnp.int32, lg.shape, 1)
    lane_f = lane.astype(F32)
    neg = jnp.float32(-jnp.inf)
    big = jnp.float32(1e9)
    is_g = (lane >= N_EXPERTS) & (lane < N_EXPERTS + N_GROUPS)
    gl = jnp.where(is_g, lg, neg)
    gmax = jnp.max(gl, axis=-1, keepdims=True)
    gidx = jnp.min(jnp.where(gl == gmax, lane_f - N_EXPERTS, big), axis=-1, keepdims=True)
    p_grp = 1.0 / jnp.sum(jnp.where(is_g, jnp.exp(gl - gmax), 0.0), axis=-1, keepdims=True)
    in_grp = (lane < N_EXPERTS) & ((lane // EXPERTS_PER_GROUP).astype(F32) == gidx)
    el = jnp.where(in_grp, lg, neg)
    m1 = jnp.max(el, axis=-1, keepdims=True)
    i1 = jnp.min(jnp.where(el == m1, lane_f, big), axis=-1, keepdims=True)
    el2 = jnp.where(lane_f == i1, neg, el)
    m2 = jnp.max(el2, axis=-1, keepdims=True)
    i2 = jnp.min(jnp.where(el2 == m2, lane_f, big), axis=-1, keepdims=True)
    ex = jnp.exp(m2 - m1)
    w1 = p_grp / (1.0 + ex)
    w2 = p_grp * ex / (1.0 + ex)

    oh1 = lane_f == i1
    oh2 = lane_f == i2
    onehot = (oh1 | oh2).astype(BF16)
    rr = lax.broadcasted_iota(jnp.int32, (tm, tm), 0)
    cc = lax.broadcasted_iota(jnp.int32, (tm, tm), 1)
    before = _dot((rr > cc).astype(BF16), onehot) + carry[...]
    rank1 = jnp.sum(jnp.where(oh1, before, 0.0), axis=-1, keepdims=True)
    rank2 = jnp.sum(jnp.where(oh2, before, 0.0), axis=-1, keepdims=True)
    carry[...] = carry[...] + jnp.sum(onehot.astype(F32), axis=0, keepdims=True)
    cnt_ref[...] = carry[...]

    info = jnp.where(lane == 0, i1, 0.0)
    info = jnp.where(lane == 1, i2, info)
    info = jnp.where(lane == 2, w1, info)
    info = jnp.where(lane == 3, w2, info)
    info = jnp.where(lane == 4, rank1, info)
    info = jnp.where(lane == 5, rank2, info)
    info_ref[...] = info


def _route(logits, tm):
    n = logits.shape[0]
    return pl.pallas_call(
        _route_kernel,
        grid=(n // tm,),
        in_specs=[pl.BlockSpec((tm, ROUTE_LANES), lambda i: (i, 0))],
        out_specs=[pl.BlockSpec((tm, ROUTE_LANES), lambda i: (i, 0)),
                   pl.BlockSpec((1, ROUTE_LANES), lambda i: (0, 0))],
        out_shape=[jax.ShapeDtypeStruct((n, ROUTE_LANES), F32),
                   jax.ShapeDtypeStruct((1, ROUTE_LANES), F32)],
        scratch_shapes=[pltpu.VMEM((1, ROUTE_LANES), F32)],
        compiler_params=_cparams("arbitrary"),
        name="moe_route",
    )(logits)


def _slot_rows_kernel(info_ref, seg_ref, o_ref, *, slabs):
    info = info_ref[...]
    lane = lax.broadcasted_iota(jnp.int32, info.shape, 1)
    lane_f = lane.astype(F32)
    seg = seg_ref[...]
    d0 = jnp.sum(jnp.where(lane_f == info[:, 0:1], seg, 0.0), axis=-1, keepdims=True) + info[:, 4:5] * slabs
    d1 = jnp.sum(jnp.where(lane_f == info[:, 1:2], seg, 0.0), axis=-1, keepdims=True) + info[:, 5:6] * slabs
    o_ref[...] = jnp.where(lane == 0, d0, jnp.where(lane == 1, d1, 0.0)).astype(jnp.int32)


def _slot_rows(info, seg_row, slabs, tm):
    n = info.shape[0]
    return pl.pallas_call(
        functools.partial(_slot_rows_kernel, slabs=slabs),
        grid=(n // tm,),
        in_specs=[pl.BlockSpec((tm, ROUTE_LANES), lambda i: (i, 0)),
                  pl.BlockSpec((1, ROUTE_LANES), lambda i: (0, 0))],
        out_specs=pl.BlockSpec((tm, ROUTE_LANES), lambda i: (i, 0)),
        out_shape=jax.ShapeDtypeStruct((n, ROUTE_LANES), jnp.int32),
        compiler_params=_cparams("parallel"),
        name="moe_slot_rows",
    )(info, seg_row)


def _dispatch_kernel(d0_ref, d1_ref, pad_row_ref, pad_len_ref, na_ref, h_ref, buf_out, zeros, sem, zsem, *,
                     slabs, n_blocks):
    tm = h_ref.shape[0] // slabs

    @pl.when(pl.program_id(0) == 0)
    def _():
        zeros[...] = jnp.zeros_like(zeros)
        bits = [1 << b for b in reversed(range(EXPERT_BLOCK.bit_length() - 1))]

        def pad_copy(e, bit):
            done = pad_len_ref[e] & ~(2 * bit - 1)
            dst = pl.multiple_of(pad_row_ref[e] + done * slabs, slabs)
            return pltpu.make_async_copy(zeros.at[pl.ds(0, bit * slabs)], buf_out.at[pl.ds(dst, bit * slabs)], zsem)

        def block_copy(j):
            dst = pl.multiple_of(j * (EXPERT_BLOCK * slabs), EXPERT_BLOCK * slabs)
            return pltpu.make_async_copy(zeros, buf_out.at[pl.ds(dst, EXPERT_BLOCK * slabs)], zsem)

        def each(action):
            def per_expert(e, c):
                for bit in bits:
                    @pl.when((pad_len_ref[e] & bit) != 0)
                    def _():
                        action(pad_copy(e, bit))
                return c

            def per_block(j, c):
                action(block_copy(j))
                return c

            lax.fori_loop(0, N_EXPERTS, per_expert, 0)
            lax.fori_loop(na_ref[0], n_blocks, per_block, 0)

        each(lambda c: c.start())
        each(lambda c: c.wait())

    def copy(t, dest_ref):
        src = pl.multiple_of(t * slabs, slabs)
        dst = pl.multiple_of(dest_ref[t], slabs)
        return pltpu.make_async_copy(h_ref.at[pl.ds(src, slabs)], buf_out.at[pl.ds(dst, slabs)], sem)

    def start(t, c):
        copy(t, d0_ref).start(priority=0)
        copy(t, d1_ref).start(priority=1)
        return c

    lax.fori_loop(0, tm, start, 0, unroll=4)
    for _ in range(2):
        pltpu.make_async_copy(h_ref, buf_out.at[pl.ds(0, tm * slabs)], sem).wait()


def _dispatch(h2p, dest_rows, pad_row, pad_len, n_active, cap, slabs, tm):
    n = h2p.shape[0] // slabs
    smem = pl.BlockSpec((tm,), lambda i: (i,), memory_space=pltpu.SMEM)
    table = lambda a: pl.BlockSpec(a.shape, lambda i: (0,), memory_space=pltpu.SMEM)
    return pl.pallas_call(
        functools.partial(_dispatch_kernel, slabs=slabs, n_blocks=cap // EXPERT_BLOCK),
        grid=(n // tm,),
        in_specs=[smem, smem, table(pad_row), table(pad_len), table(n_active),
                  pl.BlockSpec((tm * slabs, LANES), lambda i: (i, 0))],
        out_specs=pl.BlockSpec(memory_space=pl.ANY),
        out_shape=jax.ShapeDtypeStruct((cap * slabs, LANES), h2p.dtype),
        scratch_shapes=[pltpu.VMEM((EXPERT_BLOCK * slabs, LANES), h2p.dtype), pltpu.SemaphoreType.DMA,
                        pltpu.SemaphoreType.DMA],
        compiler_params=_cparams("arbitrary"),
        name="moe_dispatch",
    )(dest_rows[0], dest_rows[1], pad_row, pad_len, n_active, h2p)


def _moe_kernel(na_ref, eseq_ref, epos_ref, nd_ref, x_ref, w1_hbm, w3_hbm, w2_hbm, o_ref,
                w1f, w3f, w2f, w1b, w3b, w2b, sem):
    j = pl.program_id(0)
    active = j < na_ref[0]
    pos = epos_ref[j]
    fresh = (j == 0) | (pos != epos_ref[jnp.maximum(j - 1, 0)])

    slabs = w1b.shape[0]
    half = slabs * LANES

    def weight_copies(p):
        e = eseq_ref[p]
        slot = p % 2
        return [pltpu.make_async_copy(w_hbm.at[e], stage.at[slot], sem.at[slot, i])
                for i, (w_hbm, stage) in enumerate(((w1_hbm, w1f), (w3_hbm, w3f), (w2_hbm, w2f)))]

    @pl.when(j == 0)
    def _():
        for c in weight_copies(0):
            c.start()

    @pl.when(active & fresh)
    def _():
        for c in weight_copies(pos):
            c.wait()

        @pl.when(pos + 1 < nd_ref[0])
        def _():
            for c in weight_copies(pos + 1):
                c.start()

        slot = pos % 2
        for s in range(slabs):
            for dst, src in ((w1b, w1f), (w3b, w3f)):
                dst[s, :LANES, :] = src[slot, s * LANES:(s + 1) * LANES, :].astype(BF16)
                dst[s, LANES:, :] = src[slot, half + s * LANES:half + (s + 1) * LANES, :].astype(BF16)
        w2b[...] = w2f[slot].astype(BF16)

    @pl.when(active)
    def _():
        acc1 = jnp.zeros((EXPERT_BLOCK, w1b.shape[2]), F32)
        acc3 = jnp.zeros((EXPERT_BLOCK, w1b.shape[2]), F32)
        for s, (lo, hi) in enumerate(_load_packed(x_ref, EXPERT_BLOCK, slabs)):
            lhs = jnp.concatenate([lo.astype(BF16), hi.astype(BF16)], axis=1)
            acc1 = acc1 + _dot(lhs, w1b[s])
            acc3 = acc3 + _dot(lhs, w3b[s])
        hid = (acc1 * _sigmoid(acc1)) * acc3
        _store_packed(o_ref, _dot(hid.astype(BF16), w2b[...]), EXPERT_BLOCK)


def _moe(x_buf, n_active, expert_seq, block_pos, n_used, w1, w3, w2, slabs):
    cap = x_buf.shape[0] // slabs
    d, de = w1.shape[1], w1.shape[2]
    nb = cap // EXPERT_BLOCK

    def xmap(j, na, *_):
        return (jnp.minimum(j, na[0] - 1), 0)

    xspec = pl.BlockSpec((EXPERT_BLOCK * slabs, LANES), xmap)
    hbm = pl.BlockSpec(memory_space=pl.ANY)
    grid_spec = pltpu.PrefetchScalarGridSpec(
        num_scalar_prefetch=4,
        grid=(nb,),
        in_specs=[xspec, hbm, hbm, hbm],
        out_specs=xspec,
        scratch_shapes=[pltpu.VMEM((2, d, de), F32), pltpu.VMEM((2, d, de), F32), pltpu.VMEM((2, de, d), F32),
                        pltpu.VMEM((slabs, 2 * LANES, de), BF16), pltpu.VMEM((slabs, 2 * LANES, de), BF16),
                        pltpu.VMEM((de, d), BF16), pltpu.SemaphoreType.DMA((2, 3))],
    )
    return pl.pallas_call(
        _moe_kernel,
        grid_spec=grid_spec,
        out_shape=jax.ShapeDtypeStruct(x_buf.shape, x_buf.dtype),
        input_output_aliases={4: 0},
        compiler_params=_cparams("arbitrary"),
        name="moe_experts",
    )(n_active, expert_seq, block_pos, n_used, x_buf, w1, w3, w2)


def _combine_kernel(d0_ref, d1_ref, d0n_ref, d1n_ref, y_ref, info_ref, x1_ref, g_ref, gt_ref, o_ref,
                    rows, sem, *, slabs, tiles_per_idx, chunk):
    i = pl.program_id(0)
    n_tiles = pl.num_programs(0)
    tm = x1_ref.shape[0]
    slot = i % 2
    nxt_slot = 1 - slot
    nxt = jnp.minimum(i + 1, n_tiles - 1)

    def copy(tile, to_slot, t, dest_ref, k):
        off = (tile % tiles_per_idx) * tm
        src = pl.multiple_of(dest_ref[off + t], slabs)
        dst = pl.multiple_of(t * slabs, slabs)
        return pltpu.make_async_copy(y_ref.at[pl.ds(src, slabs)], rows.at[to_slot, k, pl.ds(dst, slabs)],
                                     sem.at[to_slot])

    def drain(which):
        for k in range(2):
            pltpu.make_async_copy(y_ref.at[pl.ds(0, tm * slabs)], rows.at[which, k], sem.at[which]).wait()

    @pl.when(i == 0)
    def _():
        def start(t, c):
            copy(i, slot, t, d0_ref, 0).start(priority=0)
            copy(i, slot, t, d1_ref, 1).start(priority=1)
            return c

        lax.fori_loop(0, tm, start, 0, unroll=4)

    drain(slot)
    g_row = g_ref[...]
    gt_row = gt_ref[0]
    for c in range(tm // chunk):
        tok = slice(c * chunk, (c + 1) * chunk)
        info = info_ref[tok, :]
        w1 = info[:, 2:3]
        w2 = info[:, 3:4]
        lo_parts, hi_parts = [], []
        for s in range(slabs):
            at_s = pl.ds(c * chunk * slabs + s, chunk, stride=slabs)
            lo0, hi0 = _unpack_pair(rows[slot, 0, at_s, :])
            lo1, hi1 = _unpack_pair(rows[slot, 1, at_s, :])
            lo_parts.append(w1 * lo0 + w2 * lo1)
            hi_parts.append(w1 * hi0 + w2 * hi1)
        ffn = jnp.concatenate(lo_parts + hi_parts, axis=1)
        o_ref[tok, :] = x1_ref[tok, :] + gt_row * _rms(ffn, g_row)
        for t in range(c * chunk, (c + 1) * chunk):
            copy(nxt, nxt_slot, t, d0n_ref, 0).start(priority=0)
            copy(nxt, nxt_slot, t, d1n_ref, 1).start(priority=1)

    @pl.when(i == n_tiles - 1)
    def _():
        drain(nxt_slot)


def _combine(y_buf, dest_rows, info, x1, g, gt2, seqlen, slabs, tm):
    n, d = x1.shape
    tpb = seqlen // tm
    idx_block = max(tm, SMEM_INDEX_BLOCK)
    per = idx_block // tm
    last = n // tm - 1
    smem = pl.BlockSpec((idx_block,), lambda i: (i // per,), memory_space=pltpu.SMEM)
    smem_next = pl.BlockSpec((idx_block,), lambda i: (jnp.minimum(i + 1, last) // per,), memory_space=pltpu.SMEM)
    return pl.pallas_call(
        functools.partial(_combine_kernel, slabs=slabs, tiles_per_idx=per, chunk=min(tm, COMBINE_CHUNK)),
        grid=(n // tm,),
        in_specs=[smem, smem, smem_next, smem_next,
                  pl.BlockSpec(memory_space=pl.ANY),
                  pl.BlockSpec((tm, ROUTE_LANES), lambda i: (i, 0)),
                  pl.BlockSpec((tm, d), lambda i: (i, 0)),
                  pl.BlockSpec((1, d), lambda i: (0, 0)),
                  pl.BlockSpec((1, 1, d), lambda i: (i // tpb, 0, 0))],
        out_specs=pl.BlockSpec((tm, d), lambda i: (i, 0)),
        out_shape=jax.ShapeDtypeStruct((n, d), F32),
        scratch_shapes=[pltpu.VMEM((2, 2, tm * slabs, LANES), y_buf.dtype), pltpu.SemaphoreType.DMA((2,))],
        compiler_params=_cparams("arbitrary"),
        name="moe_combine",
    )(dest_rows[0], dest_rows[1], dest_rows[0], dest_rows[1], y_buf, info, x1, g.reshape(1, d),
      gt2[:, None, :])


def _pick(n, pref):
    while n % pref:
        pref //= 2
    return pref


def _layer(x2, mod, p, bsz, seqlen):
    n, d = x2.shape
    sh1, sc1, gt1, sh2, sc2, gt2 = jnp.split(mod, 6, axis=-1)
    d_ssm = p["ssm_d"].shape[0]

    w_in = p["w_in"].astype(BF16)
    z = _norm_proj(x2, p["norm_mix_pre"], sc1, sh1, w_in[:, d_ssm:], seqlen, _pick(seqlen, 512), "in_proj_z")

    tabs = _s5_tables(p["ssm_lam_re"], p["ssm_lam_im"], p["ssm_log_dt"], p["ssm_b_re"], p["ssm_b_im"],
                      p["ssm_c_re"], p["ssm_c_im"])
    y_ssm = _s5_glu(x2, p["norm_mix_pre"], sc1, sh1, w_in[:, :d_ssm], tabs, p["ssm_d"], p["glu_w"],
                    p["glu_b"], bsz, seqlen, _pick(seqlen, S5_TIME_BLOCK))

    y_rwkv = _rwkv(z, p["rwkv_mu"], p["rwkv_w0"], p["rwkv_w_up"], p["rwkv_a0"], p["rwkv_a_up"],
                   p["rwkv_g_up"], p["rwkv_k_k"], p["rwkv_k_a"], p["rwkv_r_k"], p["rwkv_ln_w"],
                   p["rwkv_ln_b"], bsz, seqlen)

    w_route = jnp.zeros((d, ROUTE_LANES), F32)
    w_route = w_route.at[:, :N_EXPERTS].set(p["moe_w_exp"].astype(F32))
    w_route = w_route.at[:, N_EXPERTS:N_EXPERTS + N_GROUPS].set(p["moe_w_grp"].astype(F32))
    b_route = jnp.zeros((ROUTE_LANES,), F32)
    b_route = b_route.at[:N_EXPERTS].set(p["moe_b_exp"].astype(F32))
    b_route = b_route.at[N_EXPERTS:N_EXPERTS + N_GROUPS].set(p["moe_b_grp"].astype(F32))
    x1, h2p, logits = _post_mix(y_ssm, y_rwkv, p["w_out"], x2, p["norm_mix_post"], gt1, p["norm_ffn_pre"],
                                sc2, sh2, w_route, b_route, seqlen, _pick(seqlen, 512))
    slabs = d // (2 * LANES)

    info, counts = _route(logits, _pick(n, 512))
    cnt = counts[0, :N_EXPERTS].astype(jnp.int32)
    padded = (cnt + EXPERT_BLOCK - 1) // EXPERT_BLOCK * EXPERT_BLOCK
    pend = jnp.cumsum(padded)
    pstart = pend - padded
    n_blocks = -(-(2 * n) // EXPERT_BLOCK) + N_EXPERTS
    cap = n_blocks * EXPERT_BLOCK
    seg_row = jnp.zeros((1, ROUTE_LANES), F32).at[0, :N_EXPERTS].set((pstart * slabs).astype(F32))
    dest = _slot_rows(info, seg_row, slabs, _pick(n, 2048))
    dest_rows = (dest[:, 0], dest[:, 1])
    block_e = jnp.minimum(
        jnp.searchsorted(pend, jnp.arange(n_blocks, dtype=jnp.int32) * EXPERT_BLOCK, side="right"),
        N_EXPERTS - 1).astype(jnp.int32)
    n_active = (pend[-1:] // EXPERT_BLOCK).astype(jnp.int32)
    used = cnt > 0
    expert_seq = jnp.nonzero(used, size=N_EXPERTS, fill_value=0)[0].astype(jnp.int32)
    block_pos = (jnp.cumsum(used.astype(jnp.int32)) - 1)[block_e].astype(jnp.int32)
    n_used = jnp.sum(used.astype(jnp.int32)).reshape(1)

    x_buf = _dispatch(h2p, dest_rows, ((pstart + cnt) * slabs).astype(jnp.int32),
                      (padded - cnt).astype(jnp.int32), n_active, cap, slabs, _pick(n, 1024))
    y_buf = _moe(x_buf, n_active, expert_seq, block_pos, n_used, p["moe_w1"], p["moe_w3"], p["moe_w2"], slabs)
    return _combine(y_buf, dest_rows, info, x1, p["norm_ffn_post"], gt2, seqlen, slabs, _pick(seqlen, 512))


def kernel(x, c, ada_w, ada_b, norm_mix_pre, norm_mix_post, norm_ffn_pre, norm_ffn_post, w_in, w_out, ssm_lam_re, ssm_lam_im, ssm_log_dt, ssm_b_re, ssm_b_im, ssm_c_re, ssm_c_im, ssm_d, glu_w, glu_b, rwkv_mu, rwkv_w0, rwkv_w_up, rwkv_a0, rwkv_a_up, rwkv_g_up, rwkv_k_k, rwkv_k_a, rwkv_r_k, rwkv_ln_w, rwkv_ln_b, moe_w_grp, moe_b_grp, moe_w_exp, moe_b_exp, moe_w1, moe_w3, moe_w2):
    bsz, seqlen, d = x.shape
    params = dict(norm_mix_pre=norm_mix_pre, norm_mix_post=norm_mix_post, norm_ffn_pre=norm_ffn_pre,
                  norm_ffn_post=norm_ffn_post, w_in=w_in, w_out=w_out, ssm_lam_re=ssm_lam_re,
                  ssm_lam_im=ssm_lam_im, ssm_log_dt=ssm_log_dt, ssm_b_re=ssm_b_re, ssm_b_im=ssm_b_im,
                  ssm_c_re=ssm_c_re, ssm_c_im=ssm_c_im, ssm_d=ssm_d, glu_w=glu_w, glu_b=glu_b,
                  rwkv_mu=rwkv_mu, rwkv_w0=rwkv_w0, rwkv_w_up=rwkv_w_up, rwkv_a0=rwkv_a0,
                  rwkv_a_up=rwkv_a_up, rwkv_g_up=rwkv_g_up, rwkv_k_k=rwkv_k_k, rwkv_k_a=rwkv_k_a,
                  rwkv_r_k=rwkv_r_k, rwkv_ln_w=rwkv_ln_w, rwkv_ln_b=rwkv_ln_b, moe_w_grp=moe_w_grp,
                  moe_b_grp=moe_b_grp, moe_w_exp=moe_w_exp, moe_b_exp=moe_b_exp, moe_w1=moe_w1,
                  moe_w3=moe_w3, moe_w2=moe_w2)
    x2 = x.reshape(bsz * seqlen, d)
    for layer in range(ada_w.shape[0]):
        mod = _ada(c, ada_w[layer], ada_b[layer])
        x2 = _layer(x2, mod, {k: v[layer] for k, v in params.items()}, bsz, seqlen)
    return x2.reshape(bsz, seqlen, d)
```

```python
import functools
import math

import jax
import jax.numpy as jnp
from jax import lax
from jax.experimental import pallas as pl
from jax.experimental.pallas import tpu as pltpu

F32 = jnp.float32
BF16 = jnp.bfloat16

SSM_GROUP = 16
SSM_STATE = 64
S5_SLAB = 256
S5_TIME_BLOCK = 64
RWKV_HEAD = 64
RWKV_CHUNK = 64
RWKV_BATCH_PER_STEP = 4
HEADS_PER_TILE = 4
LORA_W = 64
LORA_A = 64
LORA_G = 128
N_GROUPS = 8
EXPERTS_PER_GROUP = 8
N_EXPERTS = N_GROUPS * EXPERTS_PER_GROUP
EXPERT_BLOCK = 256
RMS_EPS = 1e-6
GN_EPS = 64e-5
DECAY_SCALE = math.exp(-0.5)
LANES = 128
POST_SUB_ROWS = 256
COMBINE_CHUNK = 64
SMEM_INDEX_BLOCK = 1024
ROUTE_LANES = LANES
VMEM_LIMIT = 52 * 1024 * 1024


def _cparams(*sem):
    return pltpu.CompilerParams(dimension_semantics=sem, vmem_limit_bytes=VMEM_LIMIT)


def _sigmoid(x):
    return 1.0 / (1.0 + jnp.exp(-x))


def _dot(a, b):
    return jnp.dot(a, b, preferred_element_type=F32)


def _dot_nt(a, b):
    return lax.dot_general(a, b, (((1,), (1,)), ((), ())), preferred_element_type=F32)


def _dot_tn(a, b):
    return lax.dot_general(a, b, (((0,), (0,)), ((), ())), preferred_element_type=F32)


def _split_bf16(x):
    hi = x.astype(BF16)
    lo = (x - hi.astype(F32)).astype(BF16)
    return hi, lo


def _pack_pair(a, b):
    ua = lax.bitcast_convert_type(a.astype(BF16).astype(F32), jnp.uint32)
    ub = lax.bitcast_convert_type(b.astype(BF16).astype(F32), jnp.uint32)
    return ub | (ua >> 16)


def _unpack_pair(w):
    lo = lax.bitcast_convert_type(w << 16, F32)
    hi = lax.bitcast_convert_type(w & jnp.uint32(0xFFFF0000), F32)
    return lo, hi


def _store_packed(ref, val, n_rows):
    d = val.shape[1]
    slabs = d // (2 * LANES)
    for s in range(slabs):
        a = val[:, s * LANES:(s + 1) * LANES]
        b = val[:, d // 2 + s * LANES:d // 2 + (s + 1) * LANES]
        ref[pl.ds(s, n_rows, stride=slabs), :] = _pack_pair(a, b)


def _load_packed(ref, n_rows, slabs):
    return [_unpack_pair(ref[pl.ds(s, n_rows, stride=slabs), :]) for s in range(slabs)]


def _ada_kernel(c_ref, w_ref, b_ref, o_ref):
    c = c_ref[...]
    cond = c * _sigmoid(c)
    o_ref[...] = jnp.dot(cond, w_ref[...], preferred_element_type=F32,
                         precision=lax.Precision.HIGHEST) + b_ref[...]


def _ada(c, ada_w, ada_b):
    bsz, d = c.shape
    n = ada_w.shape[1]
    tn = 1024
    return pl.pallas_call(
        _ada_kernel,
        grid=(n // tn,),
        in_specs=[pl.BlockSpec((bsz, d), lambda j: (0, 0)),
                  pl.BlockSpec((d, tn), lambda j: (0, j)),
                  pl.BlockSpec((1, tn), lambda j: (0, j))],
        out_specs=pl.BlockSpec((bsz, tn), lambda j: (0, j)),
        out_shape=jax.ShapeDtypeStruct((bsz, n), F32),
        compiler_params=_cparams("arbitrary"),
        name="ada_mod",
    )(c, ada_w, ada_b.reshape(1, n))


def _rms(x, g):
    return x * lax.rsqrt(jnp.mean(x * x, axis=-1, keepdims=True) + RMS_EPS) * g


def _norm_proj_kernel(x_ref, g_ref, sc_ref, sh_ref, w_ref, o_ref):
    h = _rms(x_ref[...], g_ref[...]) * (1.0 + sc_ref[0]) + sh_ref[0]
    o_ref[...] = _dot(h.astype(BF16), w_ref[...]).astype(o_ref.dtype)


def _norm_proj(x2, g, sc, sh, w, seqlen, tm, name):
    n, d = x2.shape
    nout = w.shape[1]
    tpb = seqlen // tm
    return pl.pallas_call(
        _norm_proj_kernel,
        grid=(n // tm,),
        in_specs=[pl.BlockSpec((tm, d), lambda i: (i, 0)),
                  pl.BlockSpec((1, d), lambda i: (0, 0)),
                  pl.BlockSpec((1, 1, d), lambda i: (i // tpb, 0, 0)),
                  pl.BlockSpec((1, 1, d), lambda i: (i // tpb, 0, 0)),
                  pl.BlockSpec((d, nout), lambda i: (0, 0), pipeline_mode=pl.Buffered(1))],
        out_specs=pl.BlockSpec((tm, nout), lambda i: (i, 0)),
        out_shape=jax.ShapeDtypeStruct((n, nout), BF16),
        compiler_params=_cparams("parallel"),
        name=name,
    )(x2, g.reshape(1, d), sc[:, None, :], sh[:, None, :], w)


def _s5_tables(lam_re, lam_im, log_dt, b_re, b_im, c_re, c_im):
    g, p, cg = b_re.shape
    gs = S5_SLAB // cg
    ns = g // gs
    lr = jnp.minimum(lam_re.astype(F32), -1e-4)
    li = lam_im.astype(F32)
    dt = jnp.exp(log_dt.astype(F32))[:, None]
    mag = jnp.exp(lr * dt)
    ar, ai = mag * jnp.cos(li * dt), mag * jnp.sin(li * dt)
    den = lr * lr + li * li
    qr = ((ar - 1.0) * lr + ai * li) / den
    qi = (ai * lr - (ar - 1.0) * li) / den
    br, bi = b_re.astype(F32), b_im.astype(F32)
    bbr = qr[..., None] * br - qi[..., None] * bi
    bbi = qr[..., None] * bi + qi[..., None] * br

    def in_rows(t):
        return t.reshape(ns, gs, p, cg).transpose(0, 1, 3, 2).reshape(ns, gs * cg, p).astype(BF16)

    def out_cols(t):
        return t.reshape(ns, gs, cg, p).transpose(0, 3, 1, 2).reshape(ns, p, gs * cg).astype(BF16)

    a_tab = jnp.stack([ar.reshape(ns, gs * p), ai.reshape(ns, gs * p)], axis=1)
    return (in_rows(bbr), in_rows(bbi), out_cols(c_re.astype(F32)), out_cols(-c_im.astype(F32)), a_tab)


def _gelu_tanh(x):
    return 0.5 * x * (1.0 + jnp.tanh(math.sqrt(2.0 / math.pi) * (x + 0.044715 * (x * x * x))))


def _s5_kernel(x_ref, g_ref, sc_ref, sh_ref, wu_ref, perm_ref, bre_ref, bim_ref, cre_ref, cim_ref, a_ref,
               d_ref, gw_ref, gb_ref, o_ref, b_ref, c_ref, u_scr, bscr, sscr, yscr, st_ref):
    bsz, lb, d_in = x_ref.shape
    rows = bsz * lb
    dch = wu_ref.shape[1]
    half = st_ref.shape[2] // 2
    n_p = bre_ref.shape[2]

    @pl.when(pl.program_id(0) == 0)
    def _():
        st_ref[...] = jnp.zeros_like(st_ref)
        tile_in = (lax.broadcasted_iota(jnp.int32, (n_p, half), 0)
                   == lax.broadcasted_iota(jnp.int32, (n_p, half), 1) % n_p).astype(BF16)
        tile_out = (lax.broadcasted_iota(jnp.int32, (half, n_p), 0) % n_p
                    == lax.broadcasted_iota(jnp.int32, (half, n_p), 1)).astype(BF16)
        in_mask = (lax.broadcasted_iota(jnp.int32, (S5_SLAB, half), 0) // SSM_GROUP
                   == lax.broadcasted_iota(jnp.int32, (S5_SLAB, half), 1) // n_p)
        out_mask = (lax.broadcasted_iota(jnp.int32, (half, S5_SLAB), 0) // n_p
                    == lax.broadcasted_iota(jnp.int32, (half, S5_SLAB), 1) // SSM_GROUP)
        for s in range(dch // S5_SLAB):
            b_ref[s, :, :half] = jnp.where(in_mask, _dot(bre_ref[s], tile_in), 0.0).astype(BF16)
            b_ref[s, :, half:] = jnp.where(in_mask, _dot(bim_ref[s], tile_in), 0.0).astype(BF16)
            c_ref[s, :half, :] = jnp.where(out_mask, _dot(tile_out, cre_ref[s]), 0.0).astype(BF16)
            c_ref[s, half:, :] = jnp.where(out_mask, _dot(tile_out, cim_ref[s]), 0.0).astype(BF16)

    h = (_rms(x_ref[...], g_ref[...]) * (1.0 + sc_ref[...]) + sh_ref[...]).astype(BF16)
    u_nat = _dot(h.reshape(rows, d_in), wu_ref[...]).astype(BF16)
    u_scr[...] = _dot(perm_ref[...], u_nat).astype(BF16)
    n_slab = dch // S5_SLAB
    slab = lambda s: slice(s * S5_SLAB, (s + 1) * S5_SLAB)

    def project_in(s):
        bscr[s % 2] = _dot(u_scr[:, slab(s)], b_ref[s])

    project_in(0)
    for s in range(n_slab):
        if s + 1 < n_slab:
            project_in(s + 1)
        buf = s % 2
        a_re = a_ref[s, 0:1, :]
        a_im = a_ref[s, 1:2, :]
        s_r = st_ref[s, :, :half]
        s_i = st_ref[s, :, half:]
        for l in range(rows // bsz):
            at_l = slice(l * bsz, (l + 1) * bsz)
            s_r, s_i = (a_re * s_r - a_im * s_i + bscr[buf, at_l, :half],
                        a_re * s_i + a_im * s_r + bscr[buf, at_l, half:])
            sscr[buf, at_l, :half] = s_r
            sscr[buf, at_l, half:] = s_i
        st_ref[s, :, :half] = s_r
        st_ref[s, :, half:] = s_i
        y = _dot(sscr[buf].astype(BF16), c_ref[s]) + d_ref[:, slab(s)] * u_scr[:, slab(s)].astype(F32)
        yscr[:, slab(s)] = _gelu_tanh(y).astype(BF16)
    y = yscr[...]
    gate = _sigmoid(_dot(y, gw_ref[...]) + gb_ref[...])
    out_tm = (y.astype(F32) * gate).astype(BF16)
    out_nat = _dot_tn(perm_ref[...], out_tm).astype(o_ref.dtype)
    o_ref[...] = out_nat.reshape(bsz, lb, dch)


def _s5_glu(x2, g, sc, sh, w_u, tabs, d_skip, glu_w, glu_b, bsz, seqlen, lb):
    b_re, b_im, c_re, c_im, a_tab = tabs
    d_in, dch = w_u.shape
    ns, _, n_half = a_tab.shape
    n_state = 2 * n_half
    rows = lb * bsz
    r_idx = jnp.arange(rows)
    perm = ((r_idx % bsz) * lb + r_idx // bsz)[:, None] == r_idx[None, :]
    full = lambda a: pl.BlockSpec(a.shape, lambda i: (0,) * a.ndim, pipeline_mode=pl.Buffered(1))
    args = (x2.reshape(bsz, seqlen, d_in), g.reshape(1, 1, d_in), sc[:, None, :], sh[:, None, :], w_u,
            perm.astype(BF16), b_re, b_im, c_re, c_im, a_tab,
            d_skip.astype(F32).reshape(1, dch), glu_w.astype(BF16), glu_b.astype(F32).reshape(1, dch))
    y = pl.pallas_call(
        _s5_kernel,
        grid=(seqlen // lb,),
        in_specs=[pl.BlockSpec((bsz, lb, d_in), lambda i: (0, i, 0))] + [full(a) for a in args[1:]],
        out_specs=pl.BlockSpec((bsz, lb, dch), lambda i: (0, i, 0)),
        out_shape=jax.ShapeDtypeStruct((bsz, seqlen, dch), BF16),
        scratch_shapes=[pltpu.VMEM((ns, S5_SLAB, n_state), BF16), pltpu.VMEM((ns, n_state, S5_SLAB), BF16),
                        pltpu.VMEM((rows, dch), BF16), pltpu.VMEM((2, rows, n_state), F32),
                        pltpu.VMEM((2, rows, n_state), F32), pltpu.VMEM((rows, dch), BF16),
                        pltpu.VMEM((ns, bsz, n_state), F32)],
        compiler_params=_cparams("arbitrary"),
        name="s5_mixer_glu",
    )(*args)
    return y.reshape(bsz * seqlen, dch)


def _rwkv_kernel(z_ref, mu_ref, w0_ref, a0_ref, kkw_ref, ka_ref, wa_ref, gup_ref, rk_ref, lnw_ref, lnb_ref,
                 ones_ref, o_ref, s_ref, zlast_ref):
    nb, t, _ = z_ref.shape
    d_r = o_ref.shape[2]
    tile = HEADS_PER_TILE * RWKV_HEAD
    n_tiles = d_r // tile
    first_chunk = pl.program_id(1) == 0

    @pl.when(first_chunk)
    def _():
        s_ref[...] = jnp.zeros_like(s_ref)

    ones_bd = ones_ref[...]

    def seg_sum(x):
        return _dot(x.astype(BF16), ones_bd)

    row = lax.broadcasted_iota(jnp.int32, (t, t), 0)
    col = lax.broadcasted_iota(jnp.int32, (t, t), 1)
    tri = (row >= col).astype(BF16)
    st = HEADS_PER_TILE * t
    rs = lax.broadcasted_iota(jnp.int32, (2 * t, 2 * st), 0)
    t_r = rs % t
    t_c = lax.broadcasted_iota(jnp.int32, (2 * t, 2 * st), 1) % t
    keep = (t_r > t_c) | ((rs >= t) & (t_r == t_c))
    eye_w = (lax.broadcasted_iota(jnp.int32, (t, st), 0)
             == lax.broadcasted_iota(jnp.int32, (t, st), 1) % t).astype(F32)
    blk_mask = (lax.broadcasted_iota(jnp.int32, (st, st), 0) // t
                == lax.broadcasted_iota(jnp.int32, (st, st), 1) // t)
    lane = lax.broadcasted_iota(jnp.int32, (1, tile), 1)
    head_masks = [(lane >= j * RWKV_HEAD) & (lane < (j + 1) * RWKV_HEAD) for j in range(HEADS_PER_TILE)]
    bd_r = lax.broadcasted_iota(jnp.int32, (tile, tile), 0) // RWKV_HEAD
    bd_c = lax.broadcasted_iota(jnp.int32, (tile, tile), 1) // RWKV_HEAD
    bd_mask = bd_r == bd_c
    n_levels = int(math.log2(t))
    slices = [slice(hg * tile, (hg + 1) * tile) for hg in range(n_tiles)]
    units = [(bb, hg) for bb in range(nb) for hg in range(n_tiles)]
    n_u = range(len(units))

    def stack(x):
        zero = jnp.zeros_like(x)
        return jnp.concatenate([jnp.where(m, x, zero) for m in head_masks], axis=0)

    def bf(x):
        return x.astype(BF16)

    at, qt, bt, kt, vv, em, etm, wtot, rkb, gate = ([] for _ in range(10))
    row0 = lax.broadcasted_iota(jnp.int32, (t, 1), 0) == 0
    lora_lane = lax.broadcasted_iota(jnp.int32, (t, LORA_W + LORA_A), 1)
    for bb in range(nb):
        z = z_ref[bb].astype(F32)
        prev_row = jnp.where(first_chunk, 0.0, zlast_ref[bb, 0:1, :])
        zl = z + mu_ref[...] * (jnp.where(row0, prev_row, pltpu.roll(z, 1, 0)) - z)
        zlast_ref[bb, 0:1, :] = z[t - 1:t, :]
        xwa = zl[:, 3 * d_r:3 * d_r + LORA_W + LORA_A]
        lhs = jnp.where(lora_lane < LORA_W, jnp.tanh(xwa), xwa).astype(BF16)
        wa = _dot(lhs, wa_ref[...])
        lw = -DECAY_SCALE * _sigmoid(w0_ref[...] + wa[:, :d_r])
        asig = _sigmoid(a0_ref[...] + wa[:, d_r:])
        r = zl[:, :d_r]
        k_raw = zl[:, d_r:2 * d_r]
        kk = k_raw * kkw_ref[...]
        kp = k_raw * (1.0 + (asig - 1.0) * ka_ref[...])
        xg = zl[:, 3 * d_r + LORA_W + LORA_A:3 * d_r + LORA_W + LORA_A + LORA_G]
        g_full = _dot(_sigmoid(xg).astype(BF16), gup_ref[...])
        lw_hi, lw_lo = _split_bf16(lw)
        cum = _dot(tri, lw_hi) + _dot(tri, lw_lo)
        mid = cum[t // 2 - 1:t // 2, :]
        tot = cum[t - 1:t, :]
        e1 = jnp.exp(cum - mid)
        e2 = jnp.exp(mid - cum)
        e1p = e1 * jnp.exp(-lw)
        kk2 = kk * kk
        kkn = kk / jnp.maximum(jnp.sqrt(jnp.concatenate([seg_sum(kk2[:, sl]) for sl in slices], axis=1)), 1e-12)
        full = dict(at=-kkn * e1p, qt=r * e1, bt=kkn * asig * e2, kt=kp * e2,
                    vv=zl[:, 2 * d_r:3 * d_r], em=jnp.exp(mid), etm=jnp.exp(tot - mid), wtot=jnp.exp(tot),
                    rkb=r * kp * rk_ref[...], gate=g_full)
        for dst, key in ((at, "at"), (qt, "qt"), (bt, "bt"), (kt, "kt"), (vv, "vv"), (em, "em"),
                         (etm, "etm"), (wtot, "wtot"), (rkb, "rkb"), (gate, "gate")):
            dst.extend(full[key][:, sl] for sl in slices)

    s_old = [s_ref[i] for i in n_u]
    wide = []
    for i in n_u:
        lhs = bf(jnp.concatenate([at[i], qt[i]], axis=0))
        rhs = jnp.concatenate([stack(bf(bt[i])), stack(bf(kt[i]))], axis=0)
        a_w = jnp.where(keep, _dot_nt(lhs, rhs), 0.0)
        wide.append([a_w[:t, :st], a_w[:t, st:], a_w[t:, :st], a_w[t:, st:]])
    x_state = [_dot_nt(bf(jnp.concatenate([at[i], qt[i]], axis=0) * em[i]), bf(s_old[i]))
               for i in n_u]
    sv = [stack(bf(vv[i])) for i in n_u]
    akv = [_dot(bf(wide[i][1]), sv[i]) for i in n_u]

    def expand(x_w):
        x_b = bf(x_w)
        return jnp.where(blk_mask, jnp.concatenate([x_b] * HEADS_PER_TILE, axis=0), jnp.zeros((st, st), BF16))

    p_acc = [eye_w + w[0] for w in wide]
    q_bd = [expand(w[0]) for w in wide]
    q_pow = [_dot(bf(wide[i][0]), q_bd[i]) for i in n_u]
    for lev in range(1, n_levels):
        for i in n_u:
            q_bd[i] = expand(q_pow[i])
            if lev < n_levels - 1:
                both = _dot(bf(jnp.concatenate([p_acc[i], q_pow[i]], axis=0)), q_bd[i])
                p_acc[i] = p_acc[i] + both[:t]
                q_pow[i] = both[t:]
            else:
                p_acc[i] = p_acc[i] + _dot(bf(p_acc[i]), q_bd[i])
    u_all = [_dot(bf(p_acc[i]), stack(bf(x_state[i][:t] + akv[i]))) for i in n_u]
    y_all = [x_state[i][t:]
             + _dot(bf(jnp.concatenate([wide[i][2], wide[i][3]], axis=1)),
                    jnp.concatenate([stack(bf(u_all[i])), sv[i]], axis=0)) for i in n_u]

    for i, (bb, hg) in enumerate(units):
        sl = slices[hg]
        y = y_all[i]
        uv = bf(jnp.concatenate([u_all[i], vv[i]], axis=0))
        bk_end = bf(jnp.concatenate([bt[i], kt[i]], axis=0) * etm[i])
        s_ref[i] = s_old[i] * wtot[i] + jnp.where(bd_mask, _dot_tn(uv, bk_end), 0.0)

        mean = seg_sum(y) * (1.0 / RWKV_HEAD)
        dlt = y - mean
        var = seg_sum(dlt * dlt) * (1.0 / RWKV_HEAD)
        yn = dlt * lax.rsqrt(var + GN_EPS) * lnw_ref[:, sl] + lnb_ref[:, sl]
        out = (yn + seg_sum(rkb[i]) * vv[i]) * gate[i]
        o_ref[bb, :, sl] = out.astype(o_ref.dtype)


def _rwkv(z, mu, w0, w_up, a0, a_up, g_up, k_k, k_a, r_k, ln_w, ln_b, bsz, seqlen):
    n, dz = z.shape
    d_r = w0.shape[0]
    t = RWKV_CHUNK
    nb = RWKV_BATCH_PER_STEP
    nch = seqlen // t
    tile = HEADS_PER_TILE * RWKV_HEAD
    hid = jnp.arange(tile) // RWKV_HEAD
    ones_bd = (hid[:, None] == hid[None, :]).astype(BF16)
    wa = jnp.zeros((LORA_W + LORA_A, 2 * d_r), F32)
    wa = wa.at[:LORA_W, :d_r].set(w_up.astype(F32)).at[LORA_W:, d_r:].set(a_up.astype(F32)).astype(BF16)
    row = lambda a: a.astype(F32).reshape(1, -1)
    full = lambda a: pl.BlockSpec(a.shape, lambda b, c: (0, 0))
    params = (row(mu), row(w0), row(a0), row(k_k), row(k_a), wa, g_up.astype(BF16), row(r_k), row(ln_w),
              row(ln_b), ones_bd)
    y = pl.pallas_call(
        _rwkv_kernel,
        grid=(bsz // nb, nch),
        in_specs=[pl.BlockSpec((nb, t, dz), lambda b, c: (b, c, 0))] + [full(a) for a in params],
        out_specs=pl.BlockSpec((nb, t, d_r), lambda b, c: (b, c, 0)),
        out_shape=jax.ShapeDtypeStruct((bsz, seqlen, d_r), BF16),
        scratch_shapes=[pltpu.VMEM((nb * (d_r // tile), tile, tile), F32), pltpu.VMEM((nb, 8, dz), F32)],
        compiler_params=_cparams("parallel", "arbitrary"),
        name="rwkv7_chunked",
    )(z.reshape(bsz, seqlen, dz), *params)
    return y.reshape(n, d_r)


def _post_mix_kernel(ys_ref, yr_ref, wo1_ref, wo2_ref, x_ref, g1_ref, gt_ref, g2_ref, sc_ref, sh_ref,
                     wr_both_ref, br_ref, x1_out, h2_out, lg_out):
    tm = x_ref.shape[0]
    sub = min(tm, POST_SUB_ROWS)
    slabs = h2_out.shape[0] // tm
    nl = lg_out.shape[1]
    for c in range(tm // sub):
        rows = slice(c * sub, (c + 1) * sub)
        mixed = _dot(ys_ref[rows, :], wo1_ref[...]) + _dot(yr_ref[rows, :], wo2_ref[...])
        x1 = x_ref[rows, :] + gt_ref[0] * _rms(mixed, g1_ref[...])
        x1_out[rows, :] = x1
        h2 = _rms(x1, g2_ref[...]) * (1.0 + sc_ref[0]) + sh_ref[0]
        _store_packed(h2_out.at[pl.ds(c * sub * slabs, sub * slabs)], h2, sub)
        hi, lo = _split_bf16(h2)
        both = _dot(hi, wr_both_ref[...])
        lg_out[rows, :] = both[:, :nl] + both[:, nl:] + _dot(lo, wr_both_ref[:, :nl]) + br_ref[...]


def _post_mix(ys, yr, w_out, x2, g1, gt1, g2, sc2, sh2, w_route, b_route, seqlen, tm):
    n, d = x2.shape
    ds = ys.shape[1]
    tpb = seqlen // tm
    slabs = d // (2 * LANES)
    wr_both = jnp.concatenate(_split_bf16(w_route), axis=1)
    rows = lambda w: pl.BlockSpec((tm, w), lambda i: (i, 0))
    full = lambda a: pl.BlockSpec(a.shape, lambda i: (0,) * a.ndim)
    bat = pl.BlockSpec((1, 1, d), lambda i: (i // tpb, 0, 0))
    args = (ys, yr, w_out[:ds].astype(BF16), w_out[ds:].astype(BF16), x2, g1.reshape(1, d), gt1[:, None, :],
            g2.reshape(1, d),
            sc2[:, None, :], sh2[:, None, :], wr_both, b_route.reshape(1, -1))
    in_specs = [rows(ds), rows(yr.shape[1]), full(args[2]), full(args[3]), rows(d), full(args[5]), bat,
                full(args[7]), bat, bat, full(wr_both), full(args[11])]
    return pl.pallas_call(
        _post_mix_kernel,
        grid=(n // tm,),
        in_specs=in_specs,
        out_specs=[rows(d), pl.BlockSpec((tm * slabs, LANES), lambda i: (i, 0)), rows(ROUTE_LANES)],
        out_shape=[jax.ShapeDtypeStruct((n, d), F32), jax.ShapeDtypeStruct((n * slabs, LANES), jnp.uint32),
                   jax.ShapeDtypeStruct((n, ROUTE_LANES), F32)],
        compiler_params=_cparams("parallel"),
        name="out_proj_post",
    )(*args)


def _route_kernel(lg_ref, info_ref, cnt_ref, carry):
    i = pl.program_id(0)
    tm = lg_ref.shape[0]

    @pl.when(i == 0)
    def _():
        carry[...] = jnp.zeros_like(carry)

    lg = lg_ref[...]
    lane = lax.broadcasted_iota(jnp.int32, lg.shape, 1)
    lane_f = lane.astype(F32)
    neg = jnp.float32(-jnp.inf)
    big = jnp.float32(1e9)
    is_g = (lane >= N_EXPERTS) & (lane < N_EXPERTS + N_GROUPS)
    gl = jnp.where(is_g, lg, neg)
    gmax = jnp.max(gl, axis=-1, keepdims=True)
    gidx = jnp.min(jnp.where(gl == gmax, lane_f - N_EXPERTS, big), axis=-1, keepdims=True)
    p_grp = 1.0 / jnp.sum(jnp.where(is_g, jnp.exp(gl - gmax), 0.0), axis=-1, keepdims=True)
    in_grp = (lane < N_EXPERTS) & ((lane // EXPERTS_PER_GROUP).astype(F32) == gidx)
    el = jnp.where(in_grp, lg, neg)
    m1 = jnp.max(el, axis=-1, keepdims=True)
    i1 = jnp.min(jnp.where(el == m1, lane_f, big), axis=-1, keepdims=True)
    el2 = jnp.where(lane_f == i1, neg, el)
    m2 = jnp.max(el2, axis=-1, keepdims=True)
    i2 = jnp.min(jnp.where(el2 == m2, lane_f, big), axis=-1, keepdims=True)
    ex = jnp.exp(m2 - m1)
    w1 = p_grp / (1.0 + ex)
    w2 = p_grp * ex / (1.0 + ex)

    oh1 = lane_f == i1
    oh2 = lane_f == i2
    onehot = (oh1 | oh2).astype(BF16)
    rr = lax.broadcasted_iota(jnp.int32, (tm, tm), 0)
    cc = lax.broadcasted_iota(jnp.int32, (tm, tm), 1)
    before = _dot((rr > cc).astype(BF16), onehot) + carry[...]
    rank1 = jnp.sum(jnp.where(oh1, before, 0.0), axis=-1, keepdims=True)
    rank2 = jnp.sum(jnp.where(oh2, before, 0.0), axis=-1, keepdims=True)
    carry[...] = carry[...] + jnp.sum(onehot.astype(F32), axis=0, keepdims=True)
    cnt_ref[...] = carry[...]

    info = jnp.where(lane == 0, i1, 0.0)
    info = jnp.where(lane == 1, i2, info)
    info = jnp.where(lane == 2, w1, info)
    info = jnp.where(lane == 3, w2, info)
    info = jnp.where(lane == 4, rank1, info)
    info = jnp.where(lane == 5, rank2, info)
    info_ref[...] = info


def _route(logits, tm):
    n = logits.shape[0]
    return pl.pallas_call(
        _route_kernel,
        grid=(n // tm,),
        in_specs=[pl.BlockSpec((tm, ROUTE_LANES), lambda i: (i, 0))],
        out_specs=[pl.BlockSpec((tm, ROUTE_LANES), lambda i: (i, 0)),
                   pl.BlockSpec((1, ROUTE_LANES), lambda i: (0, 0))],
        out_shape=[jax.ShapeDtypeStruct((n, ROUTE_LANES), F32),
                   jax.ShapeDtypeStruct((1, ROUTE_LANES), F32)],
        scratch_shapes=[pltpu.VMEM((1, ROUTE_LANES), F32)],
        compiler_params=_cparams("arbitrary"),
        name="moe_route",
    )(logits)


def _slot_rows_kernel(info_ref, seg_ref, o_ref, *, slabs):
    info = info_ref[...]
    lane = lax.broadcasted_iota(jnp.int32, info.shape, 1)
    lane_f = lane.astype(F32)
    seg = seg_ref[...]
    d0 = jnp.sum(jnp.where(lane_f == info[:, 0:1], seg, 0.0), axis=-1, keepdims=True) + info[:, 4:5] * slabs
    d1 = jnp.sum(jnp.where(lane_f == info[:, 1:2], seg, 0.0), axis=-1, keepdims=True) + info[:, 5:6] * slabs
    o_ref[...] = jnp.where(lane == 0, d0, jnp.where(lane == 1, d1, 0.0)).astype(jnp.int32)


def _slot_rows(info, seg_row, slabs, tm):
    n = info.shape[0]
    return pl.pallas_call(
        functools.partial(_slot_rows_kernel, slabs=slabs),
        grid=(n // tm,),
        in_specs=[pl.BlockSpec((tm, ROUTE_LANES), lambda i: (i, 0)),
                  pl.BlockSpec((1, ROUTE_LANES), lambda i: (0, 0))],
        out_specs=pl.BlockSpec((tm, ROUTE_LANES), lambda i: (i, 0)),
        out_shape=jax.ShapeDtypeStruct((n, ROUTE_LANES), jnp.int32),
        compiler_params=_cparams("parallel"),
        name="moe_slot_rows",
    )(info, seg_row)


def _dispatch_kernel(d0_ref, d1_ref, pad_row_ref, pad_len_ref, na_ref, h_ref, buf_out, zeros, sem, zsem, *,
                     slabs, n_blocks):
    tm = h_ref.shape[0] // slabs

    @pl.when(pl.program_id(0) == 0)
    def _():
        zeros[...] = jnp.zeros_like(zeros)
        bits = [1 << b for b in reversed(range(EXPERT_BLOCK.bit_length() - 1))]

        def pad_copy(e, bit):
            done = pad_len_ref[e] & ~(2 * bit - 1)
            dst = pl.multiple_of(pad_row_ref[e] + done * slabs, slabs)
            return pltpu.make_async_copy(zeros.at[pl.ds(0, bit * slabs)], buf_out.at[pl.ds(dst, bit * slabs)], zsem)

        def block_copy(j):
            dst = pl.multiple_of(j * (EXPERT_BLOCK * slabs), EXPERT_BLOCK * slabs)
            return pltpu.make_async_copy(zeros, buf_out.at[pl.ds(dst, EXPERT_BLOCK * slabs)], zsem)

        def each(action):
            def per_expert(e, c):
                for bit in bits:
                    @pl.when((pad_len_ref[e] & bit) != 0)
                    def _():
                        action(pad_copy(e, bit))
                return c

            def per_block(j, c):
                action(block_copy(j))
                return c

            lax.fori_loop(0, N_EXPERTS, per_expert, 0)
            lax.fori_loop(na_ref[0], n_blocks, per_block, 0)

        each(lambda c: c.start())
        each(lambda c: c.wait())

    def copy(t, dest_ref):
        src = pl.multiple_of(t * slabs, slabs)
        dst = pl.multiple_of(dest_ref[t], slabs)
        return pltpu.make_async_copy(h_ref.at[pl.ds(src, slabs)], buf_out.at[pl.ds(dst, slabs)], sem)

    def start(t, c):
        copy(t, d0_ref).start(priority=0)
        copy(t, d1_ref).start(priority=1)
        return c

    lax.fori_loop(0, tm, start, 0, unroll=4)
    for _ in range(2):
        pltpu.make_async_copy(h_ref, buf_out.at[pl.ds(0, tm * slabs)], sem).wait()


def _dispatch(h2p, dest_rows, pad_row, pad_len, n_active, cap, slabs, tm):
    n = h2p.shape[0] // slabs
    smem = pl.BlockSpec((tm,), lambda i: (i,), memory_space=pltpu.SMEM)
    table = lambda a: pl.BlockSpec(a.shape, lambda i: (0,), memory_space=pltpu.SMEM)
    return pl.pallas_call(
        functools.partial(_dispatch_kernel, slabs=slabs, n_blocks=cap // EXPERT_BLOCK),
        grid=(n // tm,),
        in_specs=[smem, smem, table(pad_row), table(pad_len), table(n_active),
                  pl.BlockSpec((tm * slabs, LANES), lambda i: (i, 0))],
        out_specs=pl.BlockSpec(memory_space=pl.ANY),
        out_shape=jax.ShapeDtypeStruct((cap * slabs, LANES), h2p.dtype),
        scratch_shapes=[pltpu.VMEM((EXPERT_BLOCK * slabs, LANES), h2p.dtype), pltpu.SemaphoreType.DMA,
                        pltpu.SemaphoreType.DMA],
        compiler_params=_cparams("arbitrary"),
        name="moe_dispatch",
    )(dest_rows[0], dest_rows[1], pad_row, pad_len, n_active, h2p)


def _moe_kernel(na_ref, eseq_ref, epos_ref, nd_ref, x_ref, w1_hbm, w3_hbm, w2_hbm, o_ref,
                w1f, w3f, w2f, w1b, w3b, w2b, sem):
    j = pl.program_id(0)
    active = j < na_ref[0]
    pos = epos_ref[j]
    fresh = (j == 0) | (pos != epos_ref[jnp.maximum(j - 1, 0)])

    slabs = w1b.shape[0]
    half = slabs * LANES

    def weight_copies(p):
        e = eseq_ref[p]
        slot = p % 2
        return [pltpu.make_async_copy(w_hbm.at[e], stage.at[slot], sem.at[slot, i])
                for i, (w_hbm, stage) in enumerate(((w1_hbm, w1f), (w3_hbm, w3f), (w2_hbm, w2f)))]

    @pl.when(j == 0)
    def _():
        for c in weight_copies(0):
            c.start()

    @pl.when(active & fresh)
    def _():
        for c in weight_copies(pos):
            c.wait()

        @pl.when(pos + 1 < nd_ref[0])
        def _():
            for c in weight_copies(pos + 1):
                c.start()

        slot = pos % 2
        for s in range(slabs):
            for dst, src in ((w1b, w1f), (w3b, w3f)):
                dst[s, :LANES, :] = src[slot, s * LANES:(s + 1) * LANES, :].astype(BF16)
                dst[s, LANES:, :] = src[slot, half + s * LANES:half + (s + 1) * LANES, :].astype(BF16)
        w2b[...] = w2f[slot].astype(BF16)

    @pl.when(active)
    def _():
        acc1 = jnp.zeros((EXPERT_BLOCK, w1b.shape[2]), F32)
        acc3 = jnp.zeros((EXPERT_BLOCK, w1b.shape[2]), F32)
        for s, (lo, hi) in enumerate(_load_packed(x_ref, EXPERT_BLOCK, slabs)):
            lhs = jnp.concatenate([lo.astype(BF16), hi.astype(BF16)], axis=1)
            acc1 = acc1 + _dot(lhs, w1b[s])
            acc3 = acc3 + _dot(lhs, w3b[s])
        hid = (acc1 * _sigmoid(acc1)) * acc3
        _store_packed(o_ref, _dot(hid.astype(BF16), w2b[...]), EXPERT_BLOCK)


def _moe(x_buf, n_active, expert_seq, block_pos, n_used, w1, w3, w2, slabs):
    cap = x_buf.shape[0] // slabs
    d, de = w1.shape[1], w1.shape[2]
    nb = cap // EXPERT_BLOCK

    def xmap(j, na, *_):
        return (jnp.minimum(j, na[0] - 1), 0)

    xspec = pl.BlockSpec((EXPERT_BLOCK * slabs, LANES), xmap)
    hbm = pl.BlockSpec(memory_space=pl.ANY)
    grid_spec = pltpu.PrefetchScalarGridSpec(
        num_scalar_prefetch=4,
        grid=(nb,),
        in_specs=[xspec, hbm, hbm, hbm],
        out_specs=xspec,
        scratch_shapes=[pltpu.VMEM((2, d, de), F32), pltpu.VMEM((2, d, de), F32), pltpu.VMEM((2, de, d), F32),
                        pltpu.VMEM((slabs, 2 * LANES, de), BF16), pltpu.VMEM((slabs, 2 * LANES, de), BF16),
                        pltpu.VMEM((de, d), BF16), pltpu.SemaphoreType.DMA((2, 3))],
    )
    return pl.pallas_call(
        _moe_kernel,
        grid_spec=grid_spec,
        out_shape=jax.ShapeDtypeStruct(x_buf.shape, x_buf.dtype),
        input_output_aliases={4: 0},
        compiler_params=_cparams("arbitrary"),
        name="moe_experts",
    )(n_active, expert_seq, block_pos, n_used, x_buf, w1, w3, w2)


def _combine_kernel(d0_ref, d1_ref, d0n_ref, d1n_ref, y_ref, info_ref, x1_ref, g_ref, gt_ref, o_ref,
                    rows, sem, *, slabs, tiles_per_idx, chunk):
    i = pl.program_id(0)
    n_tiles = pl.num_programs(0)
    tm = x1_ref.shape[0]
    slot = i % 2
    nxt_slot = 1 - slot
    nxt = jnp.minimum(i + 1, n_tiles - 1)

    def copy(tile, to_slot, t, dest_ref, k):
        off = (tile % tiles_per_idx) * tm
        src = pl.multiple_of(dest_ref[off + t], slabs)
        dst = pl.multiple_of(t * slabs, slabs)
        return pltpu.make_async_copy(y_ref.at[pl.ds(src, slabs)], rows.at[to_slot, k, pl.ds(dst, slabs)],
                                     sem.at[to_slot])

    def drain(which):
        for k in range(2):
            pltpu.make_async_copy(y_ref.at[pl.ds(0, tm * slabs)], rows.at[which, k], sem.at[which]).wait()

    @pl.when(i == 0)
    def _():
        def start(t, c):
            copy(i, slot, t, d0_ref, 0).start(priority=0)
            copy(i, slot, t, d1_ref, 1).start(priority=1)
            return c

        lax.fori_loop(0, tm, start, 0, unroll=4)

    drain(slot)
    g_row = g_ref[...]
    gt_row = gt_ref[0]
    for c in range(tm // chunk):
        tok = slice(c * chunk, (c + 1) * chunk)
        info = info_ref[tok, :]
        w1 = info[:, 2:3]
        w2 = info[:, 3:4]
        lo_parts, hi_parts = [], []
        for s in range(slabs):
            at_s = pl.ds(c * chunk * slabs + s, chunk, stride=slabs)
            lo0, hi0 = _unpack_pair(rows[slot, 0, at_s, :])
            lo1, hi1 = _unpack_pair(rows[slot, 1, at_s, :])
            lo_parts.append(w1 * lo0 + w2 * lo1)
            hi_parts.append(w1 * hi0 + w2 * hi1)
        ffn = jnp.concatenate(lo_parts + hi_parts, axis=1)
        o_ref[tok, :] = x1_ref[tok, :] + gt_row * _rms(ffn, g_row)
        for t in range(c * chunk, (c + 1) * chunk):
            copy(nxt, nxt_slot, t, d0n_ref, 0).start(priority=0)
            copy(nxt, nxt_slot, t, d1n_ref, 1).start(priority=1)

    @pl.when(i == n_tiles - 1)
    def _():
        drain(nxt_slot)


def _combine(y_buf, dest_rows, info, x1, g, gt2, seqlen, slabs, tm):
    n, d = x1.shape
    tpb = seqlen // tm
    idx_block = max(tm, SMEM_INDEX_BLOCK)
    per = idx_block // tm
    last = n // tm - 1
    smem = pl.BlockSpec((idx_block,), lambda i: (i // per,), memory_space=pltpu.SMEM)
    smem_next = pl.BlockSpec((idx_block,), lambda i: (jnp.minimum(i + 1, last) // per,), memory_space=pltpu.SMEM)
    return pl.pallas_call(
        functools.partial(_combine_kernel, slabs=slabs, tiles_per_idx=per, chunk=min(tm, COMBINE_CHUNK)),
        grid=(n // tm,),
        in_specs=[smem, smem, smem_next, smem_next,
                  pl.BlockSpec(memory_space=pl.ANY),
                  pl.BlockSpec((tm, ROUTE_LANES), lambda i: (i, 0)),
                  pl.BlockSpec((tm, d), lambda i: (i, 0)),
                  pl.BlockSpec((1, d), lambda i: (0, 0)),
                  pl.BlockSpec((1, 1, d), lambda i: (i // tpb, 0, 0))],
        out_specs=pl.BlockSpec((tm, d), lambda i: (i, 0)),
        out_shape=jax.ShapeDtypeStruct((n, d), F32),
        scratch_shapes=[pltpu.VMEM((2, 2, tm * slabs, LANES), y_buf.dtype), pltpu.SemaphoreType.DMA((2,))],
        compiler_params=_cparams("arbitrary"),
        name="moe_combine",
    )(dest_rows[0], dest_rows[1], dest_rows[0], dest_rows[1], y_buf, info, x1, g.reshape(1, d),
      gt2[:, None, :])


def _pick(n, pref):
    while n % pref:
        pref //= 2
    return pref


def _layer(x2, mod, p, bsz, seqlen):
    n, d = x2.shape
    sh1, sc1, gt1, sh2, sc2, gt2 = jnp.split(mod, 6, axis=-1)
    d_ssm = p["ssm_d"].shape[0]

    z = _norm_proj(x2, p["norm_mix_pre"], sc1, sh1, p["w_in"][:, d_ssm:].astype(BF16), seqlen,
                   _pick(seqlen, 512), "in_proj_z")

    tabs = _s5_tables(p["ssm_lam_re"], p["ssm_lam_im"], p["ssm_log_dt"], p["ssm_b_re"], p["ssm_b_im"],
                      p["ssm_c_re"], p["ssm_c_im"])
    y_ssm = _s5_glu(x2, p["norm_mix_pre"], sc1, sh1, p["w_in"][:, :d_ssm].astype(BF16), tabs, p["ssm_d"], p["glu_w"],
                    p["glu_b"], bsz, seqlen, _pick(seqlen, S5_TIME_BLOCK))

    y_rwkv = _rwkv(z, p["rwkv_mu"], p["rwkv_w0"], p["rwkv_w_up"], p["rwkv_a0"], p["rwkv_a_up"],
                   p["rwkv_g_up"], p["rwkv_k_k"], p["rwkv_k_a"], p["rwkv_r_k"], p["rwkv_ln_w"],
                   p["rwkv_ln_b"], bsz, seqlen)

    w_route = jnp.zeros((d, ROUTE_LANES), F32)
    w_route = w_route.at[:, :N_EXPERTS].set(p["moe_w_exp"].astype(F32))
    w_route = w_route.at[:, N_EXPERTS:N_EXPERTS + N_GROUPS].set(p["moe_w_grp"].astype(F32))
    b_route = jnp.zeros((ROUTE_LANES,), F32)
    b_route = b_route.at[:N_EXPERTS].set(p["moe_b_exp"].astype(F32))
    b_route = b_route.at[N_EXPERTS:N_EXPERTS + N_GROUPS].set(p["moe_b_grp"].astype(F32))
    x1, h2p, logits = _post_mix(y_ssm, y_rwkv, p["w_out"], x2, p["norm_mix_post"], gt1, p["norm_ffn_pre"],
                                sc2, sh2, w_route, b_route, seqlen, _pick(seqlen, 512))
    slabs = d // (2 * LANES)

    info, counts = _route(logits, _pick(n, 512))
    cnt = counts[0, :N_EXPERTS].astype(jnp.int32)
    padded = (cnt + EXPERT_BLOCK - 1) // EXPERT_BLOCK * EXPERT_BLOCK
    pend = jnp.cumsum(padded)
    pstart = pend - padded
    n_blocks = -(-(2 * n) // EXPERT_BLOCK) + N_EXPERTS
    cap = n_blocks * EXPERT_BLOCK
    seg_row = jnp.zeros((1, ROUTE_LANES), F32).at[0, :N_EXPERTS].set((pstart * slabs).astype(F32))
    dest = _slot_rows(info, seg_row, slabs, _pick(n, 2048))
    dest_rows = (dest[:, 0], dest[:, 1])
    n_active = (pend[-1:] // EXPERT_BLOCK).astype(jnp.int32)
    used = (cnt > 0).astype(jnp.int32)
    used_pos = jnp.cumsum(used) - 1
    slot_ids = jnp.arange(N_EXPERTS, dtype=jnp.int32)
    expert_seq = jnp.sum(jnp.where((used_pos[None, :] == slot_ids[:, None]) & (used[None, :] > 0),
                                   slot_ids[None, :], 0), axis=1).astype(jnp.int32)
    block_first = jnp.arange(n_blocks, dtype=jnp.int32) * EXPERT_BLOCK
    block_pos = jnp.sum(jnp.where(pend[None, :] <= block_first[:, None], used[None, :], 0),
                        axis=1).astype(jnp.int32)
    n_used = jnp.sum(used).reshape(1)

    x_buf = _dispatch(h2p, dest_rows, ((pstart + cnt) * slabs).astype(jnp.int32),
                      (padded - cnt).astype(jnp.int32), n_active, cap, slabs, _pick(n, 1024))
    y_buf = _moe(x_buf, n_active, expert_seq, block_pos, n_used, p["moe_w1"], p["moe_w3"], p["moe_w2"], slabs)
    return _combine(y_buf, dest_rows, info, x1, p["norm_ffn_post"], gt2, seqlen, slabs, _pick(seqlen, 512))


def kernel(x, c, ada_w, ada_b, norm_mix_pre, norm_mix_post, norm_ffn_pre, norm_ffn_post, w_in, w_out, ssm_lam_re, ssm_lam_im, ssm_log_dt, ssm_b_re, ssm_b_im, ssm_c_re, ssm_c_im, ssm_d, glu_w, glu_b, rwkv_mu, rwkv_w0, rwkv_w_up, rwkv_a0, rwkv_a_up, rwkv_g_up, rwkv_k_k, rwkv_k_a, rwkv_r_k, rwkv_ln_w, rwkv_ln_b, moe_w_grp, moe_b_grp, moe_w_exp, moe_b_exp, moe_w1, moe_w3, moe_w2):
    bsz, seqlen, d = x.shape
    params = dict(norm_mix_pre=norm_mix_pre, norm_mix_post=norm_mix_post, norm_ffn_pre=norm_ffn_pre,
                  norm_ffn_post=norm_ffn_post, w_in=w_in, w_out=w_out, ssm_lam_re=ssm_lam_re,
                  ssm_lam_im=ssm_lam_im, ssm_log_dt=ssm_log_dt, ssm_b_re=ssm_b_re, ssm_b_im=ssm_b_im,
                  ssm_c_re=ssm_c_re, ssm_c_im=ssm_c_im, ssm_d=ssm_d, glu_w=glu_w, glu_b=glu_b,
                  rwkv_mu=rwkv_mu, rwkv_w0=rwkv_w0, rwkv_w_up=rwkv_w_up, rwkv_a0=rwkv_a0,
                  rwkv_a_up=rwkv_a_up, rwkv_g_up=rwkv_g_up, rwkv_k_k=rwkv_k_k, rwkv_k_a=rwkv_k_a,
                  rwkv_r_k=rwkv_r_k, rwkv_ln_w=rwkv_ln_w, rwkv_ln_b=rwkv_ln_b, moe_w_grp=moe_w_grp,
                  moe_b_grp=moe_b_grp, moe_w_exp=moe_w_exp, moe_b_exp=moe_b_exp, moe_w1=moe_w1,
                  moe_w3=moe_w3, moe_w2=moe_w2)
    x2 = x.reshape(bsz * seqlen, d)
    for layer in range(ada_w.shape[0]):
        mod = _ada(c, ada_w[layer], ada_b[layer])
        x2 = _layer(x2, mod, {k: v[layer] for k, v in params.items()}, bsz, seqlen)
    return x2.reshape(bsz, seqlen, d)
```

```python
import functools
import math

import jax
import jax.numpy as jnp
from jax import lax
from jax.experimental import pallas as pl
from jax.experimental.pallas import tpu as pltpu

F32 = jnp.float32
BF16 = jnp.bfloat16

SSM_GROUP = 16
S5_SLAB = 256
S5_TIME_BLOCK = 64
RWKV_HEAD = 64
RWKV_CHUNK = 64
RWKV_BATCH_PER_STEP = 4
HEADS_PER_TILE = 4
LORA_W = 64
LORA_A = 64
LORA_G = 128
N_GROUPS = 8
EXPERTS_PER_GROUP = 8
N_EXPERTS = N_GROUPS * EXPERTS_PER_GROUP
EXPERT_BLOCK = 256
RMS_EPS = 1e-6
GN_EPS = 64e-5
DECAY_SCALE = math.exp(-0.5)
LANES = 128
POST_SUB_ROWS = 512
COMBINE_CHUNK = 64
SMEM_INDEX_BLOCK = 1024
ROUTE_LANES = LANES
V7X_VMEM_BYTES = 64 * 1024 * 1024
VMEM_LIMIT = V7X_VMEM_BYTES - 12 * 1024 * 1024


def _cparams(*sem):
    return pltpu.CompilerParams(dimension_semantics=sem, vmem_limit_bytes=VMEM_LIMIT)


def _sigmoid(x):
    return 1.0 / (1.0 + jnp.exp(-x))


def _dot(a, b):
    return jnp.dot(a, b, preferred_element_type=F32)


def _dot_nt(a, b):
    return lax.dot_general(a, b, (((1,), (1,)), ((), ())), preferred_element_type=F32)


def _dot_tn(a, b):
    return lax.dot_general(a, b, (((0,), (0,)), ((), ())), preferred_element_type=F32)


def _split_bf16(x):
    hi = x.astype(BF16)
    lo = (x - hi.astype(F32)).astype(BF16)
    return hi, lo


def _pack_pair(a, b):
    ua = lax.bitcast_convert_type(a.astype(BF16).astype(F32), jnp.uint32)
    ub = lax.bitcast_convert_type(b.astype(BF16).astype(F32), jnp.uint32)
    return ub | (ua >> 16)


def _unpack_pair(w):
    lo = lax.bitcast_convert_type(w << 16, F32)
    hi = lax.bitcast_convert_type(w & jnp.uint32(0xFFFF0000), F32)
    return lo, hi


def _store_packed(ref, val, n_rows):
    d = val.shape[1]
    slabs = d // (2 * LANES)
    for s in range(slabs):
        a = val[:, s * LANES:(s + 1) * LANES]
        b = val[:, d // 2 + s * LANES:d // 2 + (s + 1) * LANES]
        ref[pl.ds(s, n_rows, stride=slabs), :] = _pack_pair(a, b)


def _load_packed(ref, n_rows, slabs):
    return [_unpack_pair(ref[pl.ds(s, n_rows, stride=slabs), :]) for s in range(slabs)]


def _ada_kernel(c_ref, w_ref, b_ref, o_ref):
    c = c_ref[...]
    cond = c * _sigmoid(c)
    o_ref[...] = jnp.dot(cond, w_ref[...], preferred_element_type=F32,
                         precision=lax.Precision.HIGHEST) + b_ref[...]


def _ada(c, ada_w, ada_b):
    bsz, d = c.shape
    n = ada_w.shape[1]
    tn = 1024
    return pl.pallas_call(
        _ada_kernel,
        grid=(n // tn,),
        in_specs=[pl.BlockSpec((bsz, d), lambda j: (0, 0)),
                  pl.BlockSpec((d, tn), lambda j: (0, j)),
                  pl.BlockSpec((1, tn), lambda j: (0, j))],
        out_specs=pl.BlockSpec((bsz, tn), lambda j: (0, j)),
        out_shape=jax.ShapeDtypeStruct((bsz, n), F32),
        compiler_params=_cparams("arbitrary"),
        name="ada_mod",
    )(c, ada_w, ada_b.reshape(1, n))


def _rms(x, g):
    return x * lax.rsqrt(jnp.mean(x * x, axis=-1, keepdims=True) + RMS_EPS) * g


def _norm_proj_kernel(x_ref, g_ref, sc_ref, sh_ref, w_ref, o_ref):
    h = _rms(x_ref[...], g_ref[...]) * (1.0 + sc_ref[0]) + sh_ref[0]
    o_ref[...] = _dot(h.astype(BF16), w_ref[...]).astype(o_ref.dtype)


def _norm_proj(x2, g, sc, sh, w, seqlen, tm, name):
    n, d = x2.shape
    nout = w.shape[1]
    tpb = seqlen // tm
    return pl.pallas_call(
        _norm_proj_kernel,
        grid=(n // tm,),
        in_specs=[pl.BlockSpec((tm, d), lambda i: (i, 0)),
                  pl.BlockSpec((1, d), lambda i: (0, 0)),
                  pl.BlockSpec((1, 1, d), lambda i: (i // tpb, 0, 0)),
                  pl.BlockSpec((1, 1, d), lambda i: (i // tpb, 0, 0)),
                  pl.BlockSpec((d, nout), lambda i: (0, 0), pipeline_mode=pl.Buffered(1))],
        out_specs=pl.BlockSpec((tm, nout), lambda i: (i, 0)),
        out_shape=jax.ShapeDtypeStruct((n, nout), BF16),
        compiler_params=_cparams("parallel"),
        name=name,
    )(x2, g.reshape(1, d), sc[:, None, :], sh[:, None, :], w)


def _s5_tables(lam_re, lam_im, log_dt, b_re, b_im, c_re, c_im):
    g, p, cg = b_re.shape
    gs = S5_SLAB // cg
    ns = g // gs
    lr = jnp.minimum(lam_re.astype(F32), -1e-4)
    li = lam_im.astype(F32)
    dt = jnp.exp(log_dt.astype(F32))[:, None]
    mag = jnp.exp(lr * dt)
    ar, ai = mag * jnp.cos(li * dt), mag * jnp.sin(li * dt)
    den = lr * lr + li * li
    qr = ((ar - 1.0) * lr + ai * li) / den
    qi = (ai * lr - (ar - 1.0) * li) / den
    br, bi = b_re.astype(F32), b_im.astype(F32)
    bbr = qr[..., None] * br - qi[..., None] * bi
    bbi = qr[..., None] * bi + qi[..., None] * br

    def in_rows(t):
        return t.reshape(ns, gs, p, cg).transpose(0, 1, 3, 2).reshape(ns, gs * cg, p).astype(BF16)

    def out_cols(t):
        return t.reshape(ns, gs, cg, p).transpose(0, 3, 1, 2).reshape(ns, p, gs * cg).astype(BF16)

    a_tab = jnp.stack([ar.reshape(ns, gs * p), ai.reshape(ns, gs * p)], axis=1)
    return (in_rows(bbr), in_rows(bbi), out_cols(c_re.astype(F32)), out_cols(-c_im.astype(F32)), a_tab)


def _gelu_tanh(x):
    return 0.5 * x * (1.0 + jnp.tanh(math.sqrt(2.0 / math.pi) * (x + 0.044715 * (x * x * x))))


def _s5_kernel(x_ref, g_ref, sc_ref, sh_ref, wu_ref, perm_ref, bre_ref, bim_ref, cre_ref, cim_ref, a_ref,
               d_ref, gw_ref, gb_ref, o_ref, b_ref, c_ref, u_scr, bscr, sscr, yscr, st_ref):
    bsz, lb, d_in = x_ref.shape
    rows = bsz * lb
    dch = wu_ref.shape[1]
    half = st_ref.shape[2] // 2
    n_p = bre_ref.shape[2]

    @pl.when(pl.program_id(0) == 0)
    def _():
        st_ref[...] = jnp.zeros_like(st_ref)
        tile_in = (lax.broadcasted_iota(jnp.int32, (n_p, half), 0)
                   == lax.broadcasted_iota(jnp.int32, (n_p, half), 1) % n_p).astype(BF16)
        tile_out = (lax.broadcasted_iota(jnp.int32, (half, n_p), 0) % n_p
                    == lax.broadcasted_iota(jnp.int32, (half, n_p), 1)).astype(BF16)
        in_mask = (lax.broadcasted_iota(jnp.int32, (S5_SLAB, half), 0) // SSM_GROUP
                   == lax.broadcasted_iota(jnp.int32, (S5_SLAB, half), 1) // n_p)
        out_mask = (lax.broadcasted_iota(jnp.int32, (half, S5_SLAB), 0) // n_p
                    == lax.broadcasted_iota(jnp.int32, (half, S5_SLAB), 1) // SSM_GROUP)
        for s in range(dch // S5_SLAB):
            b_ref[s, :, :half] = jnp.where(in_mask, _dot(bre_ref[s], tile_in), 0.0).astype(BF16)
            b_ref[s, :, half:] = jnp.where(in_mask, _dot(bim_ref[s], tile_in), 0.0).astype(BF16)
            c_ref[s, :half, :] = jnp.where(out_mask, _dot(tile_out, cre_ref[s]), 0.0).astype(BF16)
            c_ref[s, half:, :] = jnp.where(out_mask, _dot(tile_out, cim_ref[s]), 0.0).astype(BF16)

    h = (_rms(x_ref[...], g_ref[...]) * (1.0 + sc_ref[...]) + sh_ref[...]).astype(BF16)
    u_nat = _dot(h.reshape(rows, d_in), wu_ref[...]).astype(BF16)
    u_scr[...] = _dot(perm_ref[...], u_nat).astype(BF16)
    n_slab = dch // S5_SLAB
    slab = lambda s: slice(s * S5_SLAB, (s + 1) * S5_SLAB)

    def project_in(s):
        bscr[s % 2] = _dot(u_scr[:, slab(s)], b_ref[s])

    project_in(0)
    for s in range(n_slab):
        if s + 1 < n_slab:
            project_in(s + 1)
        buf = s % 2
        a_re = a_ref[s, 0:1, :]
        a_im = a_ref[s, 1:2, :]
        s_r = st_ref[s, :, :half]
        s_i = st_ref[s, :, half:]
        for l in range(rows // bsz):
            at_l = slice(l * bsz, (l + 1) * bsz)
            s_r, s_i = (a_re * s_r - a_im * s_i + bscr[buf, at_l, :half],
                        a_re * s_i + a_im * s_r + bscr[buf, at_l, half:])
            sscr[buf, at_l, :half] = s_r
            sscr[buf, at_l, half:] = s_i
        st_ref[s, :, :half] = s_r
        st_ref[s, :, half:] = s_i
        y = _dot(sscr[buf].astype(BF16), c_ref[s]) + d_ref[:, slab(s)] * u_scr[:, slab(s)].astype(F32)
        yscr[:, slab(s)] = _gelu_tanh(y).astype(BF16)
    y = yscr[...]
    gate = _sigmoid(_dot(y, gw_ref[...]) + gb_ref[...])
    out_tm = (y.astype(F32) * gate).astype(BF16)
    out_nat = _dot_tn(perm_ref[...], out_tm).astype(o_ref.dtype)
    o_ref[...] = out_nat.reshape(bsz, lb, dch)


def _s5_glu(x2, g, sc, sh, w_u, tabs, d_skip, glu_w, glu_b, bsz, seqlen, lb):
    b_re, b_im, c_re, c_im, a_tab = tabs
    d_in, dch = w_u.shape
    ns, _, n_half = a_tab.shape
    n_state = 2 * n_half
    rows = lb * bsz
    r_idx = jnp.arange(rows)
    perm = ((r_idx % bsz) * lb + r_idx // bsz)[:, None] == r_idx[None, :]
    full = lambda a: pl.BlockSpec(a.shape, lambda i: (0,) * a.ndim, pipeline_mode=pl.Buffered(1))
    args = (x2.reshape(bsz, seqlen, d_in), g.reshape(1, 1, d_in), sc[:, None, :], sh[:, None, :], w_u,
            perm.astype(BF16), b_re, b_im, c_re, c_im, a_tab,
            d_skip.astype(F32).reshape(1, dch), glu_w.astype(BF16), glu_b.astype(F32).reshape(1, dch))
    y = pl.pallas_call(
        _s5_kernel,
        grid=(seqlen // lb,),
        in_specs=[pl.BlockSpec((bsz, lb, d_in), lambda i: (0, i, 0))] + [full(a) for a in args[1:]],
        out_specs=pl.BlockSpec((bsz, lb, dch), lambda i: (0, i, 0)),
        out_shape=jax.ShapeDtypeStruct((bsz, seqlen, dch), BF16),
        scratch_shapes=[pltpu.VMEM((ns, S5_SLAB, n_state), BF16), pltpu.VMEM((ns, n_state, S5_SLAB), BF16),
                        pltpu.VMEM((rows, dch), BF16), pltpu.VMEM((2, rows, n_state), F32),
                        pltpu.VMEM((2, rows, n_state), F32), pltpu.VMEM((rows, dch), BF16),
                        pltpu.VMEM((ns, bsz, n_state), F32)],
        compiler_params=_cparams("arbitrary"),
        name="s5_mixer_glu",
    )(*args)
    return y.reshape(bsz * seqlen, dch)


def _rwkv_kernel(z_ref, mu_ref, w0_ref, a0_ref, kkw_ref, ka_ref, wa_ref, gup_ref, rk_ref, lnw_ref, lnb_ref,
                 ones_ref, o_ref, s_ref, zlast_ref):
    nb, t, _ = z_ref.shape
    d_r = o_ref.shape[2]
    tile = HEADS_PER_TILE * RWKV_HEAD
    n_tiles = d_r // tile
    first_chunk = pl.program_id(1) == 0

    @pl.when(first_chunk)
    def _():
        s_ref[...] = jnp.zeros_like(s_ref)

    ones_bd = ones_ref[...]

    def seg_sum(x):
        return _dot(x.astype(BF16), ones_bd)

    row = lax.broadcasted_iota(jnp.int32, (t, t), 0)
    col = lax.broadcasted_iota(jnp.int32, (t, t), 1)
    tri = (row >= col).astype(BF16)
    st = HEADS_PER_TILE * t
    rs = lax.broadcasted_iota(jnp.int32, (2 * t, 2 * st), 0)
    t_r = rs % t
    t_c = lax.broadcasted_iota(jnp.int32, (2 * t, 2 * st), 1) % t
    keep = (t_r > t_c) | ((rs >= t) & (t_r == t_c))
    eye_w = (lax.broadcasted_iota(jnp.int32, (t, st), 0)
             == lax.broadcasted_iota(jnp.int32, (t, st), 1) % t).astype(F32)
    blk_mask = (lax.broadcasted_iota(jnp.int32, (st, st), 0) // t
                == lax.broadcasted_iota(jnp.int32, (st, st), 1) // t)
    lane = lax.broadcasted_iota(jnp.int32, (1, tile), 1)
    head_masks = [(lane >= j * RWKV_HEAD) & (lane < (j + 1) * RWKV_HEAD) for j in range(HEADS_PER_TILE)]
    bd_r = lax.broadcasted_iota(jnp.int32, (tile, tile), 0) // RWKV_HEAD
    bd_c = lax.broadcasted_iota(jnp.int32, (tile, tile), 1) // RWKV_HEAD
    bd_mask = bd_r == bd_c
    n_levels = int(math.log2(t))
    slices = [slice(hg * tile, (hg + 1) * tile) for hg in range(n_tiles)]
    units = [(bb, hg) for bb in range(nb) for hg in range(n_tiles)]
    n_u = range(len(units))

    def stack(x):
        zero = jnp.zeros_like(x)
        return jnp.concatenate([jnp.where(m, x, zero) for m in head_masks], axis=0)

    def bf(x):
        return x.astype(BF16)

    at, qt, bt, kt, vv, em, etm, wtot, rkb, gate = ([] for _ in range(10))
    row0 = lax.broadcasted_iota(jnp.int32, (t, 1), 0) == 0
    lora_lane = lax.broadcasted_iota(jnp.int32, (t, LORA_W + LORA_A), 1)
    for bb in range(nb):
        z = z_ref[bb].astype(F32)
        prev_row = jnp.where(first_chunk, 0.0, zlast_ref[bb, 0:1, :])
        zl = z + mu_ref[...] * (jnp.where(row0, prev_row, pltpu.roll(z, 1, 0)) - z)
        zlast_ref[bb, 0:1, :] = z[t - 1:t, :]
        xwa = zl[:, 3 * d_r:3 * d_r + LORA_W + LORA_A]
        lhs = jnp.where(lora_lane < LORA_W, jnp.tanh(xwa), xwa).astype(BF16)
        wa = _dot(lhs, wa_ref[...])
        lw = -DECAY_SCALE * _sigmoid(w0_ref[...] + wa[:, :d_r])
        asig = _sigmoid(a0_ref[...] + wa[:, d_r:])
        r = zl[:, :d_r]
        k_raw = zl[:, d_r:2 * d_r]
        kk = k_raw * kkw_ref[...]
        kp = k_raw * (1.0 + (asig - 1.0) * ka_ref[...])
        xg = zl[:, 3 * d_r + LORA_W + LORA_A:3 * d_r + LORA_W + LORA_A + LORA_G]
        g_full = _dot(_sigmoid(xg).astype(BF16), gup_ref[...])
        lw_hi, lw_lo = _split_bf16(lw)
        cum = _dot(tri, lw_hi) + _dot(tri, lw_lo)
        mid = cum[t // 2 - 1:t // 2, :]
        tot = cum[t - 1:t, :]
        e1 = jnp.exp(cum - mid)
        e2 = jnp.exp(mid - cum)
        e1p = e1 * jnp.exp(-lw)
        kk2 = kk * kk
        kkn = kk / jnp.maximum(jnp.sqrt(jnp.concatenate([seg_sum(kk2[:, sl]) for sl in slices], axis=1)), 1e-12)
        full = dict(at=-kkn * e1p, qt=r * e1, bt=kkn * asig * e2, kt=kp * e2,
                    vv=zl[:, 2 * d_r:3 * d_r], em=jnp.exp(mid), etm=jnp.exp(tot - mid), wtot=jnp.exp(tot),
                    rkb=r * kp * rk_ref[...], gate=g_full)
        for dst, key in ((at, "at"), (qt, "qt"), (bt, "bt"), (kt, "kt"), (vv, "vv"), (em, "em"),
                         (etm, "etm"), (wtot, "wtot"), (rkb, "rkb"), (gate, "gate")):
            dst.extend(full[key][:, sl] for sl in slices)

    s_old = [s_ref[i] for i in n_u]
    wide = []
    for i in n_u:
        lhs = bf(jnp.concatenate([at[i], qt[i]], axis=0))
        rhs = jnp.concatenate([stack(bf(bt[i])), stack(bf(kt[i]))], axis=0)
        a_w = bf(jnp.where(keep, _dot_nt(lhs, rhs), 0.0))
        wide.append([a_w[:t, :st], a_w[:t, st:], a_w[t:, :]])
    x_state = [_dot_nt(bf(jnp.concatenate([at[i], qt[i]], axis=0) * em[i]), bf(s_old[i]))
               for i in n_u]
    sv = [stack(bf(vv[i])) for i in n_u]
    akv = [_dot(wide[i][1], sv[i]) for i in n_u]

    def expand(x_w):
        x_b = bf(x_w)
        return jnp.where(blk_mask, jnp.concatenate([x_b] * HEADS_PER_TILE, axis=0), jnp.zeros((st, st), BF16))

    p_acc = [eye_w + w[0].astype(F32) for w in wide]
    q_bd = [expand(w[0]) for w in wide]
    q_pow = [_dot(wide[i][0], q_bd[i]) for i in n_u]
    for lev in range(1, n_levels):
        for i in n_u:
            q_bd[i] = expand(q_pow[i])
            if lev < n_levels - 1:
                both = _dot(bf(jnp.concatenate([p_acc[i], q_pow[i]], axis=0)), q_bd[i])
                p_acc[i] = p_acc[i] + both[:t]
                q_pow[i] = both[t:]
            else:
                p_acc[i] = p_acc[i] + _dot(bf(p_acc[i]), q_bd[i])
    u_all = [_dot(bf(p_acc[i]), stack(bf(x_state[i][:t] + akv[i]))) for i in n_u]
    y_all = [x_state[i][t:]
             + _dot(wide[i][2], jnp.concatenate([stack(bf(u_all[i])), sv[i]], axis=0)) for i in n_u]

    for i, (bb, hg) in enumerate(units):
        sl = slices[hg]
        y = y_all[i]
        uv = bf(jnp.concatenate([u_all[i], vv[i]], axis=0))
        bk_end = bf(jnp.concatenate([bt[i], kt[i]], axis=0) * etm[i])
        s_ref[i] = s_old[i] * wtot[i] + jnp.where(bd_mask, _dot_tn(uv, bk_end), 0.0)

        mean = seg_sum(y) * (1.0 / RWKV_HEAD)
        dlt = y - mean
        var = seg_sum(dlt * dlt) * (1.0 / RWKV_HEAD)
        yn = dlt * lax.rsqrt(var + GN_EPS) * lnw_ref[:, sl] + lnb_ref[:, sl]
        out = (yn + seg_sum(rkb[i]) * vv[i]) * gate[i]
        o_ref[bb, :, sl] = out.astype(o_ref.dtype)


def _rwkv(z, mu, w0, w_up, a0, a_up, g_up, k_k, k_a, r_k, ln_w, ln_b, bsz, seqlen):
    n, dz = z.shape
    d_r = w0.shape[0]
    t = RWKV_CHUNK
    nb = RWKV_BATCH_PER_STEP
    nch = seqlen // t
    tile = HEADS_PER_TILE * RWKV_HEAD
    hid = jnp.arange(tile) // RWKV_HEAD
    ones_bd = (hid[:, None] == hid[None, :]).astype(BF16)
    wa = jnp.zeros((LORA_W + LORA_A, 2 * d_r), F32)
    wa = wa.at[:LORA_W, :d_r].set(w_up.astype(F32)).at[LORA_W:, d_r:].set(a_up.astype(F32)).astype(BF16)
    row = lambda a: a.astype(F32).reshape(1, -1)
    full = lambda a: pl.BlockSpec(a.shape, lambda b, c: (0, 0))
    params = (row(mu), row(w0), row(a0), row(k_k), row(k_a), wa, g_up.astype(BF16), row(r_k), row(ln_w),
              row(ln_b), ones_bd)
    y = pl.pallas_call(
        _rwkv_kernel,
        grid=(bsz // nb, nch),
        in_specs=[pl.BlockSpec((nb, t, dz), lambda b, c: (b, c, 0))] + [full(a) for a in params],
        out_specs=pl.BlockSpec((nb, t, d_r), lambda b, c: (b, c, 0)),
        out_shape=jax.ShapeDtypeStruct((bsz, seqlen, d_r), BF16),
        scratch_shapes=[pltpu.VMEM((nb * (d_r // tile), tile, tile), F32), pltpu.VMEM((nb, 8, dz), F32)],
        compiler_params=_cparams("parallel", "arbitrary"),
        name="rwkv7_chunked",
    )(z.reshape(bsz, seqlen, dz), *params)
    return y.reshape(n, d_r)


def _post_mix_kernel(ys_ref, yr_ref, wo1_ref, wo2_ref, x_ref, g1_ref, gt_ref, g2_ref, sc_ref, sh_ref,
                     wr_both_ref, br_ref, x1_out, h2_out, lg_out):
    tm = x_ref.shape[0]
    sub = min(tm, POST_SUB_ROWS)
    slabs = h2_out.shape[0] // tm
    nl = lg_out.shape[1]
    for c in range(tm // sub):
        rows = slice(c * sub, (c + 1) * sub)
        mixed = _dot(ys_ref[rows, :], wo1_ref[...]) + _dot(yr_ref[rows, :], wo2_ref[...])
        x1 = x_ref[rows, :] + gt_ref[0] * _rms(mixed, g1_ref[...])
        x1_out[rows, :] = x1
        h2 = _rms(x1, g2_ref[...]) * (1.0 + sc_ref[0]) + sh_ref[0]
        _store_packed(h2_out.at[pl.ds(c * sub * slabs, sub * slabs)], h2, sub)
        hi, lo = _split_bf16(h2)
        both = _dot(hi, wr_both_ref[...])
        lg_out[rows, :] = both[:, :nl] + both[:, nl:] + _dot(lo, wr_both_ref[:, :nl]) + br_ref[...]


def _post_mix(ys, yr, w_out, x2, g1, gt1, g2, sc2, sh2, w_route, b_route, seqlen, tm):
    n, d = x2.shape
    ds = ys.shape[1]
    tpb = seqlen // tm
    slabs = d // (2 * LANES)
    wr_both = jnp.concatenate(_split_bf16(w_route), axis=1)
    rows = lambda w: pl.BlockSpec((tm, w), lambda i: (i, 0))
    full = lambda a: pl.BlockSpec(a.shape, lambda i: (0,) * a.ndim)
    bat = pl.BlockSpec((1, 1, d), lambda i: (i // tpb, 0, 0))
    args = (ys, yr, w_out[:ds].astype(BF16), w_out[ds:].astype(BF16), x2, g1.reshape(1, d), gt1[:, None, :],
            g2.reshape(1, d),
            sc2[:, None, :], sh2[:, None, :], wr_both, b_route.reshape(1, -1))
    in_specs = [rows(ds), rows(yr.shape[1]), full(args[2]), full(args[3]), rows(d), full(args[5]), bat,
                full(args[7]), bat, bat, full(wr_both), full(args[11])]
    return pl.pallas_call(
        _post_mix_kernel,
        grid=(n // tm,),
        in_specs=in_specs,
        out_specs=[rows(d), pl.BlockSpec((tm * slabs, LANES), lambda i: (i, 0)), rows(ROUTE_LANES)],
        out_shape=[jax.ShapeDtypeStruct((n, d), F32), jax.ShapeDtypeStruct((n * slabs, LANES), jnp.uint32),
                   jax.ShapeDtypeStruct((n, ROUTE_LANES), F32)],
        compiler_params=_cparams("parallel"),
        name="out_proj_post",
    )(*args)


def _route_kernel(lg_ref, info_ref, cnt_ref, carry):
    i = pl.program_id(0)
    tm = lg_ref.shape[0]

    @pl.when(i == 0)
    def _():
        carry[...] = jnp.zeros_like(carry)

    lg = lg_ref[...]
    lane = lax.broadcasted_iota(jnp.int32, lg.shape, 1)
    lane_f = lane.astype(F32)
    neg = jnp.float32(-jnp.inf)
    big = jnp.float32(1e9)
    is_g = (lane >= N_EXPERTS) & (lane < N_EXPERTS + N_GROUPS)
    gl = jnp.where(is_g, lg, neg)
    gmax = jnp.max(gl, axis=-1, keepdims=True)
    gidx = jnp.min(jnp.where(gl == gmax, lane_f - N_EXPERTS, big), axis=-1, keepdims=True)
    p_grp = 1.0 / jnp.sum(jnp.where(is_g, jnp.exp(gl - gmax), 0.0), axis=-1, keepdims=True)
    in_grp = (lane < N_EXPERTS) & ((lane // EXPERTS_PER_GROUP).astype(F32) == gidx)
    el = jnp.where(in_grp, lg, neg)
    m1 = jnp.max(el, axis=-1, keepdims=True)
    i1 = jnp.min(jnp.where(el == m1, lane_f, big), axis=-1, keepdims=True)
    el2 = jnp.where(lane_f == i1, neg, el)
    m2 = jnp.max(el2, axis=-1, keepdims=True)
    i2 = jnp.min(jnp.where(el2 == m2, lane_f, big), axis=-1, keepdims=True)
    ex = jnp.exp(m2 - m1)
    w1 = p_grp / (1.0 + ex)
    w2 = p_grp * ex / (1.0 + ex)

    oh1 = lane_f == i1
    oh2 = lane_f == i2
    onehot = (oh1 | oh2).astype(BF16)
    rr = lax.broadcasted_iota(jnp.int32, (tm, tm), 0)
    cc = lax.broadcasted_iota(jnp.int32, (tm, tm), 1)
    before = _dot((rr > cc).astype(BF16), onehot) + carry[...]
    rank1 = jnp.sum(jnp.where(oh1, before, 0.0), axis=-1, keepdims=True)
    rank2 = jnp.sum(jnp.where(oh2, before, 0.0), axis=-1, keepdims=True)
    carry[...] = carry[...] + jnp.sum(onehot.astype(F32), axis=0, keepdims=True)
    cnt_ref[...] = carry[...]

    info = jnp.where(lane == 0, i1, 0.0)
    info = jnp.where(lane == 1, i2, info)
    info = jnp.where(lane == 2, w1, info)
    info = jnp.where(lane == 3, w2, info)
    info = jnp.where(lane == 4, rank1, info)
    info = jnp.where(lane == 5, rank2, info)
    info_ref[...] = info


def _route(logits, tm):
    n = logits.shape[0]
    return pl.pallas_call(
        _route_kernel,
        grid=(n // tm,),
        in_specs=[pl.BlockSpec((tm, ROUTE_LANES), lambda i: (i, 0))],
        out_specs=[pl.BlockSpec((tm, ROUTE_LANES), lambda i: (i, 0)),
                   pl.BlockSpec((1, ROUTE_LANES), lambda i: (0, 0))],
        out_shape=[jax.ShapeDtypeStruct((n, ROUTE_LANES), F32),
                   jax.ShapeDtypeStruct((1, ROUTE_LANES), F32)],
        scratch_shapes=[pltpu.VMEM((1, ROUTE_LANES), F32)],
        compiler_params=_cparams("arbitrary"),
        name="moe_route",
    )(logits)


def _slot_rows_kernel(info_ref, seg_ref, o_ref, *, slabs):
    info = info_ref[...]
    lane = lax.broadcasted_iota(jnp.int32, info.shape, 1)
    lane_f = lane.astype(F32)
    seg = seg_ref[...]
    d0 = jnp.sum(jnp.where(lane_f == info[:, 0:1], seg, 0.0), axis=-1, keepdims=True) + info[:, 4:5] * slabs
    d1 = jnp.sum(jnp.where(lane_f == info[:, 1:2], seg, 0.0), axis=-1, keepdims=True) + info[:, 5:6] * slabs
    o_ref[...] = jnp.where(lane == 0, d0, jnp.where(lane == 1, d1, 0.0)).astype(jnp.int32)


def _slot_rows(info, seg_row, slabs, tm):
    n = info.shape[0]
    return pl.pallas_call(
        functools.partial(_slot_rows_kernel, slabs=slabs),
        grid=(n // tm,),
        in_specs=[pl.BlockSpec((tm, ROUTE_LANES), lambda i: (i, 0)),
                  pl.BlockSpec((1, ROUTE_LANES), lambda i: (0, 0))],
        out_specs=pl.BlockSpec((tm, ROUTE_LANES), lambda i: (i, 0)),
        out_shape=jax.ShapeDtypeStruct((n, ROUTE_LANES), jnp.int32),
        compiler_params=_cparams("parallel"),
        name="moe_slot_rows",
    )(info, seg_row)


def _dispatch_kernel(d0_ref, d1_ref, pad_row_ref, pad_len_ref, na_ref, h_ref, buf_out, zeros, sem, zsem, *,
                     slabs, n_blocks):
    tm = h_ref.shape[0] // slabs

    @pl.when(pl.program_id(0) == 0)
    def _():
        zeros[...] = jnp.zeros_like(zeros)
        bits = [1 << b for b in reversed(range(EXPERT_BLOCK.bit_length() - 1))]

        def pad_copy(e, bit):
            done = pad_len_ref[e] & ~(2 * bit - 1)
            dst = pl.multiple_of(pad_row_ref[e] + done * slabs, slabs)
            return pltpu.make_async_copy(zeros.at[pl.ds(0, bit * slabs)], buf_out.at[pl.ds(dst, bit * slabs)], zsem)

        def block_copy(j):
            dst = pl.multiple_of(j * (EXPERT_BLOCK * slabs), EXPERT_BLOCK * slabs)
            return pltpu.make_async_copy(zeros, buf_out.at[pl.ds(dst, EXPERT_BLOCK * slabs)], zsem)

        def each(action):
            def per_expert(e, c):
                for bit in bits:
                    @pl.when((pad_len_ref[e] & bit) != 0)
                    def _():
                        action(pad_copy(e, bit))
                return c

            def per_block(j, c):
                action(block_copy(j))
                return c

            lax.fori_loop(0, N_EXPERTS, per_expert, 0)
            lax.fori_loop(na_ref[0], n_blocks, per_block, 0)

        each(lambda c: c.start())
        each(lambda c: c.wait())

    def copy(t, dest_ref):
        src = pl.multiple_of(t * slabs, slabs)
        dst = pl.multiple_of(dest_ref[t], slabs)
        return pltpu.make_async_copy(h_ref.at[pl.ds(src, slabs)], buf_out.at[pl.ds(dst, slabs)], sem)

    def start(t, c):
        copy(t, d0_ref).start(priority=0)
        copy(t, d1_ref).start(priority=1)
        return c

    lax.fori_loop(0, tm, start, 0, unroll=4)
    for _ in range(2):
        pltpu.make_async_copy(h_ref, buf_out.at[pl.ds(0, tm * slabs)], sem).wait()


def _dispatch(h2p, dest_rows, pad_row, pad_len, n_active, cap, slabs, tm):
    n = h2p.shape[0] // slabs
    smem = pl.BlockSpec((tm,), lambda i: (i,), memory_space=pltpu.SMEM)
    table = lambda a: pl.BlockSpec(a.shape, lambda i: (0,), memory_space=pltpu.SMEM)
    return pl.pallas_call(
        functools.partial(_dispatch_kernel, slabs=slabs, n_blocks=cap // EXPERT_BLOCK),
        grid=(n // tm,),
        in_specs=[smem, smem, table(pad_row), table(pad_len), table(n_active),
                  pl.BlockSpec((tm * slabs, LANES), lambda i: (i, 0))],
        out_specs=pl.BlockSpec(memory_space=pl.ANY),
        out_shape=jax.ShapeDtypeStruct((cap * slabs, LANES), h2p.dtype),
        scratch_shapes=[pltpu.VMEM((EXPERT_BLOCK * slabs, LANES), h2p.dtype), pltpu.SemaphoreType.DMA,
                        pltpu.SemaphoreType.DMA],
        compiler_params=_cparams("arbitrary"),
        name="moe_dispatch",
    )(dest_rows[0], dest_rows[1], pad_row, pad_len, n_active, h2p)


def _moe_kernel(na_ref, eseq_ref, epos_ref, nd_ref, x_ref, w1_hbm, w3_hbm, w2_hbm, o_ref,
                w1f, w3f, w2f, w1b, w3b, w2b, sem):
    j = pl.program_id(0)
    active = j < na_ref[0]
    pos = epos_ref[j]
    fresh = (j == 0) | (pos != epos_ref[jnp.maximum(j - 1, 0)])

    slabs = w1b.shape[0]
    half = slabs * LANES

    def weight_copies(p):
        e = eseq_ref[p]
        slot = p % 2
        return [pltpu.make_async_copy(w_hbm.at[e], stage.at[slot], sem.at[slot, i])
                for i, (w_hbm, stage) in enumerate(((w1_hbm, w1f), (w3_hbm, w3f), (w2_hbm, w2f)))]

    @pl.when(j == 0)
    def _():
        for c in weight_copies(0):
            c.start()

    @pl.when(active & fresh)
    def _():
        for c in weight_copies(pos):
            c.wait()

        @pl.when(pos + 1 < nd_ref[0])
        def _():
            for c in weight_copies(pos + 1):
                c.start()

        slot = pos % 2
        for s in range(slabs):
            for dst, src in ((w1b, w1f), (w3b, w3f)):
                dst[s, :LANES, :] = src[slot, s * LANES:(s + 1) * LANES, :].astype(BF16)
                dst[s, LANES:, :] = src[slot, half + s * LANES:half + (s + 1) * LANES, :].astype(BF16)
        w2b[...] = w2f[slot].astype(BF16)

    @pl.when(active)
    def _():
        acc1 = jnp.zeros((EXPERT_BLOCK, w1b.shape[2]), F32)
        acc3 = jnp.zeros((EXPERT_BLOCK, w1b.shape[2]), F32)
        for s, (lo, hi) in enumerate(_load_packed(x_ref, EXPERT_BLOCK, slabs)):
            lhs = jnp.concatenate([lo.astype(BF16), hi.astype(BF16)], axis=1)
            acc1 = acc1 + _dot(lhs, w1b[s])
            acc3 = acc3 + _dot(lhs, w3b[s])
        hid = (acc1 * _sigmoid(acc1)) * acc3
        _store_packed(o_ref, _dot(hid.astype(BF16), w2b[...]), EXPERT_BLOCK)


def _moe(x_buf, n_active, expert_seq, block_pos, n_used, w1, w3, w2, slabs):
    cap = x_buf.shape[0] // slabs
    d, de = w1.shape[1], w1.shape[2]
    nb = cap // EXPERT_BLOCK

    def xmap(j, na, *_):
        return (jnp.minimum(j, na[0] - 1), 0)

    xspec = pl.BlockSpec((EXPERT_BLOCK * slabs, LANES), xmap)
    hbm = pl.BlockSpec(memory_space=pl.ANY)
    grid_spec = pltpu.PrefetchScalarGridSpec(
        num_scalar_prefetch=4,
        grid=(nb,),
        in_specs=[xspec, hbm, hbm, hbm],
        out_specs=xspec,
        scratch_shapes=[pltpu.VMEM((2, d, de), F32), pltpu.VMEM((2, d, de), F32), pltpu.VMEM((2, de, d), F32),
                        pltpu.VMEM((slabs, 2 * LANES, de), BF16), pltpu.VMEM((slabs, 2 * LANES, de), BF16),
                        pltpu.VMEM((de, d), BF16), pltpu.SemaphoreType.DMA((2, 3))],
    )
    return pl.pallas_call(
        _moe_kernel,
        grid_spec=grid_spec,
        out_shape=jax.ShapeDtypeStruct(x_buf.shape, x_buf.dtype),
        input_output_aliases={4: 0},
        compiler_params=_cparams("arbitrary"),
        name="moe_experts",
    )(n_active, expert_seq, block_pos, n_used, x_buf, w1, w3, w2)


def _combine_kernel(d0_ref, d1_ref, d0n_ref, d1n_ref, y_ref, info_ref, x1_ref, g_ref, gt_ref, o_ref,
                    rows, sem, *, slabs, tiles_per_idx, chunk):
    i = pl.program_id(0)
    n_tiles = pl.num_programs(0)
    tm = x1_ref.shape[0]
    slot = i % 2
    nxt_slot = 1 - slot
    nxt = jnp.minimum(i + 1, n_tiles - 1)

    def copy(tile, to_slot, t, dest_ref, k):
        off = (tile % tiles_per_idx) * tm
        src = pl.multiple_of(dest_ref[off + t], slabs)
        dst = pl.multiple_of(t * slabs, slabs)
        return pltpu.make_async_copy(y_ref.at[pl.ds(src, slabs)], rows.at[to_slot, k, pl.ds(dst, slabs)],
                                     sem.at[to_slot])

    def drain(which):
        for k in range(2):
            pltpu.make_async_copy(y_ref.at[pl.ds(0, tm * slabs)], rows.at[which, k], sem.at[which]).wait()

    @pl.when(i == 0)
    def _():
        def start(t, c):
            copy(i, slot, t, d0_ref, 0).start(priority=0)
            copy(i, slot, t, d1_ref, 1).start(priority=1)
            return c

        lax.fori_loop(0, tm, start, 0, unroll=4)

    drain(slot)
    g_row = g_ref[...]
    gt_row = gt_ref[0]
    for c in range(tm // chunk):
        tok = slice(c * chunk, (c + 1) * chunk)
        info = info_ref[tok, :]
        w1 = info[:, 2:3]
        w2 = info[:, 3:4]
        lo_parts, hi_parts = [], []
        for s in range(slabs):
            at_s = pl.ds(c * chunk * slabs + s, chunk, stride=slabs)
            lo0, hi0 = _unpack_pair(rows[slot, 0, at_s, :])
            lo1, hi1 = _unpack_pair(rows[slot, 1, at_s, :])
            lo_parts.append(w1 * lo0 + w2 * lo1)
            hi_parts.append(w1 * hi0 + w2 * hi1)
        ffn = jnp.concatenate(lo_parts + hi_parts, axis=1)
        o_ref[tok, :] = x1_ref[tok, :] + gt_row * _rms(ffn, g_row)
        for t in range(c * chunk, (c + 1) * chunk):
            copy(nxt, nxt_slot, t, d0n_ref, 0).start(priority=0)
            copy(nxt, nxt_slot, t, d1n_ref, 1).start(priority=1)

    @pl.when(i == n_tiles - 1)
    def _():
        drain(nxt_slot)


def _combine(y_buf, dest_rows, info, x1, g, gt2, seqlen, slabs, tm):
    n, d = x1.shape
    tpb = seqlen // tm
    idx_block = max(tm, SMEM_INDEX_BLOCK)
    per = idx_block // tm
    last = n // tm - 1
    smem = pl.BlockSpec((idx_block,), lambda i: (i // per,), memory_space=pltpu.SMEM)
    smem_next = pl.BlockSpec((idx_block,), lambda i: (jnp.minimum(i + 1, last) // per,), memory_space=pltpu.SMEM)
    return pl.pallas_call(
        functools.partial(_combine_kernel, slabs=slabs, tiles_per_idx=per, chunk=min(tm, COMBINE_CHUNK)),
        grid=(n // tm,),
        in_specs=[smem, smem, smem_next, smem_next,
                  pl.BlockSpec(memory_space=pl.ANY),
                  pl.BlockSpec((tm, ROUTE_LANES), lambda i: (i, 0)),
                  pl.BlockSpec((tm, d), lambda i: (i, 0)),
                  pl.BlockSpec((1, d), lambda i: (0, 0)),
                  pl.BlockSpec((1, 1, d), lambda i: (i // tpb, 0, 0))],
        out_specs=pl.BlockSpec((tm, d), lambda i: (i, 0)),
        out_shape=jax.ShapeDtypeStruct((n, d), F32),
        scratch_shapes=[pltpu.VMEM((2, 2, tm * slabs, LANES), y_buf.dtype), pltpu.SemaphoreType.DMA((2,))],
        compiler_params=_cparams("arbitrary"),
        name="moe_combine",
    )(dest_rows[0], dest_rows[1], dest_rows[0], dest_rows[1], y_buf, info, x1, g.reshape(1, d),
      gt2[:, None, :])


def _pick(n, pref):
    while n % pref:
        pref //= 2
    return pref


def _layer(x2, mod, p, bsz, seqlen):
    n, d = x2.shape
    sh1, sc1, gt1, sh2, sc2, gt2 = jnp.split(mod, 6, axis=-1)
    d_ssm = p["ssm_d"].shape[0]

    z = _norm_proj(x2, p["norm_mix_pre"], sc1, sh1, p["w_in"][:, d_ssm:].astype(BF16), seqlen,
                   _pick(seqlen, 512), "in_proj_z")

    tabs = _s5_tables(p["ssm_lam_re"], p["ssm_lam_im"], p["ssm_log_dt"], p["ssm_b_re"], p["ssm_b_im"],
                      p["ssm_c_re"], p["ssm_c_im"])
    y_ssm = _s5_glu(x2, p["norm_mix_pre"], sc1, sh1, p["w_in"][:, :d_ssm].astype(BF16), tabs, p["ssm_d"], p["glu_w"],
                    p["glu_b"], bsz, seqlen, _pick(seqlen, S5_TIME_BLOCK))

    y_rwkv = _rwkv(z, p["rwkv_mu"], p["rwkv_w0"], p["rwkv_w_up"], p["rwkv_a0"], p["rwkv_a_up"],
                   p["rwkv_g_up"], p["rwkv_k_k"], p["rwkv_k_a"], p["rwkv_r_k"], p["rwkv_ln_w"],
                   p["rwkv_ln_b"], bsz, seqlen)

    w_route = jnp.zeros((d, ROUTE_LANES), F32)
    w_route = w_route.at[:, :N_EXPERTS].set(p["moe_w_exp"].astype(F32))
    w_route = w_route.at[:, N_EXPERTS:N_EXPERTS + N_GROUPS].set(p["moe_w_grp"].astype(F32))
    b_route = jnp.zeros((ROUTE_LANES,), F32)
    b_route = b_route.at[:N_EXPERTS].set(p["moe_b_exp"].astype(F32))
    b_route = b_route.at[N_EXPERTS:N_EXPERTS + N_GROUPS].set(p["moe_b_grp"].astype(F32))
    x1, h2p, logits = _post_mix(y_ssm, y_rwkv, p["w_out"], x2, p["norm_mix_post"], gt1, p["norm_ffn_pre"],
                                sc2, sh2, w_route, b_route, seqlen, _pick(seqlen, 512))
    slabs = d // (2 * LANES)

    info, counts = _route(logits, _pick(n, 512))
    cnt = counts[0, :N_EXPERTS].astype(jnp.int32)
    padded = (cnt + EXPERT_BLOCK - 1) // EXPERT_BLOCK * EXPERT_BLOCK
    pend = jnp.cumsum(padded)
    pstart = pend - padded
    n_blocks = -(-(2 * n) // EXPERT_BLOCK) + N_EXPERTS
    cap = n_blocks * EXPERT_BLOCK
    seg_row = jnp.zeros((1, ROUTE_LANES), F32).at[0, :N_EXPERTS].set((pstart * slabs).astype(F32))
    dest = _slot_rows(info, seg_row, slabs, _pick(n, 2048))
    dest_rows = (dest[:, 0], dest[:, 1])
    n_active = (pend[-1:] // EXPERT_BLOCK).astype(jnp.int32)
    used = (cnt > 0).astype(jnp.int32)
    used_pos = jnp.cumsum(used) - 1
    slot_ids = jnp.arange(N_EXPERTS, dtype=jnp.int32)
    expert_seq = jnp.sum(jnp.where((used_pos[None, :] == slot_ids[:, None]) & (used[None, :] > 0),
                                   slot_ids[None, :], 0), axis=1).astype(jnp.int32)
    block_first = jnp.arange(n_blocks, dtype=jnp.int32) * EXPERT_BLOCK
    block_pos = jnp.sum(jnp.where(pend[None, :] <= block_first[:, None], used[None, :], 0),
                        axis=1).astype(jnp.int32)
    n_used = jnp.sum(used).reshape(1)

    x_buf = _dispatch(h2p, dest_rows, ((pstart + cnt) * slabs).astype(jnp.int32),
                      (padded - cnt).astype(jnp.int32), n_active, cap, slabs, _pick(n, 1024))
    y_buf = _moe(x_buf, n_active, expert_seq, block_pos, n_used, p["moe_w1"], p["moe_w3"], p["moe_w2"], slabs)
    return _combine(y_buf, dest_rows, info, x1, p["norm_ffn_post"], gt2, seqlen, slabs, _pick(seqlen, 512))


def kernel(x, c, ada_w, ada_b, norm_mix_pre, norm_mix_post, norm_ffn_pre, norm_ffn_post, w_in, w_out, ssm_lam_re, ssm_lam_im, ssm_log_dt, ssm_b_re, ssm_b_im, ssm_c_re, ssm_c_im, ssm_d, glu_w, glu_b, rwkv_mu, rwkv_w0, rwkv_w_up, rwkv_a0, rwkv_a_up, rwkv_g_up, rwkv_k_k, rwkv_k_a, rwkv_r_k, rwkv_ln_w, rwkv_ln_b, moe_w_grp, moe_b_grp, moe_w_exp, moe_b_exp, moe_w1, moe_w3, moe_w2):
    bsz, seqlen, d = x.shape
    params = dict(norm_mix_pre=norm_mix_pre, norm_mix_post=norm_mix_post, norm_ffn_pre=norm_ffn_pre,
                  norm_ffn_post=norm_ffn_post, w_in=w_in, w_out=w_out, ssm_lam_re=ssm_lam_re,
                  ssm_lam_im=ssm_lam_im, ssm_log_dt=ssm_log_dt, ssm_b_re=ssm_b_re, ssm_b_im=ssm_b_im,
                  ssm_c_re=ssm_c_re, ssm_c_im=ssm_c_im, ssm_d=ssm_d, glu_w=glu_w, glu_b=glu_b,
                  rwkv_mu=rwkv_mu, rwkv_w0=rwkv_w0, rwkv_w_up=rwkv_w_up, rwkv_a0=rwkv_a0,
                  rwkv_a_up=rwkv_a_up, rwkv_g_up=rwkv_g_up, rwkv_k_k=rwkv_k_k, rwkv_k_a=rwkv_k_a,
                  rwkv_r_k=rwkv_r_k, rwkv_ln_w=rwkv_ln_w, rwkv_ln_b=rwkv_ln_b, moe_w_grp=moe_w_grp,
                  moe_b_grp=moe_b_grp, moe_w_exp=moe_w_exp, moe_b_exp=moe_b_exp, moe_w1=moe_w1,
                  moe_w3=moe_w3, moe_w2=moe_w2)
    x2 = x.reshape(bsz * seqlen, d)
    for layer in range(ada_w.shape[0]):
        mod = _ada(c, ada_w[layer], ada_b[layer])
        x2 = _layer(x2, mod, {k: v[layer] for k, v in params.items()}, bsz, seqlen)
    return x2.reshape(bsz, seqlen, d)
```

```python
import functools
import math

import jax
import jax.numpy as jnp
from jax import lax
from jax.experimental import pallas as pl
from jax.experimental.pallas import tpu as pltpu

F32 = jnp.float32
BF16 = jnp.bfloat16

SSM_GROUP = 16
S5_SLAB = 256
S5_TIME_BLOCK = 64
RWKV_HEAD = 64
RWKV_CHUNK = 64
RWKV_BATCH_PER_STEP = 4
HEADS_PER_TILE = 4
LORA_W = 64
LORA_A = 64
LORA_G = 128
N_GROUPS = 8
EXPERTS_PER_GROUP = 8
N_EXPERTS = N_GROUPS * EXPERTS_PER_GROUP
EXPERT_BLOCK = 256
RMS_EPS = 1e-6
GN_EPS = 64e-5
DECAY_SCALE = math.exp(-0.5)
LANES = 128
COMBINE_CHUNK = 64
SMEM_INDEX_BLOCK = 1024
ROUTE_LANES = LANES
V7X_VMEM_BYTES = 64 * 1024 * 1024
VMEM_LIMIT = V7X_VMEM_BYTES - 12 * 1024 * 1024


def _cparams(*sem):
    return pltpu.CompilerParams(dimension_semantics=sem, vmem_limit_bytes=VMEM_LIMIT)


def _sigmoid(x):
    return 1.0 / (1.0 + jnp.exp(-x))


def _dot(a, b):
    return jnp.dot(a, b, preferred_element_type=F32)


def _dot_nt(a, b):
    return lax.dot_general(a, b, (((1,), (1,)), ((), ())), preferred_element_type=F32)


def _dot_tn(a, b):
    return lax.dot_general(a, b, (((0,), (0,)), ((), ())), preferred_element_type=F32)


def _split_bf16(x):
    hi = x.astype(BF16)
    lo = (x - hi.astype(F32)).astype(BF16)
    return hi, lo


def _pack_pair(a, b):
    ua = lax.bitcast_convert_type(a.astype(BF16).astype(F32), jnp.uint32)
    ub = lax.bitcast_convert_type(b.astype(BF16).astype(F32), jnp.uint32)
    return ub | (ua >> 16)


def _unpack_pair(w):
    lo = lax.bitcast_convert_type(w << 16, F32)
    hi = lax.bitcast_convert_type(w & jnp.uint32(0xFFFF0000), F32)
    return lo, hi


def _store_packed(ref, val, n_rows):
    d = val.shape[1]
    slabs = d // (2 * LANES)
    for s in range(slabs):
        a = val[:, s * LANES:(s + 1) * LANES]
        b = val[:, d // 2 + s * LANES:d // 2 + (s + 1) * LANES]
        ref[pl.ds(s, n_rows, stride=slabs), :] = _pack_pair(a, b)


def _load_packed(ref, n_rows, slabs):
    return [_unpack_pair(ref[pl.ds(s, n_rows, stride=slabs), :]) for s in range(slabs)]


def _ada_kernel(c_ref, w_ref, b_ref, o_ref):
    c = c_ref[...]
    bsz = c.shape[0]
    c_hi, c_lo = _split_bf16(c * _sigmoid(c))
    w_hi, w_lo = _split_bf16(w_ref[...])
    first = _dot(jnp.concatenate([c_hi, c_lo], axis=0), w_hi)
    o_ref[...] = first[:bsz] + first[bsz:] + _dot(c_hi, w_lo) + b_ref[...]


def _ada(c, ada_w, ada_b):
    bsz, d = c.shape
    n = ada_w.shape[1]
    tn = 1024
    return pl.pallas_call(
        _ada_kernel,
        grid=(n // tn,),
        in_specs=[pl.BlockSpec((bsz, d), lambda j: (0, 0)),
                  pl.BlockSpec((d, tn), lambda j: (0, j)),
                  pl.BlockSpec((1, tn), lambda j: (0, j))],
        out_specs=pl.BlockSpec((bsz, tn), lambda j: (0, j)),
        out_shape=jax.ShapeDtypeStruct((bsz, n), F32),
        compiler_params=_cparams("arbitrary"),
        name="ada_mod",
    )(c, ada_w, ada_b.reshape(1, n))


def _rms(x, g):
    return x * lax.rsqrt(jnp.mean(x * x, axis=-1, keepdims=True) + RMS_EPS) * g


def _norm_proj_kernel(x_ref, g_ref, sc_ref, sh_ref, w_ref, o_ref):
    h = _rms(x_ref[...], g_ref[...]) * (1.0 + sc_ref[0]) + sh_ref[0]
    o_ref[...] = _dot(h.astype(BF16), w_ref[...]).astype(o_ref.dtype)


def _norm_proj(x2, g, sc, sh, w, seqlen, tm, name):
    n, d = x2.shape
    nout = w.shape[1]
    tpb = seqlen // tm
    return pl.pallas_call(
        _norm_proj_kernel,
        grid=(n // tm,),
        in_specs=[pl.BlockSpec((tm, d), lambda i: (i, 0)),
                  pl.BlockSpec((1, d), lambda i: (0, 0)),
                  pl.BlockSpec((1, 1, d), lambda i: (i // tpb, 0, 0)),
                  pl.BlockSpec((1, 1, d), lambda i: (i // tpb, 0, 0)),
                  pl.BlockSpec((d, nout), lambda i: (0, 0), pipeline_mode=pl.Buffered(1))],
        out_specs=pl.BlockSpec((tm, nout), lambda i: (i, 0)),
        out_shape=jax.ShapeDtypeStruct((n, nout), BF16),
        compiler_params=_cparams("parallel"),
        name=name,
    )(x2, g.reshape(1, d), sc[:, None, :], sh[:, None, :], w)


def _s5_tables(lam_re, lam_im, log_dt, b_re, b_im, c_re, c_im):
    g, p, cg = b_re.shape
    gs = S5_SLAB // cg
    ns = g // gs
    lr = jnp.minimum(lam_re.astype(F32), -1e-4)
    li = lam_im.astype(F32)
    dt = jnp.exp(log_dt.astype(F32))[:, None]
    mag = jnp.exp(lr * dt)
    ar, ai = mag * jnp.cos(li * dt), mag * jnp.sin(li * dt)
    den = lr * lr + li * li
    qr = ((ar - 1.0) * lr + ai * li) / den
    qi = (ai * lr - (ar - 1.0) * li) / den
    br, bi = b_re.astype(F32), b_im.astype(F32)
    bbr = qr[..., None] * br - qi[..., None] * bi
    bbi = qr[..., None] * bi + qi[..., None] * br

    def in_rows(t):
        return t.reshape(ns, gs, p, cg).transpose(0, 1, 3, 2).reshape(ns, gs * cg, p).astype(BF16)

    def out_cols(t):
        return t.reshape(ns, gs, cg, p).transpose(0, 3, 1, 2).reshape(ns, p, gs * cg).astype(BF16)

    a_tab = jnp.stack([ar.reshape(ns, gs * p), ai.reshape(ns, gs * p)], axis=1)
    return (in_rows(bbr), in_rows(bbi), out_cols(c_re.astype(F32)), out_cols(-c_im.astype(F32)), a_tab)


def _gelu_tanh(x):
    return 0.5 * x * (1.0 + jnp.tanh(math.sqrt(2.0 / math.pi) * (x + 0.044715 * (x * x * x))))


def _s5_kernel(x_ref, g_ref, sc_ref, sh_ref, wu_ref, perm_ref, bre_ref, bim_ref, cre_ref, cim_ref, a_ref,
               d_ref, gw_ref, gb_ref, o_ref, b_ref, c_ref, u_scr, bscr, sscr, yscr, st_ref):
    bsz, lb, d_in = x_ref.shape
    rows = bsz * lb
    dch = wu_ref.shape[1]
    half = st_ref.shape[2] // 2
    n_p = bre_ref.shape[2]

    @pl.when(pl.program_id(0) == 0)
    def _():
        st_ref[...] = jnp.zeros_like(st_ref)
        tile_in = (lax.broadcasted_iota(jnp.int32, (n_p, half), 0)
                   == lax.broadcasted_iota(jnp.int32, (n_p, half), 1) % n_p).astype(BF16)
        tile_out = (lax.broadcasted_iota(jnp.int32, (half, n_p), 0) % n_p
                    == lax.broadcasted_iota(jnp.int32, (half, n_p), 1)).astype(BF16)
        in_mask = (lax.broadcasted_iota(jnp.int32, (S5_SLAB, half), 0) // SSM_GROUP
                   == lax.broadcasted_iota(jnp.int32, (S5_SLAB, half), 1) // n_p)
        out_mask = (lax.broadcasted_iota(jnp.int32, (half, S5_SLAB), 0) // n_p
                    == lax.broadcasted_iota(jnp.int32, (half, S5_SLAB), 1) // SSM_GROUP)
        for s in range(dch // S5_SLAB):
            b_ref[s, :, :half] = jnp.where(in_mask, _dot(bre_ref[s], tile_in), 0.0).astype(BF16)
            b_ref[s, :, half:] = jnp.where(in_mask, _dot(bim_ref[s], tile_in), 0.0).astype(BF16)
            c_ref[s, :half, :] = jnp.where(out_mask, _dot(tile_out, cre_ref[s]), 0.0).astype(BF16)
            c_ref[s, half:, :] = jnp.where(out_mask, _dot(tile_out, cim_ref[s]), 0.0).astype(BF16)

    h = (_rms(x_ref[...], g_ref[...]) * (1.0 + sc_ref[...]) + sh_ref[...]).astype(BF16)
    u_nat = _dot(h.reshape(rows, d_in), wu_ref[...]).astype(BF16)
    u_scr[...] = _dot(perm_ref[...], u_nat).astype(BF16)
    n_slab = dch // S5_SLAB
    slab = lambda s: slice(s * S5_SLAB, (s + 1) * S5_SLAB)

    def project_in(s):
        bscr[s % 2] = _dot(u_scr[:, slab(s)], b_ref[s])

    project_in(0)
    for s in range(n_slab):
        if s + 1 < n_slab:
            project_in(s + 1)
        buf = s % 2
        a_re = a_ref[s, 0:1, :]
        a_im = a_ref[s, 1:2, :]
        s_r = st_ref[s, :, :half]
        s_i = st_ref[s, :, half:]
        for l in range(rows // bsz):
            at_l = slice(l * bsz, (l + 1) * bsz)
            s_r, s_i = (a_re * s_r - a_im * s_i + bscr[buf, at_l, :half],
                        a_re * s_i + a_im * s_r + bscr[buf, at_l, half:])
            sscr[buf, at_l, :half] = s_r
            sscr[buf, at_l, half:] = s_i
        st_ref[s, :, :half] = s_r
        st_ref[s, :, half:] = s_i
        y = _dot(sscr[buf].astype(BF16), c_ref[s]) + d_ref[:, slab(s)] * u_scr[:, slab(s)].astype(F32)
        yscr[:, slab(s)] = _gelu_tanh(y).astype(BF16)
    y = yscr[...]
    gate = _sigmoid(_dot(y, gw_ref[...]) + gb_ref[...])
    out_tm = (y.astype(F32) * gate).astype(BF16)
    out_nat = _dot_tn(perm_ref[...], out_tm).astype(o_ref.dtype)
    o_ref[...] = out_nat.reshape(bsz, lb, dch)


def _s5_glu(x2, g, sc, sh, w_u, tabs, d_skip, glu_w, glu_b, bsz, seqlen, lb):
    b_re, b_im, c_re, c_im, a_tab = tabs
    d_in, dch = w_u.shape
    ns, _, n_half = a_tab.shape
    n_state = 2 * n_half
    rows = lb * bsz
    r_idx = jnp.arange(rows)
    perm = ((r_idx % bsz) * lb + r_idx // bsz)[:, None] == r_idx[None, :]
    full = lambda a: pl.BlockSpec(a.shape, lambda i: (0,) * a.ndim, pipeline_mode=pl.Buffered(1))
    args = (x2.reshape(bsz, seqlen, d_in), g.reshape(1, 1, d_in), sc[:, None, :], sh[:, None, :], w_u,
            perm.astype(BF16), b_re, b_im, c_re, c_im, a_tab,
            d_skip.astype(F32).reshape(1, dch), glu_w.astype(BF16), glu_b.astype(F32).reshape(1, dch))
    y = pl.pallas_call(
        _s5_kernel,
        grid=(seqlen // lb,),
        in_specs=[pl.BlockSpec((bsz, lb, d_in), lambda i: (0, i, 0))] + [full(a) for a in args[1:]],
        out_specs=pl.BlockSpec((bsz, lb, dch), lambda i: (0, i, 0)),
        out_shape=jax.ShapeDtypeStruct((bsz, seqlen, dch), BF16),
        scratch_shapes=[pltpu.VMEM((ns, S5_SLAB, n_state), BF16), pltpu.VMEM((ns, n_state, S5_SLAB), BF16),
                        pltpu.VMEM((rows, dch), BF16), pltpu.VMEM((2, rows, n_state), F32),
                        pltpu.VMEM((2, rows, n_state), F32), pltpu.VMEM((rows, dch), BF16),
                        pltpu.VMEM((ns, bsz, n_state), F32)],
        compiler_params=_cparams("arbitrary"),
        name="s5_mixer_glu",
    )(*args)
    return y.reshape(bsz * seqlen, dch)


def _rwkv_kernel(z_ref, mu_ref, w0_ref, a0_ref, kkw_ref, ka_ref, wa_ref, gup_ref, rk_ref, lnw_ref, lnb_ref,
                 ones_ref, o_ref, s_ref, zlast_ref):
    nb, t, _ = z_ref.shape
    d_r = o_ref.shape[2]
    tile = HEADS_PER_TILE * RWKV_HEAD
    n_tiles = d_r // tile
    first_chunk = pl.program_id(1) == 0

    @pl.when(first_chunk)
    def _():
        s_ref[...] = jnp.zeros_like(s_ref)

    ones_bd = ones_ref[...]

    def seg_sum(x):
        return _dot(x.astype(BF16), ones_bd)

    row = lax.broadcasted_iota(jnp.int32, (t, t), 0)
    col = lax.broadcasted_iota(jnp.int32, (t, t), 1)
    tri = (row >= col).astype(BF16)
    st = HEADS_PER_TILE * t
    rs = lax.broadcasted_iota(jnp.int32, (2 * t, 2 * st), 0)
    t_r = rs % t
    t_c = lax.broadcasted_iota(jnp.int32, (2 * t, 2 * st), 1) % t
    keep = (t_r > t_c) | ((rs >= t) & (t_r == t_c))
    eye_w = (lax.broadcasted_iota(jnp.int32, (t, st), 0)
             == lax.broadcasted_iota(jnp.int32, (t, st), 1) % t).astype(F32)
    blk_mask = (lax.broadcasted_iota(jnp.int32, (st, st), 0) // t
                == lax.broadcasted_iota(jnp.int32, (st, st), 1) // t)
    lane = lax.broadcasted_iota(jnp.int32, (1, tile), 1)
    head_masks = [(lane >= j * RWKV_HEAD) & (lane < (j + 1) * RWKV_HEAD) for j in range(HEADS_PER_TILE)]
    bd_r = lax.broadcasted_iota(jnp.int32, (tile, tile), 0) // RWKV_HEAD
    bd_c = lax.broadcasted_iota(jnp.int32, (tile, tile), 1) // RWKV_HEAD
    bd_mask = bd_r == bd_c
    n_levels = int(math.log2(t))
    slices = [slice(hg * tile, (hg + 1) * tile) for hg in range(n_tiles)]
    units = [(bb, hg) for bb in range(nb) for hg in range(n_tiles)]
    n_u = range(len(units))

    def stack(x):
        zero = jnp.zeros_like(x)
        return jnp.concatenate([jnp.where(m, x, zero) for m in head_masks], axis=0)

    def bf(x):
        return x.astype(BF16)

    at, qt, bt, kt, vv, em, etm, wtot, rkb, gate = ([] for _ in range(10))
    row0 = lax.broadcasted_iota(jnp.int32, (t, 1), 0) == 0
    lora_lane = lax.broadcasted_iota(jnp.int32, (t, LORA_W + LORA_A), 1)
    for bb in range(nb):
        z = z_ref[bb].astype(F32)
        prev_row = jnp.where(first_chunk, 0.0, zlast_ref[bb, 0:1, :])
        zl = z + mu_ref[...] * (jnp.where(row0, prev_row, pltpu.roll(z, 1, 0)) - z)
        zlast_ref[bb, 0:1, :] = z[t - 1:t, :]
        xwa = zl[:, 3 * d_r:3 * d_r + LORA_W + LORA_A]
        lhs = jnp.where(lora_lane < LORA_W, jnp.tanh(xwa), xwa).astype(BF16)
        wa = _dot(lhs, wa_ref[...])
        lw = -DECAY_SCALE * _sigmoid(w0_ref[...] + wa[:, :d_r])
        asig = _sigmoid(a0_ref[...] + wa[:, d_r:])
        r = zl[:, :d_r]
        k_raw = zl[:, d_r:2 * d_r]
        kk = k_raw * kkw_ref[...]
        kp = k_raw * (1.0 + (asig - 1.0) * ka_ref[...])
        xg = zl[:, 3 * d_r + LORA_W + LORA_A:3 * d_r + LORA_W + LORA_A + LORA_G]
        g_full = _dot(_sigmoid(xg).astype(BF16), gup_ref[...])
        lw_hi, lw_lo = _split_bf16(lw)
        cum = _dot(tri, lw_hi) + _dot(tri, lw_lo)
        mid = cum[t // 2 - 1:t // 2, :]
        tot = cum[t - 1:t, :]
        e1 = jnp.exp(cum - mid)
        e2 = jnp.exp(mid - cum)
        e1p = e1 * jnp.exp(-lw)
        kk2 = kk * kk
        kkn = kk / jnp.maximum(jnp.sqrt(jnp.concatenate([seg_sum(kk2[:, sl]) for sl in slices], axis=1)), 1e-12)
        full = dict(at=-kkn * e1p, qt=r * e1, bt=kkn * asig * e2, kt=kp * e2,
                    vv=zl[:, 2 * d_r:3 * d_r], em=jnp.exp(mid), etm=jnp.exp(tot - mid), wtot=jnp.exp(tot),
                    rkb=r * kp * rk_ref[...], gate=g_full)
        for dst, key in ((at, "at"), (qt, "qt"), (bt, "bt"), (kt, "kt"), (vv, "vv"), (em, "em"),
                         (etm, "etm"), (wtot, "wtot"), (rkb, "rkb"), (gate, "gate")):
            dst.extend(full[key][:, sl] for sl in slices)

    s_old = [s_ref[i] for i in n_u]
    wide = []
    for i in n_u:
        lhs = bf(jnp.concatenate([at[i], qt[i]], axis=0))
        rhs = jnp.concatenate([stack(bf(bt[i])), stack(bf(kt[i]))], axis=0)
        a_w = bf(jnp.where(keep, _dot_nt(lhs, rhs), 0.0))
        wide.append([a_w[:t, :st], a_w[:t, st:], a_w[t:, :]])
    x_state = [_dot_nt(bf(jnp.concatenate([at[i], qt[i]], axis=0) * em[i]), bf(s_old[i]))
               for i in n_u]
    sv = [stack(bf(vv[i])) for i in n_u]
    akv = [_dot(wide[i][1], sv[i]) for i in n_u]

    def expand(x_w):
        x_b = bf(x_w)
        return jnp.where(blk_mask, jnp.concatenate([x_b] * HEADS_PER_TILE, axis=0), jnp.zeros((st, st), BF16))

    p_acc = [eye_w + w[0].astype(F32) for w in wide]
    q_bd = [expand(w[0]) for w in wide]
    q_pow = [_dot(wide[i][0], q_bd[i]) for i in n_u]
    for lev in range(1, n_levels):
        for i in n_u:
            q_bd[i] = expand(q_pow[i])
            if lev < n_levels - 1:
                both = _dot(bf(jnp.concatenate([p_acc[i], q_pow[i]], axis=0)), q_bd[i])
                p_acc[i] = p_acc[i] + both[:t]
                q_pow[i] = both[t:]
            else:
                p_acc[i] = p_acc[i] + _dot(bf(p_acc[i]), q_bd[i])
    u_all = [_dot(bf(p_acc[i]), stack(bf(x_state[i][:t] + akv[i]))) for i in n_u]
    y_all = [x_state[i][t:]
             + _dot(wide[i][2], jnp.concatenate([stack(bf(u_all[i])), sv[i]], axis=0)) for i in n_u]

    for i, (bb, hg) in enumerate(units):
        sl = slices[hg]
        y = y_all[i]
        uv = bf(jnp.concatenate([u_all[i], vv[i]], axis=0))
        bk_end = bf(jnp.concatenate([bt[i], kt[i]], axis=0) * etm[i])
        s_ref[i] = s_old[i] * wtot[i] + jnp.where(bd_mask, _dot_tn(uv, bk_end), 0.0)

        mean = seg_sum(y) * (1.0 / RWKV_HEAD)
        dlt = y - mean
        var = seg_sum(dlt * dlt) * (1.0 / RWKV_HEAD)
        yn = dlt * lax.rsqrt(var + GN_EPS) * lnw_ref[:, sl] + lnb_ref[:, sl]
        out = (yn + seg_sum(rkb[i]) * vv[i]) * gate[i]
        o_ref[bb, :, sl] = out.astype(o_ref.dtype)


def _rwkv(z, mu, w0, w_up, a0, a_up, g_up, k_k, k_a, r_k, ln_w, ln_b, bsz, seqlen):
    n, dz = z.shape
    d_r = w0.shape[0]
    t = RWKV_CHUNK
    nb = RWKV_BATCH_PER_STEP
    nch = seqlen // t
    tile = HEADS_PER_TILE * RWKV_HEAD
    hid = jnp.arange(tile) // RWKV_HEAD
    ones_bd = (hid[:, None] == hid[None, :]).astype(BF16)
    wa = jnp.zeros((LORA_W + LORA_A, 2 * d_r), F32)
    wa = wa.at[:LORA_W, :d_r].set(w_up.astype(F32)).at[LORA_W:, d_r:].set(a_up.astype(F32)).astype(BF16)
    row = lambda a: a.astype(F32).reshape(1, -1)
    full = lambda a: pl.BlockSpec(a.shape, lambda b, c: (0, 0))
    params = (row(mu), row(w0), row(a0), row(k_k), row(k_a), wa, g_up.astype(BF16), row(r_k), row(ln_w),
              row(ln_b), ones_bd)
    y = pl.pallas_call(
        _rwkv_kernel,
        grid=(bsz // nb, nch),
        in_specs=[pl.BlockSpec((nb, t, dz), lambda b, c: (b, c, 0))] + [full(a) for a in params],
        out_specs=pl.BlockSpec((nb, t, d_r), lambda b, c: (b, c, 0)),
        out_shape=jax.ShapeDtypeStruct((bsz, seqlen, d_r), BF16),
        scratch_shapes=[pltpu.VMEM((nb * (d_r // tile), tile, tile), F32), pltpu.VMEM((nb, 8, dz), F32)],
        compiler_params=_cparams("parallel", "arbitrary"),
        name="rwkv7_chunked",
    )(z.reshape(bsz, seqlen, dz), *params)
    return y.reshape(n, d_r)


def _post_mix_kernel(ys_ref, yr_ref, wo1_ref, wo2_ref, x_ref, g1_ref, gt_ref, g2_ref, sc_ref, sh_ref,
                     wr_both_ref, br_ref, x1_out, h2_out, lg_out):
    nl = lg_out.shape[1]
    mixed = _dot(ys_ref[...], wo1_ref[...]) + _dot(yr_ref[...], wo2_ref[...])
    x1 = x_ref[...] + gt_ref[0] * _rms(mixed, g1_ref[...])
    x1_out[...] = x1
    h2 = _rms(x1, g2_ref[...]) * (1.0 + sc_ref[0]) + sh_ref[0]
    _store_packed(h2_out, h2, h2.shape[0])
    hi, lo = _split_bf16(h2)
    both = _dot(hi, wr_both_ref[...])
    lg_out[...] = both[:, :nl] + both[:, nl:] + _dot(lo, wr_both_ref[:, :nl]) + br_ref[...]


def _post_mix(ys, yr, w_out, x2, g1, gt1, g2, sc2, sh2, w_route, b_route, seqlen, tm):
    n, d = x2.shape
    ds = ys.shape[1]
    tpb = seqlen // tm
    slabs = d // (2 * LANES)
    wr_both = jnp.concatenate(_split_bf16(w_route), axis=1)
    rows = lambda w: pl.BlockSpec((tm, w), lambda i: (i, 0))
    full = lambda a: pl.BlockSpec(a.shape, lambda i: (0,) * a.ndim)
    bat = pl.BlockSpec((1, 1, d), lambda i: (i // tpb, 0, 0))
    args = (ys, yr, w_out[:ds].astype(BF16), w_out[ds:].astype(BF16), x2, g1.reshape(1, d), gt1[:, None, :],
            g2.reshape(1, d), sc2[:, None, :], sh2[:, None, :], wr_both, b_route.reshape(1, -1))
    in_specs = [rows(ds), rows(yr.shape[1]), full(args[2]), full(args[3]), rows(d), full(args[5]), bat,
                full(args[7]), bat, bat, full(wr_both), full(args[11])]
    return pl.pallas_call(
        _post_mix_kernel,
        grid=(n // tm,),
        in_specs=in_specs,
        out_specs=[rows(d), pl.BlockSpec((tm * slabs, LANES), lambda i: (i, 0)), rows(ROUTE_LANES)],
        out_shape=[jax.ShapeDtypeStruct((n, d), F32), jax.ShapeDtypeStruct((n * slabs, LANES), jnp.uint32),
                   jax.ShapeDtypeStruct((n, ROUTE_LANES), F32)],
        compiler_params=_cparams("parallel"),
        name="out_proj_post",
    )(*args)


def _route_kernel(lg_ref, info_ref, cnt_ref, carry):
    i = pl.program_id(0)
    tm = lg_ref.shape[0]

    @pl.when(i == 0)
    def _():
        carry[...] = jnp.zeros_like(carry)

    lg = lg_ref[...]
    lane = lax.broadcasted_iota(jnp.int32, lg.shape, 1)
    lane_f = lane.astype(F32)
    neg = jnp.float32(-jnp.inf)
    big = jnp.float32(1e9)
    is_g = (lane >= N_EXPERTS) & (lane < N_EXPERTS + N_GROUPS)
    gl = jnp.where(is_g, lg, neg)
    gmax = jnp.max(gl, axis=-1, keepdims=True)
    gidx = jnp.min(jnp.where(gl == gmax, lane_f - N_EXPERTS, big), axis=-1, keepdims=True)
    p_grp = 1.0 / jnp.sum(jnp.where(is_g, jnp.exp(gl - gmax), 0.0), axis=-1, keepdims=True)
    in_grp = (lane < N_EXPERTS) & ((lane // EXPERTS_PER_GROUP).astype(F32) == gidx)
    el = jnp.where(in_grp, lg, neg)
    m1 = jnp.max(el, axis=-1, keepdims=True)
    i1 = jnp.min(jnp.where(el == m1, lane_f, big), axis=-1, keepdims=True)
    el2 = jnp.where(lane_f == i1, neg, el)
    m2 = jnp.max(el2, axis=-1, keepdims=True)
    i2 = jnp.min(jnp.where(el2 == m2, lane_f, big), axis=-1, keepdims=True)
    ex = jnp.exp(m2 - m1)
    w1 = p_grp / (1.0 + ex)
    w2 = p_grp * ex / (1.0 + ex)

    oh1 = lane_f == i1
    oh2 = lane_f == i2
    onehot = (oh1 | oh2).astype(BF16)
    rr = lax.broadcasted_iota(jnp.int32, (tm, tm), 0)
    cc = lax.broadcasted_iota(jnp.int32, (tm, tm), 1)
    before = _dot((rr > cc).astype(BF16), onehot) + carry[...]
    rank1 = jnp.sum(jnp.where(oh1, before, 0.0), axis=-1, keepdims=True)
    rank2 = jnp.sum(jnp.where(oh2, before, 0.0), axis=-1, keepdims=True)
    carry[...] = carry[...] + jnp.sum(onehot.astype(F32), axis=0, keepdims=True)
    cnt_ref[...] = carry[...]

    info = jnp.where(lane == 0, i1, 0.0)
    info = jnp.where(lane == 1, i2, info)
    info = jnp.where(lane == 2, w1, info)
    info = jnp.where(lane == 3, w2, info)
    info = jnp.where(lane == 4, rank1, info)
    info = jnp.where(lane == 5, rank2, info)
    info_ref[...] = info


def _route(logits, tm):
    n = logits.shape[0]
    return pl.pallas_call(
        _route_kernel,
        grid=(n // tm,),
        in_specs=[pl.BlockSpec((tm, ROUTE_LANES), lambda i: (i, 0))],
        out_specs=[pl.BlockSpec((tm, ROUTE_LANES), lambda i: (i, 0)),
                   pl.BlockSpec((1, ROUTE_LANES), lambda i: (0, 0))],
        out_shape=[jax.ShapeDtypeStruct((n, ROUTE_LANES), F32),
                   jax.ShapeDtypeStruct((1, ROUTE_LANES), F32)],
        scratch_shapes=[pltpu.VMEM((1, ROUTE_LANES), F32)],
        compiler_params=_cparams("arbitrary"),
        name="moe_route",
    )(logits)


def _slot_rows_kernel(info_ref, seg_ref, o_ref, *, slabs):
    info = info_ref[...]
    lane = lax.broadcasted_iota(jnp.int32, info.shape, 1)
    lane_f = lane.astype(F32)
    seg = seg_ref[...]
    d0 = jnp.sum(jnp.where(lane_f == info[:, 0:1], seg, 0.0), axis=-1, keepdims=True) + info[:, 4:5] * slabs
    d1 = jnp.sum(jnp.where(lane_f == info[:, 1:2], seg, 0.0), axis=-1, keepdims=True) + info[:, 5:6] * slabs
    o_ref[...] = jnp.where(lane == 0, d0, jnp.where(lane == 1, d1, 0.0)).astype(jnp.int32)


def _slot_rows(info, seg_row, slabs, tm):
    n = info.shape[0]
    return pl.pallas_call(
        functools.partial(_slot_rows_kernel, slabs=slabs),
        grid=(n // tm,),
        in_specs=[pl.BlockSpec((tm, ROUTE_LANES), lambda i: (i, 0)),
                  pl.BlockSpec((1, ROUTE_LANES), lambda i: (0, 0))],
        out_specs=pl.BlockSpec((tm, ROUTE_LANES), lambda i: (i, 0)),
        out_shape=jax.ShapeDtypeStruct((n, ROUTE_LANES), jnp.int32),
        compiler_params=_cparams("parallel"),
        name="moe_slot_rows",
    )(info, seg_row)


def _dispatch_kernel(d0_ref, d1_ref, pad_row_ref, pad_len_ref, na_ref, h_ref, buf_out, zeros, sem, zsem, *,
                     slabs, n_blocks):
    tm = h_ref.shape[0] // slabs

    @pl.when(pl.program_id(0) == 0)
    def _():
        zeros[...] = jnp.zeros_like(zeros)
        bits = [1 << b for b in reversed(range(EXPERT_BLOCK.bit_length() - 1))]

        def pad_copy(e, bit):
            done = pad_len_ref[e] & ~(2 * bit - 1)
            dst = pl.multiple_of(pad_row_ref[e] + done * slabs, slabs)
            return pltpu.make_async_copy(zeros.at[pl.ds(0, bit * slabs)], buf_out.at[pl.ds(dst, bit * slabs)], zsem)

        def block_copy(j):
            dst = pl.multiple_of(j * (EXPERT_BLOCK * slabs), EXPERT_BLOCK * slabs)
            return pltpu.make_async_copy(zeros, buf_out.at[pl.ds(dst, EXPERT_BLOCK * slabs)], zsem)

        def each(action):
            def per_expert(e, c):
                for bit in bits:
                    @pl.when((pad_len_ref[e] & bit) != 0)
                    def _():
                        action(pad_copy(e, bit))
                return c

            def per_block(j, c):
                action(block_copy(j))
                return c

            lax.fori_loop(0, N_EXPERTS, per_expert, 0)
            lax.fori_loop(na_ref[0], n_blocks, per_block, 0)

        each(lambda c: c.start())
        each(lambda c: c.wait())

    def copy(t, dest_ref):
        src = pl.multiple_of(t * slabs, slabs)
        dst = pl.multiple_of(dest_ref[t], slabs)
        return pltpu.make_async_copy(h_ref.at[pl.ds(src, slabs)], buf_out.at[pl.ds(dst, slabs)], sem)

    def start(t, c):
        copy(t, d0_ref).start(priority=0)
        copy(t, d1_ref).start(priority=1)
        return c

    lax.fori_loop(0, tm, start, 0, unroll=4)
    for _ in range(2):
        pltpu.make_async_copy(h_ref, buf_out.at[pl.ds(0, tm * slabs)], sem).wait()


def _dispatch(h2p, dest_rows, pad_row, pad_len, n_active, cap, slabs, tm):
    n = h2p.shape[0] // slabs
    smem = pl.BlockSpec((tm,), lambda i: (i,), memory_space=pltpu.SMEM)
    table = lambda a: pl.BlockSpec(a.shape, lambda i: (0,), memory_space=pltpu.SMEM)
    return pl.pallas_call(
        functools.partial(_dispatch_kernel, slabs=slabs, n_blocks=cap // EXPERT_BLOCK),
        grid=(n // tm,),
        in_specs=[smem, smem, table(pad_row), table(pad_len), table(n_active),
                  pl.BlockSpec((tm * slabs, LANES), lambda i: (i, 0))],
        out_specs=pl.BlockSpec(memory_space=pl.ANY),
        out_shape=jax.ShapeDtypeStruct((cap * slabs, LANES), h2p.dtype),
        scratch_shapes=[pltpu.VMEM((EXPERT_BLOCK * slabs, LANES), h2p.dtype), pltpu.SemaphoreType.DMA,
                        pltpu.SemaphoreType.DMA],
        compiler_params=_cparams("arbitrary"),
        name="moe_dispatch",
    )(dest_rows[0], dest_rows[1], pad_row, pad_len, n_active, h2p)


def _moe_kernel(na_ref, eseq_ref, epos_ref, nd_ref, x_ref, w1_hbm, w3_hbm, w2_hbm, o_ref,
                w1f, w3f, w2f, w1b, w3b, w2b, sem):
    j = pl.program_id(0)
    active = j < na_ref[0]
    pos = epos_ref[j]
    fresh = (j == 0) | (pos != epos_ref[jnp.maximum(j - 1, 0)])

    slabs = w1b.shape[0]
    half = slabs * LANES

    def weight_copies(p):
        e = eseq_ref[p]
        slot = p % 2
        return [pltpu.make_async_copy(w_hbm.at[e], stage.at[slot], sem.at[slot, i])
                for i, (w_hbm, stage) in enumerate(((w1_hbm, w1f), (w3_hbm, w3f), (w2_hbm, w2f)))]

    @pl.when(j == 0)
    def _():
        for c in weight_copies(0):
            c.start()

    @pl.when(active & fresh)
    def _():
        for c in weight_copies(pos):
            c.wait()

        @pl.when(pos + 1 < nd_ref[0])
        def _():
            for c in weight_copies(pos + 1):
                c.start()

        slot = pos % 2
        for s in range(slabs):
            for dst, src in ((w1b, w1f), (w3b, w3f)):
                dst[s, :LANES, :] = src[slot, s * LANES:(s + 1) * LANES, :].astype(BF16)
                dst[s, LANES:, :] = src[slot, half + s * LANES:half + (s + 1) * LANES, :].astype(BF16)
        w2b[...] = w2f[slot].astype(BF16)

    @pl.when(active)
    def _():
        acc1 = jnp.zeros((EXPERT_BLOCK, w1b.shape[2]), F32)
        acc3 = jnp.zeros((EXPERT_BLOCK, w1b.shape[2]), F32)
        for s, (lo, hi) in enumerate(_load_packed(x_ref, EXPERT_BLOCK, slabs)):
            lhs = jnp.concatenate([lo.astype(BF16), hi.astype(BF16)], axis=1)
            acc1 = acc1 + _dot(lhs, w1b[s])
            acc3 = acc3 + _dot(lhs, w3b[s])
        hid = (acc1 * _sigmoid(acc1)) * acc3
        _store_packed(o_ref, _dot(hid.astype(BF16), w2b[...]), EXPERT_BLOCK)


def _moe(x_buf, n_active, expert_seq, block_pos, n_used, w1, w3, w2, slabs):
    cap = x_buf.shape[0] // slabs
    d, de = w1.shape[1], w1.shape[2]
    nb = cap // EXPERT_BLOCK

    def xmap(j, na, *_):
        return (jnp.minimum(j, na[0] - 1), 0)

    xspec = pl.BlockSpec((EXPERT_BLOCK * slabs, LANES), xmap)
    hbm = pl.BlockSpec(memory_space=pl.ANY)
    grid_spec = pltpu.PrefetchScalarGridSpec(
        num_scalar_prefetch=4,
        grid=(nb,),
        in_specs=[xspec, hbm, hbm, hbm],
        out_specs=xspec,
        scratch_shapes=[pltpu.VMEM((2, d, de), F32), pltpu.VMEM((2, d, de), F32), pltpu.VMEM((2, de, d), F32),
                        pltpu.VMEM((slabs, 2 * LANES, de), BF16), pltpu.VMEM((slabs, 2 * LANES, de), BF16),
                        pltpu.VMEM((de, d), BF16), pltpu.SemaphoreType.DMA((2, 3))],
    )
    return pl.pallas_call(
        _moe_kernel,
        grid_spec=grid_spec,
        out_shape=jax.ShapeDtypeStruct(x_buf.shape, x_buf.dtype),
        input_output_aliases={4: 0},
        compiler_params=_cparams("arbitrary"),
        name="moe_experts",
    )(n_active, expert_seq, block_pos, n_used, x_buf, w1, w3, w2)


def _combine_kernel(d0_ref, d1_ref, d0n_ref, d1n_ref, y_ref, info_ref, x1_ref, g_ref, gt_ref, o_ref,
                    rows, sem, *, slabs, tiles_per_idx, chunk):
    i = pl.program_id(0)
    n_tiles = pl.num_programs(0)
    tm = x1_ref.shape[0]
    slot = i % 2
    nxt_slot = 1 - slot
    nxt = jnp.minimum(i + 1, n_tiles - 1)

    def copy(tile, to_slot, t, dest_ref, k):
        off = (tile % tiles_per_idx) * tm
        src = pl.multiple_of(dest_ref[off + t], slabs)
        dst = pl.multiple_of(t * slabs, slabs)
        return pltpu.make_async_copy(y_ref.at[pl.ds(src, slabs)], rows.at[to_slot, k, pl.ds(dst, slabs)],
                                     sem.at[to_slot])

    def drain(which):
        for k in range(2):
            pltpu.make_async_copy(y_ref.at[pl.ds(0, tm * slabs)], rows.at[which, k], sem.at[which]).wait()

    @pl.when(i == 0)
    def _():
        def start(t, c):
            copy(i, slot, t, d0_ref, 0).start(priority=0)
            copy(i, slot, t, d1_ref, 1).start(priority=1)
            return c

        lax.fori_loop(0, tm, start, 0, unroll=4)

    drain(slot)
    g_row = g_ref[...]
    gt_row = gt_ref[0]
    for c in range(tm // chunk):
        tok = slice(c * chunk, (c + 1) * chunk)
        info = info_ref[tok, :]
        w1 = info[:, 2:3]
        w2 = info[:, 3:4]
        lo_parts, hi_parts = [], []
        for s in range(slabs):
            at_s = pl.ds(c * chunk * slabs + s, chunk, stride=slabs)
            lo0, hi0 = _unpack_pair(rows[slot, 0, at_s, :])
            lo1, hi1 = _unpack_pair(rows[slot, 1, at_s, :])
            lo_parts.append(w1 * lo0 + w2 * lo1)
            hi_parts.append(w1 * hi0 + w2 * hi1)
        ffn = jnp.concatenate(lo_parts + hi_parts, axis=1)
        o_ref[tok, :] = x1_ref[tok, :] + gt_row * _rms(ffn, g_row)
        for t in range(c * chunk, (c + 1) * chunk):
            copy(nxt, nxt_slot, t, d0n_ref, 0).start(priority=0)
            copy(nxt, nxt_slot, t, d1n_ref, 1).start(priority=1)

    @pl.when(i == n_tiles - 1)
    def _():
        drain(nxt_slot)


def _combine(y_buf, dest_rows, info, x1, g, gt2, seqlen, slabs, tm):
    n, d = x1.shape
    tpb = seqlen // tm
    idx_block = max(tm, SMEM_INDEX_BLOCK)
    per = idx_block // tm
    last = n // tm - 1
    smem = pl.BlockSpec((idx_block,), lambda i: (i // per,), memory_space=pltpu.SMEM)
    smem_next = pl.BlockSpec((idx_block,), lambda i: (jnp.minimum(i + 1, last) // per,), memory_space=pltpu.SMEM)
    return pl.pallas_call(
        functools.partial(_combine_kernel, slabs=slabs, tiles_per_idx=per, chunk=min(tm, COMBINE_CHUNK)),
        grid=(n // tm,),
        in_specs=[smem, smem, smem_next, smem_next,
                  pl.BlockSpec(memory_space=pl.ANY),
                  pl.BlockSpec((tm, ROUTE_LANES), lambda i: (i, 0)),
                  pl.BlockSpec((tm, d), lambda i: (i, 0)),
                  pl.BlockSpec((1, d), lambda i: (0, 0)),
                  pl.BlockSpec((1, 1, d), lambda i: (i // tpb, 0, 0))],
        out_specs=pl.BlockSpec((tm, d), lambda i: (i, 0)),
        out_shape=jax.ShapeDtypeStruct((n, d), F32),
        scratch_shapes=[pltpu.VMEM((2, 2, tm * slabs, LANES), y_buf.dtype), pltpu.SemaphoreType.DMA((2,))],
        compiler_params=_cparams("arbitrary"),
        name="moe_combine",
    )(dest_rows[0], dest_rows[1], dest_rows[0], dest_rows[1], y_buf, info, x1, g.reshape(1, d),
      gt2[:, None, :])


def _pick(n, pref):
    while n % pref:
        pref //= 2
    return pref


def _layer(x2, mod, p, bsz, seqlen):
    n, d = x2.shape
    sh1, sc1, gt1, sh2, sc2, gt2 = jnp.split(mod, 6, axis=-1)
    d_ssm = p["ssm_d"].shape[0]

    z = _norm_proj(x2, p["norm_mix_pre"], sc1, sh1, p["w_in"][:, d_ssm:].astype(BF16), seqlen,
                   _pick(seqlen, 512), "in_proj_z")

    tabs = _s5_tables(p["ssm_lam_re"], p["ssm_lam_im"], p["ssm_log_dt"], p["ssm_b_re"], p["ssm_b_im"],
                      p["ssm_c_re"], p["ssm_c_im"])
    y_ssm = _s5_glu(x2, p["norm_mix_pre"], sc1, sh1, p["w_in"][:, :d_ssm].astype(BF16), tabs, p["ssm_d"], p["glu_w"],
                    p["glu_b"], bsz, seqlen, _pick(seqlen, S5_TIME_BLOCK))

    y_rwkv = _rwkv(z, p["rwkv_mu"], p["rwkv_w0"], p["rwkv_w_up"], p["rwkv_a0"], p["rwkv_a_up"],
                   p["rwkv_g_up"], p["rwkv_k_k"], p["rwkv_k_a"], p["rwkv_r_k"], p["rwkv_ln_w"],
                   p["rwkv_ln_b"], bsz, seqlen)

    w_route = jnp.zeros((d, ROUTE_LANES), F32)
    w_route = w_route.at[:, :N_EXPERTS].set(p["moe_w_exp"].astype(F32))
    w_route = w_route.at[:, N_EXPERTS:N_EXPERTS + N_GROUPS].set(p["moe_w_grp"].astype(F32))
    b_route = jnp.zeros((ROUTE_LANES,), F32)
    b_route = b_route.at[:N_EXPERTS].set(p["moe_b_exp"].astype(F32))
    b_route = b_route.at[N_EXPERTS:N_EXPERTS + N_GROUPS].set(p["moe_b_grp"].astype(F32))
    x1, h2p, logits = _post_mix(y_ssm, y_rwkv, p["w_out"], x2, p["norm_mix_post"], gt1, p["norm_ffn_pre"],
                                sc2, sh2, w_route, b_route, seqlen, _pick(seqlen, 512))
    slabs = d // (2 * LANES)

    info, counts = _route(logits, _pick(n, 512))
    cnt = counts[0, :N_EXPERTS].astype(jnp.int32)
    padded = (cnt + EXPERT_BLOCK - 1) // EXPERT_BLOCK * EXPERT_BLOCK
    pend = jnp.cumsum(padded)
    pstart = pend - padded
    n_blocks = -(-(2 * n) // EXPERT_BLOCK) + N_EXPERTS
    cap = n_blocks * EXPERT_BLOCK
    seg_row = jnp.zeros((1, ROUTE_LANES), F32).at[0, :N_EXPERTS].set((pstart * slabs).astype(F32))
    dest = _slot_rows(info, seg_row, slabs, _pick(n, 2048))
    dest_rows = (dest[:, 0], dest[:, 1])
    n_active = (pend[-1:] // EXPERT_BLOCK).astype(jnp.int32)
    used = (cnt > 0).astype(jnp.int32)
    used_pos = jnp.cumsum(used) - 1
    slot_ids = jnp.arange(N_EXPERTS, dtype=jnp.int32)
    expert_seq = jnp.sum(jnp.where((used_pos[None, :] == slot_ids[:, None]) & (used[None, :] > 0),
                                   slot_ids[None, :], 0), axis=1).astype(jnp.int32)
    block_first = jnp.arange(n_blocks, dtype=jnp.int32) * EXPERT_BLOCK
    block_pos = jnp.sum(jnp.where(pend[None, :] <= block_first[:, None], used[None, :], 0),
                        axis=1).astype(jnp.int32)
    n_used = jnp.sum(used).reshape(1)

    x_buf = _dispatch(h2p, dest_rows, ((pstart + cnt) * slabs).astype(jnp.int32),
                      (padded - cnt).astype(jnp.int32), n_active, cap, slabs, _pick(n, 1024))
    y_buf = _moe(x_buf, n_active, expert_seq, block_pos, n_used, p["moe_w1"], p["moe_w3"], p["moe_w2"], slabs)
    return _combine(y_buf, dest_rows, info, x1, p["norm_ffn_post"], gt2, seqlen, slabs, _pick(seqlen, 512))


def kernel(x, c, ada_w, ada_b, norm_mix_pre, norm_mix_post, norm_ffn_pre, norm_ffn_post, w_in, w_out, ssm_lam_re, ssm_lam_im, ssm_log_dt, ssm_b_re, ssm_b_im, ssm_c_re, ssm_c_im, ssm_d, glu_w, glu_b, rwkv_mu, rwkv_w0, rwkv_w_up, rwkv_a0, rwkv_a_up, rwkv_g_up, rwkv_k_k, rwkv_k_a, rwkv_r_k, rwkv_ln_w, rwkv_ln_b, moe_w_grp, moe_b_grp, moe_w_exp, moe_b_exp, moe_w1, moe_w3, moe_w2):
    bsz, seqlen, d = x.shape
    params = dict(norm_mix_pre=norm_mix_pre, norm_mix_post=norm_mix_post, norm_ffn_pre=norm_ffn_pre,
                  norm_ffn_post=norm_ffn_post, w_in=w_in, w_out=w_out, ssm_lam_re=ssm_lam_re,
                  ssm_lam_im=ssm_lam_im, ssm_log_dt=ssm_log_dt, ssm_b_re=ssm_b_re, ssm_b_im=ssm_b_im,
                  ssm_c_re=ssm_c_re, ssm_c_im=ssm_c_im, ssm_d=ssm_d, glu_w=glu_w, glu_b=glu_b,
                  rwkv_mu=rwkv_mu, rwkv_w0=rwkv_w0, rwkv_w_up=rwkv_w_up, rwkv_a0=rwkv_a0,
                  rwkv_a_up=rwkv_a_up, rwkv_g_up=rwkv_g_up, rwkv_k_k=rwkv_k_k, rwkv_k_a=rwkv_k_a,
                  rwkv_r_k=rwkv_r_k, rwkv_ln_w=rwkv_ln_w, rwkv_ln_b=rwkv_ln_b, moe_w_grp=moe_w_grp,
                  moe_b_grp=moe_b_grp, moe_w_exp=moe_w_exp, moe_b_exp=moe_b_exp, moe_w1=moe_w1,
                  moe_w3=moe_w3, moe_w2=moe_w2)
    x2 = x.reshape(bsz * seqlen, d)
    for layer in range(ada_w.shape[0]):
        mod = _ada(c, ada_w[layer], ada_b[layer])
        x2 = _layer(x2, mod, {k: v[layer] for k, v in params.items()}, bsz, seqlen)
    return x2.reshape(bsz, seqlen, d)
```

```python
import functools
import math

import jax
import jax.numpy as jnp
from jax import lax
from jax.experimental import pallas as pl
from jax.experimental.pallas import tpu as pltpu

F32 = jnp.float32
BF16 = jnp.bfloat16

SSM_GROUP = 16
S5_SLAB = 256
S5_TIME_BLOCK = 64
RWKV_HEAD = 64
RWKV_CHUNK = 64
RWKV_BATCH_PER_STEP = 4
HEADS_PER_TILE = 4
LORA_W = 64
LORA_A = 64
LORA_G = 128
N_GROUPS = 8
EXPERTS_PER_GROUP = 8
N_EXPERTS = N_GROUPS * EXPERTS_PER_GROUP
EXPERT_BLOCK = 256
RMS_EPS = 1e-6
GN_EPS = 64e-5
DECAY_SCALE = math.exp(-0.5)
LANES = 128
COMBINE_CHUNK = 64
SMEM_INDEX_BLOCK = 1024
ROUTE_LANES = LANES
V7X_VMEM_BYTES = 64 * 1024 * 1024
VMEM_LIMIT = V7X_VMEM_BYTES - 12 * 1024 * 1024


def _cparams(*sem):
    return pltpu.CompilerParams(dimension_semantics=sem, vmem_limit_bytes=VMEM_LIMIT)


def _sigmoid(x):
    return 1.0 / (1.0 + jnp.exp(-x))


def _dot(a, b):
    return jnp.dot(a, b, preferred_element_type=F32)


def _dot_nt(a, b):
    return lax.dot_general(a, b, (((1,), (1,)), ((), ())), preferred_element_type=F32)


def _dot_tn(a, b):
    return lax.dot_general(a, b, (((0,), (0,)), ((), ())), preferred_element_type=F32)


def _split_bf16(x):
    hi = x.astype(BF16)
    lo = (x - hi.astype(F32)).astype(BF16)
    return hi, lo


def _pack_pair(a, b):
    ua = lax.bitcast_convert_type(a.astype(BF16).astype(F32), jnp.uint32)
    ub = lax.bitcast_convert_type(b.astype(BF16).astype(F32), jnp.uint32)
    return ub | (ua >> 16)


def _unpack_pair(w):
    lo = lax.bitcast_convert_type(w << 16, F32)
    hi = lax.bitcast_convert_type(w & jnp.uint32(0xFFFF0000), F32)
    return lo, hi


def _store_packed(ref, val, n_rows):
    d = val.shape[1]
    slabs = d // (2 * LANES)
    for s in range(slabs):
        a = val[:, s * LANES:(s + 1) * LANES]
        b = val[:, d // 2 + s * LANES:d // 2 + (s + 1) * LANES]
        ref[pl.ds(s, n_rows, stride=slabs), :] = _pack_pair(a, b)


def _load_packed(ref, n_rows, slabs):
    return [_unpack_pair(ref[pl.ds(s, n_rows, stride=slabs), :]) for s in range(slabs)]


def _side_cast(src_ref, dst_ref, regroup):
    for e in range(src_ref.shape[0]):
        if regroup:
            slabs = dst_ref.shape[1]
            half = slabs * LANES
            for s in range(slabs):
                dst_ref[e, s, :LANES, :] = src_ref[e, s * LANES:(s + 1) * LANES, :].astype(BF16)
                dst_ref[e, s, LANES:, :] = src_ref[e, half + s * LANES:half + (s + 1) * LANES, :].astype(BF16)
        else:
            dst_ref[e] = src_ref[e].astype(BF16)


def _side_io(w, n_steps, step_of, regroup):
    n_e, rows, cols = w.shape
    per = n_e // n_steps
    assert per * n_steps == n_e, (n_e, n_steps)
    in_spec = pl.BlockSpec((per, rows, cols), lambda *i: (step_of(*i), 0, 0))
    if regroup:
        shape = (n_e, rows // (2 * LANES), 2 * LANES, cols)
        out_spec = pl.BlockSpec((per,) + shape[1:], lambda *i: (step_of(*i), 0, 0, 0))
    else:
        shape = w.shape
        out_spec = pl.BlockSpec((per, rows, cols), lambda *i: (step_of(*i), 0, 0))
    return in_spec, out_spec, jax.ShapeDtypeStruct(shape, BF16)


def _ada_kernel(c_ref, w_ref, b_ref, o_ref):
    c = c_ref[...]
    bsz = c.shape[0]
    c_hi, c_lo = _split_bf16(c * _sigmoid(c))
    w_hi, w_lo = _split_bf16(w_ref[...])
    first = _dot(jnp.concatenate([c_hi, c_lo], axis=0), w_hi)
    o_ref[...] = first[:bsz] + first[bsz:] + _dot(c_hi, w_lo) + b_ref[...]


def _ada(c, ada_w, ada_b):
    bsz, d = c.shape
    n = ada_w.shape[1]
    tn = 1024
    return pl.pallas_call(
        _ada_kernel,
        grid=(n // tn,),
        in_specs=[pl.BlockSpec((bsz, d), lambda j: (0, 0)),
                  pl.BlockSpec((d, tn), lambda j: (0, j)),
                  pl.BlockSpec((1, tn), lambda j: (0, j))],
        out_specs=pl.BlockSpec((bsz, tn), lambda j: (0, j)),
        out_shape=jax.ShapeDtypeStruct((bsz, n), F32),
        compiler_params=_cparams("arbitrary"),
        name="ada_mod",
    )(c, ada_w, ada_b.reshape(1, n))


def _rms(x, g):
    return x * lax.rsqrt(jnp.mean(x * x, axis=-1, keepdims=True) + RMS_EPS) * g


def _norm_proj_kernel(x_ref, g_ref, sc_ref, sh_ref, w_ref, side_ref, o_ref, side_out):
    h = _rms(x_ref[...], g_ref[...]) * (1.0 + sc_ref[0]) + sh_ref[0]
    o_ref[...] = _dot(h.astype(BF16), w_ref[...]).astype(o_ref.dtype)
    _side_cast(side_ref, side_out, regroup=False)


def _norm_proj(x2, g, sc, sh, w, side_w, seqlen, tm, name):
    n, d = x2.shape
    nout = w.shape[1]
    tpb = seqlen // tm
    side_in, side_out, side_shape = _side_io(side_w, n // tm, lambda i: i, regroup=False)
    return pl.pallas_call(
        _norm_proj_kernel,
        grid=(n // tm,),
        in_specs=[pl.BlockSpec((tm, d), lambda i: (i, 0)),
                  pl.BlockSpec((1, d), lambda i: (0, 0)),
                  pl.BlockSpec((1, 1, d), lambda i: (i // tpb, 0, 0)),
                  pl.BlockSpec((1, 1, d), lambda i: (i // tpb, 0, 0)),
                  pl.BlockSpec((d, nout), lambda i: (0, 0), pipeline_mode=pl.Buffered(1)),
                  side_in],
        out_specs=[pl.BlockSpec((tm, nout), lambda i: (i, 0)), side_out],
        out_shape=[jax.ShapeDtypeStruct((n, nout), BF16), side_shape],
        compiler_params=_cparams("parallel"),
        name=name,
    )(x2, g.reshape(1, d), sc[:, None, :], sh[:, None, :], w, side_w)


def _s5_tables(lam_re, lam_im, log_dt, b_re, b_im, c_re, c_im):
    g, p, cg = b_re.shape
    gs = S5_SLAB // cg
    ns = g // gs
    lr = jnp.minimum(lam_re.astype(F32), -1e-4)
    li = lam_im.astype(F32)
    dt = jnp.exp(log_dt.astype(F32))[:, None]
    mag = jnp.exp(lr * dt)
    ar, ai = mag * jnp.cos(li * dt), mag * jnp.sin(li * dt)
    den = lr * lr + li * li
    qr = ((ar - 1.0) * lr + ai * li) / den
    qi = (ai * lr - (ar - 1.0) * li) / den
    br, bi = b_re.astype(F32), b_im.astype(F32)
    bbr = qr[..., None] * br - qi[..., None] * bi
    bbi = qr[..., None] * bi + qi[..., None] * br

    def in_rows(t):
        return t.reshape(ns, gs, p, cg).transpose(0, 1, 3, 2).reshape(ns, gs * cg, p).astype(BF16)

    def out_cols(t):
        return t.reshape(ns, gs, cg, p).transpose(0, 3, 1, 2).reshape(ns, p, gs * cg).astype(BF16)

    a_tab = jnp.stack([ar.reshape(ns, gs * p), ai.reshape(ns, gs * p)], axis=1)
    return (in_rows(bbr), in_rows(bbi), out_cols(c_re.astype(F32)), out_cols(-c_im.astype(F32)), a_tab)


def _gelu_tanh(x):
    return 0.5 * x * (1.0 + jnp.tanh(math.sqrt(2.0 / math.pi) * (x + 0.044715 * (x * x * x))))


def _s5_kernel(x_ref, g_ref, sc_ref, sh_ref, wu_ref, perm_ref, bre_ref, bim_ref, cre_ref, cim_ref, a_ref,
               d_ref, gw_ref, gb_ref, o_ref, b_ref, c_ref, u_scr, bscr, sscr, yscr, st_ref):
    bsz, lb, d_in = x_ref.shape
    rows = bsz * lb
    dch = wu_ref.shape[1]
    half = st_ref.shape[2] // 2
    n_p = bre_ref.shape[2]

    @pl.when(pl.program_id(0) == 0)
    def _():
        st_ref[...] = jnp.zeros_like(st_ref)
        tile_in = (lax.broadcasted_iota(jnp.int32, (n_p, half), 0)
                   == lax.broadcasted_iota(jnp.int32, (n_p, half), 1) % n_p).astype(BF16)
        tile_out = (lax.broadcasted_iota(jnp.int32, (half, n_p), 0) % n_p
                    == lax.broadcasted_iota(jnp.int32, (half, n_p), 1)).astype(BF16)
        in_mask = (lax.broadcasted_iota(jnp.int32, (S5_SLAB, half), 0) // SSM_GROUP
                   == lax.broadcasted_iota(jnp.int32, (S5_SLAB, half), 1) // n_p)
        out_mask = (lax.broadcasted_iota(jnp.int32, (half, S5_SLAB), 0) // n_p
                    == lax.broadcasted_iota(jnp.int32, (half, S5_SLAB), 1) // SSM_GROUP)
        for s in range(dch // S5_SLAB):
            b_ref[s, :, :half] = jnp.where(in_mask, _dot(bre_ref[s], tile_in), 0.0).astype(BF16)
            b_ref[s, :, half:] = jnp.where(in_mask, _dot(bim_ref[s], tile_in), 0.0).astype(BF16)
            c_ref[s, :half, :] = jnp.where(out_mask, _dot(tile_out, cre_ref[s]), 0.0).astype(BF16)
            c_ref[s, half:, :] = jnp.where(out_mask, _dot(tile_out, cim_ref[s]), 0.0).astype(BF16)

    h = (_rms(x_ref[...], g_ref[...]) * (1.0 + sc_ref[...]) + sh_ref[...]).astype(BF16)
    u_nat = _dot(h.reshape(rows, d_in), wu_ref[...]).astype(BF16)
    u_scr[...] = _dot(perm_ref[...], u_nat).astype(BF16)
    n_slab = dch // S5_SLAB
    slab = lambda s: slice(s * S5_SLAB, (s + 1) * S5_SLAB)

    def project_in(s):
        bscr[s % 2] = _dot(u_scr[:, slab(s)], b_ref[s])

    project_in(0)
    for s in range(n_slab):
        if s + 1 < n_slab:
            project_in(s + 1)
        buf = s % 2
        a_re = a_ref[s, 0:1, :]
        a_im = a_ref[s, 1:2, :]
        s_r = st_ref[s, :, :half]
        s_i = st_ref[s, :, half:]
        for l in range(rows // bsz):
            at_l = slice(l * bsz, (l + 1) * bsz)
            s_r, s_i = (a_re * s_r - a_im * s_i + bscr[buf, at_l, :half],
                        a_re * s_i + a_im * s_r + bscr[buf, at_l, half:])
            sscr[buf, at_l, :half] = s_r
            sscr[buf, at_l, half:] = s_i
        st_ref[s, :, :half] = s_r
        st_ref[s, :, half:] = s_i
        y = _dot(sscr[buf].astype(BF16), c_ref[s]) + d_ref[:, slab(s)] * u_scr[:, slab(s)].astype(F32)
        yscr[:, slab(s)] = _gelu_tanh(y).astype(BF16)
    y = yscr[...]
    gate = _sigmoid(_dot(y, gw_ref[...]) + gb_ref[...])
    out_tm = (y.astype(F32) * gate).astype(BF16)
    out_nat = _dot_tn(perm_ref[...], out_tm).astype(o_ref.dtype)
    o_ref[...] = out_nat.reshape(bsz, lb, dch)


def _s5_glu(x2, g, sc, sh, w_u, tabs, d_skip, glu_w, glu_b, bsz, seqlen, lb):
    b_re, b_im, c_re, c_im, a_tab = tabs
    d_in, dch = w_u.shape
    ns, _, n_half = a_tab.shape
    n_state = 2 * n_half
    rows = lb * bsz
    r_idx = jnp.arange(rows)
    perm = ((r_idx % bsz) * lb + r_idx // bsz)[:, None] == r_idx[None, :]
    full = lambda a: pl.BlockSpec(a.shape, lambda i: (0,) * a.ndim, pipeline_mode=pl.Buffered(1))
    args = (x2.reshape(bsz, seqlen, d_in), g.reshape(1, 1, d_in), sc[:, None, :], sh[:, None, :], w_u,
            perm.astype(BF16), b_re, b_im, c_re, c_im, a_tab,
            d_skip.astype(F32).reshape(1, dch), glu_w.astype(BF16), glu_b.astype(F32).reshape(1, dch))
    y = pl.pallas_call(
        _s5_kernel,
        grid=(seqlen // lb,),
        in_specs=[pl.BlockSpec((bsz, lb, d_in), lambda i: (0, i, 0))] + [full(a) for a in args[1:]],
        out_specs=pl.BlockSpec((bsz, lb, dch), lambda i: (0, i, 0)),
        out_shape=jax.ShapeDtypeStruct((bsz, seqlen, dch), BF16),
        scratch_shapes=[pltpu.VMEM((ns, S5_SLAB, n_state), BF16), pltpu.VMEM((ns, n_state, S5_SLAB), BF16),
                        pltpu.VMEM((rows, dch), BF16), pltpu.VMEM((2, rows, n_state), F32),
                        pltpu.VMEM((2, rows, n_state), F32), pltpu.VMEM((rows, dch), BF16),
                        pltpu.VMEM((ns, bsz, n_state), F32)],
        compiler_params=_cparams("arbitrary"),
        name="s5_mixer_glu",
    )(*args)
    return y.reshape(bsz * seqlen, dch)


def _rwkv_kernel(z_ref, mu_ref, w0_ref, a0_ref, kkw_ref, ka_ref, wa_ref, gup_ref, rk_ref, lnw_ref, lnb_ref,
                 ones_ref, side_ref, o_ref, side_out, s_ref, zlast_ref):
    _side_cast(side_ref, side_out, regroup=True)
    nb, t, _ = z_ref.shape
    d_r = o_ref.shape[2]
    tile = HEADS_PER_TILE * RWKV_HEAD
    n_tiles = d_r // tile
    first_chunk = pl.program_id(1) == 0

    @pl.when(first_chunk)
    def _():
        s_ref[...] = jnp.zeros_like(s_ref)

    ones_bd = ones_ref[...]

    def seg_sum(x):
        return _dot(x.astype(BF16), ones_bd)

    row = lax.broadcasted_iota(jnp.int32, (t, t), 0)
    col = lax.broadcasted_iota(jnp.int32, (t, t), 1)
    tri = (row >= col).astype(BF16)
    st = HEADS_PER_TILE * t
    rs = lax.broadcasted_iota(jnp.int32, (2 * t, 2 * st), 0)
    t_r = rs % t
    t_c = lax.broadcasted_iota(jnp.int32, (2 * t, 2 * st), 1) % t
    keep = (t_r > t_c) | ((rs >= t) & (t_r == t_c))
    eye_w = (lax.broadcasted_iota(jnp.int32, (t, st), 0)
             == lax.broadcasted_iota(jnp.int32, (t, st), 1) % t).astype(F32)
    blk_mask = (lax.broadcasted_iota(jnp.int32, (st, st), 0) // t
                == lax.broadcasted_iota(jnp.int32, (st, st), 1) // t)
    lane = lax.broadcasted_iota(jnp.int32, (1, tile), 1)
    head_masks = [(lane >= j * RWKV_HEAD) & (lane < (j + 1) * RWKV_HEAD) for j in range(HEADS_PER_TILE)]
    bd_r = lax.broadcasted_iota(jnp.int32, (tile, tile), 0) // RWKV_HEAD
    bd_c = lax.broadcasted_iota(jnp.int32, (tile, tile), 1) // RWKV_HEAD
    bd_mask = bd_r == bd_c
    n_levels = int(math.log2(t))
    slices = [slice(hg * tile, (hg + 1) * tile) for hg in range(n_tiles)]
    units = [(bb, hg) for bb in range(nb) for hg in range(n_tiles)]
    n_u = range(len(units))

    def stack(x):
        zero = jnp.zeros_like(x)
        return jnp.concatenate([jnp.where(m, x, zero) for m in head_masks], axis=0)

    def bf(x):
        return x.astype(BF16)

    at, qt, bt, kt, vv, em, etm, wtot, rkb, gate = ([] for _ in range(10))
    row0 = lax.broadcasted_iota(jnp.int32, (t, 1), 0) == 0
    lora_lane = lax.broadcasted_iota(jnp.int32, (t, LORA_W + LORA_A), 1)
    for bb in range(nb):
        z = z_ref[bb].astype(F32)
        prev_row = jnp.where(first_chunk, 0.0, zlast_ref[bb, 0:1, :])
        zl = z + mu_ref[...] * (jnp.where(row0, prev_row, pltpu.roll(z, 1, 0)) - z)
        zlast_ref[bb, 0:1, :] = z[t - 1:t, :]
        xwa = zl[:, 3 * d_r:3 * d_r + LORA_W + LORA_A]
        lhs = jnp.where(lora_lane < LORA_W, jnp.tanh(xwa), xwa).astype(BF16)
        wa = _dot(lhs, wa_ref[...])
        lw = -DECAY_SCALE * _sigmoid(w0_ref[...] + wa[:, :d_r])
        asig = _sigmoid(a0_ref[...] + wa[:, d_r:])
        r = zl[:, :d_r]
        k_raw = zl[:, d_r:2 * d_r]
        kk = k_raw * kkw_ref[...]
        kp = k_raw * (1.0 + (asig - 1.0) * ka_ref[...])
        xg = zl[:, 3 * d_r + LORA_W + LORA_A:3 * d_r + LORA_W + LORA_A + LORA_G]
        g_full = _dot(_sigmoid(xg).astype(BF16), gup_ref[...])
        lw_hi, lw_lo = _split_bf16(lw)
        cum = _dot(tri, lw_hi) + _dot(tri, lw_lo)
        mid = cum[t // 2 - 1:t // 2, :]
        tot = cum[t - 1:t, :]
        e1 = jnp.exp(cum - mid)
        e2 = jnp.exp(mid - cum)
        e1p = e1 * jnp.exp(-lw)
        kk2 = kk * kk
        kkn = kk / jnp.maximum(jnp.sqrt(jnp.concatenate([seg_sum(kk2[:, sl]) for sl in slices], axis=1)), 1e-12)
        full = dict(at=-kkn * e1p, qt=r * e1, bt=kkn * asig * e2, kt=kp * e2,
                    vv=zl[:, 2 * d_r:3 * d_r], em=jnp.exp(mid), etm=jnp.exp(tot - mid), wtot=jnp.exp(tot),
                    rkb=r * kp * rk_ref[...], gate=g_full)
        for dst, key in ((at, "at"), (qt, "qt"), (bt, "bt"), (kt, "kt"), (vv, "vv"), (em, "em"),
                         (etm, "etm"), (wtot, "wtot"), (rkb, "rkb"), (gate, "gate")):
            dst.extend(full[key][:, sl] for sl in slices)

    s_old = [s_ref[i] for i in n_u]
    wide = []
    for i in n_u:
        lhs = bf(jnp.concatenate([at[i], qt[i]], axis=0))
        rhs = jnp.concatenate([stack(bf(bt[i])), stack(bf(kt[i]))], axis=0)
        a_w = bf(jnp.where(keep, _dot_nt(lhs, rhs), 0.0))
        wide.append([a_w[:t, :st], a_w[:t, st:], a_w[t:, :]])
    x_state = [_dot_nt(bf(jnp.concatenate([at[i], qt[i]], axis=0) * em[i]), bf(s_old[i]))
               for i in n_u]
    sv = [stack(bf(vv[i])) for i in n_u]
    akv = [_dot(wide[i][1], sv[i]) for i in n_u]

    def expand(x_w):
        x_b = bf(x_w)
        return jnp.where(blk_mask, jnp.concatenate([x_b] * HEADS_PER_TILE, axis=0), jnp.zeros((st, st), BF16))

    p_acc = [eye_w + w[0].astype(F32) for w in wide]
    q_bd = [expand(w[0]) for w in wide]
    q_pow = [_dot(wide[i][0], q_bd[i]) for i in n_u]
    for lev in range(1, n_levels):
        for i in n_u:
            q_bd[i] = expand(q_pow[i])
            if lev < n_levels - 1:
                both = _dot(bf(jnp.concatenate([p_acc[i], q_pow[i]], axis=0)), q_bd[i])
                p_acc[i] = p_acc[i] + both[:t]
                q_pow[i] = both[t:]
            else:
                p_acc[i] = p_acc[i] + _dot(bf(p_acc[i]), q_bd[i])
    u_all = [_dot(bf(p_acc[i]), stack(bf(x_state[i][:t] + akv[i]))) for i in n_u]
    y_all = [x_state[i][t:]
             + _dot(wide[i][2], jnp.concatenate([stack(bf(u_all[i])), sv[i]], axis=0)) for i in n_u]

    for i, (bb, hg) in enumerate(units):
        sl = slices[hg]
        y = y_all[i]
        uv = bf(jnp.concatenate([u_all[i], vv[i]], axis=0))
        bk_end = bf(jnp.concatenate([bt[i], kt[i]], axis=0) * etm[i])
        s_ref[i] = s_old[i] * wtot[i] + jnp.where(bd_mask, _dot_tn(uv, bk_end), 0.0)

        mean = seg_sum(y) * (1.0 / RWKV_HEAD)
        dlt = y - mean
        var = seg_sum(dlt * dlt) * (1.0 / RWKV_HEAD)
        yn = dlt * lax.rsqrt(var + GN_EPS) * lnw_ref[:, sl] + lnb_ref[:, sl]
        out = (yn + seg_sum(rkb[i]) * vv[i]) * gate[i]
        o_ref[bb, :, sl] = out.astype(o_ref.dtype)


def _rwkv(z, mu, w0, w_up, a0, a_up, g_up, k_k, k_a, r_k, ln_w, ln_b, side_w, bsz, seqlen):
    n, dz = z.shape
    d_r = w0.shape[0]
    t = RWKV_CHUNK
    nb = RWKV_BATCH_PER_STEP
    nch = seqlen // t
    tile = HEADS_PER_TILE * RWKV_HEAD
    hid = jnp.arange(tile) // RWKV_HEAD
    ones_bd = (hid[:, None] == hid[None, :]).astype(BF16)
    wa = jnp.zeros((LORA_W + LORA_A, 2 * d_r), F32)
    wa = wa.at[:LORA_W, :d_r].set(w_up.astype(F32)).at[LORA_W:, d_r:].set(a_up.astype(F32)).astype(BF16)
    row = lambda a: a.astype(F32).reshape(1, -1)
    full = lambda a: pl.BlockSpec(a.shape, lambda b, c: (0, 0))
    params = (row(mu), row(w0), row(a0), row(k_k), row(k_a), wa, g_up.astype(BF16), row(r_k), row(ln_w),
              row(ln_b), ones_bd)
    side_in, side_out, side_shape = _side_io(side_w, (bsz // nb) * nch, lambda b, c: b * nch + c, regroup=True)
    y, side = pl.pallas_call(
        _rwkv_kernel,
        grid=(bsz // nb, nch),
        in_specs=[pl.BlockSpec((nb, t, dz), lambda b, c: (b, c, 0))] + [full(a) for a in params] + [side_in],
        out_specs=[pl.BlockSpec((nb, t, d_r), lambda b, c: (b, c, 0)), side_out],
        out_shape=[jax.ShapeDtypeStruct((bsz, seqlen, d_r), BF16), side_shape],
        scratch_shapes=[pltpu.VMEM((nb * (d_r // tile), tile, tile), F32), pltpu.VMEM((nb, 8, dz), F32)],
        compiler_params=_cparams("parallel", "arbitrary"),
        name="rwkv7_chunked",
    )(z.reshape(bsz, seqlen, dz), *params, side_w)
    return y.reshape(n, d_r), side


def _post_mix_kernel(ys_ref, yr_ref, wo1_ref, wo2_ref, x_ref, g1_ref, gt_ref, g2_ref, sc_ref, sh_ref,
                     wr_both_ref, br_ref, side_ref, x1_out, h2_out, lg_out, side_out):
    _side_cast(side_ref, side_out, regroup=True)
    nl = lg_out.shape[1]
    mixed = _dot(ys_ref[...], wo1_ref[...]) + _dot(yr_ref[...], wo2_ref[...])
    x1 = x_ref[...] + gt_ref[0] * _rms(mixed, g1_ref[...])
    x1_out[...] = x1
    h2 = _rms(x1, g2_ref[...]) * (1.0 + sc_ref[0]) + sh_ref[0]
    _store_packed(h2_out, h2, h2.shape[0])
    hi, lo = _split_bf16(h2)
    both = _dot(hi, wr_both_ref[...])
    lg_out[...] = both[:, :nl] + both[:, nl:] + _dot(lo, wr_both_ref[:, :nl]) + br_ref[...]


def _post_mix(ys, yr, w_out, x2, g1, gt1, g2, sc2, sh2, w_route, b_route, side_w, seqlen, tm):
    n, d = x2.shape
    side_in, side_out, side_shape = _side_io(side_w, n // tm, lambda i: i, regroup=True)
    ds = ys.shape[1]
    tpb = seqlen // tm
    slabs = d // (2 * LANES)
    wr_both = jnp.concatenate(_split_bf16(w_route), axis=1)
    rows = lambda w: pl.BlockSpec((tm, w), lambda i: (i, 0))
    full = lambda a: pl.BlockSpec(a.shape, lambda i: (0,) * a.ndim)
    bat = pl.BlockSpec((1, 1, d), lambda i: (i // tpb, 0, 0))
    args = (ys, yr, w_out[:ds].astype(BF16), w_out[ds:].astype(BF16), x2, g1.reshape(1, d), gt1[:, None, :],
            g2.reshape(1, d), sc2[:, None, :], sh2[:, None, :], wr_both, b_route.reshape(1, -1), side_w)
    in_specs = [rows(ds), rows(yr.shape[1]), full(args[2]), full(args[3]), rows(d), full(args[5]), bat,
                full(args[7]), bat, bat, full(wr_both), full(args[11]), side_in]
    return pl.pallas_call(
        _post_mix_kernel,
        grid=(n // tm,),
        in_specs=in_specs,
        out_specs=[rows(d), pl.BlockSpec((tm * slabs, LANES), lambda i: (i, 0)), rows(ROUTE_LANES), side_out],
        out_shape=[jax.ShapeDtypeStruct((n, d), F32), jax.ShapeDtypeStruct((n * slabs, LANES), jnp.uint32),
                   jax.ShapeDtypeStruct((n, ROUTE_LANES), F32), side_shape],
        compiler_params=_cparams("parallel"),
        name="out_proj_post",
    )(*args)


def _route_kernel(lg_ref, info_ref, cnt_ref, carry):
    i = pl.program_id(0)
    tm = lg_ref.shape[0]

    @pl.when(i == 0)
    def _():
        carry[...] = jnp.zeros_like(carry)

    lg = lg_ref[...]
    lane = lax.broadcasted_iota(jnp.int32, lg.shape, 1)
    lane_f = lane.astype(F32)
    neg = jnp.float32(-jnp.inf)
    big = jnp.float32(1e9)
    is_g = (lane >= N_EXPERTS) & (lane < N_EXPERTS + N_GROUPS)
    gl = jnp.where(is_g, lg, neg)
    gmax = jnp.max(gl, axis=-1, keepdims=True)
    gidx = jnp.min(jnp.where(gl == gmax, lane_f - N_EXPERTS, big), axis=-1, keepdims=True)
    p_grp = 1.0 / jnp.sum(jnp.where(is_g, jnp.exp(gl - gmax), 0.0), axis=-1, keepdims=True)
    in_grp = (lane < N_EXPERTS) & ((lane // EXPERTS_PER_GROUP).astype(F32) == gidx)
    el = jnp.where(in_grp, lg, neg)
    m1 = jnp.max(el, axis=-1, keepdims=True)
    i1 = jnp.min(jnp.where(el == m1, lane_f, big), axis=-1, keepdims=True)
    el2 = jnp.where(lane_f == i1, neg, el)
    m2 = jnp.max(el2, axis=-1, keepdims=True)
    i2 = jnp.min(jnp.where(el2 == m2, lane_f, big), axis=-1, keepdims=True)
    ex = jnp.exp(m2 - m1)
    w1 = p_grp / (1.0 + ex)
    w2 = p_grp * ex / (1.0 + ex)

    oh1 = lane_f == i1
    oh2 = lane_f == i2
    onehot = (oh1 | oh2).astype(BF16)
    rr = lax.broadcasted_iota(jnp.int32, (tm, tm), 0)
    cc = lax.broadcasted_iota(jnp.int32, (tm, tm), 1)
    before = _dot((rr > cc).astype(BF16), onehot) + carry[...]
    rank1 = jnp.sum(jnp.where(oh1, before, 0.0), axis=-1, keepdims=True)
    rank2 = jnp.sum(jnp.where(oh2, before, 0.0), axis=-1, keepdims=True)
    carry[...] = carry[...] + jnp.sum(onehot.astype(F32), axis=0, keepdims=True)
    cnt_ref[...] = carry[...]

    info = jnp.where(lane == 0, i1, 0.0)
    info = jnp.where(lane == 1, i2, info)
    info = jnp.where(lane == 2, w1, info)
    info = jnp.where(lane == 3, w2, info)
    info = jnp.where(lane == 4, rank1, info)
    info = jnp.where(lane == 5, rank2, info)
    info_ref[...] = info


def _route(logits, tm):
    n = logits.shape[0]
    return pl.pallas_call(
        _route_kernel,
        grid=(n // tm,),
        in_specs=[pl.BlockSpec((tm, ROUTE_LANES), lambda i: (i, 0))],
        out_specs=[pl.BlockSpec((tm, ROUTE_LANES), lambda i: (i, 0)),
                   pl.BlockSpec((1, ROUTE_LANES), lambda i: (0, 0))],
        out_shape=[jax.ShapeDtypeStruct((n, ROUTE_LANES), F32),
                   jax.ShapeDtypeStruct((1, ROUTE_LANES), F32)],
        scratch_shapes=[pltpu.VMEM((1, ROUTE_LANES), F32)],
        compiler_params=_cparams("arbitrary"),
        name="moe_route",
    )(logits)


def _slot_rows_kernel(info_ref, seg_ref, o_ref, *, slabs):
    info = info_ref[...]
    lane = lax.broadcasted_iota(jnp.int32, info.shape, 1)
    lane_f = lane.astype(F32)
    seg = seg_ref[...]
    d0 = jnp.sum(jnp.where(lane_f == info[:, 0:1], seg, 0.0), axis=-1, keepdims=True) + info[:, 4:5] * slabs
    d1 = jnp.sum(jnp.where(lane_f == info[:, 1:2], seg, 0.0), axis=-1, keepdims=True) + info[:, 5:6] * slabs
    o_ref[...] = jnp.where(lane == 0, d0, jnp.where(lane == 1, d1, 0.0)).astype(jnp.int32)


def _slot_rows(info, seg_row, slabs, tm):
    n = info.shape[0]
    return pl.pallas_call(
        functools.partial(_slot_rows_kernel, slabs=slabs),
        grid=(n // tm,),
        in_specs=[pl.BlockSpec((tm, ROUTE_LANES), lambda i: (i, 0)),
                  pl.BlockSpec((1, ROUTE_LANES), lambda i: (0, 0))],
        out_specs=pl.BlockSpec((tm, ROUTE_LANES), lambda i: (i, 0)),
        out_shape=jax.ShapeDtypeStruct((n, ROUTE_LANES), jnp.int32),
        compiler_params=_cparams("parallel"),
        name="moe_slot_rows",
    )(info, seg_row)


def _dispatch_kernel(d0_ref, d1_ref, pad_row_ref, pad_len_ref, na_ref, h_ref, buf_out, zeros, sem, zsem, *,
                     slabs, n_blocks):
    tm = h_ref.shape[0] // slabs

    @pl.when(pl.program_id(0) == 0)
    def _():
        zeros[...] = jnp.zeros_like(zeros)
        bits = [1 << b for b in reversed(range(EXPERT_BLOCK.bit_length() - 1))]

        def pad_copy(e, bit):
            done = pad_len_ref[e] & ~(2 * bit - 1)
            dst = pl.multiple_of(pad_row_ref[e] + done * slabs, slabs)
            return pltpu.make_async_copy(zeros.at[pl.ds(0, bit * slabs)], buf_out.at[pl.ds(dst, bit * slabs)], zsem)

        def block_copy(j):
            dst = pl.multiple_of(j * (EXPERT_BLOCK * slabs), EXPERT_BLOCK * slabs)
            return pltpu.make_async_copy(zeros, buf_out.at[pl.ds(dst, EXPERT_BLOCK * slabs)], zsem)

        def each(action):
            def per_expert(e, c):
                for bit in bits:
                    @pl.when((pad_len_ref[e] & bit) != 0)
                    def _():
                        action(pad_copy(e, bit))
                return c

            def per_block(j, c):
                action(block_copy(j))
                return c

            lax.fori_loop(0, N_EXPERTS, per_expert, 0)
            lax.fori_loop(na_ref[0], n_blocks, per_block, 0)

        each(lambda c: c.start())
        each(lambda c: c.wait())

    def copy(t, dest_ref):
        src = pl.multiple_of(t * slabs, slabs)
        dst = pl.multiple_of(dest_ref[t], slabs)
        return pltpu.make_async_copy(h_ref.at[pl.ds(src, slabs)], buf_out.at[pl.ds(dst, slabs)], sem)

    def start(t, c):
        copy(t, d0_ref).start(priority=0)
        copy(t, d1_ref).start(priority=1)
        return c

    lax.fori_loop(0, tm, start, 0, unroll=4)
    for _ in range(2):
        pltpu.make_async_copy(h_ref, buf_out.at[pl.ds(0, tm * slabs)], sem).wait()


def _dispatch(h2p, dest_rows, pad_row, pad_len, n_active, cap, slabs, tm):
    n = h2p.shape[0] // slabs
    smem = pl.BlockSpec((tm,), lambda i: (i,), memory_space=pltpu.SMEM)
    table = lambda a: pl.BlockSpec(a.shape, lambda i: (0,), memory_space=pltpu.SMEM)
    return pl.pallas_call(
        functools.partial(_dispatch_kernel, slabs=slabs, n_blocks=cap // EXPERT_BLOCK),
        grid=(n // tm,),
        in_specs=[smem, smem, table(pad_row), table(pad_len), table(n_active),
                  pl.BlockSpec((tm * slabs, LANES), lambda i: (i, 0))],
        out_specs=pl.BlockSpec(memory_space=pl.ANY),
        out_shape=jax.ShapeDtypeStruct((cap * slabs, LANES), h2p.dtype),
        scratch_shapes=[pltpu.VMEM((EXPERT_BLOCK * slabs, LANES), h2p.dtype), pltpu.SemaphoreType.DMA,
                        pltpu.SemaphoreType.DMA],
        compiler_params=_cparams("arbitrary"),
        name="moe_dispatch",
    )(dest_rows[0], dest_rows[1], pad_row, pad_len, n_active, h2p)


def _moe_kernel(na_ref, eseq_ref, epos_ref, nd_ref, x_ref, w1_hbm, w3_hbm, w2_hbm, o_ref,
                w1b, w3b, w2b, sem):
    j = pl.program_id(0)
    active = j < na_ref[0]
    pos = epos_ref[j]
    fresh = (j == 0) | (pos != epos_ref[jnp.maximum(j - 1, 0)])
    slabs = w1b.shape[1]

    def weight_copies(p):
        e = eseq_ref[p]
        slot = p % 2
        return [pltpu.make_async_copy(w_hbm.at[e], stage.at[slot], sem.at[slot, i])
                for i, (w_hbm, stage) in enumerate(((w1_hbm, w1b), (w3_hbm, w3b), (w2_hbm, w2b)))]

    @pl.when(j == 0)
    def _():
        for c in weight_copies(0):
            c.start()

    @pl.when(active & fresh)
    def _():
        for c in weight_copies(pos):
            c.wait()

        @pl.when(pos + 1 < nd_ref[0])
        def _():
            for c in weight_copies(pos + 1):
                c.start()

    @pl.when(active)
    def _():
        slot = pos % 2
        acc1 = jnp.zeros((EXPERT_BLOCK, w1b.shape[3]), F32)
        acc3 = jnp.zeros((EXPERT_BLOCK, w1b.shape[3]), F32)
        for s, (lo, hi) in enumerate(_load_packed(x_ref, EXPERT_BLOCK, slabs)):
            lhs = jnp.concatenate([lo.astype(BF16), hi.astype(BF16)], axis=1)
            acc1 = acc1 + _dot(lhs, w1b[slot, s])
            acc3 = acc3 + _dot(lhs, w3b[slot, s])
        hid = (acc1 * _sigmoid(acc1)) * acc3
        _store_packed(o_ref, _dot(hid.astype(BF16), w2b[slot]), EXPERT_BLOCK)


def _moe(x_buf, n_active, expert_seq, block_pos, n_used, w1, w3, w2, slabs):
    cap = x_buf.shape[0] // slabs
    nb = cap // EXPERT_BLOCK

    def xmap(j, na, *_):
        return (jnp.minimum(j, na[0] - 1), 0)

    xspec = pl.BlockSpec((EXPERT_BLOCK * slabs, LANES), xmap)
    hbm = pl.BlockSpec(memory_space=pl.ANY)
    grid_spec = pltpu.PrefetchScalarGridSpec(
        num_scalar_prefetch=4,
        grid=(nb,),
        in_specs=[xspec, hbm, hbm, hbm],
        out_specs=xspec,
        scratch_shapes=[pltpu.VMEM((2,) + w1.shape[1:], BF16), pltpu.VMEM((2,) + w3.shape[1:], BF16),
                        pltpu.VMEM((2,) + w2.shape[1:], BF16), pltpu.SemaphoreType.DMA((2, 3))],
    )
    return pl.pallas_call(
        _moe_kernel,
        grid_spec=grid_spec,
        out_shape=jax.ShapeDtypeStruct(x_buf.shape, x_buf.dtype),
        input_output_aliases={4: 0},
        compiler_params=_cparams("arbitrary"),
        name="moe_experts",
    )(n_active, expert_seq, block_pos, n_used, x_buf, w1, w3, w2)


def _combine_kernel(d0_ref, d1_ref, d0n_ref, d1n_ref, y_ref, info_ref, x1_ref, g_ref, gt_ref, o_ref,
                    rows, sem, *, slabs, tiles_per_idx, chunk):
    i = pl.program_id(0)
    n_tiles = pl.num_programs(0)
    tm = x1_ref.shape[0]
    slot = i % 2
    nxt_slot = 1 - slot
    nxt = jnp.minimum(i + 1, n_tiles - 1)

    def copy(tile, to_slot, t, dest_ref, k):
        off = (tile % tiles_per_idx) * tm
        src = pl.multiple_of(dest_ref[off + t], slabs)
        dst = pl.multiple_of(t * slabs, slabs)
        return pltpu.make_async_copy(y_ref.at[pl.ds(src, slabs)], rows.at[to_slot, k, pl.ds(dst, slabs)],
                                     sem.at[to_slot])

    def drain(which):
        for k in range(2):
            pltpu.make_async_copy(y_ref.at[pl.ds(0, tm * slabs)], rows.at[which, k], sem.at[which]).wait()

    @pl.when(i == 0)
    def _():
        def start(t, c):
            copy(i, slot, t, d0_ref, 0).start(priority=0)
            copy(i, slot, t, d1_ref, 1).start(priority=1)
            return c

        lax.fori_loop(0, tm, start, 0, unroll=4)

    drain(slot)
    g_row = g_ref[...]
    gt_row = gt_ref[0]
    for c in range(tm // chunk):
        tok = slice(c * chunk, (c + 1) * chunk)
        info = info_ref[tok, :]
        w1 = info[:, 2:3]
        w2 = info[:, 3:4]
        lo_parts, hi_parts = [], []
        for s in range(slabs):
            at_s = pl.ds(c * chunk * slabs + s, chunk, stride=slabs)
            lo0, hi0 = _unpack_pair(rows[slot, 0, at_s, :])
            lo1, hi1 = _unpack_pair(rows[slot, 1, at_s, :])
            lo_parts.append(w1 * lo0 + w2 * lo1)
            hi_parts.append(w1 * hi0 + w2 * hi1)
        ffn = jnp.concatenate(lo_parts + hi_parts, axis=1)
        o_ref[tok, :] = x1_ref[tok, :] + gt_row * _rms(ffn, g_row)
        for t in range(c * chunk, (c + 1) * chunk):
            copy(nxt, nxt_slot, t, d0n_ref, 0).start(priority=0)
            copy(nxt, nxt_slot, t, d1n_ref, 1).start(priority=1)

    @pl.when(i == n_tiles - 1)
    def _():
        drain(nxt_slot)


def _combine(y_buf, dest_rows, info, x1, g, gt2, seqlen, slabs, tm):
    n, d = x1.shape
    tpb = seqlen // tm
    idx_block = max(tm, SMEM_INDEX_BLOCK)
    per = idx_block // tm
    last = n // tm - 1
    smem = pl.BlockSpec((idx_block,), lambda i: (i // per,), memory_space=pltpu.SMEM)
    smem_next = pl.BlockSpec((idx_block,), lambda i: (jnp.minimum(i + 1, last) // per,), memory_space=pltpu.SMEM)
    return pl.pallas_call(
        functools.partial(_combine_kernel, slabs=slabs, tiles_per_idx=per, chunk=min(tm, COMBINE_CHUNK)),
        grid=(n // tm,),
        in_specs=[smem, smem, smem_next, smem_next,
                  pl.BlockSpec(memory_space=pl.ANY),
                  pl.BlockSpec((tm, ROUTE_LANES), lambda i: (i, 0)),
                  pl.BlockSpec((tm, d), lambda i: (i, 0)),
                  pl.BlockSpec((1, d), lambda i: (0, 0)),
                  pl.BlockSpec((1, 1, d), lambda i: (i // tpb, 0, 0))],
        out_specs=pl.BlockSpec((tm, d), lambda i: (i, 0)),
        out_shape=jax.ShapeDtypeStruct((n, d), F32),
        scratch_shapes=[pltpu.VMEM((2, 2, tm * slabs, LANES), y_buf.dtype), pltpu.SemaphoreType.DMA((2,))],
        compiler_params=_cparams("arbitrary"),
        name="moe_combine",
    )(dest_rows[0], dest_rows[1], dest_rows[0], dest_rows[1], y_buf, info, x1, g.reshape(1, d),
      gt2[:, None, :])


def _pick(n, pref):
    while n % pref:
        pref //= 2
    return pref


def _layer(x2, mod, p, bsz, seqlen):
    n, d = x2.shape
    sh1, sc1, gt1, sh2, sc2, gt2 = jnp.split(mod, 6, axis=-1)
    d_ssm = p["ssm_d"].shape[0]

    tm_side = _pick(seqlen, max(n // N_EXPERTS, 16))
    z, w2_bf = _norm_proj(x2, p["norm_mix_pre"], sc1, sh1, p["w_in"][:, d_ssm:].astype(BF16), p["moe_w2"],
                          seqlen, tm_side, "in_proj_z")

    tabs = _s5_tables(p["ssm_lam_re"], p["ssm_lam_im"], p["ssm_log_dt"], p["ssm_b_re"], p["ssm_b_im"],
                      p["ssm_c_re"], p["ssm_c_im"])
    y_ssm = _s5_glu(x2, p["norm_mix_pre"], sc1, sh1, p["w_in"][:, :d_ssm].astype(BF16), tabs, p["ssm_d"], p["glu_w"],
                    p["glu_b"], bsz, seqlen, _pick(seqlen, S5_TIME_BLOCK))

    y_rwkv, w3_bf = _rwkv(z, p["rwkv_mu"], p["rwkv_w0"], p["rwkv_w_up"], p["rwkv_a0"], p["rwkv_a_up"],
                          p["rwkv_g_up"], p["rwkv_k_k"], p["rwkv_k_a"], p["rwkv_r_k"], p["rwkv_ln_w"],
                          p["rwkv_ln_b"], p["moe_w3"], bsz, seqlen)

    w_route = jnp.zeros((d, ROUTE_LANES), F32)
    w_route = w_route.at[:, :N_EXPERTS].set(p["moe_w_exp"].astype(F32))
    w_route = w_route.at[:, N_EXPERTS:N_EXPERTS + N_GROUPS].set(p["moe_w_grp"].astype(F32))
    b_route = jnp.zeros((ROUTE_LANES,), F32)
    b_route = b_route.at[:N_EXPERTS].set(p["moe_b_exp"].astype(F32))
    b_route = b_route.at[N_EXPERTS:N_EXPERTS + N_GROUPS].set(p["moe_b_grp"].astype(F32))
    x1, h2p, logits, w1_bf = _post_mix(y_ssm, y_rwkv, p["w_out"], x2, p["norm_mix_post"], gt1,
                                       p["norm_ffn_pre"], sc2, sh2, w_route, b_route, p["moe_w1"], seqlen, tm_side)
    slabs = d // (2 * LANES)

    info, counts = _route(logits, _pick(n, 512))
    cnt = counts[0, :N_EXPERTS].astype(jnp.int32)
    padded = (cnt + EXPERT_BLOCK - 1) // EXPERT_BLOCK * EXPERT_BLOCK
    pend = jnp.cumsum(padded)
    pstart = pend - padded
    n_blocks = -(-(2 * n) // EXPERT_BLOCK) + N_EXPERTS
    cap = n_blocks * EXPERT_BLOCK
    seg_row = jnp.zeros((1, ROUTE_LANES), F32).at[0, :N_EXPERTS].set((pstart * slabs).astype(F32))
    dest = _slot_rows(info, seg_row, slabs, _pick(n, 2048))
    dest_rows = (dest[:, 0], dest[:, 1])
    n_active = (pend[-1:] // EXPERT_BLOCK).astype(jnp.int32)
    used = (cnt > 0).astype(jnp.int32)
    used_pos = jnp.cumsum(used) - 1
    slot_ids = jnp.arange(N_EXPERTS, dtype=jnp.int32)
    expert_seq = jnp.sum(jnp.where((used_pos[None, :] == slot_ids[:, None]) & (used[None, :] > 0),
                                   slot_ids[None, :], 0), axis=1).astype(jnp.int32)
    block_first = jnp.arange(n_blocks, dtype=jnp.int32) * EXPERT_BLOCK
    block_pos = jnp.sum(jnp.where(pend[None, :] <= block_first[:, None], used[None, :], 0),
                        axis=1).astype(jnp.int32)
    n_used = jnp.sum(used).reshape(1)

    x_buf = _dispatch(h2p, dest_rows, ((pstart + cnt) * slabs).astype(jnp.int32),
                      (padded - cnt).astype(jnp.int32), n_active, cap, slabs, _pick(n, 1024))
    y_buf = _moe(x_buf, n_active, expert_seq, block_pos, n_used, w1_bf, w3_bf, w2_bf, slabs)
    return _combine(y_buf, dest_rows, info, x1, p["norm_ffn_post"], gt2, seqlen, slabs, _pick(seqlen, 512))


def kernel(x, c, ada_w, ada_b, norm_mix_pre, norm_mix_post, norm_ffn_pre, norm_ffn_post, w_in, w_out, ssm_lam_re, ssm_lam_im, ssm_log_dt, ssm_b_re, ssm_b_im, ssm_c_re, ssm_c_im, ssm_d, glu_w, glu_b, rwkv_mu, rwkv_w0, rwkv_w_up, rwkv_a0, rwkv_a_up, rwkv_g_up, rwkv_k_k, rwkv_k_a, rwkv_r_k, rwkv_ln_w, rwkv_ln_b, moe_w_grp, moe_b_grp, moe_w_exp, moe_b_exp, moe_w1, moe_w3, moe_w2):
    bsz, seqlen, d = x.shape
    params = dict(norm_mix_pre=norm_mix_pre, norm_mix_post=norm_mix_post, norm_ffn_pre=norm_ffn_pre,
                  norm_ffn_post=norm_ffn_post, w_in=w_in, w_out=w_out, ssm_lam_re=ssm_lam_re,
                  ssm_lam_im=ssm_lam_im, ssm_log_dt=ssm_log_dt, ssm_b_re=ssm_b_re, ssm_b_im=ssm_b_im,
                  ssm_c_re=ssm_c_re, ssm_c_im=ssm_c_im, ssm_d=ssm_d, glu_w=glu_w, glu_b=glu_b,
                  rwkv_mu=rwkv_mu, rwkv_w0=rwkv_w0, rwkv_w_up=rwkv_w_up, rwkv_a0=rwkv_a0,
                  rwkv_a_up=rwkv_a_up, rwkv_g_up=rwkv_g_up, rwkv_k_k=rwkv_k_k, rwkv_k_a=rwkv_k_a,
                  rwkv_r_k=rwkv_r_k, rwkv_ln_w=rwkv_ln_w, rwkv_ln_b=rwkv_ln_b, moe_w_grp=moe_w_grp,
                  moe_b_grp=moe_b_grp, moe_w_exp=moe_w_exp, moe_b_exp=moe_b_exp, moe_w1=moe_w1,
                  moe_w3=moe_w3, moe_w2=moe_w2)
    x2 = x.reshape(bsz * seqlen, d)
    for layer in range(ada_w.shape[0]):
        mod = _ada(c, ada_w[layer], ada_b[layer])
        x2 = _layer(x2, mod, {k: v[layer] for k, v in params.items()}, bsz, seqlen)
    return x2.reshape(bsz, seqlen, d)
```

```python
import functools
import math

import jax
import jax.numpy as jnp
from jax import lax
from jax.experimental import pallas as pl
from jax.experimental.pallas import tpu as pltpu

F32 = jnp.float32
BF16 = jnp.bfloat16

SSM_GROUP = 16
S5_SLAB = 256
S5_TIME_BLOCK = 64
RWKV_HEAD = 64
RWKV_CHUNK = 64
RWKV_BATCH_PER_STEP = 4
HEADS_PER_TILE = 4
LORA_W = 64
LORA_A = 64
LORA_G = 128
N_GROUPS = 8
EXPERTS_PER_GROUP = 8
N_EXPERTS = N_GROUPS * EXPERTS_PER_GROUP
EXPERT_BLOCK = 256
RMS_EPS = 1e-6
GN_EPS = 64e-5
DECAY_SCALE = math.exp(-0.5)
LANES = 128
COMBINE_CHUNK = 64
SMEM_INDEX_BLOCK = 1024
ROUTE_LANES = LANES
V7X_VMEM_BYTES = 64 * 1024 * 1024
VMEM_LIMIT = V7X_VMEM_BYTES - 12 * 1024 * 1024


def _cparams(*sem):
    return pltpu.CompilerParams(dimension_semantics=sem, vmem_limit_bytes=VMEM_LIMIT)


def _sigmoid(x):
    return 1.0 / (1.0 + jnp.exp(-x))


def _dot(a, b):
    return jnp.dot(a, b, preferred_element_type=F32)


def _dot_nt(a, b):
    return lax.dot_general(a, b, (((1,), (1,)), ((), ())), preferred_element_type=F32)


def _dot_tn(a, b):
    return lax.dot_general(a, b, (((0,), (0,)), ((), ())), preferred_element_type=F32)


def _split_bf16(x):
    hi = x.astype(BF16)
    lo = (x - hi.astype(F32)).astype(BF16)
    return hi, lo


def _pack_pair(a, b):
    ua = lax.bitcast_convert_type(a.astype(BF16).astype(F32), jnp.uint32)
    ub = lax.bitcast_convert_type(b.astype(BF16).astype(F32), jnp.uint32)
    return ub | (ua >> 16)


def _unpack_pair(w):
    lo = lax.bitcast_convert_type(w << 16, F32)
    hi = lax.bitcast_convert_type(w & jnp.uint32(0xFFFF0000), F32)
    return lo, hi


def _store_packed(ref, val, n_rows):
    d = val.shape[1]
    slabs = d // (2 * LANES)
    for s in range(slabs):
        a = val[:, s * LANES:(s + 1) * LANES]
        b = val[:, d // 2 + s * LANES:d // 2 + (s + 1) * LANES]
        ref[pl.ds(s, n_rows, stride=slabs), :] = _pack_pair(a, b)


def _load_packed(ref, n_rows, slabs):
    return [_unpack_pair(ref[pl.ds(s, n_rows, stride=slabs), :]) for s in range(slabs)]


def _side_cast(src_ref, dst_ref, regroup):
    for e in range(src_ref.shape[0]):
        if regroup:
            slabs = dst_ref.shape[1]
            half = slabs * LANES
            for s in range(slabs):
                dst_ref[e, s, :LANES, :] = src_ref[e, s * LANES:(s + 1) * LANES, :].astype(BF16)
                dst_ref[e, s, LANES:, :] = src_ref[e, half + s * LANES:half + (s + 1) * LANES, :].astype(BF16)
        else:
            dst_ref[e] = src_ref[e].astype(BF16)


def _side_io(w, n_steps, step_of, regroup):
    n_e, rows, cols = w.shape
    per = n_e // n_steps
    assert per * n_steps == n_e, (n_e, n_steps)
    in_spec = pl.BlockSpec((per, rows, cols), lambda *i: (step_of(*i), 0, 0))
    if regroup:
        shape = (n_e, rows // (2 * LANES), 2 * LANES, cols)
        out_spec = pl.BlockSpec((per,) + shape[1:], lambda *i: (step_of(*i), 0, 0, 0))
    else:
        shape = w.shape
        out_spec = pl.BlockSpec((per, rows, cols), lambda *i: (step_of(*i), 0, 0))
    return in_spec, out_spec, jax.ShapeDtypeStruct(shape, BF16)


def _ada_kernel(c_ref, w_ref, b_ref, o_ref):
    c = c_ref[...]
    bsz = c.shape[0]
    c_hi, c_lo = _split_bf16(c * _sigmoid(c))
    w_hi, w_lo = _split_bf16(w_ref[...])
    first = _dot(jnp.concatenate([c_hi, c_lo], axis=0), w_hi)
    o_ref[...] = first[:bsz] + first[bsz:] + _dot(c_hi, w_lo) + b_ref[...]


def _ada(c, ada_w, ada_b):
    bsz, d = c.shape
    n = ada_w.shape[1]
    tn = 1024
    return pl.pallas_call(
        _ada_kernel,
        grid=(n // tn,),
        in_specs=[pl.BlockSpec((bsz, d), lambda j: (0, 0)),
                  pl.BlockSpec((d, tn), lambda j: (0, j)),
                  pl.BlockSpec((1, tn), lambda j: (0, j))],
        out_specs=pl.BlockSpec((bsz, tn), lambda j: (0, j)),
        out_shape=jax.ShapeDtypeStruct((bsz, n), F32),
        compiler_params=_cparams("arbitrary"),
        name="ada_mod",
    )(c, ada_w, ada_b.reshape(1, n))


def _rms(x, g):
    return x * lax.rsqrt(jnp.mean(x * x, axis=-1, keepdims=True) + RMS_EPS) * g


def _norm_proj_kernel(x_ref, g_ref, sc_ref, sh_ref, w_ref, side_ref, o_ref, side_out):
    h = _rms(x_ref[...], g_ref[...]) * (1.0 + sc_ref[0]) + sh_ref[0]
    o_ref[...] = _dot(h.astype(BF16), w_ref[...]).astype(o_ref.dtype)
    _side_cast(side_ref, side_out, regroup=False)


def _norm_proj(x2, g, sc, sh, w, side_w, seqlen, tm, name):
    n, d = x2.shape
    nout = w.shape[1]
    tpb = seqlen // tm
    side_in, side_out, side_shape = _side_io(side_w, n // tm, lambda i: i, regroup=False)
    return pl.pallas_call(
        _norm_proj_kernel,
        grid=(n // tm,),
        in_specs=[pl.BlockSpec((tm, d), lambda i: (i, 0)),
                  pl.BlockSpec((1, d), lambda i: (0, 0)),
                  pl.BlockSpec((1, 1, d), lambda i: (i // tpb, 0, 0)),
                  pl.BlockSpec((1, 1, d), lambda i: (i // tpb, 0, 0)),
                  pl.BlockSpec((d, nout), lambda i: (0, 0), pipeline_mode=pl.Buffered(1)),
                  side_in],
        out_specs=[pl.BlockSpec((tm, nout), lambda i: (i, 0)), side_out],
        out_shape=[jax.ShapeDtypeStruct((n, nout), BF16), side_shape],
        compiler_params=_cparams("parallel"),
        name=name,
    )(x2, g.reshape(1, d), sc[:, None, :], sh[:, None, :], w, side_w)


def _s5_tables(lam_re, lam_im, log_dt, b_re, b_im, c_re, c_im):
    g, p, cg = b_re.shape
    gs = S5_SLAB // cg
    ns = g // gs
    lr = jnp.minimum(lam_re.astype(F32), -1e-4)
    li = lam_im.astype(F32)
    dt = jnp.exp(log_dt.astype(F32))[:, None]
    mag = jnp.exp(lr * dt)
    ar, ai = mag * jnp.cos(li * dt), mag * jnp.sin(li * dt)
    den = lr * lr + li * li
    qr = ((ar - 1.0) * lr + ai * li) / den
    qi = (ai * lr - (ar - 1.0) * li) / den
    br, bi = b_re.astype(F32), b_im.astype(F32)
    bbr = qr[..., None] * br - qi[..., None] * bi
    bbi = qr[..., None] * bi + qi[..., None] * br

    def in_rows(t):
        return t.reshape(ns, gs, p, cg).transpose(0, 1, 3, 2).reshape(ns, gs * cg, p).astype(BF16)

    def out_cols(t):
        return t.reshape(ns, gs, cg, p).transpose(0, 3, 1, 2).reshape(ns, p, gs * cg).astype(BF16)

    a_tab = jnp.stack([ar.reshape(ns, gs * p), ai.reshape(ns, gs * p)], axis=1)
    return (in_rows(bbr), in_rows(bbi), out_cols(c_re.astype(F32)), out_cols(-c_im.astype(F32)), a_tab)


def _gelu_tanh(x):
    return 0.5 * x * (1.0 + jnp.tanh(math.sqrt(2.0 / math.pi) * (x + 0.044715 * (x * x * x))))


def _s5_kernel(x_ref, g_ref, sc_ref, sh_ref, wu_ref, perm_ref, bre_ref, bim_ref, cre_ref, cim_ref, a_ref,
               d_ref, gw_ref, gb_ref, o_ref, b_ref, c_ref, u_scr, bscr, sscr, yscr, st_ref):
    bsz, lb, d_in = x_ref.shape
    rows = bsz * lb
    dch = wu_ref.shape[1]
    half = st_ref.shape[2] // 2
    n_p = bre_ref.shape[2]

    @pl.when(pl.program_id(0) == 0)
    def _():
        st_ref[...] = jnp.zeros_like(st_ref)
        tile_in = (lax.broadcasted_iota(jnp.int32, (n_p, half), 0)
                   == lax.broadcasted_iota(jnp.int32, (n_p, half), 1) % n_p).astype(BF16)
        tile_out = (lax.broadcasted_iota(jnp.int32, (half, n_p), 0) % n_p
                    == lax.broadcasted_iota(jnp.int32, (half, n_p), 1)).astype(BF16)
        in_mask = (lax.broadcasted_iota(jnp.int32, (S5_SLAB, half), 0) // SSM_GROUP
                   == lax.broadcasted_iota(jnp.int32, (S5_SLAB, half), 1) // n_p)
        out_mask = (lax.broadcasted_iota(jnp.int32, (half, S5_SLAB), 0) // n_p
                    == lax.broadcasted_iota(jnp.int32, (half, S5_SLAB), 1) // SSM_GROUP)
        for s in range(dch // S5_SLAB):
            b_ref[s, :, :half] = jnp.where(in_mask, _dot(bre_ref[s], tile_in), 0.0).astype(BF16)
            b_ref[s, :, half:] = jnp.where(in_mask, _dot(bim_ref[s], tile_in), 0.0).astype(BF16)
            c_ref[s, :half, :] = jnp.where(out_mask, _dot(tile_out, cre_ref[s]), 0.0).astype(BF16)
            c_ref[s, half:, :] = jnp.where(out_mask, _dot(tile_out, cim_ref[s]), 0.0).astype(BF16)

    h = (_rms(x_ref[...], g_ref[...]) * (1.0 + sc_ref[...]) + sh_ref[...]).astype(BF16)
    u_nat = _dot(h.reshape(rows, d_in), wu_ref[...]).astype(BF16)
    u_scr[...] = _dot(perm_ref[...], u_nat).astype(BF16)
    n_slab = dch // S5_SLAB
    slab = lambda s: slice(s * S5_SLAB, (s + 1) * S5_SLAB)

    def project_in(s):
        bscr[s % 2] = _dot(u_scr[:, slab(s)], b_ref[s])

    project_in(0)
    for s in range(n_slab):
        if s + 1 < n_slab:
            project_in(s + 1)
        buf = s % 2
        a_re = a_ref[s, 0:1, :]
        a_im = a_ref[s, 1:2, :]
        s_r = st_ref[s, :, :half]
        s_i = st_ref[s, :, half:]
        for l in range(rows // bsz):
            at_l = slice(l * bsz, (l + 1) * bsz)
            s_r, s_i = (a_re * s_r - a_im * s_i + bscr[buf, at_l, :half],
                        a_re * s_i + a_im * s_r + bscr[buf, at_l, half:])
            sscr[buf, at_l, :half] = s_r
            sscr[buf, at_l, half:] = s_i
        st_ref[s, :, :half] = s_r
        st_ref[s, :, half:] = s_i
        y = _dot(sscr[buf].astype(BF16), c_ref[s]) + d_ref[:, slab(s)] * u_scr[:, slab(s)].astype(F32)
        yscr[:, slab(s)] = _gelu_tanh(y).astype(BF16)
    y = yscr[...]
    gate = _sigmoid(_dot(y, gw_ref[...]) + gb_ref[...])
    out_tm = (y.astype(F32) * gate).astype(BF16)
    out_nat = _dot_tn(perm_ref[...], out_tm).astype(o_ref.dtype)
    o_ref[...] = out_nat.reshape(bsz, lb, dch)


def _s5_glu(x2, g, sc, sh, w_u, tabs, d_skip, glu_w, glu_b, bsz, seqlen, lb):
    b_re, b_im, c_re, c_im, a_tab = tabs
    d_in, dch = w_u.shape
    ns, _, n_half = a_tab.shape
    n_state = 2 * n_half
    rows = lb * bsz
    r_idx = jnp.arange(rows)
    perm = ((r_idx % bsz) * lb + r_idx // bsz)[:, None] == r_idx[None, :]
    full = lambda a: pl.BlockSpec(a.shape, lambda i: (0,) * a.ndim, pipeline_mode=pl.Buffered(1))
    args = (x2.reshape(bsz, seqlen, d_in), g.reshape(1, 1, d_in), sc[:, None, :], sh[:, None, :], w_u,
            perm.astype(BF16), b_re, b_im, c_re, c_im, a_tab,
            d_skip.astype(F32).reshape(1, dch), glu_w.astype(BF16), glu_b.astype(F32).reshape(1, dch))
    y = pl.pallas_call(
        _s5_kernel,
        grid=(seqlen // lb,),
        in_specs=[pl.BlockSpec((bsz, lb, d_in), lambda i: (0, i, 0))] + [full(a) for a in args[1:]],
        out_specs=pl.BlockSpec((bsz, lb, dch), lambda i: (0, i, 0)),
        out_shape=jax.ShapeDtypeStruct((bsz, seqlen, dch), BF16),
        scratch_shapes=[pltpu.VMEM((ns, S5_SLAB, n_state), BF16), pltpu.VMEM((ns, n_state, S5_SLAB), BF16),
                        pltpu.VMEM((rows, dch), BF16), pltpu.VMEM((2, rows, n_state), F32),
                        pltpu.VMEM((2, rows, n_state), F32), pltpu.VMEM((rows, dch), BF16),
                        pltpu.VMEM((ns, bsz, n_state), F32)],
        compiler_params=_cparams("arbitrary"),
        name="s5_mixer_glu",
    )(*args)
    return y.reshape(bsz * seqlen, dch)


def _rwkv_kernel(z_ref, mu_ref, w0_ref, a0_ref, kkw_ref, ka_ref, wa_ref, gup_ref, rk_ref, lnw_ref, lnb_ref,
                 ones_ref, side1_ref, side3_ref, o_ref, side1_out, side3_out, s_ref, zlast_ref):
    _side_cast(side1_ref, side1_out, regroup=True)
    _side_cast(side3_ref, side3_out, regroup=True)
    nb, t, _ = z_ref.shape
    d_r = o_ref.shape[2]
    tile = HEADS_PER_TILE * RWKV_HEAD
    n_tiles = d_r // tile
    first_chunk = pl.program_id(1) == 0

    @pl.when(first_chunk)
    def _():
        s_ref[...] = jnp.zeros_like(s_ref)

    ones_bd = ones_ref[...]

    def seg_sum(x):
        return _dot(x.astype(BF16), ones_bd)

    row = lax.broadcasted_iota(jnp.int32, (t, t), 0)
    col = lax.broadcasted_iota(jnp.int32, (t, t), 1)
    tri = (row >= col).astype(BF16)
    st = HEADS_PER_TILE * t
    rs = lax.broadcasted_iota(jnp.int32, (2 * t, 2 * st), 0)
    t_r = rs % t
    t_c = lax.broadcasted_iota(jnp.int32, (2 * t, 2 * st), 1) % t
    keep = (t_r > t_c) | ((rs >= t) & (t_r == t_c))
    eye_w = (lax.broadcasted_iota(jnp.int32, (t, st), 0)
             == lax.broadcasted_iota(jnp.int32, (t, st), 1) % t).astype(F32)
    blk_mask = (lax.broadcasted_iota(jnp.int32, (st, st), 0) // t
                == lax.broadcasted_iota(jnp.int32, (st, st), 1) // t)
    lane = lax.broadcasted_iota(jnp.int32, (1, tile), 1)
    head_masks = [(lane >= j * RWKV_HEAD) & (lane < (j + 1) * RWKV_HEAD) for j in range(HEADS_PER_TILE)]
    bd_r = lax.broadcasted_iota(jnp.int32, (tile, tile), 0) // RWKV_HEAD
    bd_c = lax.broadcasted_iota(jnp.int32, (tile, tile), 1) // RWKV_HEAD
    bd_mask = bd_r == bd_c
    n_levels = int(math.log2(t))
    slices = [slice(hg * tile, (hg + 1) * tile) for hg in range(n_tiles)]
    units = [(bb, hg) for bb in range(nb) for hg in range(n_tiles)]
    n_u = range(len(units))

    def stack(x):
        zero = jnp.zeros_like(x)
        return jnp.concatenate([jnp.where(m, x, zero) for m in head_masks], axis=0)

    def bf(x):
        return x.astype(BF16)

    at, qt, bt, kt, vv, em, etm, wtot, rkb, gate = ([] for _ in range(10))
    row0 = lax.broadcasted_iota(jnp.int32, (t, 1), 0) == 0
    lora_lane = lax.broadcasted_iota(jnp.int32, (t, LORA_W + LORA_A), 1)
    for bb in range(nb):
        z = z_ref[bb].astype(F32)
        prev_row = jnp.where(first_chunk, 0.0, zlast_ref[bb, 0:1, :])
        zl = z + mu_ref[...] * (jnp.where(row0, prev_row, pltpu.roll(z, 1, 0)) - z)
        zlast_ref[bb, 0:1, :] = z[t - 1:t, :]
        xwa = zl[:, 3 * d_r:3 * d_r + LORA_W + LORA_A]
        lhs = jnp.where(lora_lane < LORA_W, jnp.tanh(xwa), xwa).astype(BF16)
        wa = _dot(lhs, wa_ref[...])
        lw = -DECAY_SCALE * _sigmoid(w0_ref[...] + wa[:, :d_r])
        asig = _sigmoid(a0_ref[...] + wa[:, d_r:])
        r = zl[:, :d_r]
        k_raw = zl[:, d_r:2 * d_r]
        kk = k_raw * kkw_ref[...]
        kp = k_raw * (1.0 + (asig - 1.0) * ka_ref[...])
        xg = zl[:, 3 * d_r + LORA_W + LORA_A:3 * d_r + LORA_W + LORA_A + LORA_G]
        g_full = _dot(_sigmoid(xg).astype(BF16), gup_ref[...])
        lw_hi, lw_lo = _split_bf16(lw)
        cum = _dot(tri, lw_hi) + _dot(tri, lw_lo)
        mid = cum[t // 2 - 1:t // 2, :]
        tot = cum[t - 1:t, :]
        e1 = jnp.exp(cum - mid)
        e2 = jnp.exp(mid - cum)
        e1p = e1 * jnp.exp(-lw)
        kk2 = kk * kk
        kkn = kk / jnp.maximum(jnp.sqrt(jnp.concatenate([seg_sum(kk2[:, sl]) for sl in slices], axis=1)), 1e-12)
        full = dict(at=-kkn * e1p, qt=r * e1, bt=kkn * asig * e2, kt=kp * e2,
                    vv=zl[:, 2 * d_r:3 * d_r], em=jnp.exp(mid), etm=jnp.exp(tot - mid), wtot=jnp.exp(tot),
                    rkb=r * kp * rk_ref[...], gate=g_full)
        for dst, key in ((at, "at"), (qt, "qt"), (bt, "bt"), (kt, "kt"), (vv, "vv"), (em, "em"),
                         (etm, "etm"), (wtot, "wtot"), (rkb, "rkb"), (gate, "gate")):
            dst.extend(full[key][:, sl] for sl in slices)

    s_old = [s_ref[i] for i in n_u]
    wide = []
    for i in n_u:
        lhs = bf(jnp.concatenate([at[i], qt[i]], axis=0))
        rhs = jnp.concatenate([stack(bf(bt[i])), stack(bf(kt[i]))], axis=0)
        a_w = bf(jnp.where(keep, _dot_nt(lhs, rhs), 0.0))
        wide.append([a_w[:t, :st], a_w[:t, st:], a_w[t:, :]])
    x_state = [_dot_nt(bf(jnp.concatenate([at[i], qt[i]], axis=0) * em[i]), bf(s_old[i]))
               for i in n_u]
    sv = [stack(bf(vv[i])) for i in n_u]
    akv = [_dot(wide[i][1], sv[i]) for i in n_u]

    def expand(x_w):
        x_b = bf(x_w)
        return jnp.where(blk_mask, jnp.concatenate([x_b] * HEADS_PER_TILE, axis=0), jnp.zeros((st, st), BF16))

    p_acc = [eye_w + w[0].astype(F32) for w in wide]
    q_bd = [expand(w[0]) for w in wide]
    q_pow = [_dot(wide[i][0], q_bd[i]) for i in n_u]
    for lev in range(1, n_levels):
        for i in n_u:
            q_bd[i] = expand(q_pow[i])
            if lev < n_levels - 1:
                both = _dot(bf(jnp.concatenate([p_acc[i], q_pow[i]], axis=0)), q_bd[i])
                p_acc[i] = p_acc[i] + both[:t]
                q_pow[i] = both[t:]
            else:
                p_acc[i] = p_acc[i] + _dot(bf(p_acc[i]), q_bd[i])
    u_all = [_dot(bf(p_acc[i]), stack(bf(x_state[i][:t] + akv[i]))) for i in n_u]
    y_all = [x_state[i][t:]
             + _dot(wide[i][2], jnp.concatenate([stack(bf(u_all[i])), sv[i]], axis=0)) for i in n_u]

    for i, (bb, hg) in enumerate(units):
        sl = slices[hg]
        y = y_all[i]
        uv = bf(jnp.concatenate([u_all[i], vv[i]], axis=0))
        bk_end = bf(jnp.concatenate([bt[i], kt[i]], axis=0) * etm[i])
        s_ref[i] = s_old[i] * wtot[i] + jnp.where(bd_mask, _dot_tn(uv, bk_end), 0.0)

        mean = seg_sum(y) * (1.0 / RWKV_HEAD)
        dlt = y - mean
        var = seg_sum(dlt * dlt) * (1.0 / RWKV_HEAD)
        yn = dlt * lax.rsqrt(var + GN_EPS) * lnw_ref[:, sl] + lnb_ref[:, sl]
        out = (yn + seg_sum(rkb[i]) * vv[i]) * gate[i]
        o_ref[bb, :, sl] = out.astype(o_ref.dtype)


def _rwkv(z, mu, w0, w_up, a0, a_up, g_up, k_k, k_a, r_k, ln_w, ln_b, side_w1, side_w3, bsz, seqlen):
    n, dz = z.shape
    d_r = w0.shape[0]
    t = RWKV_CHUNK
    nb = RWKV_BATCH_PER_STEP
    nch = seqlen // t
    tile = HEADS_PER_TILE * RWKV_HEAD
    hid = jnp.arange(tile) // RWKV_HEAD
    ones_bd = (hid[:, None] == hid[None, :]).astype(BF16)
    wa = jnp.zeros((LORA_W + LORA_A, 2 * d_r), F32)
    wa = wa.at[:LORA_W, :d_r].set(w_up.astype(F32)).at[LORA_W:, d_r:].set(a_up.astype(F32)).astype(BF16)
    row = lambda a: a.astype(F32).reshape(1, -1)
    full = lambda a: pl.BlockSpec(a.shape, lambda b, c: (0, 0))
    params = (row(mu), row(w0), row(a0), row(k_k), row(k_a), wa, g_up.astype(BF16), row(r_k), row(ln_w),
              row(ln_b), ones_bd)
    side_in, side_out, side_shape = _side_io(side_w1, (bsz // nb) * nch, lambda b, c: b * nch + c, regroup=True)
    y, side1, side3 = pl.pallas_call(
        _rwkv_kernel,
        grid=(bsz // nb, nch),
        in_specs=([pl.BlockSpec((nb, t, dz), lambda b, c: (b, c, 0))] + [full(a) for a in params]
                  + [side_in, side_in]),
        out_specs=[pl.BlockSpec((nb, t, d_r), lambda b, c: (b, c, 0)), side_out, side_out],
        out_shape=[jax.ShapeDtypeStruct((bsz, seqlen, d_r), BF16), side_shape, side_shape],
        scratch_shapes=[pltpu.VMEM((nb * (d_r // tile), tile, tile), F32), pltpu.VMEM((nb, 8, dz), F32)],
        compiler_params=_cparams("parallel", "arbitrary"),
        name="rwkv7_chunked",
    )(z.reshape(bsz, seqlen, dz), *params, side_w1, side_w3)
    return y.reshape(n, d_r), side1, side3


def _post_mix_kernel(ys_ref, yr_ref, wo1_ref, wo2_ref, x_ref, g1_ref, gt_ref, g2_ref, sc_ref, sh_ref,
                     wr_both_ref, br_ref, x1_out, h2_out, lg_out):
    nl = lg_out.shape[1]
    mixed = _dot(ys_ref[...], wo1_ref[...]) + _dot(yr_ref[...], wo2_ref[...])
    x1 = x_ref[...] + gt_ref[0] * _rms(mixed, g1_ref[...])
    x1_out[...] = x1
    h2 = _rms(x1, g2_ref[...]) * (1.0 + sc_ref[0]) + sh_ref[0]
    _store_packed(h2_out, h2, h2.shape[0])
    hi, lo = _split_bf16(h2)
    both = _dot(hi, wr_both_ref[...])
    lg_out[...] = both[:, :nl] + both[:, nl:] + _dot(lo, wr_both_ref[:, :nl]) + br_ref[...]


def _post_mix(ys, yr, w_out, x2, g1, gt1, g2, sc2, sh2, w_route, b_route, seqlen, tm):
    n, d = x2.shape
    ds = ys.shape[1]
    tpb = seqlen // tm
    slabs = d // (2 * LANES)
    wr_both = jnp.concatenate(_split_bf16(w_route), axis=1)
    rows = lambda w: pl.BlockSpec((tm, w), lambda i: (i, 0))
    full = lambda a: pl.BlockSpec(a.shape, lambda i: (0,) * a.ndim)
    bat = pl.BlockSpec((1, 1, d), lambda i: (i // tpb, 0, 0))
    args = (ys, yr, w_out[:ds].astype(BF16), w_out[ds:].astype(BF16), x2, g1.reshape(1, d), gt1[:, None, :],
            g2.reshape(1, d), sc2[:, None, :], sh2[:, None, :], wr_both, b_route.reshape(1, -1))
    in_specs = [rows(ds), rows(yr.shape[1]), full(args[2]), full(args[3]), rows(d), full(args[5]), bat,
                full(args[7]), bat, bat, full(wr_both), full(args[11])]
    return pl.pallas_call(
        _post_mix_kernel,
        grid=(n // tm,),
        in_specs=in_specs,
        out_specs=[rows(d), pl.BlockSpec((tm * slabs, LANES), lambda i: (i, 0)), rows(ROUTE_LANES)],
        out_shape=[jax.ShapeDtypeStruct((n, d), F32), jax.ShapeDtypeStruct((n * slabs, LANES), jnp.uint32),
                   jax.ShapeDtypeStruct((n, ROUTE_LANES), F32)],
        compiler_params=_cparams("parallel"),
        name="out_proj_post",
    )(*args)


def _route_kernel(lg_ref, info_ref, cnt_ref, carry):
    i = pl.program_id(0)
    tm = lg_ref.shape[0]

    @pl.when(i == 0)
    def _():
        carry[...] = jnp.zeros_like(carry)

    lg = lg_ref[...]
    lane = lax.broadcasted_iota(jnp.int32, lg.shape, 1)
    lane_f = lane.astype(F32)
    neg = jnp.float32(-jnp.inf)
    big = jnp.float32(1e9)
    is_g = (lane >= N_EXPERTS) & (lane < N_EXPERTS + N_GROUPS)
    gl = jnp.where(is_g, lg, neg)
    gmax = jnp.max(gl, axis=-1, keepdims=True)
    gidx = jnp.min(jnp.where(gl == gmax, lane_f - N_EXPERTS, big), axis=-1, keepdims=True)
    p_grp = 1.0 / jnp.sum(jnp.where(is_g, jnp.exp(gl - gmax), 0.0), axis=-1, keepdims=True)
    in_grp = (lane < N_EXPERTS) & ((lane // EXPERTS_PER_GROUP).astype(F32) == gidx)
    el = jnp.where(in_grp, lg, neg)
    m1 = jnp.max(el, axis=-1, keepdims=True)
    i1 = jnp.min(jnp.where(el == m1, lane_f, big), axis=-1, keepdims=True)
    el2 = jnp.where(lane_f == i1, neg, el)
    m2 = jnp.max(el2, axis=-1, keepdims=True)
    i2 = jnp.min(jnp.where(el2 == m2, lane_f, big), axis=-1, keepdims=True)
    ex = jnp.exp(m2 - m1)
    w1 = p_grp / (1.0 + ex)
    w2 = p_grp * ex / (1.0 + ex)

    oh1 = lane_f == i1
    oh2 = lane_f == i2
    onehot = (oh1 | oh2).astype(BF16)
    rr = lax.broadcasted_iota(jnp.int32, (tm, tm), 0)
    cc = lax.broadcasted_iota(jnp.int32, (tm, tm), 1)
    before = _dot((rr > cc).astype(BF16), onehot) + carry[...]
    rank1 = jnp.sum(jnp.where(oh1, before, 0.0), axis=-1, keepdims=True)
    rank2 = jnp.sum(jnp.where(oh2, before, 0.0), axis=-1, keepdims=True)
    carry[...] = carry[...] + jnp.sum(onehot.astype(F32), axis=0, keepdims=True)
    cnt_ref[...] = carry[...]

    info = jnp.where(lane == 0, i1, 0.0)
    info = jnp.where(lane == 1, i2, info)
    info = jnp.where(lane == 2, w1, info)
    info = jnp.where(lane == 3, w2, info)
    info = jnp.where(lane == 4, rank1, info)
    info = jnp.where(lane == 5, rank2, info)
    info_ref[...] = info


def _route(logits, tm):
    n = logits.shape[0]
    return pl.pallas_call(
        _route_kernel,
        grid=(n // tm,),
        in_specs=[pl.BlockSpec((tm, ROUTE_LANES), lambda i: (i, 0))],
        out_specs=[pl.BlockSpec((tm, ROUTE_LANES), lambda i: (i, 0)),
                   pl.BlockSpec((1, ROUTE_LANES), lambda i: (0, 0))],
        out_shape=[jax.ShapeDtypeStruct((n, ROUTE_LANES), F32),
                   jax.ShapeDtypeStruct((1, ROUTE_LANES), F32)],
        scratch_shapes=[pltpu.VMEM((1, ROUTE_LANES), F32)],
        compiler_params=_cparams("arbitrary"),
        name="moe_route",
    )(logits)


def _slot_rows_kernel(info_ref, seg_ref, o_ref, *, slabs):
    info = info_ref[...]
    lane = lax.broadcasted_iota(jnp.int32, info.shape, 1)
    lane_f = lane.astype(F32)
    seg = seg_ref[...]
    d0 = jnp.sum(jnp.where(lane_f == info[:, 0:1], seg, 0.0), axis=-1, keepdims=True) + info[:, 4:5] * slabs
    d1 = jnp.sum(jnp.where(lane_f == info[:, 1:2], seg, 0.0), axis=-1, keepdims=True) + info[:, 5:6] * slabs
    o_ref[...] = jnp.where(lane == 0, d0, jnp.where(lane == 1, d1, 0.0)).astype(jnp.int32)


def _slot_rows(info, seg_row, slabs, tm):
    n = info.shape[0]
    return pl.pallas_call(
        functools.partial(_slot_rows_kernel, slabs=slabs),
        grid=(n // tm,),
        in_specs=[pl.BlockSpec((tm, ROUTE_LANES), lambda i: (i, 0)),
                  pl.BlockSpec((1, ROUTE_LANES), lambda i: (0, 0))],
        out_specs=pl.BlockSpec((tm, ROUTE_LANES), lambda i: (i, 0)),
        out_shape=jax.ShapeDtypeStruct((n, ROUTE_LANES), jnp.int32),
        compiler_params=_cparams("parallel"),
        name="moe_slot_rows",
    )(info, seg_row)


def _dispatch_kernel(d0_ref, d1_ref, pad_row_ref, pad_len_ref, na_ref, h_ref, buf_out, zeros, sem, zsem, *,
                     slabs, n_blocks):
    tm = h_ref.shape[0] // slabs

    @pl.when(pl.program_id(0) == 0)
    def _():
        zeros[...] = jnp.zeros_like(zeros)
        bits = [1 << b for b in reversed(range(EXPERT_BLOCK.bit_length() - 1))]

        def pad_copy(e, bit):
            done = pad_len_ref[e] & ~(2 * bit - 1)
            dst = pl.multiple_of(pad_row_ref[e] + done * slabs, slabs)
            return pltpu.make_async_copy(zeros.at[pl.ds(0, bit * slabs)], buf_out.at[pl.ds(dst, bit * slabs)], zsem)

        def block_copy(j):
            dst = pl.multiple_of(j * (EXPERT_BLOCK * slabs), EXPERT_BLOCK * slabs)
            return pltpu.make_async_copy(zeros, buf_out.at[pl.ds(dst, EXPERT_BLOCK * slabs)], zsem)

        def each(action):
            def per_expert(e, c):
                for bit in bits:
                    @pl.when((pad_len_ref[e] & bit) != 0)
                    def _():
                        action(pad_copy(e, bit))
                return c

            def per_block(j, c):
                action(block_copy(j))
                return c

            lax.fori_loop(0, N_EXPERTS, per_expert, 0)
            lax.fori_loop(na_ref[0], n_blocks, per_block, 0)

        each(lambda c: c.start())
        each(lambda c: c.wait())

    def copy(t, dest_ref):
        src = pl.multiple_of(t * slabs, slabs)
        dst = pl.multiple_of(dest_ref[t], slabs)
        return pltpu.make_async_copy(h_ref.at[pl.ds(src, slabs)], buf_out.at[pl.ds(dst, slabs)], sem)

    def start(t, c):
        copy(t, d0_ref).start(priority=0)
        copy(t, d1_ref).start(priority=1)
        return c

    lax.fori_loop(0, tm, start, 0, unroll=4)
    for _ in range(2):
        pltpu.make_async_copy(h_ref, buf_out.at[pl.ds(0, tm * slabs)], sem).wait()


def _dispatch(h2p, dest_rows, pad_row, pad_len, n_active, cap, slabs, tm):
    n = h2p.shape[0] // slabs
    smem = pl.BlockSpec((tm,), lambda i: (i,), memory_space=pltpu.SMEM)
    table = lambda a: pl.BlockSpec(a.shape, lambda i: (0,), memory_space=pltpu.SMEM)
    return pl.pallas_call(
        functools.partial(_dispatch_kernel, slabs=slabs, n_blocks=cap // EXPERT_BLOCK),
        grid=(n // tm,),
        in_specs=[smem, smem, table(pad_row), table(pad_len), table(n_active),
                  pl.BlockSpec((tm * slabs, LANES), lambda i: (i, 0))],
        out_specs=pl.BlockSpec(memory_space=pl.ANY),
        out_shape=jax.ShapeDtypeStruct((cap * slabs, LANES), h2p.dtype),
        scratch_shapes=[pltpu.VMEM((EXPERT_BLOCK * slabs, LANES), h2p.dtype), pltpu.SemaphoreType.DMA,
                        pltpu.SemaphoreType.DMA],
        compiler_params=_cparams("arbitrary"),
        name="moe_dispatch",
    )(dest_rows[0], dest_rows[1], pad_row, pad_len, n_active, h2p)


def _moe_kernel(na_ref, eseq_ref, epos_ref, nd_ref, x_ref, w1_hbm, w3_hbm, w2_hbm, o_ref,
                w1b, w3b, w2b, sem):
    j = pl.program_id(0)
    active = j < na_ref[0]
    pos = epos_ref[j]
    fresh = (j == 0) | (pos != epos_ref[jnp.maximum(j - 1, 0)])
    slabs = w1b.shape[1]

    def weight_copies(p):
        e = eseq_ref[p]
        slot = p % 2
        return [pltpu.make_async_copy(w_hbm.at[e], stage.at[slot], sem.at[slot, i])
                for i, (w_hbm, stage) in enumerate(((w1_hbm, w1b), (w3_hbm, w3b), (w2_hbm, w2b)))]

    @pl.when(j == 0)
    def _():
        for c in weight_copies(0):
            c.start()

    @pl.when(active & fresh)
    def _():
        for c in weight_copies(pos):
            c.wait()

        @pl.when(pos + 1 < nd_ref[0])
        def _():
            for c in weight_copies(pos + 1):
                c.start()

    @pl.when(active)
    def _():
        slot = pos % 2
        acc1 = jnp.zeros((EXPERT_BLOCK, w1b.shape[3]), F32)
        acc3 = jnp.zeros((EXPERT_BLOCK, w1b.shape[3]), F32)
        for s, (lo, hi) in enumerate(_load_packed(x_ref, EXPERT_BLOCK, slabs)):
            lhs = jnp.concatenate([lo.astype(BF16), hi.astype(BF16)], axis=1)
            acc1 = acc1 + _dot(lhs, w1b[slot, s])
            acc3 = acc3 + _dot(lhs, w3b[slot, s])
        hid = (acc1 * _sigmoid(acc1)) * acc3
        _store_packed(o_ref, _dot(hid.astype(BF16), w2b[slot]), EXPERT_BLOCK)


def _moe(x_buf, n_active, expert_seq, block_pos, n_used, w1, w3, w2, slabs):
    cap = x_buf.shape[0] // slabs
    nb = cap // EXPERT_BLOCK

    def xmap(j, na, *_):
        return (jnp.minimum(j, na[0] - 1), 0)

    xspec = pl.BlockSpec((EXPERT_BLOCK * slabs, LANES), xmap)
    hbm = pl.BlockSpec(memory_space=pl.ANY)
    grid_spec = pltpu.PrefetchScalarGridSpec(
        num_scalar_prefetch=4,
        grid=(nb,),
        in_specs=[xspec, hbm, hbm, hbm],
        out_specs=xspec,
        scratch_shapes=[pltpu.VMEM((2,) + w1.shape[1:], BF16), pltpu.VMEM((2,) + w3.shape[1:], BF16),
                        pltpu.VMEM((2,) + w2.shape[1:], BF16), pltpu.SemaphoreType.DMA((2, 3))],
    )
    return pl.pallas_call(
        _moe_kernel,
        grid_spec=grid_spec,
        out_shape=jax.ShapeDtypeStruct(x_buf.shape, x_buf.dtype),
        input_output_aliases={4: 0},
        compiler_params=_cparams("arbitrary"),
        name="moe_experts",
    )(n_active, expert_seq, block_pos, n_used, x_buf, w1, w3, w2)


def _combine_kernel(d0_ref, d1_ref, d0n_ref, d1n_ref, y_ref, info_ref, x1_ref, g_ref, gt_ref, o_ref,
                    rows, sem, *, slabs, tiles_per_idx, chunk):
    i = pl.program_id(0)
    n_tiles = pl.num_programs(0)
    tm = x1_ref.shape[0]
    slot = i % 2
    nxt_slot = 1 - slot
    nxt = jnp.minimum(i + 1, n_tiles - 1)

    def copy(tile, to_slot, t, dest_ref, k):
        off = (tile % tiles_per_idx) * tm
        src = pl.multiple_of(dest_ref[off + t], slabs)
        dst = pl.multiple_of(t * slabs, slabs)
        return pltpu.make_async_copy(y_ref.at[pl.ds(src, slabs)], rows.at[to_slot, k, pl.ds(dst, slabs)],
                                     sem.at[to_slot])

    def drain(which):
        for k in range(2):
            pltpu.make_async_copy(y_ref.at[pl.ds(0, tm * slabs)], rows.at[which, k], sem.at[which]).wait()

    @pl.when(i == 0)
    def _():
        def start(t, c):
            copy(i, slot, t, d0_ref, 0).start(priority=0)
            copy(i, slot, t, d1_ref, 1).start(priority=1)
            return c

        lax.fori_loop(0, tm, start, 0, unroll=4)

    drain(slot)
    g_row = g_ref[...]
    gt_row = gt_ref[0]
    for c in range(tm // chunk):
        tok = slice(c * chunk, (c + 1) * chunk)
        info = info_ref[tok, :]
        w1 = info[:, 2:3]
        w2 = info[:, 3:4]
        lo_parts, hi_parts = [], []
        for s in range(slabs):
            at_s = pl.ds(c * chunk * slabs + s, chunk, stride=slabs)
            lo0, hi0 = _unpack_pair(rows[slot, 0, at_s, :])
            lo1, hi1 = _unpack_pair(rows[slot, 1, at_s, :])
            lo_parts.append(w1 * lo0 + w2 * lo1)
            hi_parts.append(w1 * hi0 + w2 * hi1)
        ffn = jnp.concatenate(lo_parts + hi_parts, axis=1)
        o_ref[tok, :] = x1_ref[tok, :] + gt_row * _rms(ffn, g_row)
        for t in range(c * chunk, (c + 1) * chunk):
            copy(nxt, nxt_slot, t, d0n_ref, 0).start(priority=0)
            copy(nxt, nxt_slot, t, d1n_ref, 1).start(priority=1)

    @pl.when(i == n_tiles - 1)
    def _():
        drain(nxt_slot)


def _combine(y_buf, dest_rows, info, x1, g, gt2, seqlen, slabs, tm):
    n, d = x1.shape
    tpb = seqlen // tm
    idx_block = max(tm, SMEM_INDEX_BLOCK)
    per = idx_block // tm
    last = n // tm - 1
    smem = pl.BlockSpec((idx_block,), lambda i: (i // per,), memory_space=pltpu.SMEM)
    smem_next = pl.BlockSpec((idx_block,), lambda i: (jnp.minimum(i + 1, last) // per,), memory_space=pltpu.SMEM)
    return pl.pallas_call(
        functools.partial(_combine_kernel, slabs=slabs, tiles_per_idx=per, chunk=min(tm, COMBINE_CHUNK)),
        grid=(n // tm,),
        in_specs=[smem, smem, smem_next, smem_next,
                  pl.BlockSpec(memory_space=pl.ANY),
                  pl.BlockSpec((tm, ROUTE_LANES), lambda i: (i, 0)),
                  pl.BlockSpec((tm, d), lambda i: (i, 0)),
                  pl.BlockSpec((1, d), lambda i: (0, 0)),
                  pl.BlockSpec((1, 1, d), lambda i: (i // tpb, 0, 0))],
        out_specs=pl.BlockSpec((tm, d), lambda i: (i, 0)),
        out_shape=jax.ShapeDtypeStruct((n, d), F32),
        scratch_shapes=[pltpu.VMEM((2, 2, tm * slabs, LANES), y_buf.dtype), pltpu.SemaphoreType.DMA((2,))],
        compiler_params=_cparams("arbitrary"),
        name="moe_combine",
    )(dest_rows[0], dest_rows[1], dest_rows[0], dest_rows[1], y_buf, info, x1, g.reshape(1, d),
      gt2[:, None, :])


def _pick(n, pref):
    while n % pref:
        pref //= 2
    return pref


def _layer(x2, mod, p, bsz, seqlen):
    n, d = x2.shape
    sh1, sc1, gt1, sh2, sc2, gt2 = jnp.split(mod, 6, axis=-1)
    d_ssm = p["ssm_d"].shape[0]

    z, w2_bf = _norm_proj(x2, p["norm_mix_pre"], sc1, sh1, p["w_in"][:, d_ssm:].astype(BF16), p["moe_w2"],
                          seqlen, _pick(seqlen, max(n // N_EXPERTS, 16)), "in_proj_z")

    tabs = _s5_tables(p["ssm_lam_re"], p["ssm_lam_im"], p["ssm_log_dt"], p["ssm_b_re"], p["ssm_b_im"],
                      p["ssm_c_re"], p["ssm_c_im"])
    y_ssm = _s5_glu(x2, p["norm_mix_pre"], sc1, sh1, p["w_in"][:, :d_ssm].astype(BF16), tabs, p["ssm_d"], p["glu_w"],
                    p["glu_b"], bsz, seqlen, _pick(seqlen, S5_TIME_BLOCK))

    y_rwkv, w1_bf, w3_bf = _rwkv(z, p["rwkv_mu"], p["rwkv_w0"], p["rwkv_w_up"], p["rwkv_a0"], p["rwkv_a_up"],
                                 p["rwkv_g_up"], p["rwkv_k_k"], p["rwkv_k_a"], p["rwkv_r_k"], p["rwkv_ln_w"],
                                 p["rwkv_ln_b"], p["moe_w1"], p["moe_w3"], bsz, seqlen)

    w_route = jnp.zeros((d, ROUTE_LANES), F32)
    w_route = w_route.at[:, :N_EXPERTS].set(p["moe_w_exp"].astype(F32))
    w_route = w_route.at[:, N_EXPERTS:N_EXPERTS + N_GROUPS].set(p["moe_w_grp"].astype(F32))
    b_route = jnp.zeros((ROUTE_LANES,), F32)
    b_route = b_route.at[:N_EXPERTS].set(p["moe_b_exp"].astype(F32))
    b_route = b_route.at[N_EXPERTS:N_EXPERTS + N_GROUPS].set(p["moe_b_grp"].astype(F32))
    x1, h2p, logits = _post_mix(y_ssm, y_rwkv, p["w_out"], x2, p["norm_mix_post"], gt1, p["norm_ffn_pre"],
                                sc2, sh2, w_route, b_route, seqlen, _pick(seqlen, 512))
    slabs = d // (2 * LANES)

    info, counts = _route(logits, _pick(n, 512))
    cnt = counts[0, :N_EXPERTS].astype(jnp.int32)
    padded = (cnt + EXPERT_BLOCK - 1) // EXPERT_BLOCK * EXPERT_BLOCK
    pend = jnp.cumsum(padded)
    pstart = pend - padded
    n_blocks = -(-(2 * n) // EXPERT_BLOCK) + N_EXPERTS
    cap = n_blocks * EXPERT_BLOCK
    seg_row = jnp.zeros((1, ROUTE_LANES), F32).at[0, :N_EXPERTS].set((pstart * slabs).astype(F32))
    dest = _slot_rows(info, seg_row, slabs, _pick(n, 2048))
    dest_rows = (dest[:, 0], dest[:, 1])
    n_active = (pend[-1:] // EXPERT_BLOCK).astype(jnp.int32)
    used = (cnt > 0).astype(jnp.int32)
    used_pos = jnp.cumsum(used) - 1
    slot_ids = jnp.arange(N_EXPERTS, dtype=jnp.int32)
    expert_seq = jnp.sum(jnp.where((used_pos[None, :] == slot_ids[:, None]) & (used[None, :] > 0),
                                   slot_ids[None, :], 0), axis=1).astype(jnp.int32)
    block_first = jnp.arange(n_blocks, dtype=jnp.int32) * EXPERT_BLOCK
    block_pos = jnp.sum(jnp.where(pend[None, :] <= block_first[:, None], used[None, :], 0),
                        axis=1).astype(jnp.int32)
    n_used = jnp.sum(used).reshape(1)

    x_buf = _dispatch(h2p, dest_rows, ((pstart + cnt) * slabs).astype(jnp.int32),
                      (padded - cnt).astype(jnp.int32), n_active, cap, slabs, _pick(n, 1024))
    y_buf = _moe(x_buf, n_active, expert_seq, block_pos, n_used, w1_bf, w3_bf, w2_bf, slabs)
    return _combine(y_buf, dest_rows, info, x1, p["norm_ffn_post"], gt2, seqlen, slabs, _pick(seqlen, 512))


def kernel(x, c, ada_w, ada_b, norm_mix_pre, norm_mix_post, norm_ffn_pre, norm_ffn_post, w_in, w_out, ssm_lam_re, ssm_lam_im, ssm_log_dt, ssm_b_re, ssm_b_im, ssm_c_re, ssm_c_im, ssm_d, glu_w, glu_b, rwkv_mu, rwkv_w0, rwkv_w_up, rwkv_a0, rwkv_a_up, rwkv_g_up, rwkv_k_k, rwkv_k_a, rwkv_r_k, rwkv_ln_w, rwkv_ln_b, moe_w_grp, moe_b_grp, moe_w_exp, moe_b_exp, moe_w1, moe_w3, moe_w2):
    bsz, seqlen, d = x.shape
    params = dict(norm_mix_pre=norm_mix_pre, norm_mix_post=norm_mix_post, norm_ffn_pre=norm_ffn_pre,
                  norm_ffn_post=norm_ffn_post, w_in=w_in, w_out=w_out, ssm_lam_re=ssm_lam_re,
                  ssm_lam_im=ssm_lam_im, ssm_log_dt=ssm_log_dt, ssm_b_re=ssm_b_re, ssm_b_im=ssm_b_im,
                  ssm_c_re=ssm_c_re, ssm_c_im=ssm_c_im, ssm_d=ssm_d, glu_w=glu_w, glu_b=glu_b,
                  rwkv_mu=rwkv_mu, rwkv_w0=rwkv_w0, rwkv_w_up=rwkv_w_up, rwkv_a0=rwkv_a0,
                  rwkv_a_up=rwkv_a_up, rwkv_g_up=rwkv_g_up, rwkv_k_k=rwkv_k_k, rwkv_k_a=rwkv_k_a,
                  rwkv_r_k=rwkv_r_k, rwkv_ln_w=rwkv_ln_w, rwkv_ln_b=rwkv_ln_b, moe_w_grp=moe_w_grp,
                  moe_b_grp=moe_b_grp, moe_w_exp=moe_w_exp, moe_b_exp=moe_b_exp, moe_w1=moe_w1,
                  moe_w3=moe_w3, moe_w2=moe_w2)
    x2 = x.reshape(bsz * seqlen, d)
    for layer in range(ada_w.shape[0]):
        mod = _ada(c, ada_w[layer], ada_b[layer])
        x2 = _layer(x2, mod, {k: v[layer] for k, v in params.items()}, bsz, seqlen)
    return x2.reshape(bsz, seqlen, d)
```

```python
import functools
import math

import jax
import jax.numpy as jnp
from jax import lax
from jax.experimental import pallas as pl
from jax.experimental.pallas import tpu as pltpu

F32 = jnp.float32
BF16 = jnp.bfloat16

SSM_GROUP = 16
S5_SLAB = 256
S5_TIME_BLOCK = 64
RWKV_HEAD = 64
RWKV_CHUNK = 64
RWKV_BATCH_PER_STEP = 4
HEADS_PER_TILE = 4
LORA_W = 64
LORA_A = 64
LORA_G = 128
N_GROUPS = 8
EXPERTS_PER_GROUP = 8
N_EXPERTS = N_GROUPS * EXPERTS_PER_GROUP
EXPERT_BLOCK = 256
RMS_EPS = 1e-6
GN_EPS = 64e-5
DECAY_SCALE = math.exp(-0.5)
LANES = 128
COMBINE_CHUNK = 64
SMEM_INDEX_BLOCK = 1024
ROUTE_LANES = LANES
V7X_VMEM_BYTES = 64 * 1024 * 1024
VMEM_LIMIT = V7X_VMEM_BYTES - 12 * 1024 * 1024


def _cparams(*sem):
    return pltpu.CompilerParams(dimension_semantics=sem, vmem_limit_bytes=VMEM_LIMIT)


def _sigmoid(x):
    return 1.0 / (1.0 + jnp.exp(-x))


def _dot(a, b):
    return jnp.dot(a, b, preferred_element_type=F32)


def _dot_nt(a, b):
    return lax.dot_general(a, b, (((1,), (1,)), ((), ())), preferred_element_type=F32)


def _dot_tn(a, b):
    return lax.dot_general(a, b, (((0,), (0,)), ((), ())), preferred_element_type=F32)


def _split_bf16(x):
    hi = x.astype(BF16)
    lo = (x - hi.astype(F32)).astype(BF16)
    return hi, lo


def _pack_pair(a, b):
    ua = lax.bitcast_convert_type(a.astype(BF16).astype(F32), jnp.uint32)
    ub = lax.bitcast_convert_type(b.astype(BF16).astype(F32), jnp.uint32)
    return ub | (ua >> 16)


def _unpack_pair(w):
    lo = lax.bitcast_convert_type(w << 16, F32)
    hi = lax.bitcast_convert_type(w & jnp.uint32(0xFFFF0000), F32)
    return lo, hi


def _store_packed(ref, val, n_rows):
    d = val.shape[1]
    slabs = d // (2 * LANES)
    for s in range(slabs):
        a = val[:, s * LANES:(s + 1) * LANES]
        b = val[:, d // 2 + s * LANES:d // 2 + (s + 1) * LANES]
        ref[pl.ds(s, n_rows, stride=slabs), :] = _pack_pair(a, b)


def _load_packed(ref, n_rows, slabs):
    return [_unpack_pair(ref[pl.ds(s, n_rows, stride=slabs), :]) for s in range(slabs)]


def _side_cast(src_ref, dst_ref, regroup):
    for e in range(src_ref.shape[0]):
        if regroup:
            slabs = dst_ref.shape[1]
            half = slabs * LANES
            for s in range(slabs):
                dst_ref[e, s, :LANES, :] = src_ref[e, s * LANES:(s + 1) * LANES, :].astype(BF16)
                dst_ref[e, s, LANES:, :] = src_ref[e, half + s * LANES:half + (s + 1) * LANES, :].astype(BF16)
        else:
            dst_ref[e] = src_ref[e].astype(BF16)


def _side_io(w, n_steps, step_of, regroup):
    n_e, rows, cols = w.shape
    per = n_e // n_steps
    assert per * n_steps == n_e, (n_e, n_steps)
    in_spec = pl.BlockSpec((per, rows, cols), lambda *i: (step_of(*i), 0, 0))
    if regroup:
        shape = (n_e, rows // (2 * LANES), 2 * LANES, cols)
        out_spec = pl.BlockSpec((per,) + shape[1:], lambda *i: (step_of(*i), 0, 0, 0))
    else:
        shape = w.shape
        out_spec = pl.BlockSpec((per, rows, cols), lambda *i: (step_of(*i), 0, 0))
    return in_spec, out_spec, jax.ShapeDtypeStruct(shape, BF16)


def _ada_kernel(c_ref, w_ref, b_ref, o_ref):
    c = c_ref[...]
    bsz = c.shape[0]
    c_hi, c_lo = _split_bf16(c * _sigmoid(c))
    w_hi, w_lo = _split_bf16(w_ref[...])
    first = _dot(jnp.concatenate([c_hi, c_lo], axis=0), w_hi)
    o_ref[...] = first[:bsz] + first[bsz:] + _dot(c_hi, w_lo) + b_ref[...]


def _ada(c, ada_w, ada_b):
    bsz, d = c.shape
    n = ada_w.shape[1]
    tn = 1024
    return pl.pallas_call(
        _ada_kernel,
        grid=(n // tn,),
        in_specs=[pl.BlockSpec((bsz, d), lambda j: (0, 0)),
                  pl.BlockSpec((d, tn), lambda j: (0, j)),
                  pl.BlockSpec((1, tn), lambda j: (0, j))],
        out_specs=pl.BlockSpec((bsz, tn), lambda j: (0, j)),
        out_shape=jax.ShapeDtypeStruct((bsz, n), F32),
        compiler_params=_cparams("arbitrary"),
        name="ada_mod",
    )(c, ada_w, ada_b.reshape(1, n))


def _rms(x, g):
    return x * lax.rsqrt(jnp.mean(x * x, axis=-1, keepdims=True) + RMS_EPS) * g


def _norm_proj_kernel(x_ref, g_ref, sc_ref, sh_ref, w_ref, side_ref, o_ref, side_out):
    h = _rms(x_ref[...], g_ref[...]) * (1.0 + sc_ref[0]) + sh_ref[0]
    o_ref[...] = _dot(h.astype(BF16), w_ref[...]).astype(o_ref.dtype)
    _side_cast(side_ref, side_out, regroup=False)


def _norm_proj(x2, g, sc, sh, w, side_w, seqlen, tm, name):
    n, d = x2.shape
    nout = w.shape[1]
    tpb = seqlen // tm
    side_in, side_out, side_shape = _side_io(side_w, n // tm, lambda i: i, regroup=False)
    return pl.pallas_call(
        _norm_proj_kernel,
        grid=(n // tm,),
        in_specs=[pl.BlockSpec((tm, d), lambda i: (i, 0)),
                  pl.BlockSpec((1, d), lambda i: (0, 0)),
                  pl.BlockSpec((1, 1, d), lambda i: (i // tpb, 0, 0)),
                  pl.BlockSpec((1, 1, d), lambda i: (i // tpb, 0, 0)),
                  pl.BlockSpec((d, nout), lambda i: (0, 0), pipeline_mode=pl.Buffered(1)),
                  side_in],
        out_specs=[pl.BlockSpec((tm, nout), lambda i: (i, 0)), side_out],
        out_shape=[jax.ShapeDtypeStruct((n, nout), BF16), side_shape],
        compiler_params=_cparams("parallel"),
        name=name,
    )(x2, g.reshape(1, d), sc[:, None, :], sh[:, None, :], w, side_w)


def _s5_tables(lam_re, lam_im, log_dt, b_re, b_im, c_re, c_im):
    g, p, cg = b_re.shape
    gs = S5_SLAB // cg
    ns = g // gs
    lr = jnp.minimum(lam_re.astype(F32), -1e-4)
    li = lam_im.astype(F32)
    dt = jnp.exp(log_dt.astype(F32))[:, None]
    mag = jnp.exp(lr * dt)
    ar, ai = mag * jnp.cos(li * dt), mag * jnp.sin(li * dt)
    den = lr * lr + li * li
    qr = ((ar - 1.0) * lr + ai * li) / den
    qi = (ai * lr - (ar - 1.0) * li) / den
    br, bi = b_re.astype(F32), b_im.astype(F32)
    bbr = qr[..., None] * br - qi[..., None] * bi
    bbi = qr[..., None] * bi + qi[..., None] * br

    def in_rows(t):
        return t.reshape(ns, gs, p, cg).transpose(0, 1, 3, 2).reshape(ns, gs * cg, p).astype(BF16)

    def out_cols(t):
        return t.reshape(ns, gs, cg, p).transpose(0, 3, 1, 2).reshape(ns, p, gs * cg).astype(BF16)

    a_tab = jnp.stack([ar.reshape(ns, gs * p), ai.reshape(ns, gs * p)], axis=1)
    return (in_rows(bbr), in_rows(bbi), out_cols(c_re.astype(F32)), out_cols(-c_im.astype(F32)), a_tab)


def _gelu_tanh(x):
    return 0.5 * x * (1.0 + jnp.tanh(math.sqrt(2.0 / math.pi) * (x + 0.044715 * (x * x * x))))


def _s5_kernel(x_ref, g_ref, sc_ref, sh_ref, wu_ref, perm_ref, bre_ref, bim_ref, cre_ref, cim_ref, a_ref,
               d_ref, gw_ref, gb_ref, o_ref, b_ref, c_ref, u_scr, bscr, sscr, yscr, st_ref):
    bsz, lb, d_in = x_ref.shape
    rows = bsz * lb
    dch = wu_ref.shape[1]
    half = st_ref.shape[2] // 2
    n_p = bre_ref.shape[2]

    @pl.when(pl.program_id(0) == 0)
    def _():
        st_ref[...] = jnp.zeros_like(st_ref)
        tile_in = (lax.broadcasted_iota(jnp.int32, (n_p, half), 0)
                   == lax.broadcasted_iota(jnp.int32, (n_p, half), 1) % n_p).astype(BF16)
        tile_out = (lax.broadcasted_iota(jnp.int32, (half, n_p), 0) % n_p
                    == lax.broadcasted_iota(jnp.int32, (half, n_p), 1)).astype(BF16)
        in_mask = (lax.broadcasted_iota(jnp.int32, (S5_SLAB, half), 0) // SSM_GROUP
                   == lax.broadcasted_iota(jnp.int32, (S5_SLAB, half), 1) // n_p)
        out_mask = (lax.broadcasted_iota(jnp.int32, (half, S5_SLAB), 0) // n_p
                    == lax.broadcasted_iota(jnp.int32, (half, S5_SLAB), 1) // SSM_GROUP)
        for s in range(dch // S5_SLAB):
            b_ref[s, :, :half] = jnp.where(in_mask, _dot(bre_ref[s], tile_in), 0.0).astype(BF16)
            b_ref[s, :, half:] = jnp.where(in_mask, _dot(bim_ref[s], tile_in), 0.0).astype(BF16)
            c_ref[s, :half, :] = jnp.where(out_mask, _dot(tile_out, cre_ref[s]), 0.0).astype(BF16)
            c_ref[s, half:, :] = jnp.where(out_mask, _dot(tile_out, cim_ref[s]), 0.0).astype(BF16)

    h = (_rms(x_ref[...], g_ref[...]) * (1.0 + sc_ref[...]) + sh_ref[...]).astype(BF16)
    u_nat = _dot(h.reshape(rows, d_in), wu_ref[...]).astype(BF16)
    u_scr[...] = _dot(perm_ref[...], u_nat).astype(BF16)
    n_slab = dch // S5_SLAB
    slab = lambda s: slice(s * S5_SLAB, (s + 1) * S5_SLAB)

    def project_in(s):
        bscr[s % 2] = _dot(u_scr[:, slab(s)], b_ref[s])

    project_in(0)
    for s in range(n_slab):
        if s + 1 < n_slab:
            project_in(s + 1)
        buf = s % 2
        a_re = a_ref[s, 0:1, :]
        a_im = a_ref[s, 1:2, :]
        s_r = st_ref[s, :, :half]
        s_i = st_ref[s, :, half:]
        for l in range(rows // bsz):
            at_l = slice(l * bsz, (l + 1) * bsz)
            s_r, s_i = (a_re * s_r - a_im * s_i + bscr[buf, at_l, :half],
                        a_re * s_i + a_im * s_r + bscr[buf, at_l, half:])
            sscr[buf, at_l, :half] = s_r
            sscr[buf, at_l, half:] = s_i
        st_ref[s, :, :half] = s_r
        st_ref[s, :, half:] = s_i
        y = _dot(sscr[buf].astype(BF16), c_ref[s]) + d_ref[:, slab(s)] * u_scr[:, slab(s)].astype(F32)
        yscr[:, slab(s)] = _gelu_tanh(y).astype(BF16)
    y = yscr[...]
    gate = _sigmoid(_dot(y, gw_ref[...]) + gb_ref[...])
    out_tm = (y.astype(F32) * gate).astype(BF16)
    out_nat = _dot_tn(perm_ref[...], out_tm).astype(o_ref.dtype)
    o_ref[...] = out_nat.reshape(bsz, lb, dch)


def _s5_glu(x2, g, sc, sh, w_u, tabs, d_skip, glu_w, glu_b, bsz, seqlen, lb):
    b_re, b_im, c_re, c_im, a_tab = tabs
    d_in, dch = w_u.shape
    ns, _, n_half = a_tab.shape
    n_state = 2 * n_half
    rows = lb * bsz
    r_idx = jnp.arange(rows)
    perm = ((r_idx % bsz) * lb + r_idx // bsz)[:, None] == r_idx[None, :]
    full = lambda a: pl.BlockSpec(a.shape, lambda i: (0,) * a.ndim, pipeline_mode=pl.Buffered(1))
    args = (x2.reshape(bsz, seqlen, d_in), g.reshape(1, 1, d_in), sc[:, None, :], sh[:, None, :], w_u,
            perm.astype(BF16), b_re, b_im, c_re, c_im, a_tab,
            d_skip.astype(F32).reshape(1, dch), glu_w.astype(BF16), glu_b.astype(F32).reshape(1, dch))
    y = pl.pallas_call(
        _s5_kernel,
        grid=(seqlen // lb,),
        in_specs=[pl.BlockSpec((bsz, lb, d_in), lambda i: (0, i, 0))] + [full(a) for a in args[1:]],
        out_specs=pl.BlockSpec((bsz, lb, dch), lambda i: (0, i, 0)),
        out_shape=jax.ShapeDtypeStruct((bsz, seqlen, dch), BF16),
        scratch_shapes=[pltpu.VMEM((ns, S5_SLAB, n_state), BF16), pltpu.VMEM((ns, n_state, S5_SLAB), BF16),
                        pltpu.VMEM((rows, dch), BF16), pltpu.VMEM((2, rows, n_state), F32),
                        pltpu.VMEM((2, rows, n_state), F32), pltpu.VMEM((rows, dch), BF16),
                        pltpu.VMEM((ns, bsz, n_state), F32)],
        compiler_params=_cparams("arbitrary"),
        name="s5_mixer_glu",
    )(*args)
    return y.reshape(bsz * seqlen, dch)


def _rwkv_kernel(z_ref, mu_ref, w0_ref, a0_ref, kkw_ref, ka_ref, wa_ref, gup_ref, rk_ref, lnw_ref, lnb_ref,
                 ones_ref, side1_ref, side3_ref, o_ref, side1_out, side3_out, s_ref, zlast_ref):
    _side_cast(side1_ref, side1_out, regroup=True)
    _side_cast(side3_ref, side3_out, regroup=True)
    nb, t, _ = z_ref.shape
    d_r = o_ref.shape[2]
    tile = HEADS_PER_TILE * RWKV_HEAD
    n_tiles = d_r // tile
    first_chunk = pl.program_id(1) == 0

    @pl.when(first_chunk)
    def _():
        s_ref[...] = jnp.zeros_like(s_ref)

    ones_bd = ones_ref[...]

    def seg_sum(x):
        return _dot(x.astype(BF16), ones_bd)

    row = lax.broadcasted_iota(jnp.int32, (t, t), 0)
    col = lax.broadcasted_iota(jnp.int32, (t, t), 1)
    tri = (row >= col).astype(BF16)
    st = HEADS_PER_TILE * t
    rs = lax.broadcasted_iota(jnp.int32, (2 * t, 2 * st), 0)
    t_r = rs % t
    t_c = lax.broadcasted_iota(jnp.int32, (2 * t, 2 * st), 1) % t
    keep = (t_r > t_c) | ((rs >= t) & (t_r == t_c))
    eye_w = (lax.broadcasted_iota(jnp.int32, (t, st), 0)
             == lax.broadcasted_iota(jnp.int32, (t, st), 1) % t).astype(F32)
    blk_mask = (lax.broadcasted_iota(jnp.int32, (st, st), 0) // t
                == lax.broadcasted_iota(jnp.int32, (st, st), 1) // t)
    lane = lax.broadcasted_iota(jnp.int32, (1, tile), 1)
    head_masks = [(lane >= j * RWKV_HEAD) & (lane < (j + 1) * RWKV_HEAD) for j in range(HEADS_PER_TILE)]
    bd_r = lax.broadcasted_iota(jnp.int32, (tile, tile), 0) // RWKV_HEAD
    bd_c = lax.broadcasted_iota(jnp.int32, (tile, tile), 1) // RWKV_HEAD
    bd_mask = bd_r == bd_c
    n_levels = int(math.log2(t))
    slices = [slice(hg * tile, (hg + 1) * tile) for hg in range(n_tiles)]
    units = [(bb, hg) for bb in range(nb) for hg in range(n_tiles)]
    n_u = range(len(units))

    def stack(x):
        zero = jnp.zeros_like(x)
        return jnp.concatenate([jnp.where(m, x, zero) for m in head_masks], axis=0)

    def bf(x):
        return x.astype(BF16)

    at, qt, bt, kt, vv, em, etm, wtot, rkb, gate = ([] for _ in range(10))
    row0 = lax.broadcasted_iota(jnp.int32, (t, 1), 0) == 0
    lora_lane = lax.broadcasted_iota(jnp.int32, (t, LORA_W + LORA_A), 1)
    for bb in range(nb):
        z = z_ref[bb].astype(F32)
        prev_row = jnp.where(first_chunk, 0.0, zlast_ref[bb, 0:1, :])
        zl = z + mu_ref[...] * (jnp.where(row0, prev_row, pltpu.roll(z, 1, 0)) - z)
        zlast_ref[bb, 0:1, :] = z[t - 1:t, :]
        xwa = zl[:, 3 * d_r:3 * d_r + LORA_W + LORA_A]
        lhs = jnp.where(lora_lane < LORA_W, jnp.tanh(xwa), xwa).astype(BF16)
        wa = _dot(lhs, wa_ref[...])
        lw = -DECAY_SCALE * _sigmoid(w0_ref[...] + wa[:, :d_r])
        asig = _sigmoid(a0_ref[...] + wa[:, d_r:])
        r = zl[:, :d_r]
        k_raw = zl[:, d_r:2 * d_r]
        kk = k_raw * kkw_ref[...]
        kp = k_raw * (1.0 + (asig - 1.0) * ka_ref[...])
        xg = zl[:, 3 * d_r + LORA_W + LORA_A:3 * d_r + LORA_W + LORA_A + LORA_G]
        g_full = _dot(_sigmoid(xg).astype(BF16), gup_ref[...])
        lw_hi, lw_lo = _split_bf16(lw)
        cum = _dot(tri, lw_hi) + _dot(tri, lw_lo)
        mid = cum[t // 2 - 1:t // 2, :]
        tot = cum[t - 1:t, :]
        e1 = jnp.exp(cum - mid)
        e2 = jnp.exp(mid - cum)
        e1p = e1 * jnp.exp(-lw)
        kk2 = kk * kk
        kkn = kk / jnp.maximum(jnp.sqrt(jnp.concatenate([seg_sum(kk2[:, sl]) for sl in slices], axis=1)), 1e-12)
        full = dict(at=-kkn * e1p, qt=r * e1, bt=kkn * asig * e2, kt=kp * e2,
                    vv=zl[:, 2 * d_r:3 * d_r], em=jnp.exp(mid), etm=jnp.exp(tot - mid), wtot=jnp.exp(tot),
                    rkb=r * kp * rk_ref[...], gate=g_full)
        for dst, key in ((at, "at"), (qt, "qt"), (bt, "bt"), (kt, "kt"), (vv, "vv"), (em, "em"),
                         (etm, "etm"), (wtot, "wtot"), (rkb, "rkb"), (gate, "gate")):
            dst.extend(full[key][:, sl] for sl in slices)

    s_old = [s_ref[i] for i in n_u]
    wide = []
    for i in n_u:
        lhs = bf(jnp.concatenate([at[i], qt[i]], axis=0))
        rhs = jnp.concatenate([stack(bf(bt[i])), stack(bf(kt[i]))], axis=0)
        a_w = bf(jnp.where(keep, _dot_nt(lhs, rhs), 0.0))
        wide.append([a_w[:t, :st], a_w[:t, st:], a_w[t:, :]])
    x_state = [_dot_nt(bf(jnp.concatenate([at[i], qt[i]], axis=0) * em[i]), bf(s_old[i]))
               for i in n_u]
    sv = [stack(bf(vv[i])) for i in n_u]
    akv = [_dot(wide[i][1], sv[i]) for i in n_u]

    def expand(x_w):
        x_b = bf(x_w)
        return jnp.where(blk_mask, jnp.concatenate([x_b] * HEADS_PER_TILE, axis=0), jnp.zeros((st, st), BF16))

    p_acc = [eye_w + w[0].astype(F32) for w in wide]
    q_bd = [expand(w[0]) for w in wide]
    q_pow = [_dot(wide[i][0], q_bd[i]) for i in n_u]
    for lev in range(1, n_levels):
        for i in n_u:
            q_bd[i] = expand(q_pow[i])
            if lev < n_levels - 1:
                both = _dot(bf(jnp.concatenate([p_acc[i], q_pow[i]], axis=0)), q_bd[i])
                p_acc[i] = p_acc[i] + both[:t]
                q_pow[i] = both[t:]
            else:
                p_acc[i] = p_acc[i] + _dot(bf(p_acc[i]), q_bd[i])
    u_all = [_dot(bf(p_acc[i]), stack(bf(x_state[i][:t] + akv[i]))) for i in n_u]
    y_all = [x_state[i][t:]
             + _dot(wide[i][2], jnp.concatenate([stack(bf(u_all[i])), sv[i]], axis=0)) for i in n_u]

    for i, (bb, hg) in enumerate(units):
        sl = slices[hg]
        y = y_all[i]
        uv = bf(jnp.concatenate([u_all[i], vv[i]], axis=0))
        bk_end = bf(jnp.concatenate([bt[i], kt[i]], axis=0) * etm[i])
        s_ref[i] = s_old[i] * wtot[i] + jnp.where(bd_mask, _dot_tn(uv, bk_end), 0.0)

        mean = seg_sum(y) * (1.0 / RWKV_HEAD)
        dlt = y - mean
        var = seg_sum(dlt * dlt) * (1.0 / RWKV_HEAD)
        yn = dlt * lax.rsqrt(var + GN_EPS) * lnw_ref[:, sl] + lnb_ref[:, sl]
        out = (yn + seg_sum(rkb[i]) * vv[i]) * gate[i]
        o_ref[bb, :, sl] = out.astype(o_ref.dtype)


def _rwkv(z, mu, w0, w_up, a0, a_up, g_up, k_k, k_a, r_k, ln_w, ln_b, side_w1, side_w3, bsz, seqlen):
    n, dz = z.shape
    d_r = w0.shape[0]
    t = RWKV_CHUNK
    nb = RWKV_BATCH_PER_STEP
    nch = seqlen // t
    tile = HEADS_PER_TILE * RWKV_HEAD
    hid = jnp.arange(tile) // RWKV_HEAD
    ones_bd = (hid[:, None] == hid[None, :]).astype(BF16)
    wa = jnp.zeros((LORA_W + LORA_A, 2 * d_r), F32)
    wa = wa.at[:LORA_W, :d_r].set(w_up.astype(F32)).at[LORA_W:, d_r:].set(a_up.astype(F32)).astype(BF16)
    row = lambda a: a.astype(F32).reshape(1, -1)
    full = lambda a: pl.BlockSpec(a.shape, lambda b, c: (0, 0))
    params = (row(mu), row(w0), row(a0), row(k_k), row(k_a), wa, g_up.astype(BF16), row(r_k), row(ln_w),
              row(ln_b), ones_bd)
    side_in, side_out, side_shape = _side_io(side_w1, (bsz // nb) * nch, lambda b, c: b * nch + c, regroup=True)
    y, side1, side3 = pl.pallas_call(
        _rwkv_kernel,
        grid=(bsz // nb, nch),
        in_specs=([pl.BlockSpec((nb, t, dz), lambda b, c: (b, c, 0))] + [full(a) for a in params]
                  + [side_in, side_in]),
        out_specs=[pl.BlockSpec((nb, t, d_r), lambda b, c: (b, c, 0)), side_out, side_out],
        out_shape=[jax.ShapeDtypeStruct((bsz, seqlen, d_r), BF16), side_shape, side_shape],
        scratch_shapes=[pltpu.VMEM((nb * (d_r // tile), tile, tile), F32), pltpu.VMEM((nb, 8, dz), F32)],
        compiler_params=_cparams("parallel", "arbitrary"),
        name="rwkv7_chunked",
    )(z.reshape(bsz, seqlen, dz), *params, side_w1, side_w3)
    return y.reshape(n, d_r), side1, side3


def _post_mix_kernel(ys_ref, yr_ref, wo1_ref, wo2_ref, x_ref, g1_ref, gt_ref, g2_ref, sc_ref, sh_ref,
                     wr_both_ref, br_ref, x1_out, h2_out, lg_out):
    nl = lg_out.shape[1]
    mixed = _dot(ys_ref[...], wo1_ref[...]) + _dot(yr_ref[...], wo2_ref[...])
    x1 = x_ref[...] + gt_ref[0] * _rms(mixed, g1_ref[...])
    x1_out[...] = x1
    h2 = _rms(x1, g2_ref[...]) * (1.0 + sc_ref[0]) + sh_ref[0]
    _store_packed(h2_out, h2, h2.shape[0])
    hi, lo = _split_bf16(h2)
    both = _dot(hi, wr_both_ref[...])
    lg_out[...] = both[:, :nl] + both[:, nl:] + _dot(lo, wr_both_ref[:, :nl]) + br_ref[...]


def _post_mix(ys, yr, w_out, x2, g1, gt1, g2, sc2, sh2, w_route, b_route, seqlen, tm):
    n, d = x2.shape
    ds = ys.shape[1]
    tpb = seqlen // tm
    slabs = d // (2 * LANES)
    wr_both = jnp.concatenate(_split_bf16(w_route), axis=1)
    rows = lambda w: pl.BlockSpec((tm, w), lambda i: (i, 0))
    full = lambda a: pl.BlockSpec(a.shape, lambda i: (0,) * a.ndim)
    bat = pl.BlockSpec((1, 1, d), lambda i: (i // tpb, 0, 0))
    args = (ys, yr, w_out[:ds].astype(BF16), w_out[ds:].astype(BF16), x2, g1.reshape(1, d), gt1[:, None, :],
            g2.reshape(1, d), sc2[:, None, :], sh2[:, None, :], wr_both, b_route.reshape(1, -1))
    in_specs = [rows(ds), rows(yr.shape[1]), full(args[2]), full(args[3]), rows(d), full(args[5]), bat,
                full(args[7]), bat, bat, full(wr_both), full(args[11])]
    return pl.pallas_call(
        _post_mix_kernel,
        grid=(n // tm,),
        in_specs=in_specs,
        out_specs=[rows(d), pl.BlockSpec((tm * slabs, LANES), lambda i: (i, 0)), rows(ROUTE_LANES)],
        out_shape=[jax.ShapeDtypeStruct((n, d), F32), jax.ShapeDtypeStruct((n * slabs, LANES), jnp.uint32),
                   jax.ShapeDtypeStruct((n, ROUTE_LANES), F32)],
        compiler_params=_cparams("parallel"),
        name="out_proj_post",
    )(*args)


def _route_kernel(lg_ref, info_ref, cnt_ref, carry):
    i = pl.program_id(0)
    tm = lg_ref.shape[0]

    @pl.when(i == 0)
    def _():
        carry[...] = jnp.zeros_like(carry)

    lg = lg_ref[...]
    lane = lax.broadcasted_iota(jnp.int32, lg.shape, 1)
    lane_f = lane.astype(F32)
    neg = jnp.float32(-jnp.inf)
    big = jnp.float32(1e9)
    is_g = (lane >= N_EXPERTS) & (lane < N_EXPERTS + N_GROUPS)
    gl = jnp.where(is_g, lg, neg)
    gmax = jnp.max(gl, axis=-1, keepdims=True)
    gidx = jnp.min(jnp.where(gl == gmax, lane_f - N_EXPERTS, big), axis=-1, keepdims=True)
    p_grp = 1.0 / jnp.sum(jnp.where(is_g, jnp.exp(gl - gmax), 0.0), axis=-1, keepdims=True)
    in_grp = (lane < N_EXPERTS) & ((lane // EXPERTS_PER_GROUP).astype(F32) == gidx)
    el = jnp.where(in_grp, lg, neg)
    m1 = jnp.max(el, axis=-1, keepdims=True)
    i1 = jnp.min(jnp.where(el == m1, lane_f, big), axis=-1, keepdims=True)
    el2 = jnp.where(lane_f == i1, neg, el)
    m2 = jnp.max(el2, axis=-1, keepdims=True)
    i2 = jnp.min(jnp.where(el2 == m2, lane_f, big), axis=-1, keepdims=True)
    ex = jnp.exp(m2 - m1)
    w1 = p_grp / (1.0 + ex)
    w2 = p_grp * ex / (1.0 + ex)

    oh1 = lane_f == i1
    oh2 = lane_f == i2
    onehot = (oh1 | oh2).astype(BF16)
    rr = lax.broadcasted_iota(jnp.int32, (tm, tm), 0)
    cc = lax.broadcasted_iota(jnp.int32, (tm, tm), 1)
    before = _dot((rr > cc).astype(BF16), onehot) + carry[...]
    rank1 = jnp.sum(jnp.where(oh1, before, 0.0), axis=-1, keepdims=True)
    rank2 = jnp.sum(jnp.where(oh2, before, 0.0), axis=-1, keepdims=True)
    carry[...] = carry[...] + jnp.sum(onehot.astype(F32), axis=0, keepdims=True)
    cnt_ref[...] = carry[...]

    info = jnp.where(lane == 0, i1, 0.0)
    info = jnp.where(lane == 1, i2, info)
    info = jnp.where(lane == 2, w1, info)
    info = jnp.where(lane == 3, w2, info)
    info = jnp.where(lane == 4, rank1, info)
    info = jnp.where(lane == 5, rank2, info)
    info_ref[...] = info


def _route(logits, tm):
    n = logits.shape[0]
    return pl.pallas_call(
        _route_kernel,
        grid=(n // tm,),
        in_specs=[pl.BlockSpec((tm, ROUTE_LANES), lambda i: (i, 0))],
        out_specs=[pl.BlockSpec((tm, ROUTE_LANES), lambda i: (i, 0)),
                   pl.BlockSpec((1, ROUTE_LANES), lambda i: (0, 0))],
        out_shape=[jax.ShapeDtypeStruct((n, ROUTE_LANES), F32),
                   jax.ShapeDtypeStruct((1, ROUTE_LANES), F32)],
        scratch_shapes=[pltpu.VMEM((1, ROUTE_LANES), F32)],
        compiler_params=_cparams("arbitrary"),
        name="moe_route",
    )(logits)


def _slot_rows_kernel(info_ref, seg_ref, o_ref, *, slabs):
    info = info_ref[...]
    lane = lax.broadcasted_iota(jnp.int32, info.shape, 1)
    lane_f = lane.astype(F32)
    seg = seg_ref[...]
    d0 = jnp.sum(jnp.where(lane_f == info[:, 0:1], seg, 0.0), axis=-1, keepdims=True) + info[:, 4:5] * slabs
    d1 = jnp.sum(jnp.where(lane_f == info[:, 1:2], seg, 0.0), axis=-1, keepdims=True) + info[:, 5:6] * slabs
    o_ref[...] = jnp.where(lane == 0, d0, jnp.where(lane == 1, d1, 0.0)).astype(jnp.int32)


def _slot_rows(info, seg_row, slabs, tm):
    n = info.shape[0]
    return pl.pallas_call(
        functools.partial(_slot_rows_kernel, slabs=slabs),
        grid=(n // tm,),
        in_specs=[pl.BlockSpec((tm, ROUTE_LANES), lambda i: (i, 0)),
                  pl.BlockSpec((1, ROUTE_LANES), lambda i: (0, 0))],
        out_specs=pl.BlockSpec((tm, ROUTE_LANES), lambda i: (i, 0)),
        out_shape=jax.ShapeDtypeStruct((n, ROUTE_LANES), jnp.int32),
        compiler_params=_cparams("parallel"),
        name="moe_slot_rows",
    )(info, seg_row)


def _dispatch_kernel(d0_ref, d1_ref, pad_row_ref, pad_len_ref, na_ref, h_ref, buf_out, zeros, sem, zsem, *,
                     slabs, n_blocks):
    tm = h_ref.shape[0] // slabs
    bits = [1 << b for b in reversed(range(EXPERT_BLOCK.bit_length() - 1))]

    def pad_copy(e, bit):
        done = pad_len_ref[e] & ~(2 * bit - 1)
        dst = pl.multiple_of(pad_row_ref[e] + done * slabs, slabs)
        return pltpu.make_async_copy(zeros.at[pl.ds(0, bit * slabs)], buf_out.at[pl.ds(dst, bit * slabs)], zsem)

    def block_copy(j):
        dst = pl.multiple_of(j * (EXPERT_BLOCK * slabs), EXPERT_BLOCK * slabs)
        return pltpu.make_async_copy(zeros, buf_out.at[pl.ds(dst, EXPERT_BLOCK * slabs)], zsem)

    def each_fill(action):
        def per_expert(e, c):
            for bit in bits:
                @pl.when((pad_len_ref[e] & bit) != 0)
                def _():
                    action(pad_copy(e, bit))
            return c

        def per_block(j, c):
            action(block_copy(j))
            return c

        lax.fori_loop(0, N_EXPERTS, per_expert, 0)
        lax.fori_loop(na_ref[0], n_blocks, per_block, 0)

    @pl.when(pl.program_id(0) == 0)
    def _():
        zeros[...] = jnp.zeros_like(zeros)
        each_fill(lambda c: c.start())

    def copy(t, dest_ref):
        src = pl.multiple_of(t * slabs, slabs)
        dst = pl.multiple_of(dest_ref[t], slabs)
        return pltpu.make_async_copy(h_ref.at[pl.ds(src, slabs)], buf_out.at[pl.ds(dst, slabs)], sem)

    def start(t, c):
        copy(t, d0_ref).start(priority=0)
        copy(t, d1_ref).start(priority=1)
        return c

    lax.fori_loop(0, tm, start, 0, unroll=4)
    for _ in range(2):
        pltpu.make_async_copy(h_ref, buf_out.at[pl.ds(0, tm * slabs)], sem).wait()

    @pl.when(pl.program_id(0) == pl.num_programs(0) - 1)
    def _():
        each_fill(lambda c: c.wait())


def _dispatch(h2p, dest_rows, pad_row, pad_len, n_active, cap, slabs, tm):
    n = h2p.shape[0] // slabs
    smem = pl.BlockSpec((tm,), lambda i: (i,), memory_space=pltpu.SMEM)
    table = lambda a: pl.BlockSpec(a.shape, lambda i: (0,), memory_space=pltpu.SMEM)
    return pl.pallas_call(
        functools.partial(_dispatch_kernel, slabs=slabs, n_blocks=cap // EXPERT_BLOCK),
        grid=(n // tm,),
        in_specs=[smem, smem, table(pad_row), table(pad_len), table(n_active),
                  pl.BlockSpec((tm * slabs, LANES), lambda i: (i, 0))],
        out_specs=pl.BlockSpec(memory_space=pl.ANY),
        out_shape=jax.ShapeDtypeStruct((cap * slabs, LANES), h2p.dtype),
        scratch_shapes=[pltpu.VMEM((EXPERT_BLOCK * slabs, LANES), h2p.dtype), pltpu.SemaphoreType.DMA,
                        pltpu.SemaphoreType.DMA],
        compiler_params=_cparams("arbitrary"),
        name="moe_dispatch",
    )(dest_rows[0], dest_rows[1], pad_row, pad_len, n_active, h2p)


def _moe_kernel(na_ref, eseq_ref, epos_ref, nd_ref, x_ref, w1_hbm, w3_hbm, w2_hbm, o_ref,
                w1b, w3b, w2b, sem):
    j = pl.program_id(0)
    active = j < na_ref[0]
    pos = epos_ref[j]
    fresh = (j == 0) | (pos != epos_ref[jnp.maximum(j - 1, 0)])
    slabs = w1b.shape[1]

    def weight_copies(p):
        e = eseq_ref[p]
        slot = p % 2
        return [pltpu.make_async_copy(w_hbm.at[e], stage.at[slot], sem.at[slot, i])
                for i, (w_hbm, stage) in enumerate(((w1_hbm, w1b), (w3_hbm, w3b), (w2_hbm, w2b)))]

    @pl.when(j == 0)
    def _():
        for c in weight_copies(0):
            c.start()

    @pl.when(active & fresh)
    def _():
        for c in weight_copies(pos):
            c.wait()

        @pl.when(pos + 1 < nd_ref[0])
        def _():
            for c in weight_copies(pos + 1):
                c.start()

    @pl.when(active)
    def _():
        slot = pos % 2
        acc1 = jnp.zeros((EXPERT_BLOCK, w1b.shape[3]), F32)
        acc3 = jnp.zeros((EXPERT_BLOCK, w1b.shape[3]), F32)
        for s, (lo, hi) in enumerate(_load_packed(x_ref, EXPERT_BLOCK, slabs)):
            lhs = jnp.concatenate([lo.astype(BF16), hi.astype(BF16)], axis=1)
            acc1 = acc1 + _dot(lhs, w1b[slot, s])
            acc3 = acc3 + _dot(lhs, w3b[slot, s])
        hid = (acc1 * _sigmoid(acc1)) * acc3
        _store_packed(o_ref, _dot(hid.astype(BF16), w2b[slot]), EXPERT_BLOCK)


def _moe(x_buf, n_active, expert_seq, block_pos, n_used, w1, w3, w2, slabs):
    cap = x_buf.shape[0] // slabs
    nb = cap // EXPERT_BLOCK

    def xmap(j, na, *_):
        return (jnp.minimum(j, na[0] - 1), 0)

    xspec = pl.BlockSpec((EXPERT_BLOCK * slabs, LANES), xmap)
    hbm = pl.BlockSpec(memory_space=pl.ANY)
    grid_spec = pltpu.PrefetchScalarGridSpec(
        num_scalar_prefetch=4,
        grid=(nb,),
        in_specs=[xspec, hbm, hbm, hbm],
        out_specs=xspec,
        scratch_shapes=[pltpu.VMEM((2,) + w1.shape[1:], BF16), pltpu.VMEM((2,) + w3.shape[1:], BF16),
                        pltpu.VMEM((2,) + w2.shape[1:], BF16), pltpu.SemaphoreType.DMA((2, 3))],
    )
    return pl.pallas_call(
        _moe_kernel,
        grid_spec=grid_spec,
        out_shape=jax.ShapeDtypeStruct(x_buf.shape, x_buf.dtype),
        input_output_aliases={4: 0},
        compiler_params=_cparams("arbitrary"),
        name="moe_experts",
    )(n_active, expert_seq, block_pos, n_used, x_buf, w1, w3, w2)


def _combine_kernel(d0_ref, d1_ref, d0n_ref, d1n_ref, y_ref, info_ref, x1_ref, g_ref, gt_ref, o_ref,
                    rows, sem, *, slabs, tiles_per_idx, chunk):
    i = pl.program_id(0)
    n_tiles = pl.num_programs(0)
    tm = x1_ref.shape[0]
    slot = i % 2
    nxt_slot = 1 - slot
    nxt = jnp.minimum(i + 1, n_tiles - 1)

    def copy(tile, to_slot, t, dest_ref, k):
        off = (tile % tiles_per_idx) * tm
        src = pl.multiple_of(dest_ref[off + t], slabs)
        dst = pl.multiple_of(t * slabs, slabs)
        return pltpu.make_async_copy(y_ref.at[pl.ds(src, slabs)], rows.at[to_slot, k, pl.ds(dst, slabs)],
                                     sem.at[to_slot])

    def drain(which):
        for k in range(2):
            pltpu.make_async_copy(y_ref.at[pl.ds(0, tm * slabs)], rows.at[which, k], sem.at[which]).wait()

    @pl.when(i == 0)
    def _():
        def start(t, c):
            copy(i, slot, t, d0_ref, 0).start(priority=0)
            copy(i, slot, t, d1_ref, 1).start(priority=1)
            return c

        lax.fori_loop(0, tm, start, 0, unroll=4)

    drain(slot)
    g_row = g_ref[...]
    gt_row = gt_ref[0]
    for c in range(tm // chunk):
        tok = slice(c * chunk, (c + 1) * chunk)
        info = info_ref[tok, :]
        w1 = info[:, 2:3]
        w2 = info[:, 3:4]
        lo_parts, hi_parts = [], []
        for s in range(slabs):
            at_s = pl.ds(c * chunk * slabs + s, chunk, stride=slabs)
            lo0, hi0 = _unpack_pair(rows[slot, 0, at_s, :])
            lo1, hi1 = _unpack_pair(rows[slot, 1, at_s, :])
            lo_parts.append(w1 * lo0 + w2 * lo1)
            hi_parts.append(w1 * hi0 + w2 * hi1)
        ffn = jnp.concatenate(lo_parts + hi_parts, axis=1)
        o_ref[tok, :] = x1_ref[tok, :] + gt_row * _rms(ffn, g_row)
        for t in range(c * chunk, (c + 1) * chunk):
            copy(nxt, nxt_slot, t, d0n_ref, 0).start(priority=0)
            copy(nxt, nxt_slot, t, d1n_ref, 1).start(priority=1)

    @pl.when(i == n_tiles - 1)
    def _():
        drain(nxt_slot)


def _combine(y_buf, dest_rows, info, x1, g, gt2, seqlen, slabs, tm):
    n, d = x1.shape
    tpb = seqlen // tm
    idx_block = max(tm, SMEM_INDEX_BLOCK)
    per = idx_block // tm
    last = n // tm - 1
    smem = pl.BlockSpec((idx_block,), lambda i: (i // per,), memory_space=pltpu.SMEM)
    smem_next = pl.BlockSpec((idx_block,), lambda i: (jnp.minimum(i + 1, last) // per,), memory_space=pltpu.SMEM)
    return pl.pallas_call(
        functools.partial(_combine_kernel, slabs=slabs, tiles_per_idx=per, chunk=min(tm, COMBINE_CHUNK)),
        grid=(n // tm,),
        in_specs=[smem, smem, smem_next, smem_next,
                  pl.BlockSpec(memory_space=pl.ANY),
                  pl.BlockSpec((tm, ROUTE_LANES), lambda i: (i, 0)),
                  pl.BlockSpec((tm, d), lambda i: (i, 0)),
                  pl.BlockSpec((1, d), lambda i: (0, 0)),
                  pl.BlockSpec((1, 1, d), lambda i: (i // tpb, 0, 0))],
        out_specs=pl.BlockSpec((tm, d), lambda i: (i, 0)),
        out_shape=jax.ShapeDtypeStruct((n, d), F32),
        scratch_shapes=[pltpu.VMEM((2, 2, tm * slabs, LANES), y_buf.dtype), pltpu.SemaphoreType.DMA((2,))],
        compiler_params=_cparams("arbitrary"),
        name="moe_combine",
    )(dest_rows[0], dest_rows[1], dest_rows[0], dest_rows[1], y_buf, info, x1, g.reshape(1, d),
      gt2[:, None, :])


def _pick(n, pref):
    while n % pref:
        pref //= 2
    return pref


def _layer(x2, mod, p, bsz, seqlen):
    n, d = x2.shape
    sh1, sc1, gt1, sh2, sc2, gt2 = jnp.split(mod, 6, axis=-1)
    d_ssm = p["ssm_d"].shape[0]

    z, w2_bf = _norm_proj(x2, p["norm_mix_pre"], sc1, sh1, p["w_in"][:, d_ssm:].astype(BF16), p["moe_w2"],
                          seqlen, _pick(seqlen, max(n // N_EXPERTS, 16)), "in_proj_z")

    tabs = _s5_tables(p["ssm_lam_re"], p["ssm_lam_im"], p["ssm_log_dt"], p["ssm_b_re"], p["ssm_b_im"],
                      p["ssm_c_re"], p["ssm_c_im"])
    y_ssm = _s5_glu(x2, p["norm_mix_pre"], sc1, sh1, p["w_in"][:, :d_ssm].astype(BF16), tabs, p["ssm_d"], p["glu_w"],
                    p["glu_b"], bsz, seqlen, _pick(seqlen, S5_TIME_BLOCK))

    y_rwkv, w1_bf, w3_bf = _rwkv(z, p["rwkv_mu"], p["rwkv_w0"], p["rwkv_w_up"], p["rwkv_a0"], p["rwkv_a_up"],
                                 p["rwkv_g_up"], p["rwkv_k_k"], p["rwkv_k_a"], p["rwkv_r_k"], p["rwkv_ln_w"],
                                 p["rwkv_ln_b"], p["moe_w1"], p["moe_w3"], bsz, seqlen)

    w_route = jnp.zeros((d, ROUTE_LANES), F32)
    w_route = w_route.at[:, :N_EXPERTS].set(p["moe_w_exp"].astype(F32))
    w_route = w_route.at[:, N_EXPERTS:N_EXPERTS + N_GROUPS].set(p["moe_w_grp"].astype(F32))
    b_route = jnp.zeros((ROUTE_LANES,), F32)
    b_route = b_route.at[:N_EXPERTS].set(p["moe_b_exp"].astype(F32))
    b_route = b_route.at[N_EXPERTS:N_EXPERTS + N_GROUPS].set(p["moe_b_grp"].astype(F32))
    x1, h2p, logits = _post_mix(y_ssm, y_rwkv, p["w_out"], x2, p["norm_mix_post"], gt1, p["norm_ffn_pre"],
                                sc2, sh2, w_route, b_route, seqlen, _pick(seqlen, 512))
    slabs = d // (2 * LANES)

    info, counts = _route(logits, _pick(n, 512))
    cnt = counts[0, :N_EXPERTS].astype(jnp.int32)
    padded = (cnt + EXPERT_BLOCK - 1) // EXPERT_BLOCK * EXPERT_BLOCK
    pend = jnp.cumsum(padded)
    pstart = pend - padded
    n_blocks = -(-(2 * n) // EXPERT_BLOCK) + N_EXPERTS
    cap = n_blocks * EXPERT_BLOCK
    seg_row = jnp.zeros((1, ROUTE_LANES), F32).at[0, :N_EXPERTS].set((pstart * slabs).astype(F32))
    dest = _slot_rows(info, seg_row, slabs, _pick(n, 2048))
    dest_rows = (dest[:, 0], dest[:, 1])
    n_active = (pend[-1:] // EXPERT_BLOCK).astype(jnp.int32)
    used = (cnt > 0).astype(jnp.int32)
    used_pos = jnp.cumsum(used) - 1
    slot_ids = jnp.arange(N_EXPERTS, dtype=jnp.int32)
    expert_seq = jnp.sum(jnp.where((used_pos[None, :] == slot_ids[:, None]) & (used[None, :] > 0),
                                   slot_ids[None, :], 0), axis=1).astype(jnp.int32)
    block_first = jnp.arange(n_blocks, dtype=jnp.int32) * EXPERT_BLOCK
    block_pos = jnp.sum(jnp.where(pend[None, :] <= block_first[:, None], used[None, :], 0),
                        axis=1).astype(jnp.int32)
    n_used = jnp.sum(used).reshape(1)

    x_buf = _dispatch(h2p, dest_rows, ((pstart + cnt) * slabs).astype(jnp.int32),
                      (padded - cnt).astype(jnp.int32), n_active, cap, slabs, _pick(n, 1024))
    y_buf = _moe(x_buf, n_active, expert_seq, block_pos, n_used, w1_bf, w3_bf, w2_bf, slabs)
    return _combine(y_buf, dest_rows, info, x1, p["norm_ffn_post"], gt2, seqlen, slabs, _pick(seqlen, 512))


def kernel(x, c, ada_w, ada_b, norm_mix_pre, norm_mix_post, norm_ffn_pre, norm_ffn_post, w_in, w_out, ssm_lam_re, ssm_lam_im, ssm_log_dt, ssm_b_re, ssm_b_im, ssm_c_re, ssm_c_im, ssm_d, glu_w, glu_b, rwkv_mu, rwkv_w0, rwkv_w_up, rwkv_a0, rwkv_a_up, rwkv_g_up, rwkv_k_k, rwkv_k_a, rwkv_r_k, rwkv_ln_w, rwkv_ln_b, moe_w_grp, moe_b_grp, moe_w_exp, moe_b_exp, moe_w1, moe_w3, moe_w2):
    bsz, seqlen, d = x.shape
    params = dict(norm_mix_pre=norm_mix_pre, norm_mix_post=norm_mix_post, norm_ffn_pre=norm_ffn_pre,
                  norm_ffn_post=norm_ffn_post, w_in=w_in, w_out=w_out, ssm_lam_re=ssm_lam_re,
                  ssm_lam_im=ssm_lam_im, ssm_log_dt=ssm_log_dt, ssm_b_re=ssm_b_re, ssm_b_im=ssm_b_im,
                  ssm_c_re=ssm_c_re, ssm_c_im=ssm_c_im, ssm_d=ssm_d, glu_w=glu_w, glu_b=glu_b,
                  rwkv_mu=rwkv_mu, rwkv_w0=rwkv_w0, rwkv_w_up=rwkv_w_up, rwkv_a0=rwkv_a0,
                  rwkv_a_up=rwkv_a_up, rwkv_g_up=rwkv_g_up, rwkv_k_k=rwkv_k_k, rwkv_k_a=rwkv_k_a,
                  rwkv_r_k=rwkv_r_k, rwkv_ln_w=rwkv_ln_w, rwkv_ln_b=rwkv_ln_b, moe_w_grp=moe_w_grp,
                  moe_b_grp=moe_b_grp, moe_w_exp=moe_w_exp, moe_b_exp=moe_b_exp, moe_w1=moe_w1,
                  moe_w3=moe_w3, moe_w2=moe_w2)
    x2 = x.reshape(bsz * seqlen, d)
    for layer in range(ada_w.shape[0]):
        mod = _ada(c, ada_w[layer], ada_b[layer])
        x2 = _layer(x2, mod, {k: v[layer] for k, v in params.items()}, bsz, seqlen)
    return x2.reshape(bsz, seqlen, d)
```

```python
import functools
import math

import jax
import jax.numpy as jnp
from jax import lax
from jax.experimental import pallas as pl
from jax.experimental.pallas import tpu as pltpu

F32 = jnp.float32
BF16 = jnp.bfloat16

SSM_GROUP = 16
S5_SLAB = 256
S5_TIME_BLOCK = 64
RWKV_HEAD = 64
RWKV_CHUNK = 64
RWKV_BATCH_PER_STEP = 4
HEADS_PER_TILE = 4
LORA_W = 64
LORA_A = 64
LORA_G = 128
N_GROUPS = 8
EXPERTS_PER_GROUP = 8
N_EXPERTS = N_GROUPS * EXPERTS_PER_GROUP
EXPERT_BLOCK = 256
RMS_EPS = 1e-6
GN_EPS = 64e-5
DECAY_SCALE = math.exp(-0.5)
LANES = 128
COMBINE_CHUNK = 64
SMEM_INDEX_BLOCK = 1024
ROUTE_LANES = LANES
V7X_VMEM_BYTES = 64 * 1024 * 1024
VMEM_LIMIT = V7X_VMEM_BYTES - 12 * 1024 * 1024


def _cparams(*sem):
    return pltpu.CompilerParams(dimension_semantics=sem, vmem_limit_bytes=VMEM_LIMIT)


def _sigmoid(x):
    return 1.0 / (1.0 + jnp.exp(-x))


def _dot(a, b):
    return jnp.dot(a, b, preferred_element_type=F32)


def _dot_nt(a, b):
    return lax.dot_general(a, b, (((1,), (1,)), ((), ())), preferred_element_type=F32)


def _dot_tn(a, b):
    return lax.dot_general(a, b, (((0,), (0,)), ((), ())), preferred_element_type=F32)


def _split_bf16(x):
    hi = x.astype(BF16)
    lo = (x - hi.astype(F32)).astype(BF16)
    return hi, lo


def _pack_pair(a, b):
    ua = lax.bitcast_convert_type(a.astype(BF16).astype(F32), jnp.uint32)
    ub = lax.bitcast_convert_type(b.astype(BF16).astype(F32), jnp.uint32)
    return ub | (ua >> 16)


def _unpack_pair(w):
    lo = lax.bitcast_convert_type(w << 16, F32)
    hi = lax.bitcast_convert_type(w & jnp.uint32(0xFFFF0000), F32)
    return lo, hi


def _store_packed(ref, val, n_rows):
    d = val.shape[1]
    slabs = d // (2 * LANES)
    for s in range(slabs):
        a = val[:, s * LANES:(s + 1) * LANES]
        b = val[:, d // 2 + s * LANES:d // 2 + (s + 1) * LANES]
        ref[pl.ds(s, n_rows, stride=slabs), :] = _pack_pair(a, b)


def _load_packed(ref, n_rows, slabs):
    return [_unpack_pair(ref[pl.ds(s, n_rows, stride=slabs), :]) for s in range(slabs)]


def _side_cast(src_ref, dst_ref, regroup):
    for e in range(src_ref.shape[0]):
        if regroup:
            slabs = dst_ref.shape[1]
            half = slabs * LANES
            for s in range(slabs):
                dst_ref[e, s, :LANES, :] = src_ref[e, s * LANES:(s + 1) * LANES, :].astype(BF16)
                dst_ref[e, s, LANES:, :] = src_ref[e, half + s * LANES:half + (s + 1) * LANES, :].astype(BF16)
        else:
            dst_ref[e] = src_ref[e].astype(BF16)


def _side_io(w, n_steps, step_of, regroup):
    n_e, rows, cols = w.shape
    per = n_e // n_steps
    assert per * n_steps == n_e, (n_e, n_steps)
    in_spec = pl.BlockSpec((per, rows, cols), lambda *i: (step_of(*i), 0, 0))
    if regroup:
        shape = (n_e, rows // (2 * LANES), 2 * LANES, cols)
        out_spec = pl.BlockSpec((per,) + shape[1:], lambda *i: (step_of(*i), 0, 0, 0))
    else:
        shape = w.shape
        out_spec = pl.BlockSpec((per, rows, cols), lambda *i: (step_of(*i), 0, 0))
    return in_spec, out_spec, jax.ShapeDtypeStruct(shape, BF16)


def _ada_kernel(c_ref, w_ref, b_ref, o_ref):
    c = c_ref[...]
    bsz = c.shape[0]
    c_hi, c_lo = _split_bf16(c * _sigmoid(c))
    w_hi, w_lo = _split_bf16(w_ref[...])
    first = _dot(jnp.concatenate([c_hi, c_lo], axis=0), w_hi)
    o_ref[...] = first[:bsz] + first[bsz:] + _dot(c_hi, w_lo) + b_ref[...]


def _ada(c, ada_w, ada_b):
    bsz, d = c.shape
    n = ada_w.shape[1]
    tn = 1024
    return pl.pallas_call(
        _ada_kernel,
        grid=(n // tn,),
        in_specs=[pl.BlockSpec((bsz, d), lambda j: (0, 0)),
                  pl.BlockSpec((d, tn), lambda j: (0, j)),
                  pl.BlockSpec((1, tn), lambda j: (0, j))],
        out_specs=pl.BlockSpec((bsz, tn), lambda j: (0, j)),
        out_shape=jax.ShapeDtypeStruct((bsz, n), F32),
        compiler_params=_cparams("arbitrary"),
        name="ada_mod",
    )(c, ada_w, ada_b.reshape(1, n))


def _rms(x, g):
    return x * lax.rsqrt(jnp.mean(x * x, axis=-1, keepdims=True) + RMS_EPS) * g


def _norm_proj_kernel(x_ref, g_ref, sc_ref, sh_ref, w_ref, side_ref, o_ref, side_out):
    h = _rms(x_ref[...], g_ref[...]) * (1.0 + sc_ref[0]) + sh_ref[0]
    o_ref[...] = _dot(h.astype(BF16), w_ref[...]).astype(o_ref.dtype)
    _side_cast(side_ref, side_out, regroup=False)


def _norm_proj(x2, g, sc, sh, w, side_w, seqlen, tm, name):
    n, d = x2.shape
    nout = w.shape[1]
    tpb = seqlen // tm
    side_in, side_out, side_shape = _side_io(side_w, n // tm, lambda i: i, regroup=False)
    return pl.pallas_call(
        _norm_proj_kernel,
        grid=(n // tm,),
        in_specs=[pl.BlockSpec((tm, d), lambda i: (i, 0)),
                  pl.BlockSpec((1, d), lambda i: (0, 0)),
                  pl.BlockSpec((1, 1, d), lambda i: (i // tpb, 0, 0)),
                  pl.BlockSpec((1, 1, d), lambda i: (i // tpb, 0, 0)),
                  pl.BlockSpec((d, nout), lambda i: (0, 0), pipeline_mode=pl.Buffered(1)),
                  side_in],
        out_specs=[pl.BlockSpec((tm, nout), lambda i: (i, 0)), side_out],
        out_shape=[jax.ShapeDtypeStruct((n, nout), BF16), side_shape],
        compiler_params=_cparams("parallel"),
        name=name,
    )(x2, g.reshape(1, d), sc[:, None, :], sh[:, None, :], w, side_w)


def _s5_tables(lam_re, lam_im, log_dt, b_re, b_im, c_re, c_im):
    g, p, cg = b_re.shape
    gs = S5_SLAB // cg
    ns = g // gs
    lr = jnp.minimum(lam_re.astype(F32), -1e-4)
    li = lam_im.astype(F32)
    dt = jnp.exp(log_dt.astype(F32))[:, None]
    mag = jnp.exp(lr * dt)
    ar, ai = mag * jnp.cos(li * dt), mag * jnp.sin(li * dt)
    den = lr * lr + li * li
    qr = ((ar - 1.0) * lr + ai * li) / den
    qi = (ai * lr - (ar - 1.0) * li) / den
    br, bi = b_re.astype(F32), b_im.astype(F32)
    bbr = qr[..., None] * br - qi[..., None] * bi
    bbi = qr[..., None] * bi + qi[..., None] * br

    def in_rows(t):
        return t.reshape(ns, gs, p, cg).transpose(0, 1, 3, 2).reshape(ns, gs * cg, p).astype(BF16)

    def out_cols(t):
        return t.reshape(ns, gs, cg, p).transpose(0, 3, 1, 2).reshape(ns, p, gs * cg).astype(BF16)

    a_tab = jnp.stack([ar.reshape(ns, gs * p), ai.reshape(ns, gs * p)], axis=1)
    return (in_rows(bbr), in_rows(bbi), out_cols(c_re.astype(F32)), out_cols(-c_im.astype(F32)), a_tab)


def _gelu_tanh(x):
    return 0.5 * x * (1.0 + jnp.tanh(math.sqrt(2.0 / math.pi) * (x + 0.044715 * (x * x * x))))


def _s5_kernel(x_ref, g_ref, sc_ref, sh_ref, wu_ref, perm_ref, bre_ref, bim_ref, cre_ref, cim_ref, a_ref,
               d_ref, gw_ref, gb_ref, o_ref, b_ref, c_ref, u_scr, bscr, sscr, yscr, st_ref):
    bsz, lb, d_in = x_ref.shape
    rows = bsz * lb
    dch = wu_ref.shape[1]
    half = st_ref.shape[2] // 2
    n_p = bre_ref.shape[2]

    @pl.when(pl.program_id(0) == 0)
    def _():
        st_ref[...] = jnp.zeros_like(st_ref)
        tile_in = (lax.broadcasted_iota(jnp.int32, (n_p, half), 0)
                   == lax.broadcasted_iota(jnp.int32, (n_p, half), 1) % n_p).astype(BF16)
        tile_out = (lax.broadcasted_iota(jnp.int32, (half, n_p), 0) % n_p
                    == lax.broadcasted_iota(jnp.int32, (half, n_p), 1)).astype(BF16)
        in_mask = (lax.broadcasted_iota(jnp.int32, (S5_SLAB, half), 0) // SSM_GROUP
                   == lax.broadcasted_iota(jnp.int32, (S5_SLAB, half), 1) // n_p)
        out_mask = (lax.broadcasted_iota(jnp.int32, (half, S5_SLAB), 0) // n_p
                    == lax.broadcasted_iota(jnp.int32, (half, S5_SLAB), 1) // SSM_GROUP)
        for s in range(dch // S5_SLAB):
            b_ref[s, :, :half] = jnp.where(in_mask, _dot(bre_ref[s], tile_in), 0.0).astype(BF16)
            b_ref[s, :, half:] = jnp.where(in_mask, _dot(bim_ref[s], tile_in), 0.0).astype(BF16)
            c_ref[s, :half, :] = jnp.where(out_mask, _dot(tile_out, cre_ref[s]), 0.0).astype(BF16)
            c_ref[s, half:, :] = jnp.where(out_mask, _dot(tile_out, cim_ref[s]), 0.0).astype(BF16)

    h = (_rms(x_ref[...], g_ref[...]) * (1.0 + sc_ref[...]) + sh_ref[...]).astype(BF16)
    u_nat = _dot(h.reshape(rows, d_in), wu_ref[...]).astype(BF16)
    u_scr[...] = _dot(perm_ref[...], u_nat).astype(BF16)
    n_slab = dch // S5_SLAB
    slab = lambda s: slice(s * S5_SLAB, (s + 1) * S5_SLAB)

    def project_in(s):
        bscr[s % 2] = _dot(u_scr[:, slab(s)], b_ref[s])

    project_in(0)
    for s in range(n_slab):
        if s + 1 < n_slab:
            project_in(s + 1)
        buf = s % 2
        a_re = a_ref[s, 0:1, :]
        a_im = a_ref[s, 1:2, :]
        s_r = st_ref[s, :, :half]
        s_i = st_ref[s, :, half:]
        for l in range(rows // bsz):
            at_l = slice(l * bsz, (l + 1) * bsz)
            s_r, s_i = (a_re * s_r - a_im * s_i + bscr[buf, at_l, :half],
                        a_re * s_i + a_im * s_r + bscr[buf, at_l, half:])
            sscr[buf, at_l, :half] = s_r
            sscr[buf, at_l, half:] = s_i
        st_ref[s, :, :half] = s_r
        st_ref[s, :, half:] = s_i
        y = _dot(sscr[buf].astype(BF16), c_ref[s]) + d_ref[:, slab(s)] * u_scr[:, slab(s)].astype(F32)
        yscr[:, slab(s)] = _gelu_tanh(y).astype(BF16)
    y = yscr[...]
    gate = _sigmoid(_dot(y, gw_ref[...]) + gb_ref[...])
    out_tm = (y.astype(F32) * gate).astype(BF16)
    out_nat = _dot_tn(perm_ref[...], out_tm).astype(o_ref.dtype)
    o_ref[...] = out_nat.reshape(bsz, lb, dch)


def _s5_glu(x2, g, sc, sh, w_u, tabs, d_skip, glu_w, glu_b, bsz, seqlen, lb):
    b_re, b_im, c_re, c_im, a_tab = tabs
    d_in, dch = w_u.shape
    ns, _, n_half = a_tab.shape
    n_state = 2 * n_half
    rows = lb * bsz
    r_idx = jnp.arange(rows)
    perm = ((r_idx % bsz) * lb + r_idx // bsz)[:, None] == r_idx[None, :]
    full = lambda a: pl.BlockSpec(a.shape, lambda i: (0,) * a.ndim, pipeline_mode=pl.Buffered(1))
    args = (x2.reshape(bsz, seqlen, d_in), g.reshape(1, 1, d_in), sc[:, None, :], sh[:, None, :], w_u,
            perm.astype(BF16), b_re, b_im, c_re, c_im, a_tab,
            d_skip.astype(F32).reshape(1, dch), glu_w.astype(BF16), glu_b.astype(F32).reshape(1, dch))
    y = pl.pallas_call(
        _s5_kernel,
        grid=(seqlen // lb,),
        in_specs=[pl.BlockSpec((bsz, lb, d_in), lambda i: (0, i, 0))] + [full(a) for a in args[1:]],
        out_specs=pl.BlockSpec((bsz, lb, dch), lambda i: (0, i, 0)),
        out_shape=jax.ShapeDtypeStruct((bsz, seqlen, dch), BF16),
        scratch_shapes=[pltpu.VMEM((ns, S5_SLAB, n_state), BF16), pltpu.VMEM((ns, n_state, S5_SLAB), BF16),
                        pltpu.VMEM((rows, dch), BF16), pltpu.VMEM((2, rows, n_state), F32),
                        pltpu.VMEM((2, rows, n_state), F32), pltpu.VMEM((rows, dch), BF16),
                        pltpu.VMEM((ns, bsz, n_state), F32)],
        compiler_params=_cparams("arbitrary"),
        name="s5_mixer_glu",
    )(*args)
    return y.reshape(bsz * seqlen, dch)


def _rwkv_kernel(z_ref, mu_ref, w0_ref, a0_ref, kkw_ref, ka_ref, wa_ref, gup_ref, rk_ref, lnw_ref, lnb_ref,
                 ones_ref, side1_ref, side3_ref, o_ref, side1_out, side3_out, s_ref, zlast_ref):
    _side_cast(side1_ref, side1_out, regroup=True)
    _side_cast(side3_ref, side3_out, regroup=True)
    nb, t, _ = z_ref.shape
    d_r = o_ref.shape[2]
    tile = HEADS_PER_TILE * RWKV_HEAD
    n_tiles = d_r // tile
    first_chunk = pl.program_id(1) == 0

    @pl.when(first_chunk)
    def _():
        s_ref[...] = jnp.zeros_like(s_ref)

    ones_bd = ones_ref[...]

    def seg_sum(x):
        return _dot(x.astype(BF16), ones_bd)

    row = lax.broadcasted_iota(jnp.int32, (t, t), 0)
    col = lax.broadcasted_iota(jnp.int32, (t, t), 1)
    tri = (row >= col).astype(BF16)
    st = HEADS_PER_TILE * t
    rs = lax.broadcasted_iota(jnp.int32, (2 * t, 2 * st), 0)
    t_r = rs % t
    t_c = lax.broadcasted_iota(jnp.int32, (2 * t, 2 * st), 1) % t
    keep = (t_r > t_c) | ((rs >= t) & (t_r == t_c))
    eye_w = (lax.broadcasted_iota(jnp.int32, (t, st), 0)
             == lax.broadcasted_iota(jnp.int32, (t, st), 1) % t).astype(F32)
    blk_mask = (lax.broadcasted_iota(jnp.int32, (st, st), 0) // t
                == lax.broadcasted_iota(jnp.int32, (st, st), 1) // t)
    lane = lax.broadcasted_iota(jnp.int32, (1, tile), 1)
    head_masks = [(lane >= j * RWKV_HEAD) & (lane < (j + 1) * RWKV_HEAD) for j in range(HEADS_PER_TILE)]
    bd_r = lax.broadcasted_iota(jnp.int32, (tile, tile), 0) // RWKV_HEAD
    bd_c = lax.broadcasted_iota(jnp.int32, (tile, tile), 1) // RWKV_HEAD
    bd_mask = bd_r == bd_c
    n_levels = int(math.log2(t))
    slices = [slice(hg * tile, (hg + 1) * tile) for hg in range(n_tiles)]
    units = [(bb, hg) for bb in range(nb) for hg in range(n_tiles)]
    n_u = range(len(units))

    def stack(x):
        zero = jnp.zeros_like(x)
        return jnp.concatenate([jnp.where(m, x, zero) for m in head_masks], axis=0)

    def bf(x):
        return x.astype(BF16)

    at, qt, bt, kt, vv, em, etm, wtot, rkb, gate = ([] for _ in range(10))
    row0 = lax.broadcasted_iota(jnp.int32, (t, 1), 0) == 0
    lora_lane = lax.broadcasted_iota(jnp.int32, (t, LORA_W + LORA_A), 1)
    for bb in range(nb):
        z = z_ref[bb].astype(F32)
        prev_row = jnp.where(first_chunk, 0.0, zlast_ref[bb, 0:1, :])
        zl = z + mu_ref[...] * (jnp.where(row0, prev_row, pltpu.roll(z, 1, 0)) - z)
        zlast_ref[bb, 0:1, :] = z[t - 1:t, :]
        xwa = zl[:, 3 * d_r:3 * d_r + LORA_W + LORA_A]
        lhs = jnp.where(lora_lane < LORA_W, jnp.tanh(xwa), xwa).astype(BF16)
        wa = _dot(lhs, wa_ref[...])
        lw = -DECAY_SCALE * _sigmoid(w0_ref[...] + wa[:, :d_r])
        asig = _sigmoid(a0_ref[...] + wa[:, d_r:])
        r = zl[:, :d_r]
        k_raw = zl[:, d_r:2 * d_r]
        kk = k_raw * kkw_ref[...]
        kp = k_raw * (1.0 + (asig - 1.0) * ka_ref[...])
        xg = zl[:, 3 * d_r + LORA_W + LORA_A:3 * d_r + LORA_W + LORA_A + LORA_G]
        g_full = _dot(_sigmoid(xg).astype(BF16), gup_ref[...])
        lw_hi, lw_lo = _split_bf16(lw)
        cum = _dot(tri, lw_hi) + _dot(tri, lw_lo)
        mid = cum[t // 2 - 1:t // 2, :]
        tot = cum[t - 1:t, :]
        e1 = jnp.exp(cum - mid)
        e2 = jnp.exp(mid - cum)
        e1p = e1 * jnp.exp(-lw)
        kk2 = kk * kk
        kkn = kk / jnp.maximum(jnp.sqrt(jnp.concatenate([seg_sum(kk2[:, sl]) for sl in slices], axis=1)), 1e-12)
        full = dict(at=-kkn * e1p, qt=r * e1, bt=kkn * asig * e2, kt=kp * e2,
                    vv=zl[:, 2 * d_r:3 * d_r], em=jnp.exp(mid), etm=jnp.exp(tot - mid), wtot=jnp.exp(tot),
                    rkb=r * kp * rk_ref[...], gate=g_full)
        for dst, key in ((at, "at"), (qt, "qt"), (bt, "bt"), (kt, "kt"), (vv, "vv"), (em, "em"),
                         (etm, "etm"), (wtot, "wtot"), (rkb, "rkb"), (gate, "gate")):
            dst.extend(full[key][:, sl] for sl in slices)

    s_old = [s_ref[i] for i in n_u]
    wide = []
    for i in n_u:
        lhs = bf(jnp.concatenate([at[i], qt[i]], axis=0))
        rhs = jnp.concatenate([stack(bf(bt[i])), stack(bf(kt[i]))], axis=0)
        a_w = bf(jnp.where(keep, _dot_nt(lhs, rhs), 0.0))
        wide.append([a_w[:t, :st], a_w[:t, st:], a_w[t:, :]])
    x_state = [_dot_nt(bf(jnp.concatenate([at[i], qt[i]], axis=0) * em[i]), bf(s_old[i]))
               for i in n_u]
    sv = [stack(bf(vv[i])) for i in n_u]
    akv = [_dot(wide[i][1], sv[i]) for i in n_u]

    def expand(x_w):
        x_b = bf(x_w)
        return jnp.where(blk_mask, jnp.concatenate([x_b] * HEADS_PER_TILE, axis=0), jnp.zeros((st, st), BF16))

    p_acc = [eye_w + w[0].astype(F32) for w in wide]
    q_bd = [expand(w[0]) for w in wide]
    q_pow = [_dot(wide[i][0], q_bd[i]) for i in n_u]
    for lev in range(1, n_levels):
        for i in n_u:
            q_bd[i] = expand(q_pow[i])
            if lev < n_levels - 1:
                both = _dot(bf(jnp.concatenate([p_acc[i], q_pow[i]], axis=0)), q_bd[i])
                p_acc[i] = p_acc[i] + both[:t]
                q_pow[i] = both[t:]
            else:
                p_acc[i] = p_acc[i] + _dot(bf(p_acc[i]), q_bd[i])
    u_all = [_dot(bf(p_acc[i]), stack(bf(x_state[i][:t] + akv[i]))) for i in n_u]
    y_all = [x_state[i][t:]
             + _dot(wide[i][2], jnp.concatenate([stack(bf(u_all[i])), sv[i]], axis=0)) for i in n_u]

    for i, (bb, hg) in enumerate(units):
        sl = slices[hg]
        y = y_all[i]
        uv = bf(jnp.concatenate([u_all[i], vv[i]], axis=0))
        bk_end = bf(jnp.concatenate([bt[i], kt[i]], axis=0) * etm[i])
        s_ref[i] = s_old[i] * wtot[i] + jnp.where(bd_mask, _dot_tn(uv, bk_end), 0.0)

        mean = seg_sum(y) * (1.0 / RWKV_HEAD)
        dlt = y - mean
        var = seg_sum(dlt * dlt) * (1.0 / RWKV_HEAD)
        yn = dlt * lax.rsqrt(var + GN_EPS) * lnw_ref[:, sl] + lnb_ref[:, sl]
        out = (yn + seg_sum(rkb[i]) * vv[i]) * gate[i]
        o_ref[bb, :, sl] = out.astype(o_ref.dtype)


def _rwkv(z, mu, w0, w_up, a0, a_up, g_up, k_k, k_a, r_k, ln_w, ln_b, side_w1, side_w3, bsz, seqlen):
    n, dz = z.shape
    d_r = w0.shape[0]
    t = RWKV_CHUNK
    nb = RWKV_BATCH_PER_STEP
    nch = seqlen // t
    tile = HEADS_PER_TILE * RWKV_HEAD
    hid = jnp.arange(tile) // RWKV_HEAD
    ones_bd = (hid[:, None] == hid[None, :]).astype(BF16)
    wa = jnp.zeros((LORA_W + LORA_A, 2 * d_r), F32)
    wa = wa.at[:LORA_W, :d_r].set(w_up.astype(F32)).at[LORA_W:, d_r:].set(a_up.astype(F32)).astype(BF16)
    row = lambda a: a.astype(F32).reshape(1, -1)
    full = lambda a: pl.BlockSpec(a.shape, lambda b, c: (0, 0))
    params = (row(mu), row(w0), row(a0), row(k_k), row(k_a), wa, g_up.astype(BF16), row(r_k), row(ln_w),
              row(ln_b), ones_bd)
    side_in, side_out, side_shape = _side_io(side_w1, (bsz // nb) * nch, lambda b, c: b * nch + c, regroup=True)
    y, side1, side3 = pl.pallas_call(
        _rwkv_kernel,
        grid=(bsz // nb, nch),
        in_specs=([pl.BlockSpec((nb, t, dz), lambda b, c: (b, c, 0))] + [full(a) for a in params]
                  + [side_in, side_in]),
        out_specs=[pl.BlockSpec((nb, t, d_r), lambda b, c: (b, c, 0)), side_out, side_out],
        out_shape=[jax.ShapeDtypeStruct((bsz, seqlen, d_r), BF16), side_shape, side_shape],
        scratch_shapes=[pltpu.VMEM((nb * (d_r // tile), tile, tile), F32), pltpu.VMEM((nb, 8, dz), F32)],
        compiler_params=_cparams("parallel", "arbitrary"),
        name="rwkv7_chunked",
    )(z.reshape(bsz, seqlen, dz), *params, side_w1, side_w3)
    return y.reshape(n, d_r), side1, side3


def _post_mix_kernel(ys_ref, yr_ref, wo1_ref, wo2_ref, x_ref, g1_ref, gt_ref, g2_ref, sc_ref, sh_ref,
                     wr_both_ref, br_ref, x1_out, h2_out, lg_out, mixed_even, mixed_odd):
    i = pl.program_id(0)
    tm = x_ref.shape[0]
    nl = lg_out.shape[1]

    @pl.when(i == 0)
    def _():
        mixed_odd[...] = jnp.zeros_like(mixed_odd)

    def step(write_ref, read_ref):
        write_ref[...] = _dot(ys_ref[...], wo1_ref[...]) + _dot(yr_ref[...], wo2_ref[...])
        x1 = x_ref[...] + gt_ref[0] * _rms(read_ref[...], g1_ref[...])
        x1_out[...] = x1
        h2 = _rms(x1, g2_ref[...]) * (1.0 + sc_ref[0]) + sh_ref[0]
        _store_packed(h2_out, h2, tm)
        hi, lo = _split_bf16(h2)
        both = _dot(hi, wr_both_ref[...])
        lg_out[...] = both[:, :nl] + both[:, nl:] + _dot(lo, wr_both_ref[:, :nl]) + br_ref[...]

    @pl.when(i % 2 == 0)
    def _():
        step(mixed_even, mixed_odd)

    @pl.when(i % 2 == 1)
    def _():
        step(mixed_odd, mixed_even)


def _post_mix(ys, yr, w_out, x2, g1, gt1, g2, sc2, sh2, w_route, b_route, seqlen, tm):
    n, d = x2.shape
    ds = ys.shape[1]
    tpb = seqlen // tm
    n_tiles = n // tm
    slabs = d // (2 * LANES)
    wr_both = jnp.concatenate(_split_bf16(w_route), axis=1)
    ahead = lambda w: pl.BlockSpec((tm, w), lambda i: (jnp.minimum(i, n_tiles - 1), 0))
    prev = lambda i: jnp.maximum(i - 1, 0)
    rows = lambda w: pl.BlockSpec((tm, w), lambda i: (prev(i), 0))
    full = lambda a: pl.BlockSpec(a.shape, lambda i: (0,) * a.ndim)
    bat = pl.BlockSpec((1, 1, d), lambda i: (prev(i) // tpb, 0, 0))
    args = (ys, yr, w_out[:ds].astype(BF16), w_out[ds:].astype(BF16), x2, g1.reshape(1, d), gt1[:, None, :],
            g2.reshape(1, d), sc2[:, None, :], sh2[:, None, :], wr_both, b_route.reshape(1, -1))
    in_specs = [ahead(ds), ahead(yr.shape[1]), full(args[2]), full(args[3]), rows(d), full(args[5]), bat,
                full(args[7]), bat, bat, full(wr_both), full(args[11])]
    return pl.pallas_call(
        _post_mix_kernel,
        grid=(n_tiles + 1,),
        in_specs=in_specs,
        out_specs=[rows(d), pl.BlockSpec((tm * slabs, LANES), lambda i: (prev(i), 0)), rows(ROUTE_LANES)],
        out_shape=[jax.ShapeDtypeStruct((n, d), F32), jax.ShapeDtypeStruct((n * slabs, LANES), jnp.uint32),
                   jax.ShapeDtypeStruct((n, ROUTE_LANES), F32)],
        scratch_shapes=[pltpu.VMEM((tm, d), F32), pltpu.VMEM((tm, d), F32)],
        compiler_params=_cparams("arbitrary"),
        name="out_proj_post",
    )(*args)


def _route_kernel(lg_ref, info_ref, cnt_ref, carry):
    i = pl.program_id(0)
    tm = lg_ref.shape[0]

    @pl.when(i == 0)
    def _():
        carry[...] = jnp.zeros_like(carry)

    lg = lg_ref[...]
    lane = lax.broadcasted_iota(jnp.int32, lg.shape, 1)
    lane_f = lane.astype(F32)
    neg = jnp.float32(-jnp.inf)
    big = jnp.float32(1e9)
    is_g = (lane >= N_EXPERTS) & (lane < N_EXPERTS + N_GROUPS)
    gl = jnp.where(is_g, lg, neg)
    gmax = jnp.max(gl, axis=-1, keepdims=True)
    gidx = jnp.min(jnp.where(gl == gmax, lane_f - N_EXPERTS, big), axis=-1, keepdims=True)
    p_grp = 1.0 / jnp.sum(jnp.where(is_g, jnp.exp(gl - gmax), 0.0), axis=-1, keepdims=True)
    in_grp = (lane < N_EXPERTS) & ((lane // EXPERTS_PER_GROUP).astype(F32) == gidx)
    el = jnp.where(in_grp, lg, neg)
    m1 = jnp.max(el, axis=-1, keepdims=True)
    i1 = jnp.min(jnp.where(el == m1, lane_f, big), axis=-1, keepdims=True)
    el2 = jnp.where(lane_f == i1, neg, el)
    m2 = jnp.max(el2, axis=-1, keepdims=True)
    i2 = jnp.min(jnp.where(el2 == m2, lane_f, big), axis=-1, keepdims=True)
    ex = jnp.exp(m2 - m1)
    w1 = p_grp / (1.0 + ex)
    w2 = p_grp * ex / (1.0 + ex)

    oh1 = lane_f == i1
    oh2 = lane_f == i2
    onehot = (oh1 | oh2).astype(BF16)
    rr = lax.broadcasted_iota(jnp.int32, (tm, tm), 0)
    cc = lax.broadcasted_iota(jnp.int32, (tm, tm), 1)
    before = _dot((rr > cc).astype(BF16), onehot) + carry[...]
    rank1 = jnp.sum(jnp.where(oh1, before, 0.0), axis=-1, keepdims=True)
    rank2 = jnp.sum(jnp.where(oh2, before, 0.0), axis=-1, keepdims=True)
    carry[...] = carry[...] + jnp.sum(onehot.astype(F32), axis=0, keepdims=True)
    cnt_ref[...] = carry[...]

    info = jnp.where(lane == 0, i1, 0.0)
    info = jnp.where(lane == 1, i2, info)
    info = jnp.where(lane == 2, w1, info)
    info = jnp.where(lane == 3, w2, info)
    info = jnp.where(lane == 4, rank1, info)
    info = jnp.where(lane == 5, rank2, info)
    info_ref[...] = info


def _route(logits, tm):
    n = logits.shape[0]
    return pl.pallas_call(
        _route_kernel,
        grid=(n // tm,),
        in_specs=[pl.BlockSpec((tm, ROUTE_LANES), lambda i: (i, 0))],
        out_specs=[pl.BlockSpec((tm, ROUTE_LANES), lambda i: (i, 0)),
                   pl.BlockSpec((1, ROUTE_LANES), lambda i: (0, 0))],
        out_shape=[jax.ShapeDtypeStruct((n, ROUTE_LANES), F32),
                   jax.ShapeDtypeStruct((1, ROUTE_LANES), F32)],
        scratch_shapes=[pltpu.VMEM((1, ROUTE_LANES), F32)],
        compiler_params=_cparams("arbitrary"),
        name="moe_route",
    )(logits)


def _slot_rows_kernel(info_ref, seg_ref, o_ref, *, slabs):
    info = info_ref[...]
    lane = lax.broadcasted_iota(jnp.int32, info.shape, 1)
    lane_f = lane.astype(F32)
    seg = seg_ref[...]
    d0 = jnp.sum(jnp.where(lane_f == info[:, 0:1], seg, 0.0), axis=-1, keepdims=True) + info[:, 4:5] * slabs
    d1 = jnp.sum(jnp.where(lane_f == info[:, 1:2], seg, 0.0), axis=-1, keepdims=True) + info[:, 5:6] * slabs
    o_ref[...] = jnp.where(lane == 0, d0, jnp.where(lane == 1, d1, 0.0)).astype(jnp.int32)


def _slot_rows(info, seg_row, slabs, tm):
    n = info.shape[0]
    return pl.pallas_call(
        functools.partial(_slot_rows_kernel, slabs=slabs),
        grid=(n // tm,),
        in_specs=[pl.BlockSpec((tm, ROUTE_LANES), lambda i: (i, 0)),
                  pl.BlockSpec((1, ROUTE_LANES), lambda i: (0, 0))],
        out_specs=pl.BlockSpec((tm, ROUTE_LANES), lambda i: (i, 0)),
        out_shape=jax.ShapeDtypeStruct((n, ROUTE_LANES), jnp.int32),
        compiler_params=_cparams("parallel"),
        name="moe_slot_rows",
    )(info, seg_row)


def _dispatch_kernel(d0_ref, d1_ref, pad_row_ref, pad_len_ref, na_ref, h_ref, buf_out, zeros, sem, zsem, *,
                     slabs, n_blocks):
    tm = h_ref.shape[0] // slabs

    @pl.when(pl.program_id(0) == 0)
    def _():
        zeros[...] = jnp.zeros_like(zeros)
        bits = [1 << b for b in reversed(range(EXPERT_BLOCK.bit_length() - 1))]

        def pad_copy(e, bit):
            done = pad_len_ref[e] & ~(2 * bit - 1)
            dst = pl.multiple_of(pad_row_ref[e] + done * slabs, slabs)
            return pltpu.make_async_copy(zeros.at[pl.ds(0, bit * slabs)], buf_out.at[pl.ds(dst, bit * slabs)], zsem)

        def block_copy(j):
            dst = pl.multiple_of(j * (EXPERT_BLOCK * slabs), EXPERT_BLOCK * slabs)
            return pltpu.make_async_copy(zeros, buf_out.at[pl.ds(dst, EXPERT_BLOCK * slabs)], zsem)

        def each(action):
            def per_expert(e, c):
                for bit in bits:
                    @pl.when((pad_len_ref[e] & bit) != 0)
                    def _():
                        action(pad_copy(e, bit))
                return c

            def per_block(j, c):
                action(block_copy(j))
                return c

            lax.fori_loop(0, N_EXPERTS, per_expert, 0)
            lax.fori_loop(na_ref[0], n_blocks, per_block, 0)

        each(lambda c: c.start())
        each(lambda c: c.wait())

    def copy(t, dest_ref):
        src = pl.multiple_of(t * slabs, slabs)
        dst = pl.multiple_of(dest_ref[t], slabs)
        return pltpu.make_async_copy(h_ref.at[pl.ds(src, slabs)], buf_out.at[pl.ds(dst, slabs)], sem)

    def start(t, c):
        copy(t, d0_ref).start(priority=0)
        copy(t, d1_ref).start(priority=1)
        return c

    lax.fori_loop(0, tm, start, 0, unroll=4)
    for _ in range(2):
        pltpu.make_async_copy(h_ref, buf_out.at[pl.ds(0, tm * slabs)], sem).wait()


def _dispatch(h2p, dest_rows, pad_row, pad_len, n_active, cap, slabs, tm):
    n = h2p.shape[0] // slabs
    smem = pl.BlockSpec((tm,), lambda i: (i,), memory_space=pltpu.SMEM)
    table = lambda a: pl.BlockSpec(a.shape, lambda i: (0,), memory_space=pltpu.SMEM)
    return pl.pallas_call(
        functools.partial(_dispatch_kernel, slabs=slabs, n_blocks=cap // EXPERT_BLOCK),
        grid=(n // tm,),
        in_specs=[smem, smem, table(pad_row), table(pad_len), table(n_active),
                  pl.BlockSpec((tm * slabs, LANES), lambda i: (i, 0))],
        out_specs=pl.BlockSpec(memory_space=pl.ANY),
        out_shape=jax.ShapeDtypeStruct((cap * slabs, LANES), h2p.dtype),
        scratch_shapes=[pltpu.VMEM((EXPERT_BLOCK * slabs, LANES), h2p.dtype), pltpu.SemaphoreType.DMA,
                        pltpu.SemaphoreType.DMA],
        compiler_params=_cparams("arbitrary"),
        name="moe_dispatch",
    )(dest_rows[0], dest_rows[1], pad_row, pad_len, n_active, h2p)


def _moe_kernel(na_ref, eseq_ref, epos_ref, nd_ref, x_ref, w1_hbm, w3_hbm, w2_hbm, o_ref,
                w1b, w3b, w2b, sem):
    j = pl.program_id(0)
    active = j < na_ref[0]
    pos = epos_ref[j]
    fresh = (j == 0) | (pos != epos_ref[jnp.maximum(j - 1, 0)])
    slabs = w1b.shape[1]

    def weight_copies(p):
        e = eseq_ref[p]
        slot = p % 2
        return [pltpu.make_async_copy(w_hbm.at[e], stage.at[slot], sem.at[slot, i])
                for i, (w_hbm, stage) in enumerate(((w1_hbm, w1b), (w3_hbm, w3b), (w2_hbm, w2b)))]

    @pl.when(j == 0)
    def _():
        for c in weight_copies(0):
            c.start()

    @pl.when(active & fresh)
    def _():
        for c in weight_copies(pos):
            c.wait()

        @pl.when(pos + 1 < nd_ref[0])
        def _():
            for c in weight_copies(pos + 1):
                c.start()

    @pl.when(active)
    def _():
        slot = pos % 2
        acc1 = jnp.zeros((EXPERT_BLOCK, w1b.shape[3]), F32)
        acc3 = jnp.zeros((EXPERT_BLOCK, w1b.shape[3]), F32)
        for s, (lo, hi) in enumerate(_load_packed(x_ref, EXPERT_BLOCK, slabs)):
            lhs = jnp.concatenate([lo.astype(BF16), hi.astype(BF16)], axis=1)
            acc1 = acc1 + _dot(lhs, w1b[slot, s])
            acc3 = acc3 + _dot(lhs, w3b[slot, s])
        hid = (acc1 * _sigmoid(acc1)) * acc3
        _store_packed(o_ref, _dot(hid.astype(BF16), w2b[slot]), EXPERT_BLOCK)


def _moe(x_buf, n_active, expert_seq, block_pos, n_used, w1, w3, w2, slabs):
    cap = x_buf.shape[0] // slabs
    nb = cap // EXPERT_BLOCK

    def xmap(j, na, *_):
        return (jnp.minimum(j, na[0] - 1), 0)

    xspec = pl.BlockSpec((EXPERT_BLOCK * slabs, LANES), xmap)
    hbm = pl.BlockSpec(memory_space=pl.ANY)
    grid_spec = pltpu.PrefetchScalarGridSpec(
        num_scalar_prefetch=4,
        grid=(nb,),
        in_specs=[xspec, hbm, hbm, hbm],
        out_specs=xspec,
        scratch_shapes=[pltpu.VMEM((2,) + w1.shape[1:], BF16), pltpu.VMEM((2,) + w3.shape[1:], BF16),
                        pltpu.VMEM((2,) + w2.shape[1:], BF16), pltpu.SemaphoreType.DMA((2, 3))],
    )
    return pl.pallas_call(
        _moe_kernel,
        grid_spec=grid_spec,
        out_shape=jax.ShapeDtypeStruct(x_buf.shape, x_buf.dtype),
        input_output_aliases={4: 0},
        compiler_params=_cparams("arbitrary"),
        name="moe_experts",
    )(n_active, expert_seq, block_pos, n_used, x_buf, w1, w3, w2)


def _combine_kernel(d0_ref, d1_ref, d0n_ref, d1n_ref, y_ref, info_ref, x1_ref, g_ref, gt_ref, o_ref,
                    rows, sem, *, slabs, tiles_per_idx, chunk):
    i = pl.program_id(0)
    n_tiles = pl.num_programs(0)
    tm = x1_ref.shape[0]
    slot = i % 2
    nxt_slot = 1 - slot
    nxt = jnp.minimum(i + 1, n_tiles - 1)

    def copy(tile, to_slot, t, dest_ref, k):
        off = (tile % tiles_per_idx) * tm
        src = pl.multiple_of(dest_ref[off + t], slabs)
        dst = pl.multiple_of(t * slabs, slabs)
        return pltpu.make_async_copy(y_ref.at[pl.ds(src, slabs)], rows.at[to_slot, k, pl.ds(dst, slabs)],
                                     sem.at[to_slot])

    def drain(which):
        for k in range(2):
            pltpu.make_async_copy(y_ref.at[pl.ds(0, tm * slabs)], rows.at[which, k], sem.at[which]).wait()

    @pl.when(i == 0)
    def _():
        def start(t, c):
            copy(i, slot, t, d0_ref, 0).start(priority=0)
            copy(i, slot, t, d1_ref, 1).start(priority=1)
            return c

        lax.fori_loop(0, tm, start, 0, unroll=4)

    drain(slot)
    g_row = g_ref[...]
    gt_row = gt_ref[0]
    for c in range(tm // chunk):
        tok = slice(c * chunk, (c + 1) * chunk)
        info = info_ref[tok, :]
        w1 = info[:, 2:3]
        w2 = info[:, 3:4]
        lo_parts, hi_parts = [], []
        for s in range(slabs):
            at_s = pl.ds(c * chunk * slabs + s, chunk, stride=slabs)
            lo0, hi0 = _unpack_pair(rows[slot, 0, at_s, :])
            lo1, hi1 = _unpack_pair(rows[slot, 1, at_s, :])
            lo_parts.append(w1 * lo0 + w2 * lo1)
            hi_parts.append(w1 * hi0 + w2 * hi1)
        ffn = jnp.concatenate(lo_parts + hi_parts, axis=1)
        o_ref[tok, :] = x1_ref[tok, :] + gt_row * _rms(ffn, g_row)
        for t in range(c * chunk, (c + 1) * chunk):
            copy(nxt, nxt_slot, t, d0n_ref, 0).start(priority=0)
            copy(nxt, nxt_slot, t, d1n_ref, 1).start(priority=1)

    @pl.when(i == n_tiles - 1)
    def _():
        drain(nxt_slot)


def _combine(y_buf, dest_rows, info, x1, g, gt2, seqlen, slabs, tm):
    n, d = x1.shape
    tpb = seqlen // tm
    idx_block = max(tm, SMEM_INDEX_BLOCK)
    per = idx_block // tm
    last = n // tm - 1
    smem = pl.BlockSpec((idx_block,), lambda i: (i // per,), memory_space=pltpu.SMEM)
    smem_next = pl.BlockSpec((idx_block,), lambda i: (jnp.minimum(i + 1, last) // per,), memory_space=pltpu.SMEM)
    return pl.pallas_call(
        functools.partial(_combine_kernel, slabs=slabs, tiles_per_idx=per, chunk=min(tm, COMBINE_CHUNK)),
        grid=(n // tm,),
        in_specs=[smem, smem, smem_next, smem_next,
                  pl.BlockSpec(memory_space=pl.ANY),
                  pl.BlockSpec((tm, ROUTE_LANES), lambda i: (i, 0)),
                  pl.BlockSpec((tm, d), lambda i: (i, 0)),
                  pl.BlockSpec((1, d), lambda i: (0, 0)),
                  pl.BlockSpec((1, 1, d), lambda i: (i // tpb, 0, 0))],
        out_specs=pl.BlockSpec((tm, d), lambda i: (i, 0)),
        out_shape=jax.ShapeDtypeStruct((n, d), F32),
        scratch_shapes=[pltpu.VMEM((2, 2, tm * slabs, LANES), y_buf.dtype), pltpu.SemaphoreType.DMA((2,))],
        compiler_params=_cparams("arbitrary"),
        name="moe_combine",
    )(dest_rows[0], dest_rows[1], dest_rows[0], dest_rows[1], y_buf, info, x1, g.reshape(1, d),
      gt2[:, None, :])


def _pick(n, pref):
    while n % pref:
        pref //= 2
    return pref


def _layer(x2, mod, p, bsz, seqlen):
    n, d = x2.shape
    sh1, sc1, gt1, sh2, sc2, gt2 = jnp.split(mod, 6, axis=-1)
    d_ssm = p["ssm_d"].shape[0]

    z, w2_bf = _norm_proj(x2, p["norm_mix_pre"], sc1, sh1, p["w_in"][:, d_ssm:].astype(BF16), p["moe_w2"],
                          seqlen, _pick(seqlen, max(n // N_EXPERTS, 16)), "in_proj_z")

    tabs = _s5_tables(p["ssm_lam_re"], p["ssm_lam_im"], p["ssm_log_dt"], p["ssm_b_re"], p["ssm_b_im"],
                      p["ssm_c_re"], p["ssm_c_im"])
    y_ssm = _s5_glu(x2, p["norm_mix_pre"], sc1, sh1, p["w_in"][:, :d_ssm].astype(BF16), tabs, p["ssm_d"], p["glu_w"],
                    p["glu_b"], bsz, seqlen, _pick(seqlen, S5_TIME_BLOCK))

    y_rwkv, w1_bf, w3_bf = _rwkv(z, p["rwkv_mu"], p["rwkv_w0"], p["rwkv_w_up"], p["rwkv_a0"], p["rwkv_a_up"],
                                 p["rwkv_g_up"], p["rwkv_k_k"], p["rwkv_k_a"], p["rwkv_r_k"], p["rwkv_ln_w"],
                                 p["rwkv_ln_b"], p["moe_w1"], p["moe_w3"], bsz, seqlen)

    w_route = jnp.zeros((d, ROUTE_LANES), F32)
    w_route = w_route.at[:, :N_EXPERTS].set(p["moe_w_exp"].astype(F32))
    w_route = w_route.at[:, N_EXPERTS:N_EXPERTS + N_GROUPS].set(p["moe_w_grp"].astype(F32))
    b_route = jnp.zeros((ROUTE_LANES,), F32)
    b_route = b_route.at[:N_EXPERTS].set(p["moe_b_exp"].astype(F32))
    b_route = b_route.at[N_EXPERTS:N_EXPERTS + N_GROUPS].set(p["moe_b_grp"].astype(F32))
    x1, h2p, logits = _post_mix(y_ssm, y_rwkv, p["w_out"], x2, p["norm_mix_post"], gt1, p["norm_ffn_pre"],
                                sc2, sh2, w_route, b_route, seqlen, _pick(seqlen, 512))
    slabs = d // (2 * LANES)

    info, counts = _route(logits, _pick(n, 512))
    cnt = counts[0, :N_EXPERTS].astype(jnp.int32)
    padded = (cnt + EXPERT_BLOCK - 1) // EXPERT_BLOCK * EXPERT_BLOCK
    pend = jnp.cumsum(padded)
    pstart = pend - padded
    n_blocks = -(-(2 * n) // EXPERT_BLOCK) + N_EXPERTS
    cap = n_blocks * EXPERT_BLOCK
    seg_row = jnp.zeros((1, ROUTE_LANES), F32).at[0, :N_EXPERTS].set((pstart * slabs).astype(F32))
    dest = _slot_rows(info, seg_row, slabs, _pick(n, 2048))
    dest_rows = (dest[:, 0], dest[:, 1])
    n_active = (pend[-1:] // EXPERT_BLOCK).astype(jnp.int32)
    used = (cnt > 0).astype(jnp.int32)
    used_pos = jnp.cumsum(used) - 1
    slot_ids = jnp.arange(N_EXPERTS, dtype=jnp.int32)
    expert_seq = jnp.sum(jnp.where((used_pos[None, :] == slot_ids[:, None]) & (used[None, :] > 0),
                                   slot_ids[None, :], 0), axis=1).astype(jnp.int32)
    block_first = jnp.arange(n_blocks, dtype=jnp.int32) * EXPERT_BLOCK
    block_pos = jnp.sum(jnp.where(pend[None, :] <= block_first[:, None], used[None, :], 0),
                        axis=1).astype(jnp.int32)
    n_used = jnp.sum(used).reshape(1)

    x_buf = _dispatch(h2p, dest_rows, ((pstart + cnt) * slabs).astype(jnp.int32),
                      (padded - cnt).astype(jnp.int32), n_active, cap, slabs, _pick(n, 1024))
    y_buf = _moe(x_buf, n_active, expert_seq, block_pos, n_used, w1_bf, w3_bf, w2_bf, slabs)
    return _combine(y_buf, dest_rows, info, x1, p["norm_ffn_post"], gt2, seqlen, slabs, _pick(seqlen, 512))


def kernel(x, c, ada_w, ada_b, norm_mix_pre, norm_mix_post, norm_ffn_pre, norm_ffn_post, w_in, w_out, ssm_lam_re, ssm_lam_im, ssm_log_dt, ssm_b_re, ssm_b_im, ssm_c_re, ssm_c_im, ssm_d, glu_w, glu_b, rwkv_mu, rwkv_w0, rwkv_w_up, rwkv_a0, rwkv_a_up, rwkv_g_up, rwkv_k_k, rwkv_k_a, rwkv_r_k, rwkv_ln_w, rwkv_ln_b, moe_w_grp, moe_b_grp, moe_w_exp, moe_b_exp, moe_w1, moe_w3, moe_w2):
    bsz, seqlen, d = x.shape
    params = dict(norm_mix_pre=norm_mix_pre, norm_mix_post=norm_mix_post, norm_ffn_pre=norm_ffn_pre,
                  norm_ffn_post=norm_ffn_post, w_in=w_in, w_out=w_out, ssm_lam_re=ssm_lam_re,
                  ssm_lam_im=ssm_lam_im, ssm_log_dt=ssm_log_dt, ssm_b_re=ssm_b_re, ssm_b_im=ssm_b_im,
                  ssm_c_re=ssm_c_re, ssm_c_im=ssm_c_im, ssm_d=ssm_d, glu_w=glu_w, glu_b=glu_b,
                  rwkv_mu=rwkv_mu, rwkv_w0=rwkv_w0, rwkv_w_up=rwkv_w_up, rwkv_a0=rwkv_a0,
                  rwkv_a_up=rwkv_a_up, rwkv_g_up=rwkv_g_up, rwkv_k_k=rwkv_k_k, rwkv_k_a=rwkv_k_a,
                  rwkv_r_k=rwkv_r_k, rwkv_ln_w=rwkv_ln_w, rwkv_ln_b=rwkv_ln_b, moe_w_grp=moe_w_grp,
                  moe_b_grp=moe_b_grp, moe_w_exp=moe_w_exp, moe_b_exp=moe_b_exp, moe_w1=moe_w1,
                  moe_w3=moe_w3, moe_w2=moe_w2)
    x2 = x.reshape(bsz * seqlen, d)
    for layer in range(ada_w.shape[0]):
        mod = _ada(c, ada_w[layer], ada_b[layer])
        x2 = _layer(x2, mod, {k: v[layer] for k, v in params.items()}, bsz, seqlen)
    return x2.reshape(bsz, seqlen, d)
```

```python
import functools
import math

import jax
import jax.numpy as jnp
from jax import lax
from jax.experimental import pallas as pl
from jax.experimental.pallas import tpu as pltpu

F32 = jnp.float32
BF16 = jnp.bfloat16

SSM_GROUP = 16
S5_SLAB = 256
S5_TIME_BLOCK = 64
RWKV_HEAD = 64
RWKV_CHUNK = 64
RWKV_BATCH_PER_STEP = 4
HEADS_PER_TILE = 4
LORA_W = 64
LORA_A = 64
LORA_G = 128
N_GROUPS = 8
EXPERTS_PER_GROUP = 8
N_EXPERTS = N_GROUPS * EXPERTS_PER_GROUP
EXPERT_BLOCK = 256
RMS_EPS = 1e-6
GN_EPS = 64e-5
DECAY_SCALE = math.exp(-0.5)
LANES = 128
COMBINE_CHUNK = 64
SMEM_INDEX_BLOCK = 1024
ROUTE_LANES = LANES
V7X_VMEM_BYTES = 64 * 1024 * 1024
VMEM_LIMIT = V7X_VMEM_BYTES - 12 * 1024 * 1024


def _cparams(*sem):
    return pltpu.CompilerParams(dimension_semantics=sem, vmem_limit_bytes=VMEM_LIMIT)


def _sigmoid(x):
    return 1.0 / (1.0 + jnp.exp(-x))


def _dot(a, b):
    return jnp.dot(a, b, preferred_element_type=F32)


def _dot_nt(a, b):
    return lax.dot_general(a, b, (((1,), (1,)), ((), ())), preferred_element_type=F32)


def _dot_tn(a, b):
    return lax.dot_general(a, b, (((0,), (0,)), ((), ())), preferred_element_type=F32)


def _split_bf16(x):
    hi = x.astype(BF16)
    lo = (x - hi.astype(F32)).astype(BF16)
    return hi, lo


def _pack_pair(a, b):
    ua = lax.bitcast_convert_type(a.astype(BF16).astype(F32), jnp.uint32)
    ub = lax.bitcast_convert_type(b.astype(BF16).astype(F32), jnp.uint32)
    return ub | (ua >> 16)


def _unpack_pair(w):
    lo = lax.bitcast_convert_type(w << 16, F32)
    hi = lax.bitcast_convert_type(w & jnp.uint32(0xFFFF0000), F32)
    return lo, hi


def _store_packed(ref, val, n_rows):
    d = val.shape[1]
    slabs = d // (2 * LANES)
    for s in range(slabs):
        a = val[:, s * LANES:(s + 1) * LANES]
        b = val[:, d // 2 + s * LANES:d // 2 + (s + 1) * LANES]
        ref[pl.ds(s, n_rows, stride=slabs), :] = _pack_pair(a, b)


def _load_packed(ref, n_rows, slabs):
    return [_unpack_pair(ref[pl.ds(s, n_rows, stride=slabs), :]) for s in range(slabs)]


def _side_cast(src_ref, dst_ref, regroup):
    for e in range(src_ref.shape[0]):
        if regroup:
            slabs = dst_ref.shape[1]
            half = slabs * LANES
            for s in range(slabs):
                dst_ref[e, s, :LANES, :] = src_ref[e, s * LANES:(s + 1) * LANES, :].astype(BF16)
                dst_ref[e, s, LANES:, :] = src_ref[e, half + s * LANES:half + (s + 1) * LANES, :].astype(BF16)
        else:
            dst_ref[e] = src_ref[e].astype(BF16)


def _side_io(w, n_steps, step_of, regroup):
    n_e, rows, cols = w.shape
    per = n_e // n_steps
    assert per * n_steps == n_e, (n_e, n_steps)
    in_spec = pl.BlockSpec((per, rows, cols), lambda *i: (step_of(*i), 0, 0))
    if regroup:
        shape = (n_e, rows // (2 * LANES), 2 * LANES, cols)
        out_spec = pl.BlockSpec((per,) + shape[1:], lambda *i: (step_of(*i), 0, 0, 0))
    else:
        shape = w.shape
        out_spec = pl.BlockSpec((per, rows, cols), lambda *i: (step_of(*i), 0, 0))
    return in_spec, out_spec, jax.ShapeDtypeStruct(shape, BF16)


def _ada_kernel(c_ref, w_ref, b_ref, o_ref):
    c = c_ref[...]
    bsz = c.shape[0]
    c_hi, c_lo = _split_bf16(c * _sigmoid(c))
    w_hi, w_lo = _split_bf16(w_ref[...])
    first = _dot(jnp.concatenate([c_hi, c_lo], axis=0), w_hi)
    o_ref[...] = first[:bsz] + first[bsz:] + _dot(c_hi, w_lo) + b_ref[...]


def _ada(c, ada_w, ada_b):
    bsz, d = c.shape
    n = ada_w.shape[1]
    tn = 1024
    return pl.pallas_call(
        _ada_kernel,
        grid=(n // tn,),
        in_specs=[pl.BlockSpec((bsz, d), lambda j: (0, 0)),
                  pl.BlockSpec((d, tn), lambda j: (0, j)),
                  pl.BlockSpec((1, tn), lambda j: (0, j))],
        out_specs=pl.BlockSpec((bsz, tn), lambda j: (0, j)),
        out_shape=jax.ShapeDtypeStruct((bsz, n), F32),
        compiler_params=_cparams("arbitrary"),
        name="ada_mod",
    )(c, ada_w, ada_b.reshape(1, n))


def _rms(x, g):
    return x * lax.rsqrt(jnp.mean(x * x, axis=-1, keepdims=True) + RMS_EPS) * g


def _norm_proj_kernel(x_ref, g_ref, sc_ref, sh_ref, w_ref, side_ref, o_ref, side_out):
    h = _rms(x_ref[...], g_ref[...]) * (1.0 + sc_ref[0]) + sh_ref[0]
    o_ref[...] = _dot(h.astype(BF16), w_ref[...]).astype(o_ref.dtype)
    _side_cast(side_ref, side_out, regroup=False)


def _norm_proj(x2, g, sc, sh, w, side_w, seqlen, tm, name):
    n, d = x2.shape
    nout = w.shape[1]
    tpb = seqlen // tm
    side_in, side_out, side_shape = _side_io(side_w, n // tm, lambda i: i, regroup=False)
    return pl.pallas_call(
        _norm_proj_kernel,
        grid=(n // tm,),
        in_specs=[pl.BlockSpec((tm, d), lambda i: (i, 0)),
                  pl.BlockSpec((1, d), lambda i: (0, 0)),
                  pl.BlockSpec((1, 1, d), lambda i: (i // tpb, 0, 0)),
                  pl.BlockSpec((1, 1, d), lambda i: (i // tpb, 0, 0)),
                  pl.BlockSpec((d, nout), lambda i: (0, 0), pipeline_mode=pl.Buffered(1)),
                  side_in],
        out_specs=[pl.BlockSpec((tm, nout), lambda i: (i, 0)), side_out],
        out_shape=[jax.ShapeDtypeStruct((n, nout), BF16), side_shape],
        compiler_params=_cparams("parallel"),
        name=name,
    )(x2, g.reshape(1, d), sc[:, None, :], sh[:, None, :], w, side_w)


def _s5_tables(lam_re, lam_im, log_dt, b_re, b_im, c_re, c_im):
    g, p, cg = b_re.shape
    gs = S5_SLAB // cg
    ns = g // gs
    lr = jnp.minimum(lam_re.astype(F32), -1e-4)
    li = lam_im.astype(F32)
    dt = jnp.exp(log_dt.astype(F32))[:, None]
    mag = jnp.exp(lr * dt)
    ar, ai = mag * jnp.cos(li * dt), mag * jnp.sin(li * dt)
    den = lr * lr + li * li
    qr = ((ar - 1.0) * lr + ai * li) / den
    qi = (ai * lr - (ar - 1.0) * li) / den
    br, bi = b_re.astype(F32), b_im.astype(F32)
    bbr = qr[..., None] * br - qi[..., None] * bi
    bbi = qr[..., None] * bi + qi[..., None] * br

    def in_rows(t):
        return t.reshape(ns, gs, p, cg).transpose(0, 1, 3, 2).reshape(ns, gs * cg, p).astype(BF16)

    def out_cols(t):
        return t.reshape(ns, gs, cg, p).transpose(0, 3, 1, 2).reshape(ns, p, gs * cg).astype(BF16)

    a_tab = jnp.stack([ar.reshape(ns, gs * p), ai.reshape(ns, gs * p)], axis=1)
    return (in_rows(bbr), in_rows(bbi), out_cols(c_re.astype(F32)), out_cols(-c_im.astype(F32)), a_tab)


def _gelu_tanh(x):
    return 0.5 * x * (1.0 + jnp.tanh(math.sqrt(2.0 / math.pi) * (x + 0.044715 * (x * x * x))))


def _s5_kernel(x_ref, g_ref, sc_ref, sh_ref, wu_ref, perm_ref, bre_ref, bim_ref, cre_ref, cim_ref, a_ref,
               d_ref, gw_ref, gb_ref, o_ref, b_ref, c_ref, u_scr, bscr, sscr, yscr, st_ref):
    bsz, lb, d_in = x_ref.shape
    rows = bsz * lb
    dch = wu_ref.shape[1]
    half = st_ref.shape[2] // 2
    n_p = bre_ref.shape[2]

    @pl.when(pl.program_id(0) == 0)
    def _():
        st_ref[...] = jnp.zeros_like(st_ref)
        tile_in = (lax.broadcasted_iota(jnp.int32, (n_p, half), 0)
                   == lax.broadcasted_iota(jnp.int32, (n_p, half), 1) % n_p).astype(BF16)
        tile_out = (lax.broadcasted_iota(jnp.int32, (half, n_p), 0) % n_p
                    == lax.broadcasted_iota(jnp.int32, (half, n_p), 1)).astype(BF16)
        in_mask = (lax.broadcasted_iota(jnp.int32, (S5_SLAB, half), 0) // SSM_GROUP
                   == lax.broadcasted_iota(jnp.int32, (S5_SLAB, half), 1) // n_p)
        out_mask = (lax.broadcasted_iota(jnp.int32, (half, S5_SLAB), 0) // n_p
                    == lax.broadcasted_iota(jnp.int32, (half, S5_SLAB), 1) // SSM_GROUP)
        for s in range(dch // S5_SLAB):
            b_ref[s, :, :half] = jnp.where(in_mask, _dot(bre_ref[s], tile_in), 0.0).astype(BF16)
            b_ref[s, :, half:] = jnp.where(in_mask, _dot(bim_ref[s], tile_in), 0.0).astype(BF16)
            c_ref[s, :half, :] = jnp.where(out_mask, _dot(tile_out, cre_ref[s]), 0.0).astype(BF16)
            c_ref[s, half:, :] = jnp.where(out_mask, _dot(tile_out, cim_ref[s]), 0.0).astype(BF16)

    h = (_rms(x_ref[...], g_ref[...]) * (1.0 + sc_ref[...]) + sh_ref[...]).astype(BF16)
    u_nat = _dot(h.reshape(rows, d_in), wu_ref[...]).astype(BF16)
    u_scr[...] = _dot(perm_ref[...], u_nat).astype(BF16)
    n_slab = dch // S5_SLAB
    slab = lambda s: slice(s * S5_SLAB, (s + 1) * S5_SLAB)

    def project_in(s):
        bscr[s % 2] = _dot(u_scr[:, slab(s)], b_ref[s])

    project_in(0)
    for s in range(n_slab):
        if s + 1 < n_slab:
            project_in(s + 1)
        buf = s % 2
        a_re = a_ref[s, 0:1, :]
        a_im = a_ref[s, 1:2, :]
        s_r = st_ref[s, :, :half]
        s_i = st_ref[s, :, half:]
        for l in range(rows // bsz):
            at_l = slice(l * bsz, (l + 1) * bsz)
            s_r, s_i = (a_re * s_r - a_im * s_i + bscr[buf, at_l, :half],
                        a_re * s_i + a_im * s_r + bscr[buf, at_l, half:])
            sscr[buf, at_l, :half] = s_r
            sscr[buf, at_l, half:] = s_i
        st_ref[s, :, :half] = s_r
        st_ref[s, :, half:] = s_i
        y = _dot(sscr[buf].astype(BF16), c_ref[s]) + d_ref[:, slab(s)] * u_scr[:, slab(s)].astype(F32)
        yscr[:, slab(s)] = _gelu_tanh(y).astype(BF16)
    y = yscr[...]
    gate = _sigmoid(_dot(y, gw_ref[...]) + gb_ref[...])
    out_tm = (y.astype(F32) * gate).astype(BF16)
    out_nat = _dot_tn(perm_ref[...], out_tm).astype(o_ref.dtype)
    o_ref[...] = out_nat.reshape(bsz, lb, dch)


def _s5_glu(x2, g, sc, sh, w_u, tabs, d_skip, glu_w, glu_b, bsz, seqlen, lb):
    b_re, b_im, c_re, c_im, a_tab = tabs
    d_in, dch = w_u.shape
    ns, _, n_half = a_tab.shape
    n_state = 2 * n_half
    rows = lb * bsz
    r_idx = jnp.arange(rows)
    perm = ((r_idx % bsz) * lb + r_idx // bsz)[:, None] == r_idx[None, :]
    full = lambda a: pl.BlockSpec(a.shape, lambda i: (0,) * a.ndim, pipeline_mode=pl.Buffered(1))
    args = (x2.reshape(bsz, seqlen, d_in), g.reshape(1, 1, d_in), sc[:, None, :], sh[:, None, :], w_u,
            perm.astype(BF16), b_re, b_im, c_re, c_im, a_tab,
            d_skip.astype(F32).reshape(1, dch), glu_w.astype(BF16), glu_b.astype(F32).reshape(1, dch))
    y = pl.pallas_call(
        _s5_kernel,
        grid=(seqlen // lb,),
        in_specs=[pl.BlockSpec((bsz, lb, d_in), lambda i: (0, i, 0))] + [full(a) for a in args[1:]],
        out_specs=pl.BlockSpec((bsz, lb, dch), lambda i: (0, i, 0)),
        out_shape=jax.ShapeDtypeStruct((bsz, seqlen, dch), BF16),
        scratch_shapes=[pltpu.VMEM((ns, S5_SLAB, n_state), BF16), pltpu.VMEM((ns, n_state, S5_SLAB), BF16),
                        pltpu.VMEM((rows, dch), BF16), pltpu.VMEM((2, rows, n_state), F32),
                        pltpu.VMEM((2, rows, n_state), F32), pltpu.VMEM((rows, dch), BF16),
                        pltpu.VMEM((ns, bsz, n_state), F32)],
        compiler_params=_cparams("arbitrary"),
        name="s5_mixer_glu",
    )(*args)
    return y.reshape(bsz * seqlen, dch)


def _rwkv_kernel(z_ref, mu_ref, w0_ref, a0_ref, kkw_ref, ka_ref, wa_ref, gup_ref, rk_ref, lnw_ref, lnb_ref,
                 ones_ref, side1_ref, side3_ref, o_ref, side1_out, side3_out, s_ref, zlast_ref):
    _side_cast(side1_ref, side1_out, regroup=True)
    _side_cast(side3_ref, side3_out, regroup=True)
    nb, t, _ = z_ref.shape
    d_r = o_ref.shape[2]
    tile = HEADS_PER_TILE * RWKV_HEAD
    n_tiles = d_r // tile
    first_chunk = pl.program_id(1) == 0

    @pl.when(first_chunk)
    def _():
        s_ref[...] = jnp.zeros_like(s_ref)

    ones_bd = ones_ref[...]

    def seg_sum(x):
        return _dot(x.astype(BF16), ones_bd)

    row = lax.broadcasted_iota(jnp.int32, (t, t), 0)
    col = lax.broadcasted_iota(jnp.int32, (t, t), 1)
    tri = (row >= col).astype(BF16)
    st = HEADS_PER_TILE * t
    rs = lax.broadcasted_iota(jnp.int32, (2 * t, 2 * st), 0)
    t_r = rs % t
    t_c = lax.broadcasted_iota(jnp.int32, (2 * t, 2 * st), 1) % t
    keep = (t_r > t_c) | ((rs >= t) & (t_r == t_c))
    eye_w = (lax.broadcasted_iota(jnp.int32, (t, st), 0)
             == lax.broadcasted_iota(jnp.int32, (t, st), 1) % t).astype(F32)
    blk_mask = (lax.broadcasted_iota(jnp.int32, (st, st), 0) // t
                == lax.broadcasted_iota(jnp.int32, (st, st), 1) // t)
    lane = lax.broadcasted_iota(jnp.int32, (1, tile), 1)
    head_masks = [(lane >= j * RWKV_HEAD) & (lane < (j + 1) * RWKV_HEAD) for j in range(HEADS_PER_TILE)]
    bd_r = lax.broadcasted_iota(jnp.int32, (tile, tile), 0) // RWKV_HEAD
    bd_c = lax.broadcasted_iota(jnp.int32, (tile, tile), 1) // RWKV_HEAD
    bd_mask = bd_r == bd_c
    n_levels = int(math.log2(t))
    slices = [slice(hg * tile, (hg + 1) * tile) for hg in range(n_tiles)]
    units = [(bb, hg) for bb in range(nb) for hg in range(n_tiles)]
    n_u = range(len(units))

    def stack(x):
        zero = jnp.zeros_like(x)
        return jnp.concatenate([jnp.where(m, x, zero) for m in head_masks], axis=0)

    def bf(x):
        return x.astype(BF16)

    at, qt, bt, kt, vv, em, etm, wtot, rkb, gate = ([] for _ in range(10))
    row0 = lax.broadcasted_iota(jnp.int32, (t, 1), 0) == 0
    lora_lane = lax.broadcasted_iota(jnp.int32, (t, LORA_W + LORA_A), 1)
    for bb in range(nb):
        z = z_ref[bb].astype(F32)
        prev_row = jnp.where(first_chunk, 0.0, zlast_ref[bb, 0:1, :])
        zl = z + mu_ref[...] * (jnp.where(row0, prev_row, pltpu.roll(z, 1, 0)) - z)
        zlast_ref[bb, 0:1, :] = z[t - 1:t, :]
        xwa = zl[:, 3 * d_r:3 * d_r + LORA_W + LORA_A]
        lhs = jnp.where(lora_lane < LORA_W, jnp.tanh(xwa), xwa).astype(BF16)
        wa = _dot(lhs, wa_ref[...])
        lw = -DECAY_SCALE * _sigmoid(w0_ref[...] + wa[:, :d_r])
        asig = _sigmoid(a0_ref[...] + wa[:, d_r:])
        r = zl[:, :d_r]
        k_raw = zl[:, d_r:2 * d_r]
        kk = k_raw * kkw_ref[...]
        kp = k_raw * (1.0 + (asig - 1.0) * ka_ref[...])
        xg = zl[:, 3 * d_r + LORA_W + LORA_A:3 * d_r + LORA_W + LORA_A + LORA_G]
        g_full = _dot(_sigmoid(xg).astype(BF16), gup_ref[...])
        lw_hi, lw_lo = _split_bf16(lw)
        cum = _dot(tri, lw_hi) + _dot(tri, lw_lo)
        mid = cum[t // 2 - 1:t // 2, :]
        tot = cum[t - 1:t, :]
        e1 = jnp.exp(cum - mid)
        e2 = jnp.exp(mid - cum)
        e1p = e1 * jnp.exp(-lw)
        kk2 = kk * kk
        kkn = kk / jnp.maximum(jnp.sqrt(jnp.concatenate([seg_sum(kk2[:, sl]) for sl in slices], axis=1)), 1e-12)
        full = dict(at=-kkn * e1p, qt=r * e1, bt=kkn * asig * e2, kt=kp * e2,
                    vv=zl[:, 2 * d_r:3 * d_r], em=jnp.exp(mid), etm=jnp.exp(tot - mid), wtot=jnp.exp(tot),
                    rkb=r * kp * rk_ref[...], gate=g_full)
        for dst, key in ((at, "at"), (qt, "qt"), (bt, "bt"), (kt, "kt"), (vv, "vv"), (em, "em"),
                         (etm, "etm"), (wtot, "wtot"), (rkb, "rkb"), (gate, "gate")):
            dst.extend(full[key][:, sl] for sl in slices)

    s_old = [s_ref[i] for i in n_u]
    wide = []
    for i in n_u:
        lhs = bf(jnp.concatenate([at[i], qt[i]], axis=0))
        rhs = jnp.concatenate([stack(bf(bt[i])), stack(bf(kt[i]))], axis=0)
        a_w = bf(jnp.where(keep, _dot_nt(lhs, rhs), 0.0))
        wide.append([a_w[:t, :st], a_w[:t, st:], a_w[t:, :]])
    x_state = [_dot_nt(bf(jnp.concatenate([at[i], qt[i]], axis=0) * em[i]), bf(s_old[i]))
               for i in n_u]
    sv = [stack(bf(vv[i])) for i in n_u]
    akv = [_dot(wide[i][1], sv[i]) for i in n_u]

    def expand(x_w):
        x_b = bf(x_w)
        return jnp.where(blk_mask, jnp.concatenate([x_b] * HEADS_PER_TILE, axis=0), jnp.zeros((st, st), BF16))

    p_acc = [eye_w + w[0].astype(F32) for w in wide]
    q_bd = [expand(w[0]) for w in wide]
    q_pow = [_dot(wide[i][0], q_bd[i]) for i in n_u]
    for lev in range(1, n_levels):
        for i in n_u:
            q_bd[i] = expand(q_pow[i])
            if lev < n_levels - 1:
                both = _dot(bf(jnp.concatenate([p_acc[i], q_pow[i]], axis=0)), q_bd[i])
                p_acc[i] = p_acc[i] + both[:t]
                q_pow[i] = both[t:]
            else:
                p_acc[i] = p_acc[i] + _dot(bf(p_acc[i]), q_bd[i])
    u_all = [_dot(bf(p_acc[i]), stack(bf(x_state[i][:t] + akv[i]))) for i in n_u]
    y_all = [x_state[i][t:]
             + _dot(wide[i][2], jnp.concatenate([stack(bf(u_all[i])), sv[i]], axis=0)) for i in n_u]

    for i, (bb, hg) in enumerate(units):
        sl = slices[hg]
        y = y_all[i]
        uv = bf(jnp.concatenate([u_all[i], vv[i]], axis=0))
        bk_end = bf(jnp.concatenate([bt[i], kt[i]], axis=0) * etm[i])
        s_ref[i] = s_old[i] * wtot[i] + jnp.where(bd_mask, _dot_tn(uv, bk_end), 0.0)

        mean = seg_sum(y) * (1.0 / RWKV_HEAD)
        dlt = y - mean
        var = seg_sum(dlt * dlt) * (1.0 / RWKV_HEAD)
        yn = dlt * lax.rsqrt(var + GN_EPS) * lnw_ref[:, sl] + lnb_ref[:, sl]
        out = (yn + seg_sum(rkb[i]) * vv[i]) * gate[i]
        o_ref[bb, :, sl] = out.astype(o_ref.dtype)


def _rwkv(z, mu, w0, w_up, a0, a_up, g_up, k_k, k_a, r_k, ln_w, ln_b, side_w1, side_w3, bsz, seqlen):
    n, dz = z.shape
    d_r = w0.shape[0]
    t = RWKV_CHUNK
    nb = RWKV_BATCH_PER_STEP
    nch = seqlen // t
    tile = HEADS_PER_TILE * RWKV_HEAD
    hid = jnp.arange(tile) // RWKV_HEAD
    ones_bd = (hid[:, None] == hid[None, :]).astype(BF16)
    wa = jnp.zeros((LORA_W + LORA_A, 2 * d_r), F32)
    wa = wa.at[:LORA_W, :d_r].set(w_up.astype(F32)).at[LORA_W:, d_r:].set(a_up.astype(F32)).astype(BF16)
    row = lambda a: a.astype(F32).reshape(1, -1)
    full = lambda a: pl.BlockSpec(a.shape, lambda b, c: (0, 0))
    params = (row(mu), row(w0), row(a0), row(k_k), row(k_a), wa, g_up.astype(BF16), row(r_k), row(ln_w),
              row(ln_b), ones_bd)
    side_in, side_out, side_shape = _side_io(side_w1, (bsz // nb) * nch, lambda b, c: b * nch + c, regroup=True)
    y, side1, side3 = pl.pallas_call(
        _rwkv_kernel,
        grid=(bsz // nb, nch),
        in_specs=([pl.BlockSpec((nb, t, dz), lambda b, c: (b, c, 0))] + [full(a) for a in params]
                  + [side_in, side_in]),
        out_specs=[pl.BlockSpec((nb, t, d_r), lambda b, c: (b, c, 0)), side_out, side_out],
        out_shape=[jax.ShapeDtypeStruct((bsz, seqlen, d_r), BF16), side_shape, side_shape],
        scratch_shapes=[pltpu.VMEM((nb * (d_r // tile), tile, tile), F32), pltpu.VMEM((nb, 8, dz), F32)],
        compiler_params=_cparams("parallel", "arbitrary"),
        name="rwkv7_chunked",
    )(z.reshape(bsz, seqlen, dz), *params, side_w1, side_w3)
    return y.reshape(n, d_r), side1, side3


def _post_mix_kernel(ys_ref, yr_ref, wo1_ref, wo2_ref, x_ref, g1_ref, gt_ref, g2_ref, sc_ref, sh_ref,
                     wr_both_ref, br_ref, x1_out, h2_out, info_out, cnt_out, carry):
    nl = info_out.shape[1]
    mixed = _dot(ys_ref[...], wo1_ref[...]) + _dot(yr_ref[...], wo2_ref[...])
    x1 = x_ref[...] + gt_ref[0] * _rms(mixed, g1_ref[...])
    x1_out[...] = x1
    h2 = _rms(x1, g2_ref[...]) * (1.0 + sc_ref[0]) + sh_ref[0]
    _store_packed(h2_out, h2, h2.shape[0])
    hi, lo = _split_bf16(h2)
    both = _dot(hi, wr_both_ref[...])
    logits = both[:, :nl] + both[:, nl:] + _dot(lo, wr_both_ref[:, :nl]) + br_ref[...]
    _route_tile(logits, info_out, cnt_out, carry)


def _post_mix(ys, yr, w_out, x2, g1, gt1, g2, sc2, sh2, w_route, b_route, seqlen, tm):
    n, d = x2.shape
    ds = ys.shape[1]
    tpb = seqlen // tm
    slabs = d // (2 * LANES)
    wr_both = jnp.concatenate(_split_bf16(w_route), axis=1)
    rows = lambda w: pl.BlockSpec((tm, w), lambda i: (i, 0))
    full = lambda a: pl.BlockSpec(a.shape, lambda i: (0,) * a.ndim)
    bat = pl.BlockSpec((1, 1, d), lambda i: (i // tpb, 0, 0))
    args = (ys, yr, w_out[:ds].astype(BF16), w_out[ds:].astype(BF16), x2, g1.reshape(1, d), gt1[:, None, :],
            g2.reshape(1, d), sc2[:, None, :], sh2[:, None, :], wr_both, b_route.reshape(1, -1))
    in_specs = [rows(ds), rows(yr.shape[1]), full(args[2]), full(args[3]), rows(d), full(args[5]), bat,
                full(args[7]), bat, bat, full(wr_both), full(args[11])]
    return pl.pallas_call(
        _post_mix_kernel,
        grid=(n // tm,),
        in_specs=in_specs,
        out_specs=[rows(d), pl.BlockSpec((tm * slabs, LANES), lambda i: (i, 0)), rows(ROUTE_LANES),
                   pl.BlockSpec((1, ROUTE_LANES), lambda i: (0, 0))],
        out_shape=[jax.ShapeDtypeStruct((n, d), F32), jax.ShapeDtypeStruct((n * slabs, LANES), jnp.uint32),
                   jax.ShapeDtypeStruct((n, ROUTE_LANES), F32), jax.ShapeDtypeStruct((1, ROUTE_LANES), F32)],
        scratch_shapes=[pltpu.VMEM((1, ROUTE_LANES), F32)],
        compiler_params=_cparams("arbitrary"),
        name="out_proj_post",
    )(*args)


def _route_tile(lg, info_ref, cnt_ref, carry):
    i = pl.program_id(0)
    tm = lg.shape[0]

    @pl.when(i == 0)
    def _():
        carry[...] = jnp.zeros_like(carry)

    lane =lax.broadcasted_iota(jnp.int32, lg.shape, 1)
    lane_f = lane.astype(F32)
    neg = jnp.float32(-jnp.inf)
    big = jnp.float32(1e9)
    is_g = (lane >= N_EXPERTS) & (lane < N_EXPERTS + N_GROUPS)
    gl = jnp.where(is_g, lg, neg)
    gmax = jnp.max(gl, axis=-1, keepdims=True)
    gidx = jnp.min(jnp.where(gl == gmax, lane_f - N_EXPERTS, big), axis=-1, keepdims=True)
    p_grp = 1.0 / jnp.sum(jnp.where(is_g, jnp.exp(gl - gmax), 0.0), axis=-1, keepdims=True)
    in_grp = (lane < N_EXPERTS) & ((lane // EXPERTS_PER_GROUP).astype(F32) == gidx)
    el = jnp.where(in_grp, lg, neg)
    m1 = jnp.max(el, axis=-1, keepdims=True)
    i1 = jnp.min(jnp.where(el == m1, lane_f, big), axis=-1, keepdims=True)
    el2 = jnp.where(lane_f == i1, neg, el)
    m2 = jnp.max(el2, axis=-1, keepdims=True)
    i2 = jnp.min(jnp.where(el2 == m2, lane_f, big), axis=-1, keepdims=True)
    ex = jnp.exp(m2 - m1)
    w1 = p_grp / (1.0 + ex)
    w2 = p_grp * ex / (1.0 + ex)

    oh1 = lane_f == i1
    oh2 = lane_f == i2
    onehot = (oh1 | oh2).astype(BF16)
    rr = lax.broadcasted_iota(jnp.int32, (tm, tm), 0)
    cc = lax.broadcasted_iota(jnp.int32, (tm, tm), 1)
    before = _dot((rr > cc).astype(BF16), onehot) + carry[...]
    rank1 = jnp.sum(jnp.where(oh1, before, 0.0), axis=-1, keepdims=True)
    rank2 = jnp.sum(jnp.where(oh2, before, 0.0), axis=-1, keepdims=True)
    carry[...] = carry[...] + jnp.sum(onehot.astype(F32), axis=0, keepdims=True)
    cnt_ref[...] = carry[...]

    info = jnp.where(lane == 0, i1, 0.0)
    info = jnp.where(lane == 1, i2, info)
    info = jnp.where(lane == 2, w1, info)
    info = jnp.where(lane == 3, w2, info)
    info = jnp.where(lane == 4, rank1, info)
    info = jnp.where(lane == 5, rank2, info)
    info_ref[...] = info


def _slot_rows_kernel(info_ref, seg_ref, o_ref, *, slabs):
    info = info_ref[...]
    lane = lax.broadcasted_iota(jnp.int32, info.shape, 1)
    lane_f = lane.astype(F32)
    seg = seg_ref[...]
    d0 = jnp.sum(jnp.where(lane_f == info[:, 0:1], seg, 0.0), axis=-1, keepdims=True) + info[:, 4:5] * slabs
    d1 = jnp.sum(jnp.where(lane_f == info[:, 1:2], seg, 0.0), axis=-1, keepdims=True) + info[:, 5:6] * slabs
    o_ref[...] = jnp.where(lane == 0, d0, jnp.where(lane == 1, d1, 0.0)).astype(jnp.int32)


def _slot_rows(info, seg_row, slabs, tm):
    n = info.shape[0]
    return pl.pallas_call(
        functools.partial(_slot_rows_kernel, slabs=slabs),
        grid=(n // tm,),
        in_specs=[pl.BlockSpec((tm, ROUTE_LANES), lambda i: (i, 0)),
                  pl.BlockSpec((1, ROUTE_LANES), lambda i: (0, 0))],
        out_specs=pl.BlockSpec((tm, ROUTE_LANES), lambda i: (i, 0)),
        out_shape=jax.ShapeDtypeStruct((n, ROUTE_LANES), jnp.int32),
        compiler_params=_cparams("parallel"),
        name="moe_slot_rows",
    )(info, seg_row)


def _dispatch_kernel(d0_ref, d1_ref, pad_row_ref, pad_len_ref, na_ref, h_ref, buf_out, zeros, sem, zsem, *,
                     slabs, n_blocks):
    tm = h_ref.shape[0] // slabs

    @pl.when(pl.program_id(0) == 0)
    def _():
        zeros[...] = jnp.zeros_like(zeros)
        bits = [1 << b for b in reversed(range(EXPERT_BLOCK.bit_length() - 1))]

        def pad_copy(e, bit):
            done = pad_len_ref[e] & ~(2 * bit - 1)
            dst = pl.multiple_of(pad_row_ref[e] + done * slabs, slabs)
            return pltpu.make_async_copy(zeros.at[pl.ds(0, bit * slabs)], buf_out.at[pl.ds(dst, bit * slabs)], zsem)

        def block_copy(j):
            dst = pl.multiple_of(j * (EXPERT_BLOCK * slabs), EXPERT_BLOCK * slabs)
            return pltpu.make_async_copy(zeros, buf_out.at[pl.ds(dst, EXPERT_BLOCK * slabs)], zsem)

        def each(action):
            def per_expert(e, c):
                for bit in bits:
                    @pl.when((pad_len_ref[e] & bit) != 0)
                    def _():
                        action(pad_copy(e, bit))
                return c

            def per_block(j, c):
                action(block_copy(j))
                return c

            lax.fori_loop(0, N_EXPERTS, per_expert, 0)
            lax.fori_loop(na_ref[0], n_blocks, per_block, 0)

        each(lambda c: c.start())
        each(lambda c: c.wait())

    def copy(t, dest_ref):
        src = pl.multiple_of(t * slabs, slabs)
        dst = pl.multiple_of(dest_ref[t], slabs)
        return pltpu.make_async_copy(h_ref.at[pl.ds(src, slabs)], buf_out.at[pl.ds(dst, slabs)], sem)

    def start(t, c):
        copy(t, d0_ref).start(priority=0)
        copy(t, d1_ref).start(priority=1)
        return c

    lax.fori_loop(0, tm, start, 0, unroll=4)
    for _ in range(2):
        pltpu.make_async_copy(h_ref, buf_out.at[pl.ds(0, tm * slabs)], sem).wait()


def _dispatch(h2p, dest_rows, pad_row, pad_len, n_active, cap, slabs, tm):
    n = h2p.shape[0] // slabs
    smem = pl.BlockSpec((tm,), lambda i: (i,), memory_space=pltpu.SMEM)
    table = lambda a: pl.BlockSpec(a.shape, lambda i: (0,), memory_space=pltpu.SMEM)
    return pl.pallas_call(
        functools.partial(_dispatch_kernel, slabs=slabs, n_blocks=cap // EXPERT_BLOCK),
        grid=(n // tm,),
        in_specs=[smem, smem, table(pad_row), table(pad_len), table(n_active),
                  pl.BlockSpec((tm * slabs, LANES), lambda i: (i, 0))],
        out_specs=pl.BlockSpec(memory_space=pl.ANY),
        out_shape=jax.ShapeDtypeStruct((cap * slabs, LANES), h2p.dtype),
        scratch_shapes=[pltpu.VMEM((EXPERT_BLOCK * slabs, LANES), h2p.dtype), pltpu.SemaphoreType.DMA,
                        pltpu.SemaphoreType.DMA],
        compiler_params=_cparams("arbitrary"),
        name="moe_dispatch",
    )(dest_rows[0], dest_rows[1], pad_row, pad_len, n_active, h2p)


def _moe_kernel(na_ref, eseq_ref, epos_ref, nd_ref, x_ref, w1_hbm, w3_hbm, w2_hbm, o_ref,
                w1b, w3b, w2b, sem):
    j = pl.program_id(0)
    active = j < na_ref[0]
    pos = epos_ref[j]
    fresh = (j == 0) | (pos != epos_ref[jnp.maximum(j - 1, 0)])
    slabs = w1b.shape[1]

    def weight_copies(p):
        e = eseq_ref[p]
        slot = p % 2
        return [pltpu.make_async_copy(w_hbm.at[e], stage.at[slot], sem.at[slot, i])
                for i, (w_hbm, stage) in enumerate(((w1_hbm, w1b), (w3_hbm, w3b), (w2_hbm, w2b)))]

    @pl.when(j == 0)
    def _():
        for c in weight_copies(0):
            c.start()

    @pl.when(active & fresh)
    def _():
        for c in weight_copies(pos):
            c.wait()

        @pl.when(pos + 1 < nd_ref[0])
        def _():
            for c in weight_copies(pos + 1):
                c.start()

    @pl.when(active)
    def _():
        slot = pos % 2
        acc1 = jnp.zeros((EXPERT_BLOCK, w1b.shape[3]), F32)
        acc3 = jnp.zeros((EXPERT_BLOCK, w1b.shape[3]), F32)
        for s, (lo, hi) in enumerate(_load_packed(x_ref, EXPERT_BLOCK, slabs)):
            lhs = jnp.concatenate([lo.astype(BF16), hi.astype(BF16)], axis=1)
            acc1 = acc1 + _dot(lhs, w1b[slot, s])
            acc3 = acc3 + _dot(lhs, w3b[slot, s])
        hid = (acc1 * _sigmoid(acc1)) * acc3
        _store_packed(o_ref, _dot(hid.astype(BF16), w2b[slot]), EXPERT_BLOCK)


def _moe(x_buf, n_active, expert_seq, block_pos, n_used, w1, w3, w2, slabs):
    cap = x_buf.shape[0] // slabs
    nb = cap // EXPERT_BLOCK

    def xmap(j, na, *_):
        return (jnp.minimum(j, na[0] - 1), 0)

    xspec = pl.BlockSpec((EXPERT_BLOCK * slabs, LANES), xmap)
    hbm = pl.BlockSpec(memory_space=pl.ANY)
    grid_spec = pltpu.PrefetchScalarGridSpec(
        num_scalar_prefetch=4,
        grid=(nb,),
        in_specs=[xspec, hbm, hbm, hbm],
        out_specs=xspec,
        scratch_shapes=[pltpu.VMEM((2,) + w1.shape[1:], BF16), pltpu.VMEM((2,) + w3.shape[1:], BF16),
                        pltpu.VMEM((2,) + w2.shape[1:], BF16), pltpu.SemaphoreType.DMA((2, 3))],
    )
    return pl.pallas_call(
        _moe_kernel,
        grid_spec=grid_spec,
        out_shape=jax.ShapeDtypeStruct(x_buf.shape, x_buf.dtype),
        input_output_aliases={4: 0},
        compiler_params=_cparams("arbitrary"),
        name="moe_experts",
    )(n_active, expert_seq, block_pos, n_used, x_buf, w1, w3, w2)


def _combine_kernel(d0_ref, d1_ref, d0n_ref, d1n_ref, y_ref, info_ref, x1_ref, g_ref, gt_ref, o_ref,
                    rows, sem, *, slabs, tiles_per_idx, chunk):
    i = pl.program_id(0)
    n_tiles = pl.num_programs(0)
    tm = x1_ref.shape[0]
    slot = i % 2
    nxt_slot = 1 - slot
    nxt = jnp.minimum(i + 1, n_tiles - 1)

    def copy(tile, to_slot, t, dest_ref, k):
        off = (tile % tiles_per_idx) * tm
        src = pl.multiple_of(dest_ref[off + t], slabs)
        dst = pl.multiple_of(t * slabs, slabs)
        return pltpu.make_async_copy(y_ref.at[pl.ds(src, slabs)], rows.at[to_slot, k, pl.ds(dst, slabs)],
                                     sem.at[to_slot])

    def drain(which):
        for k in range(2):
            pltpu.make_async_copy(y_ref.at[pl.ds(0, tm * slabs)], rows.at[which, k], sem.at[which]).wait()

    @pl.when(i == 0)
    def _():
        def start(t, c):
            copy(i, slot, t, d0_ref, 0).start(priority=0)
            copy(i, slot, t, d1_ref, 1).start(priority=1)
            return c

        lax.fori_loop(0, tm, start, 0, unroll=4)

    drain(slot)
    g_row = g_ref[...]
    gt_row = gt_ref[0]
    for c in range(tm // chunk):
        tok = slice(c * chunk, (c + 1) * chunk)
        info = info_ref[tok, :]
        w1 = info[:, 2:3]
        w2 = info[:, 3:4]
        lo_parts, hi_parts = [], []
        for s in range(slabs):
            at_s = pl.ds(c * chunk * slabs + s, chunk, stride=slabs)
            lo0, hi0 = _unpack_pair(rows[slot, 0, at_s, :])
            lo1, hi1 = _unpack_pair(rows[slot, 1, at_s, :])
            lo_parts.append(w1 * lo0 + w2 * lo1)
            hi_parts.append(w1 * hi0 + w2 * hi1)
        ffn = jnp.concatenate(lo_parts + hi_parts, axis=1)
        o_ref[tok, :] = x1_ref[tok, :] + gt_row * _rms(ffn, g_row)
        for t in range(c * chunk, (c + 1) * chunk):
            copy(nxt, nxt_slot, t, d0n_ref, 0).start(priority=0)
            copy(nxt, nxt_slot, t, d1n_ref, 1).start(priority=1)

    @pl.when(i == n_tiles - 1)
    def _():
        drain(nxt_slot)


def _combine(y_buf, dest_rows, info, x1, g, gt2, seqlen, slabs, tm):
    n, d = x1.shape
    tpb = seqlen // tm
    idx_block = max(tm, SMEM_INDEX_BLOCK)
    per = idx_block // tm
    last = n // tm - 1
    smem = pl.BlockSpec((idx_block,), lambda i: (i // per,), memory_space=pltpu.SMEM)
    smem_next = pl.BlockSpec((idx_block,), lambda i: (jnp.minimum(i + 1, last) // per,), memory_space=pltpu.SMEM)
    return pl.pallas_call(
        functools.partial(_combine_kernel, slabs=slabs, tiles_per_idx=per, chunk=min(tm, COMBINE_CHUNK)),
        grid=(n // tm,),
        in_specs=[smem, smem, smem_next, smem_next,
                  pl.BlockSpec(memory_space=pl.ANY),
                  pl.BlockSpec((tm, ROUTE_LANES), lambda i: (i, 0)),
                  pl.BlockSpec((tm, d), lambda i: (i, 0)),
                  pl.BlockSpec((1, d), lambda i: (0, 0)),
                  pl.BlockSpec((1, 1, d), lambda i: (i // tpb, 0, 0))],
        out_specs=pl.BlockSpec((tm, d), lambda i: (i, 0)),
        out_shape=jax.ShapeDtypeStruct((n, d), F32),
        scratch_shapes=[pltpu.VMEM((2, 2, tm * slabs, LANES), y_buf.dtype), pltpu.SemaphoreType.DMA((2,))],
        compiler_params=_cparams("arbitrary"),
        name="moe_combine",
    )(dest_rows[0], dest_rows[1], dest_rows[0], dest_rows[1], y_buf, info, x1, g.reshape(1, d),
      gt2[:, None, :])


def _pick(n, pref):
    while n % pref:
        pref //= 2
    return pref


def _layer(x2, mod, p, bsz, seqlen):
    n, d = x2.shape
    sh1, sc1, gt1, sh2, sc2, gt2 = jnp.split(mod, 6, axis=-1)
    d_ssm = p["ssm_d"].shape[0]

    z, w2_bf = _norm_proj(x2, p["norm_mix_pre"], sc1, sh1, p["w_in"][:, d_ssm:].astype(BF16), p["moe_w2"],
                          seqlen, _pick(seqlen, max(n // N_EXPERTS, 16)), "in_proj_z")

    tabs = _s5_tables(p["ssm_lam_re"], p["ssm_lam_im"], p["ssm_log_dt"], p["ssm_b_re"], p["ssm_b_im"],
                      p["ssm_c_re"], p["ssm_c_im"])
    y_ssm = _s5_glu(x2, p["norm_mix_pre"], sc1, sh1, p["w_in"][:, :d_ssm].astype(BF16), tabs, p["ssm_d"], p["glu_w"],
                    p["glu_b"], bsz, seqlen, _pick(seqlen, S5_TIME_BLOCK))

    y_rwkv, w1_bf, w3_bf = _rwkv(z, p["rwkv_mu"], p["rwkv_w0"], p["rwkv_w_up"], p["rwkv_a0"], p["rwkv_a_up"],
                                 p["rwkv_g_up"], p["rwkv_k_k"], p["rwkv_k_a"], p["rwkv_r_k"], p["rwkv_ln_w"],
                                 p["rwkv_ln_b"], p["moe_w1"], p["moe_w3"], bsz, seqlen)

    w_route = jnp.zeros((d, ROUTE_LANES), F32)
    w_route = w_route.at[:, :N_EXPERTS].set(p["moe_w_exp"].astype(F32))
    w_route = w_route.at[:, N_EXPERTS:N_EXPERTS + N_GROUPS].set(p["moe_w_grp"].astype(F32))
    b_route = jnp.zeros((ROUTE_LANES,), F32)
    b_route = b_route.at[:N_EXPERTS].set(p["moe_b_exp"].astype(F32))
    b_route = b_route.at[N_EXPERTS:N_EXPERTS + N_GROUPS].set(p["moe_b_grp"].astype(F32))
    x1, h2p, info, counts = _post_mix(y_ssm, y_rwkv, p["w_out"], x2, p["norm_mix_post"], gt1, p["norm_ffn_pre"],
                                      sc2, sh2, w_route, b_route, seqlen, _pick(seqlen, 512))
    slabs = d // (2 * LANES)

    cnt = counts[0, :N_EXPERTS].astype(jnp.int32)
    padded = (cnt + EXPERT_BLOCK - 1) // EXPERT_BLOCK * EXPERT_BLOCK
    pend = jnp.cumsum(padded)
    pstart = pend - padded
    n_blocks = -(-(2 * n) // EXPERT_BLOCK) + N_EXPERTS
    cap = n_blocks * EXPERT_BLOCK
    seg_row = jnp.zeros((1, ROUTE_LANES), F32).at[0, :N_EXPERTS].set((pstart * slabs).astype(F32))
    dest = _slot_rows(info, seg_row, slabs, _pick(n, 2048))
    dest_rows = (dest[:, 0], dest[:, 1])
    n_active = (pend[-1:] // EXPERT_BLOCK).astype(jnp.int32)
    used = (cnt > 0).astype(jnp.int32)
    used_pos = jnp.cumsum(used) - 1
    slot_ids = jnp.arange(N_EXPERTS, dtype=jnp.int32)
    expert_seq = jnp.sum(jnp.where((used_pos[None, :] == slot_ids[:, None]) & (used[None, :] > 0),
                                   slot_ids[None, :], 0), axis=1).astype(jnp.int32)
    block_first = jnp.arange(n_blocks, dtype=jnp.int32) * EXPERT_BLOCK
    block_pos = jnp.sum(jnp.where(pend[None, :] <= block_first[:, None], used[None, :], 0),
                        axis=1).astype(jnp.int32)
    n_used = jnp.sum(used).reshape(1)

    x_buf = _dispatch(h2p, dest_rows, ((pstart + cnt) * slabs).astype(jnp.int32),
                      (padded - cnt).astype(jnp.int32), n_active, cap, slabs, _pick(n, 1024))
    y_buf = _moe(x_buf, n_active, expert_seq, block_pos, n_used, w1_bf, w3_bf, w2_bf, slabs)
    return _combine(y_buf, dest_rows, info, x1, p["norm_ffn_post"], gt2, seqlen, slabs, _pick(seqlen, 512))


def kernel(x, c, ada_w, ada_b, norm_mix_pre, norm_mix_post, norm_ffn_pre, norm_ffn_post, w_in, w_out, ssm_lam_re, ssm_lam_im, ssm_log_dt, ssm_b_re, ssm_b_im, ssm_c_re, ssm_c_im, ssm_d, glu_w, glu_b, rwkv_mu, rwkv_w0, rwkv_w_up, rwkv_a0, rwkv_a_up, rwkv_g_up, rwkv_k_k, rwkv_k_a, rwkv_r_k, rwkv_ln_w, rwkv_ln_b, moe_w_grp, moe_b_grp, moe_w_exp, moe_b_exp, moe_w1, moe_w3, moe_w2):
    bsz, seqlen, d = x.shape
    params = dict(norm_mix_pre=norm_mix_pre, norm_mix_post=norm_mix_post, norm_ffn_pre=norm_ffn_pre,
                  norm_ffn_post=norm_ffn_post, w_in=w_in, w_out=w_out, ssm_lam_re=ssm_lam_re,
                  ssm_lam_im=ssm_lam_im, ssm_log_dt=ssm_log_dt, ssm_b_re=ssm_b_re, ssm_b_im=ssm_b_im,
                  ssm_c_re=ssm_c_re, ssm_c_im=ssm_c_im, ssm_d=ssm_d, glu_w=glu_w, glu_b=glu_b,
                  rwkv_mu=rwkv_mu, rwkv_w0=rwkv_w0, rwkv_w_up=rwkv_w_up, rwkv_a0=rwkv_a0,
                  rwkv_a_up=rwkv_a_up, rwkv_g_up=rwkv_g_up, rwkv_k_k=rwkv_k_k, rwkv_k_a=rwkv_k_a,
                  rwkv_r_k=rwkv_r_k, rwkv_ln_w=rwkv_ln_w, rwkv_ln_b=rwkv_ln_b, moe_w_grp=moe_w_grp,
                  moe_b_grp=moe_b_grp, moe_w_exp=moe_w_exp, moe_b_exp=moe_b_exp, moe_w1=moe_w1,
                  moe_w3=moe_w3, moe_w2=moe_w2)
    x2 = x.reshape(bsz * seqlen, d)
    for layer in range(ada_w.shape[0]):
        mod = _ada(c, ada_w[layer], ada_b[layer])
        x2 = _layer(x2, mod, {k: v[layer] for k, v in params.items()}, bsz, seqlen)
    return x2.reshape(bsz, seqlen, d)
```

```python
import functools
import math

import jax
import jax.numpy as jnp
from jax import lax
from jax.experimental import pallas as pl
from jax.experimental.pallas import tpu as pltpu

F32 = jnp.float32
BF16 = jnp.bfloat16

SSM_GROUP = 16
S5_SLAB = 256
S5_TIME_BLOCK = 64
RWKV_HEAD = 64
RWKV_CHUNK = 64
RWKV_BATCH_PER_STEP = 4
HEADS_PER_TILE = 4
LORA_W = 64
LORA_A = 64
LORA_G = 128
N_GROUPS = 8
EXPERTS_PER_GROUP = 8
N_EXPERTS = N_GROUPS * EXPERTS_PER_GROUP
EXPERT_BLOCK = 256
RMS_EPS = 1e-6
GN_EPS = 64e-5
DECAY_SCALE = math.exp(-0.5)
LANES = 128
COMBINE_CHUNK = 32
SMEM_INDEX_BLOCK = 1024
ROUTE_LANES = LANES
V7X_VMEM_BYTES = 64 * 1024 * 1024
VMEM_LIMIT = V7X_VMEM_BYTES - 12 * 1024 * 1024


def _cparams(*sem):
    return pltpu.CompilerParams(dimension_semantics=sem, vmem_limit_bytes=VMEM_LIMIT)


def _sigmoid(x):
    return 1.0 / (1.0 + jnp.exp(-x))


def _dot(a, b):
    return jnp.dot(a, b, preferred_element_type=F32)


def _dot_nt(a, b):
    return lax.dot_general(a, b, (((1,), (1,)), ((), ())), preferred_element_type=F32)


def _dot_tn(a, b):
    return lax.dot_general(a, b, (((0,), (0,)), ((), ())), preferred_element_type=F32)


def _split_bf16(x):
    hi = x.astype(BF16)
    lo = (x - hi.astype(F32)).astype(BF16)
    return hi, lo


def _pack_pair(a, b):
    ua = lax.bitcast_convert_type(a.astype(BF16).astype(F32), jnp.uint32)
    ub = lax.bitcast_convert_type(b.astype(BF16).astype(F32), jnp.uint32)
    return ub | (ua >> 16)


def _unpack_pair(w):
    lo = lax.bitcast_convert_type(w << 16, F32)
    hi = lax.bitcast_convert_type(w & jnp.uint32(0xFFFF0000), F32)
    return lo, hi


def _store_packed(ref, val, n_rows):
    d = val.shape[1]
    slabs = d // (2 * LANES)
    for s in range(slabs):
        a = val[:, s * LANES:(s + 1) * LANES]
        b = val[:, d // 2 + s * LANES:d // 2 + (s + 1) * LANES]
        ref[pl.ds(s, n_rows, stride=slabs), :] = _pack_pair(a, b)


def _load_packed(ref, n_rows, slabs):
    return [_unpack_pair(ref[pl.ds(s, n_rows, stride=slabs), :]) for s in range(slabs)]


def _side_cast(src_ref, dst_ref, regroup):
    for e in range(src_ref.shape[0]):
        if regroup:
            slabs = dst_ref.shape[1]
            half = slabs * LANES
            for s in range(slabs):
                dst_ref[e, s, :LANES, :] = src_ref[e, s * LANES:(s + 1) * LANES, :].astype(BF16)
                dst_ref[e, s, LANES:, :] = src_ref[e, half + s * LANES:half + (s + 1) * LANES, :].astype(BF16)
        else:
            dst_ref[e] = src_ref[e].astype(BF16)


def _side_io(w, n_steps, step_of, regroup):
    n_e, rows, cols = w.shape
    per = n_e // n_steps
    assert per * n_steps == n_e, (n_e, n_steps)
    in_spec = pl.BlockSpec((per, rows, cols), lambda *i: (step_of(*i), 0, 0))
    if regroup:
        shape = (n_e, rows // (2 * LANES), 2 * LANES, cols)
        out_spec = pl.BlockSpec((per,) + shape[1:], lambda *i: (step_of(*i), 0, 0, 0))
    else:
        shape = w.shape
        out_spec = pl.BlockSpec((per, rows, cols), lambda *i: (step_of(*i), 0, 0))
    return in_spec, out_spec, jax.ShapeDtypeStruct(shape, BF16)


def _ada_kernel(c_ref, w_ref, b_ref, o_ref):
    c = c_ref[...]
    bsz = c.shape[0]
    c_hi, c_lo = _split_bf16(c * _sigmoid(c))
    w_hi, w_lo = _split_bf16(w_ref[...])
    first = _dot(jnp.concatenate([c_hi, c_lo], axis=0), w_hi)
    o_ref[...] = first[:bsz] + first[bsz:] + _dot(c_hi, w_lo) + b_ref[...]


def _ada(c, ada_w, ada_b):
    bsz, d = c.shape
    n = ada_w.shape[1]
    tn = 1024
    return pl.pallas_call(
        _ada_kernel,
        grid=(n // tn,),
        in_specs=[pl.BlockSpec((bsz, d), lambda j: (0, 0)),
                  pl.BlockSpec((d, tn), lambda j: (0, j)),
                  pl.BlockSpec((1, tn), lambda j: (0, j))],
        out_specs=pl.BlockSpec((bsz, tn), lambda j: (0, j)),
        out_shape=jax.ShapeDtypeStruct((bsz, n), F32),
        compiler_params=_cparams("arbitrary"),
        name="ada_mod",
    )(c, ada_w, ada_b.reshape(1, n))


def _rms(x, g):
    return x * lax.rsqrt(jnp.mean(x * x, axis=-1, keepdims=True) + RMS_EPS) * g


def _norm_proj_kernel(x_ref, g_ref, sc_ref, sh_ref, w_ref, side_ref, o_ref, side_out):
    h = _rms(x_ref[...], g_ref[...]) * (1.0 + sc_ref[0]) + sh_ref[0]
    o_ref[...] = _dot(h.astype(BF16), w_ref[...]).astype(o_ref.dtype)
    _side_cast(side_ref, side_out, regroup=False)


def _norm_proj(x2, g, sc, sh, w, side_w, seqlen, tm, name):
    n, d = x2.shape
    nout = w.shape[1]
    tpb = seqlen // tm
    side_in, side_out, side_shape = _side_io(side_w, n // tm, lambda i: i, regroup=False)
    return pl.pallas_call(
        _norm_proj_kernel,
        grid=(n // tm,),
        in_specs=[pl.BlockSpec((tm, d), lambda i: (i, 0)),
                  pl.BlockSpec((1, d), lambda i: (0, 0)),
                  pl.BlockSpec((1, 1, d), lambda i: (i // tpb, 0, 0)),
                  pl.BlockSpec((1, 1, d), lambda i: (i // tpb, 0, 0)),
                  pl.BlockSpec((d, nout), lambda i: (0, 0), pipeline_mode=pl.Buffered(1)),
                  side_in],
        out_specs=[pl.BlockSpec((tm, nout), lambda i: (i, 0)), side_out],
        out_shape=[jax.ShapeDtypeStruct((n, nout), BF16), side_shape],
        compiler_params=_cparams("parallel"),
        name=name,
    )(x2, g.reshape(1, d), sc[:, None, :], sh[:, None, :], w, side_w)


def _s5_tables(lam_re, lam_im, log_dt, b_re, b_im, c_re, c_im):
    g, p, cg = b_re.shape
    gs = S5_SLAB // cg
    ns = g // gs
    lr = jnp.minimum(lam_re.astype(F32), -1e-4)
    li = lam_im.astype(F32)
    dt = jnp.exp(log_dt.astype(F32))[:, None]
    mag = jnp.exp(lr * dt)
    ar, ai = mag * jnp.cos(li * dt), mag * jnp.sin(li * dt)
    den = lr * lr + li * li
    qr = ((ar - 1.0) * lr + ai * li) / den
    qi = (ai * lr - (ar - 1.0) * li) / den
    br, bi = b_re.astype(F32), b_im.astype(F32)
    bbr = qr[..., None] * br - qi[..., None] * bi
    bbi = qr[..., None] * bi + qi[..., None] * br

    def in_rows(t):
        return t.reshape(ns, gs, p, cg).transpose(0, 1, 3, 2).reshape(ns, gs * cg, p).astype(BF16)

    def out_cols(t):
        return t.reshape(ns, gs, cg, p).transpose(0, 3, 1, 2).reshape(ns, p, gs * cg).astype(BF16)

    a_tab = jnp.stack([ar.reshape(ns, gs * p), ai.reshape(ns, gs * p)], axis=1)
    return (in_rows(bbr), in_rows(bbi), out_cols(c_re.astype(F32)), out_cols(-c_im.astype(F32)), a_tab)


def _gelu_tanh(x):
    return 0.5 * x * (1.0 + jnp.tanh(math.sqrt(2.0 / math.pi) * (x + 0.044715 * (x * x * x))))


def _s5_kernel(x_ref, g_ref, sc_ref, sh_ref, wu_ref, perm_ref, bre_ref, bim_ref, cre_ref, cim_ref, a_ref,
               d_ref, gw_ref, gb_ref, o_ref, b_ref, c_ref, u_scr, bscr, sscr, yscr, st_ref):
    bsz, lb, d_in = x_ref.shape
    rows = bsz * lb
    dch = wu_ref.shape[1]
    half = st_ref.shape[2] // 2
    n_p = bre_ref.shape[2]

    @pl.when(pl.program_id(0) == 0)
    def _():
        st_ref[...] = jnp.zeros_like(st_ref)
        tile_in = (lax.broadcasted_iota(jnp.int32, (n_p, half), 0)
                   == lax.broadcasted_iota(jnp.int32, (n_p, half), 1) % n_p).astype(BF16)
        tile_out = (lax.broadcasted_iota(jnp.int32, (half, n_p), 0) % n_p
                    == lax.broadcasted_iota(jnp.int32, (half, n_p), 1)).astype(BF16)
        in_mask = (lax.broadcasted_iota(jnp.int32, (S5_SLAB, half), 0) // SSM_GROUP
                   == lax.broadcasted_iota(jnp.int32, (S5_SLAB, half), 1) // n_p)
        out_mask = (lax.broadcasted_iota(jnp.int32, (half, S5_SLAB), 0) // n_p
                    == lax.broadcasted_iota(jnp.int32, (half, S5_SLAB), 1) // SSM_GROUP)
        for s in range(dch // S5_SLAB):
            b_ref[s, :, :half] = jnp.where(in_mask, _dot(bre_ref[s], tile_in), 0.0).astype(BF16)
            b_ref[s, :, half:] = jnp.where(in_mask, _dot(bim_ref[s], tile_in), 0.0).astype(BF16)
            c_ref[s, :half, :] = jnp.where(out_mask, _dot(tile_out, cre_ref[s]), 0.0).astype(BF16)
            c_ref[s, half:, :] = jnp.where(out_mask, _dot(tile_out, cim_ref[s]), 0.0).astype(BF16)

    h = (_rms(x_ref[...], g_ref[...]) * (1.0 + sc_ref[...]) + sh_ref[...]).astype(BF16)
    u_nat = _dot(h.reshape(rows, d_in), wu_ref[...]).astype(BF16)
    u_scr[...] = _dot(perm_ref[...], u_nat).astype(BF16)
    n_slab = dch // S5_SLAB
    slab = lambda s: slice(s * S5_SLAB, (s + 1) * S5_SLAB)

    def project_in(s):
        bscr[s % 2] = _dot(u_scr[:, slab(s)], b_ref[s])

    project_in(0)
    for s in range(n_slab):
        if s + 1 < n_slab:
            project_in(s + 1)
        buf = s % 2
        a_re = a_ref[s, 0:1, :]
        a_im = a_ref[s, 1:2, :]
        s_r = st_ref[s, :, :half]
        s_i = st_ref[s, :, half:]
        for l in range(rows // bsz):
            at_l = slice(l * bsz, (l + 1) * bsz)
            s_r, s_i = (a_re * s_r - a_im * s_i + bscr[buf, at_l, :half],
                        a_re * s_i + a_im * s_r + bscr[buf, at_l, half:])
            sscr[buf, at_l, :half] = s_r
            sscr[buf, at_l, half:] = s_i
        st_ref[s, :, :half] = s_r
        st_ref[s, :, half:] = s_i
        y = _dot(sscr[buf].astype(BF16), c_ref[s]) + d_ref[:, slab(s)] * u_scr[:, slab(s)].astype(F32)
        yscr[:, slab(s)] = _gelu_tanh(y).astype(BF16)
    y = yscr[...]
    gate = _sigmoid(_dot(y, gw_ref[...]) + gb_ref[...])
    out_tm = (y.astype(F32) * gate).astype(BF16)
    out_nat = _dot_tn(perm_ref[...], out_tm).astype(o_ref.dtype)
    o_ref[...] = out_nat.reshape(bsz, lb, dch)


def _s5_glu(x2, g, sc, sh, w_u, tabs, d_skip, glu_w, glu_b, bsz, seqlen, lb):
    b_re, b_im, c_re, c_im, a_tab = tabs
    d_in, dch = w_u.shape
    ns, _, n_half = a_tab.shape
    n_state = 2 * n_half
    rows = lb * bsz
    r_idx = jnp.arange(rows)
    perm = ((r_idx % bsz) * lb + r_idx // bsz)[:, None] == r_idx[None, :]
    full = lambda a: pl.BlockSpec(a.shape, lambda i: (0,) * a.ndim, pipeline_mode=pl.Buffered(1))
    args = (x2.reshape(bsz, seqlen, d_in), g.reshape(1, 1, d_in), sc[:, None, :], sh[:, None, :], w_u,
            perm.astype(BF16), b_re, b_im, c_re, c_im, a_tab,
            d_skip.astype(F32).reshape(1, dch), glu_w.astype(BF16), glu_b.astype(F32).reshape(1, dch))
    y = pl.pallas_call(
        _s5_kernel,
        grid=(seqlen // lb,),
        in_specs=[pl.BlockSpec((bsz, lb, d_in), lambda i: (0, i, 0))] + [full(a) for a in args[1:]],
        out_specs=pl.BlockSpec((bsz, lb, dch), lambda i: (0, i, 0)),
        out_shape=jax.ShapeDtypeStruct((bsz, seqlen, dch), BF16),
        scratch_shapes=[pltpu.VMEM((ns, S5_SLAB, n_state), BF16), pltpu.VMEM((ns, n_state, S5_SLAB), BF16),
                        pltpu.VMEM((rows, dch), BF16), pltpu.VMEM((2, rows, n_state), F32),
                        pltpu.VMEM((2, rows, n_state), F32), pltpu.VMEM((rows, dch), BF16),
                        pltpu.VMEM((ns, bsz, n_state), F32)],
        compiler_params=_cparams("arbitrary"),
        name="s5_mixer_glu",
    )(*args)
    return y.reshape(bsz * seqlen, dch)


def _rwkv_kernel(z_ref, mu_ref, w0_ref, a0_ref, kkw_ref, ka_ref, wa_ref, gup_ref, rk_ref, lnw_ref, lnb_ref,
                 ones_ref, side1_ref, side3_ref, o_ref, side1_out, side3_out, s_ref, zlast_ref):
    _side_cast(side1_ref, side1_out, regroup=True)
    _side_cast(side3_ref, side3_out, regroup=True)
    nb, t, _ = z_ref.shape
    d_r = o_ref.shape[2]
    tile = HEADS_PER_TILE * RWKV_HEAD
    n_tiles = d_r // tile
    first_chunk = pl.program_id(1) == 0

    @pl.when(first_chunk)
    def _():
        s_ref[...] = jnp.zeros_like(s_ref)

    ones_bd = ones_ref[...]

    def seg_sum(x):
        return _dot(x.astype(BF16), ones_bd)

    row = lax.broadcasted_iota(jnp.int32, (t, t), 0)
    col = lax.broadcasted_iota(jnp.int32, (t, t), 1)
    tri = (row >= col).astype(BF16)
    st = HEADS_PER_TILE * t
    rs = lax.broadcasted_iota(jnp.int32, (2 * t, 2 * st), 0)
    t_r = rs % t
    t_c = lax.broadcasted_iota(jnp.int32, (2 * t, 2 * st), 1) % t
    keep = (t_r > t_c) | ((rs >= t) & (t_r == t_c))
    eye_w = (lax.broadcasted_iota(jnp.int32, (t, st), 0)
             == lax.broadcasted_iota(jnp.int32, (t, st), 1) % t).astype(F32)
    blk_mask = (lax.broadcasted_iota(jnp.int32, (st, st), 0) // t
                == lax.broadcasted_iota(jnp.int32, (st, st), 1) // t)
    lane = lax.broadcasted_iota(jnp.int32, (1, tile), 1)
    head_masks = [(lane >= j * RWKV_HEAD) & (lane < (j + 1) * RWKV_HEAD) for j in range(HEADS_PER_TILE)]
    bd_r = lax.broadcasted_iota(jnp.int32, (tile, tile), 0) // RWKV_HEAD
    bd_c = lax.broadcasted_iota(jnp.int32, (tile, tile), 1) // RWKV_HEAD
    bd_mask = bd_r == bd_c
    n_levels = int(math.log2(t))
    slices = [slice(hg * tile, (hg + 1) * tile) for hg in range(n_tiles)]
    units = [(bb, hg) for bb in range(nb) for hg in range(n_tiles)]
    n_u = range(len(units))

    def stack(x):
        zero = jnp.zeros_like(x)
        return jnp.concatenate([jnp.where(m, x, zero) for m in head_masks], axis=0)

    def bf(x):
        return x.astype(BF16)

    at, qt, bt, kt, vv, em, etm, wtot, rkb, gate = ([] for _ in range(10))
    row0 = lax.broadcasted_iota(jnp.int32, (t, 1), 0) == 0
    lora_lane = lax.broadcasted_iota(jnp.int32, (t, LORA_W + LORA_A), 1)
    for bb in range(nb):
        z = z_ref[bb].astype(F32)
        prev_row = jnp.where(first_chunk, 0.0, zlast_ref[bb, 0:1, :])
        zl = z + mu_ref[...] * (jnp.where(row0, prev_row, pltpu.roll(z, 1, 0)) - z)
        zlast_ref[bb, 0:1, :] = z[t - 1:t, :]
        xwa = zl[:, 3 * d_r:3 * d_r + LORA_W + LORA_A]
        lhs = jnp.where(lora_lane < LORA_W, jnp.tanh(xwa), xwa).astype(BF16)
        wa = _dot(lhs, wa_ref[...])
        lw = -DECAY_SCALE * _sigmoid(w0_ref[...] + wa[:, :d_r])
        asig = _sigmoid(a0_ref[...] + wa[:, d_r:])
        r = zl[:, :d_r]
        k_raw = zl[:, d_r:2 * d_r]
        kk = k_raw * kkw_ref[...]
        kp = k_raw * (1.0 + (asig - 1.0) * ka_ref[...])
        xg = zl[:, 3 * d_r + LORA_W + LORA_A:3 * d_r + LORA_W + LORA_A + LORA_G]
        g_full = _dot(_sigmoid(xg).astype(BF16), gup_ref[...])
        lw_hi, lw_lo = _split_bf16(lw)
        cum = _dot(tri, lw_hi) + _dot(tri, lw_lo)
        mid = cum[t // 2 - 1:t // 2, :]
        tot = cum[t - 1:t, :]
        e1 = jnp.exp(cum - mid)
        e2 = jnp.exp(mid - cum)
        e1p = e1 * jnp.exp(-lw)
        kk2 = kk * kk
        kkn = kk / jnp.maximum(jnp.sqrt(jnp.concatenate([seg_sum(kk2[:, sl]) for sl in slices], axis=1)), 1e-12)
        full = dict(at=-kkn * e1p, qt=r * e1, bt=kkn * asig * e2, kt=kp * e2,
                    vv=zl[:, 2 * d_r:3 * d_r], em=jnp.exp(mid), etm=jnp.exp(tot - mid), wtot=jnp.exp(tot),
                    rkb=r * kp * rk_ref[...], gate=g_full)
        for dst, key in ((at, "at"), (qt, "qt"), (bt, "bt"), (kt, "kt"), (vv, "vv"), (em, "em"),
                         (etm, "etm"), (wtot, "wtot"), (rkb, "rkb"), (gate, "gate")):
            dst.extend(full[key][:, sl] for sl in slices)

    s_old = [s_ref[i] for i in n_u]
    wide = []
    for i in n_u:
        lhs = bf(jnp.concatenate([at[i], qt[i]], axis=0))
        rhs = jnp.concatenate([stack(bf(bt[i])), stack(bf(kt[i]))], axis=0)
        a_w = bf(jnp.where(keep, _dot_nt(lhs, rhs), 0.0))
        wide.append([a_w[:t, :st], a_w[:t, st:], a_w[t:, :]])
    x_state = [_dot_nt(bf(jnp.concatenate([at[i], qt[i]], axis=0) * em[i]), bf(s_old[i]))
               for i in n_u]
    sv = [stack(bf(vv[i])) for i in n_u]
    akv = [_dot(wide[i][1], sv[i]) for i in n_u]

    def expand(x_w):
        x_b = bf(x_w)
        return jnp.where(blk_mask, jnp.concatenate([x_b] * HEADS_PER_TILE, axis=0), jnp.zeros((st, st), BF16))

    p_acc = [eye_w + w[0].astype(F32) for w in wide]
    q_bd = [expand(w[0]) for w in wide]
    q_pow = [_dot(wide[i][0], q_bd[i]) for i in n_u]
    for lev in range(1, n_levels):
        for i in n_u:
            q_bd[i] = expand(q_pow[i])
            if lev < n_levels - 1:
                both = _dot(bf(jnp.concatenate([p_acc[i], q_pow[i]], axis=0)), q_bd[i])
                p_acc[i] = p_acc[i] + both[:t]
                q_pow[i] = both[t:]
            else:
                p_acc[i] = p_acc[i] + _dot(bf(p_acc[i]), q_bd[i])
    u_all = [_dot(bf(p_acc[i]), stack(bf(x_state[i][:t] + akv[i]))) for i in n_u]
    y_all = [x_state[i][t:]
             + _dot(wide[i][2], jnp.concatenate([stack(bf(u_all[i])), sv[i]], axis=0)) for i in n_u]

    for i, (bb, hg) in enumerate(units):
        sl = slices[hg]
        y = y_all[i]
        uv = bf(jnp.concatenate([u_all[i], vv[i]], axis=0))
        bk_end = bf(jnp.concatenate([bt[i], kt[i]], axis=0) * etm[i])
        s_ref[i] = s_old[i] * wtot[i] + jnp.where(bd_mask, _dot_tn(uv, bk_end), 0.0)

        mean = seg_sum(y) * (1.0 / RWKV_HEAD)
        dlt = y - mean
        var = seg_sum(dlt * dlt) * (1.0 / RWKV_HEAD)
        yn = dlt * lax.rsqrt(var + GN_EPS) * lnw_ref[:, sl] + lnb_ref[:, sl]
        out = (yn + seg_sum(rkb[i]) * vv[i]) * gate[i]
        o_ref[bb, :, sl] = out.astype(o_ref.dtype)


def _rwkv(z, mu, w0, w_up, a0, a_up, g_up, k_k, k_a, r_k, ln_w, ln_b, side_w1, side_w3, bsz, seqlen):
    n, dz = z.shape
    d_r = w0.shape[0]
    t = RWKV_CHUNK
    nb = RWKV_BATCH_PER_STEP
    nch = seqlen // t
    tile = HEADS_PER_TILE * RWKV_HEAD
    hid = jnp.arange(tile) // RWKV_HEAD
    ones_bd = (hid[:, None] == hid[None, :]).astype(BF16)
    wa = jnp.zeros((LORA_W + LORA_A, 2 * d_r), F32)
    wa = wa.at[:LORA_W, :d_r].set(w_up.astype(F32)).at[LORA_W:, d_r:].set(a_up.astype(F32)).astype(BF16)
    row = lambda a: a.astype(F32).reshape(1, -1)
    full = lambda a: pl.BlockSpec(a.shape, lambda b, c: (0, 0))
    params = (row(mu), row(w0), row(a0), row(k_k), row(k_a), wa, g_up.astype(BF16), row(r_k), row(ln_w),
              row(ln_b), ones_bd)
    side_in, side_out, side_shape = _side_io(side_w1, (bsz // nb) * nch, lambda b, c: b * nch + c, regroup=True)
    y, side1, side3 = pl.pallas_call(
        _rwkv_kernel,
        grid=(bsz // nb, nch),
        in_specs=([pl.BlockSpec((nb, t, dz), lambda b, c: (b, c, 0))] + [full(a) for a in params]
                  + [side_in, side_in]),
        out_specs=[pl.BlockSpec((nb, t, d_r), lambda b, c: (b, c, 0)), side_out, side_out],
        out_shape=[jax.ShapeDtypeStruct((bsz, seqlen, d_r), BF16), side_shape, side_shape],
        scratch_shapes=[pltpu.VMEM((nb * (d_r // tile), tile, tile), F32), pltpu.VMEM((nb, 8, dz), F32)],
        compiler_params=_cparams("parallel", "arbitrary"),
        name="rwkv7_chunked",
    )(z.reshape(bsz, seqlen, dz), *params, side_w1, side_w3)
    return y.reshape(n, d_r), side1, side3


def _post_mix_kernel(ys_ref, yr_ref, wo1_ref, wo2_ref, x_ref, g1_ref, gt_ref, g2_ref, sc_ref, sh_ref,
                     wr_both_ref, br_ref, x1_out, h2_out, lg_out):
    nl = lg_out.shape[1]
    mixed = _dot(ys_ref[...], wo1_ref[...]) + _dot(yr_ref[...], wo2_ref[...])
    x1 = x_ref[...] + gt_ref[0] * _rms(mixed, g1_ref[...])
    x1_out[...] = x1
    h2 = _rms(x1, g2_ref[...]) * (1.0 + sc_ref[0]) + sh_ref[0]
    _store_packed(h2_out, h2, h2.shape[0])
    hi, lo = _split_bf16(h2)
    both = _dot(hi, wr_both_ref[...])
    lg_out[...] = both[:, :nl] + both[:, nl:] + _dot(lo, wr_both_ref[:, :nl]) + br_ref[...]


def _post_mix(ys, yr, w_out, x2, g1, gt1, g2, sc2, sh2, w_route, b_route, seqlen, tm):
    n, d = x2.shape
    ds = ys.shape[1]
    tpb = seqlen // tm
    slabs = d // (2 * LANES)
    wr_both = jnp.concatenate(_split_bf16(w_route), axis=1)
    rows = lambda w: pl.BlockSpec((tm, w), lambda i: (i, 0))
    full = lambda a: pl.BlockSpec(a.shape, lambda i: (0,) * a.ndim)
    bat = pl.BlockSpec((1, 1, d), lambda i: (i // tpb, 0, 0))
    args = (ys, yr, w_out[:ds].astype(BF16), w_out[ds:].astype(BF16), x2, g1.reshape(1, d), gt1[:, None, :],
            g2.reshape(1, d), sc2[:, None, :], sh2[:, None, :], wr_both, b_route.reshape(1, -1))
    in_specs = [rows(ds), rows(yr.shape[1]), full(args[2]), full(args[3]), rows(d), full(args[5]), bat,
                full(args[7]), bat, bat, full(wr_both), full(args[11])]
    return pl.pallas_call(
        _post_mix_kernel,
        grid=(n // tm,),
        in_specs=in_specs,
        out_specs=[rows(d), pl.BlockSpec((tm * slabs, LANES), lambda i: (i, 0)), rows(ROUTE_LANES)],
        out_shape=[jax.ShapeDtypeStruct((n, d), F32), jax.ShapeDtypeStruct((n * slabs, LANES), jnp.uint32),
                   jax.ShapeDtypeStruct((n, ROUTE_LANES), F32)],
        compiler_params=_cparams("parallel"),
        name="out_proj_post",
    )(*args)


def _route_kernel(lg_ref, info_ref, cnt_ref, carry):
    i = pl.program_id(0)
    tm = lg_ref.shape[0]

    @pl.when(i == 0)
    def _():
        carry[...] = jnp.zeros_like(carry)

    lg = lg_ref[...]
    lane = lax.broadcasted_iota(jnp.int32, lg.shape, 1)
    lane_f = lane.astype(F32)
    neg = jnp.float32(-jnp.inf)
    big = jnp.float32(1e9)
    is_g = (lane >= N_EXPERTS) & (lane < N_EXPERTS + N_GROUPS)
    gl = jnp.where(is_g, lg, neg)
    gmax = jnp.max(gl, axis=-1, keepdims=True)
    gidx = jnp.min(jnp.where(gl == gmax, lane_f - N_EXPERTS, big), axis=-1, keepdims=True)
    p_grp = 1.0 / jnp.sum(jnp.where(is_g, jnp.exp(gl - gmax), 0.0), axis=-1, keepdims=True)
    in_grp = (lane < N_EXPERTS) & ((lane // EXPERTS_PER_GROUP).astype(F32) == gidx)
    el = jnp.where(in_grp, lg, neg)
    m1 = jnp.max(el, axis=-1, keepdims=True)
    i1 = jnp.min(jnp.where(el == m1, lane_f, big), axis=-1, keepdims=True)
    el2 = jnp.where(lane_f == i1, neg, el)
    m2 = jnp.max(el2, axis=-1, keepdims=True)
    i2 = jnp.min(jnp.where(el2 == m2, lane_f, big), axis=-1, keepdims=True)
    ex = jnp.exp(m2 - m1)
    w1 = p_grp / (1.0 + ex)
    w2 = p_grp * ex / (1.0 + ex)

    oh1 = lane_f == i1
    oh2 = lane_f == i2
    onehot = (oh1 | oh2).astype(BF16)
    rr = lax.broadcasted_iota(jnp.int32, (tm, tm), 0)
    cc = lax.broadcasted_iota(jnp.int32, (tm, tm), 1)
    before = _dot((rr > cc).astype(BF16), onehot) + carry[...]
    rank1 = jnp.sum(jnp.where(oh1, before, 0.0), axis=-1, keepdims=True)
    rank2 = jnp.sum(jnp.where(oh2, before, 0.0), axis=-1, keepdims=True)
    carry[...] = carry[...] + jnp.sum(onehot.astype(F32), axis=0, keepdims=True)
    cnt_ref[...] = carry[...]

    info = jnp.where(lane == 0, i1, 0.0)
    info = jnp.where(lane == 1, i2, info)
    info = jnp.where(lane == 2, w1, info)
    info = jnp.where(lane == 3, w2, info)
    info = jnp.where(lane == 4, rank1, info)
    info = jnp.where(lane == 5, rank2, info)
    info_ref[...] = info


def _route(logits, tm):
    n = logits.shape[0]
    return pl.pallas_call(
        _route_kernel,
        grid=(n // tm,),
        in_specs=[pl.BlockSpec((tm, ROUTE_LANES), lambda i: (i, 0))],
        out_specs=[pl.BlockSpec((tm, ROUTE_LANES), lambda i: (i, 0)),
                   pl.BlockSpec((1, ROUTE_LANES), lambda i: (0, 0))],
        out_shape=[jax.ShapeDtypeStruct((n, ROUTE_LANES), F32),
                   jax.ShapeDtypeStruct((1, ROUTE_LANES), F32)],
        scratch_shapes=[pltpu.VMEM((1, ROUTE_LANES), F32)],
        compiler_params=_cparams("arbitrary"),
        name="moe_route",
    )(logits)


def _slot_rows_kernel(info_ref, seg_ref, o_ref, *, slabs):
    info = info_ref[...]
    lane = lax.broadcasted_iota(jnp.int32, info.shape, 1)
    lane_f = lane.astype(F32)
    seg = seg_ref[...]
    d0 = jnp.sum(jnp.where(lane_f == info[:, 0:1], seg, 0.0), axis=-1, keepdims=True) + info[:, 4:5] * slabs
    d1 = jnp.sum(jnp.where(lane_f == info[:, 1:2], seg, 0.0), axis=-1, keepdims=True) + info[:, 5:6] * slabs
    o_ref[...] = jnp.where(lane == 0, d0, jnp.where(lane == 1, d1, 0.0)).astype(jnp.int32)


def _slot_rows(info, seg_row, slabs, tm):
    n = info.shape[0]
    return pl.pallas_call(
        functools.partial(_slot_rows_kernel, slabs=slabs),
        grid=(n // tm,),
        in_specs=[pl.BlockSpec((tm, ROUTE_LANES), lambda i: (i, 0)),
                  pl.BlockSpec((1, ROUTE_LANES), lambda i: (0, 0))],
        out_specs=pl.BlockSpec((tm, ROUTE_LANES), lambda i: (i, 0)),
        out_shape=jax.ShapeDtypeStruct((n, ROUTE_LANES), jnp.int32),
        compiler_params=_cparams("parallel"),
        name="moe_slot_rows",
    )(info, seg_row)


def _dispatch_kernel(d0_ref, d1_ref, pad_row_ref, pad_len_ref, na_ref, h_ref, buf_out, zeros, sem, zsem, *,
                     slabs, n_blocks):
    tm = h_ref.shape[0] // slabs

    @pl.when(pl.program_id(0) == 0)
    def _():
        zeros[...] = jnp.zeros_like(zeros)
        bits = [1 << b for b in reversed(range(EXPERT_BLOCK.bit_length() - 1))]

        def pad_copy(e, bit):
            done = pad_len_ref[e] & ~(2 * bit - 1)
            dst = pl.multiple_of(pad_row_ref[e] + done * slabs, slabs)
            return pltpu.make_async_copy(zeros.at[pl.ds(0, bit * slabs)], buf_out.at[pl.ds(dst, bit * slabs)], zsem)

        def block_copy(j):
            dst = pl.multiple_of(j * (EXPERT_BLOCK * slabs), EXPERT_BLOCK * slabs)
            return pltpu.make_async_copy(zeros, buf_out.at[pl.ds(dst, EXPERT_BLOCK * slabs)], zsem)

        def each(action):
            def per_expert(e, c):
                for bit in bits:
                    @pl.when((pad_len_ref[e] & bit) != 0)
                    def _():
                        action(pad_copy(e, bit))
                return c

            def per_block(j, c):
                action(block_copy(j))
                return c

            lax.fori_loop(0, N_EXPERTS, per_expert, 0)
            lax.fori_loop(na_ref[0], n_blocks, per_block, 0)

        each(lambda c: c.start())
        each(lambda c: c.wait())

    def copy(t, dest_ref):
        src = pl.multiple_of(t * slabs, slabs)
        dst = pl.multiple_of(dest_ref[t], slabs)
        return pltpu.make_async_copy(h_ref.at[pl.ds(src, slabs)], buf_out.at[pl.ds(dst, slabs)], sem)

    def start(t, c):
        copy(t, d0_ref).start(priority=0)
        copy(t, d1_ref).start(priority=1)
        return c

    lax.fori_loop(0, tm, start, 0, unroll=4)
    for _ in range(2):
        pltpu.make_async_copy(h_ref, buf_out.at[pl.ds(0, tm * slabs)], sem).wait()


def _dispatch(h2p, dest_rows, pad_row, pad_len, n_active, cap, slabs, tm):
    n = h2p.shape[0] // slabs
    smem = pl.BlockSpec((tm,), lambda i: (i,), memory_space=pltpu.SMEM)
    table = lambda a: pl.BlockSpec(a.shape, lambda i: (0,), memory_space=pltpu.SMEM)
    return pl.pallas_call(
        functools.partial(_dispatch_kernel, slabs=slabs, n_blocks=cap // EXPERT_BLOCK),
        grid=(n // tm,),
        in_specs=[smem, smem, table(pad_row), table(pad_len), table(n_active),
                  pl.BlockSpec((tm * slabs, LANES), lambda i: (i, 0))],
        out_specs=pl.BlockSpec(memory_space=pl.ANY),
        out_shape=jax.ShapeDtypeStruct((cap * slabs, LANES), h2p.dtype),
        scratch_shapes=[pltpu.VMEM((EXPERT_BLOCK * slabs, LANES), h2p.dtype), pltpu.SemaphoreType.DMA,
                        pltpu.SemaphoreType.DMA],
        compiler_params=_cparams("arbitrary"),
        name="moe_dispatch",
    )(dest_rows[0], dest_rows[1], pad_row, pad_len, n_active, h2p)


def _moe_kernel(na_ref, eseq_ref, epos_ref, nd_ref, x_ref, w1_hbm, w3_hbm, w2_hbm, o_ref,
                w1b, w3b, w2b, sem):
    j = pl.program_id(0)
    active = j < na_ref[0]
    pos = epos_ref[j]
    fresh = (j == 0) | (pos != epos_ref[jnp.maximum(j - 1, 0)])
    slabs = w1b.shape[1]

    def weight_copies(p):
        e = eseq_ref[p]
        slot = p % 2
        return [pltpu.make_async_copy(w_hbm.at[e], stage.at[slot], sem.at[slot, i])
                for i, (w_hbm, stage) in enumerate(((w1_hbm, w1b), (w3_hbm, w3b), (w2_hbm, w2b)))]

    @pl.when(j == 0)
    def _():
        for c in weight_copies(0):
            c.start()

    @pl.when(active & fresh)
    def _():
        for c in weight_copies(pos):
            c.wait()

        @pl.when(pos + 1 < nd_ref[0])
        def _():
            for c in weight_copies(pos + 1):
                c.start()

    @pl.when(active)
    def _():
        slot = pos % 2
        acc1 = jnp.zeros((EXPERT_BLOCK, w1b.shape[3]), F32)
        acc3 = jnp.zeros((EXPERT_BLOCK, w1b.shape[3]), F32)
        for s, (lo, hi) in enumerate(_load_packed(x_ref, EXPERT_BLOCK, slabs)):
            lhs = jnp.concatenate([lo.astype(BF16), hi.astype(BF16)], axis=1)
            acc1 = acc1 + _dot(lhs, w1b[slot, s])
            acc3 = acc3 + _dot(lhs, w3b[slot, s])
        hid = (acc1 * _sigmoid(acc1)) * acc3
        _store_packed(o_ref, _dot(hid.astype(BF16), w2b[slot]), EXPERT_BLOCK)


def _moe(x_buf, n_active, expert_seq, block_pos, n_used, w1, w3, w2, slabs):
    cap = x_buf.shape[0] // slabs
    nb = cap // EXPERT_BLOCK

    def xmap(j, na, *_):
        return (jnp.minimum(j, na[0] - 1), 0)

    xspec = pl.BlockSpec((EXPERT_BLOCK * slabs, LANES), xmap)
    hbm = pl.BlockSpec(memory_space=pl.ANY)
    grid_spec = pltpu.PrefetchScalarGridSpec(
        num_scalar_prefetch=4,
        grid=(nb,),
        in_specs=[xspec, hbm, hbm, hbm],
        out_specs=xspec,
        scratch_shapes=[pltpu.VMEM((2,) + w1.shape[1:], BF16), pltpu.VMEM((2,) + w3.shape[1:], BF16),
                        pltpu.VMEM((2,) + w2.shape[1:], BF16), pltpu.SemaphoreType.DMA((2, 3))],
    )
    return pl.pallas_call(
        _moe_kernel,
        grid_spec=grid_spec,
        out_shape=jax.ShapeDtypeStruct(x_buf.shape, x_buf.dtype),
        input_output_aliases={4: 0},
        compiler_params=_cparams("arbitrary"),
        name="moe_experts",
    )(n_active, expert_seq, block_pos, n_used, x_buf, w1, w3, w2)


def _combine_kernel(d0_ref, d1_ref, d0n_ref, d1n_ref, y_ref, info_ref, x1_ref, g_ref, gt_ref, o_ref,
                    rows, sem, *, slabs, tiles_per_idx, chunk):
    i = pl.program_id(0)
    n_tiles = pl.num_programs(0)
    tm = x1_ref.shape[0]
    slot = i % 2
    nxt_slot = 1 - slot
    nxt = jnp.minimum(i + 1, n_tiles - 1)

    def copy(tile, to_slot, t, dest_ref, k):
        off = (tile % tiles_per_idx) * tm
        src = pl.multiple_of(dest_ref[off + t], slabs)
        dst = pl.multiple_of(t * slabs, slabs)
        return pltpu.make_async_copy(y_ref.at[pl.ds(src, slabs)], rows.at[to_slot, k, pl.ds(dst, slabs)],
                                     sem.at[to_slot])

    def drain(which):
        for k in range(2):
            pltpu.make_async_copy(y_ref.at[pl.ds(0, tm * slabs)], rows.at[which, k], sem.at[which]).wait()

    @pl.when(i == 0)
    def _():
        def start(t, c):
            copy(i, slot, t, d0_ref, 0).start(priority=0)
            copy(i, slot, t, d1_ref, 1).start(priority=1)
            return c

        lax.fori_loop(0, tm, start, 0, unroll=4)

    drain(slot)
    g_row = g_ref[...]
    gt_row = gt_ref[0]
    for c in range(tm // chunk):
        tok = slice(c * chunk, (c + 1) * chunk)
        info = info_ref[tok, :]
        w1 = info[:, 2:3]
        w2 = info[:, 3:4]
        lo_parts, hi_parts = [], []
        for s in range(slabs):
            at_s = pl.ds(c * chunk * slabs + s, chunk, stride=slabs)
            lo0, hi0 = _unpack_pair(rows[slot, 0, at_s, :])
            lo1, hi1 = _unpack_pair(rows[slot, 1, at_s, :])
            lo_parts.append(w1 * lo0 + w2 * lo1)
            hi_parts.append(w1 * hi0 + w2 * hi1)
        ffn = jnp.concatenate(lo_parts + hi_parts, axis=1)
        o_ref[tok, :] = x1_ref[tok, :] + gt_row * _rms(ffn, g_row)
        for t in range(c * chunk, (c + 1) * chunk):
            copy(nxt, nxt_slot, t, d0n_ref, 0).start(priority=0)
            copy(nxt, nxt_slot, t, d1n_ref, 1).start(priority=1)

    @pl.when(i == n_tiles - 1)
    def _():
        drain(nxt_slot)


def _combine(y_buf, dest_rows, info, x1, g, gt2, seqlen, slabs, tm):
    n, d = x1.shape
    tpb = seqlen // tm
    idx_block = max(tm, SMEM_INDEX_BLOCK)
    per = idx_block // tm
    last = n // tm - 1
    smem = pl.BlockSpec((idx_block,), lambda i: (i // per,), memory_space=pltpu.SMEM)
    smem_next = pl.BlockSpec((idx_block,), lambda i: (jnp.minimum(i + 1, last) // per,), memory_space=pltpu.SMEM)
    return pl.pallas_call(
        functools.partial(_combine_kernel, slabs=slabs, tiles_per_idx=per, chunk=min(tm, COMBINE_CHUNK)),
        grid=(n // tm,),
        in_specs=[smem, smem, smem_next, smem_next,
                  pl.BlockSpec(memory_space=pl.ANY),
                  pl.BlockSpec((tm, ROUTE_LANES), lambda i: (i, 0)),
                  pl.BlockSpec((tm, d), lambda i: (i, 0)),
                  pl.BlockSpec((1, d), lambda i: (0, 0)),
                  pl.BlockSpec((1, 1, d), lambda i: (i // tpb, 0, 0))],
        out_specs=pl.BlockSpec((tm, d), lambda i: (i, 0)),
        out_shape=jax.ShapeDtypeStruct((n, d), F32),
        scratch_shapes=[pltpu.VMEM((2, 2, tm * slabs, LANES), y_buf.dtype), pltpu.SemaphoreType.DMA((2,))],
        compiler_params=_cparams("arbitrary"),
        name="moe_combine",
    )(dest_rows[0], dest_rows[1], dest_rows[0], dest_rows[1], y_buf, info, x1, g.reshape(1, d),
      gt2[:, None, :])


def _pick(n, pref):
    while n % pref:
        pref //= 2
    return pref


def _layer(x2, mod, p, bsz, seqlen):
    n, d = x2.shape
    sh1, sc1, gt1, sh2, sc2, gt2 = jnp.split(mod, 6, axis=-1)
    d_ssm = p["ssm_d"].shape[0]

    z, w2_bf = _norm_proj(x2, p["norm_mix_pre"], sc1, sh1, p["w_in"][:, d_ssm:].astype(BF16), p["moe_w2"],
                          seqlen, _pick(seqlen, max(n // N_EXPERTS, 16)), "in_proj_z")

    tabs = _s5_tables(p["ssm_lam_re"], p["ssm_lam_im"], p["ssm_log_dt"], p["ssm_b_re"], p["ssm_b_im"],
                      p["ssm_c_re"], p["ssm_c_im"])
    y_ssm = _s5_glu(x2, p["norm_mix_pre"], sc1, sh1, p["w_in"][:, :d_ssm].astype(BF16), tabs, p["ssm_d"], p["glu_w"],
                    p["glu_b"], bsz, seqlen, _pick(seqlen, S5_TIME_BLOCK))

    y_rwkv, w1_bf, w3_bf = _rwkv(z, p["rwkv_mu"], p["rwkv_w0"], p["rwkv_w_up"], p["rwkv_a0"], p["rwkv_a_up"],
                                 p["rwkv_g_up"], p["rwkv_k_k"], p["rwkv_k_a"], p["rwkv_r_k"], p["rwkv_ln_w"],
                                 p["rwkv_ln_b"], p["moe_w1"], p["moe_w3"], bsz, seqlen)

    w_route = jnp.zeros((d, ROUTE_LANES), F32)
    w_route = w_route.at[:, :N_EXPERTS].set(p["moe_w_exp"].astype(F32))
    w_route = w_route.at[:, N_EXPERTS:N_EXPERTS + N_GROUPS].set(p["moe_w_grp"].astype(F32))
    b_route = jnp.zeros((ROUTE_LANES,), F32)
    b_route = b_route.at[:N_EXPERTS].set(p["moe_b_exp"].astype(F32))
    b_route = b_route.at[N_EXPERTS:N_EXPERTS + N_GROUPS].set(p["moe_b_grp"].astype(F32))
    x1, h2p, logits = _post_mix(y_ssm, y_rwkv, p["w_out"], x2, p["norm_mix_post"], gt1, p["norm_ffn_pre"],
                                sc2, sh2, w_route, b_route, seqlen, _pick(seqlen, 512))
    slabs = d // (2 * LANES)

    info, counts = _route(logits, _pick(n, 512))
    cnt = counts[0, :N_EXPERTS].astype(jnp.int32)
    padded = (cnt + EXPERT_BLOCK - 1) // EXPERT_BLOCK * EXPERT_BLOCK
    pend = jnp.cumsum(padded)
    pstart = pend - padded
    n_blocks = -(-(2 * n) // EXPERT_BLOCK) + N_EXPERTS
    cap = n_blocks * EXPERT_BLOCK
    seg_row = jnp.zeros((1, ROUTE_LANES), F32).at[0, :N_EXPERTS].set((pstart * slabs).astype(F32))
    dest = _slot_rows(info, seg_row, slabs, _pick(n, 2048))
    dest_rows = (dest[:, 0], dest[:, 1])
    n_active = (pend[-1:] // EXPERT_BLOCK).astype(jnp.int32)
    used = (cnt > 0).astype(jnp.int32)
    used_pos = jnp.cumsum(used) - 1
    slot_ids = jnp.arange(N_EXPERTS, dtype=jnp.int32)
    expert_seq = jnp.sum(jnp.where((used_pos[None, :] == slot_ids[:, None]) & (used[None, :] > 0),
                                   slot_ids[None, :], 0), axis=1).astype(jnp.int32)
    block_first = jnp.arange(n_blocks, dtype=jnp.int32) * EXPERT_BLOCK
    block_pos = jnp.sum(jnp.where(pend[None, :] <= block_first[:, None], used[None, :], 0),
                        axis=1).astype(jnp.int32)
    n_used = jnp.sum(used).reshape(1)

    x_buf = _dispatch(h2p, dest_rows, ((pstart + cnt) * slabs).astype(jnp.int32),
                      (padded - cnt).astype(jnp.int32), n_active, cap, slabs, _pick(n, 1024))
    y_buf = _moe(x_buf, n_active, expert_seq, block_pos, n_used, w1_bf, w3_bf, w2_bf, slabs)
    return _combine(y_buf, dest_rows, info, x1, p["norm_ffn_post"], gt2, seqlen, slabs, _pick(seqlen, 512))


def kernel(x, c, ada_w, ada_b, norm_mix_pre, norm_mix_post, norm_ffn_pre, norm_ffn_post, w_in, w_out, ssm_lam_re, ssm_lam_im, ssm_log_dt, ssm_b_re, ssm_b_im, ssm_c_re, ssm_c_im, ssm_d, glu_w, glu_b, rwkv_mu, rwkv_w0, rwkv_w_up, rwkv_a0, rwkv_a_up, rwkv_g_up, rwkv_k_k, rwkv_k_a, rwkv_r_k, rwkv_ln_w, rwkv_ln_b, moe_w_grp, moe_b_grp, moe_w_exp, moe_b_exp, moe_w1, moe_w3, moe_w2):
    bsz, seqlen, d = x.shape
    params = dict(norm_mix_pre=norm_mix_pre, norm_mix_post=norm_mix_post, norm_ffn_pre=norm_ffn_pre,
                  norm_ffn_post=norm_ffn_post, w_in=w_in, w_out=w_out, ssm_lam_re=ssm_lam_re,
                  ssm_lam_im=ssm_lam_im, ssm_log_dt=ssm_log_dt, ssm_b_re=ssm_b_re, ssm_b_im=ssm_b_im,
                  ssm_c_re=ssm_c_re, ssm_c_im=ssm_c_im, ssm_d=ssm_d, glu_w=glu_w, glu_b=glu_b,
                  rwkv_mu=rwkv_mu, rwkv_w0=rwkv_w0, rwkv_w_up=rwkv_w_up, rwkv_a0=rwkv_a0,
                  rwkv_a_up=rwkv_a_up, rwkv_g_up=rwkv_g_up, rwkv_k_k=rwkv_k_k, rwkv_k_a=rwkv_k_a,
                  rwkv_r_k=rwkv_r_k, rwkv_ln_w=rwkv_ln_w, rwkv_ln_b=rwkv_ln_b, moe_w_grp=moe_w_grp,
                  moe_b_grp=moe_b_grp, moe_w_exp=moe_w_exp, moe_b_exp=moe_b_exp, moe_w1=moe_w1,
                  moe_w3=moe_w3, moe_w2=moe_w2)
    x2 = x.reshape(bsz * seqlen, d)
    for layer in range(ada_w.shape[0]):
        mod = _ada(c, ada_w[layer], ada_b[layer])
        x2 = _layer(x2, mod, {k: v[layer] for k, v in params.items()}, bsz, seqlen)
    return x2.reshape(bsz, seqlen, d)
```
